```python
import math
import jax, jax.numpy as jnp
from jax import lax
import numpy as np

D_MODEL = 2048
BATCH = 2
SEQ = 4096
DEPTH = 2

N_A_LAYERS = DEPTH // 2
N_B_LAYERS = DEPTH - N_A_LAYERS
S5_WIDTH = D_MODEL
S5_GROUP = 16
S5_GROUPS = S5_WIDTH // S5_GROUP
S5_STATE = 64
DT_MIN = 1e-3
DT_MAX = 1e-1
FOX_HEAD_DIM = 128
FOX_HEADS = D_MODEL // FOX_HEAD_DIM
FOX_WIDTH = FOX_HEADS * FOX_HEAD_DIM
Q_BLOCK = 128
RMS_EPS = 1e-6
NEG_INF = -1e30

kernel_name = "yoco_s5_fox_hybrid"

F32 = jnp.float32


def rmsnorm(x, g):
    xf = x.astype(F32)
    y = xf * lax.rsqrt(jnp.mean(xf * xf, axis=-1, keepdims=True) + RMS_EPS)
    return (y * g.astype(F32)).astype(x.dtype)


def s5_ssm(u, a_re, a_im, log_dt, b_re, b_im, c_re, c_im, d_skip):
    bsz, seq, _ = u.shape
    uf = u.astype(F32).reshape(bsz, seq, S5_GROUPS, S5_GROUP)
    dt = jnp.exp(log_dt.astype(F32))[:, None]
    ar = a_re.astype(F32)
    ai = a_im.astype(F32)
    mag = jnp.exp(ar * dt)
    abar_re = mag * jnp.cos(ai * dt)
    abar_im = mag * jnp.sin(ai * dt)
    den = ar * ar + ai * ai
    nr = abar_re - 1.0
    coef_re = (nr * ar + abar_im * ai) / den
    coef_im = (abar_im * ar - nr * ai) / den
    bu_re = jnp.einsum('bsgc,gpc->bsgp', uf, b_re.astype(F32))
    bu_im = jnp.einsum('bsgc,gpc->bsgp', uf, b_im.astype(F32))
    x_re = coef_re * bu_re - coef_im * bu_im
    x_im = coef_re * bu_im + coef_im * bu_re
    shape_a = (1, seq, S5_GROUPS, S5_STATE)
    a_seq_re = jnp.broadcast_to(abar_re, shape_a)
    a_seq_im = jnp.broadcast_to(abar_im, shape_a)

    def combine(left, right):
        a1r, a1i, b1r, b1i = left
        a2r, a2i, b2r, b2i = right
        return (a2r * a1r - a2i * a1i,
                a2r * a1i + a2i * a1r,
                a2r * b1r - a2i * b1i + b2r,
                a2r * b1i + a2i * b1r + b2i)

    _, _, h_re, h_im = lax.associative_scan(combine, (a_seq_re, a_seq_im, x_re, x_im), axis=1)
    y = (jnp.einsum('bsgp,gcp->bsgc', h_re, c_re.astype(F32))
         - jnp.einsum('bsgp,gcp->bsgc', h_im, c_im.astype(F32)))
    y = y + d_skip.astype(F32).reshape(S5_GROUPS, S5_GROUP) * uf
    return y.reshape(bsz, seq, S5_WIDTH)


def s5_layer(h, g_pre, g_post, w_in, a_re, a_im, log_dt, b_re, b_im, c_re, c_im, d_skip, w_glu, b_glu, w_out):
    xn = rmsnorm(h, g_pre)
    uz = xn @ w_in
    u, z = jnp.split(uz, 2, axis=-1)
    y = s5_ssm(u, a_re, a_im, log_dt, b_re, b_im, c_re, c_im, d_skip)
    y = jax.nn.gelu(y)
    y = y * jax.nn.sigmoid(y @ w_glu.astype(F32) + b_glu.astype(F32))
    y = y.astype(h.dtype) * jax.nn.silu(z)
    return h + rmsnorm(y @ w_out, g_post)


def shared_kv(h, g_kv, w_kv, b_f):
    bsz, seq, _ = h.shape
    kvf = rmsnorm(h, g_kv) @ w_kv
    k = kvf[..., :FOX_WIDTH]
    v = kvf[..., FOX_WIDTH:2 * FOX_WIDTH]
    f_logit = kvf[..., 2 * FOX_WIDTH:]
    k = k.reshape(bsz, seq, FOX_HEADS, FOX_HEAD_DIM).transpose(0, 2, 1, 3)
    v = v.reshape(bsz, seq, FOX_HEADS, FOX_HEAD_DIM).transpose(0, 2, 1, 3)
    log_f = jax.nn.log_sigmoid(f_logit.astype(F32) + b_f.astype(F32))
    cum = jnp.cumsum(log_f, axis=1).transpose(0, 2, 1)
    return k, v, cum


def fox_attention(q, k, v, cum):
    bsz, nh, seq, dh = q.shape
    nblk = seq // Q_BLOCK
    qb = q.reshape(bsz, nh, nblk, Q_BLOCK, dh).transpose(2, 0, 1, 3, 4)
    cb = cum.reshape(bsz, nh, nblk, Q_BLOCK).transpose(2, 0, 1, 3)
    kpos = jnp.arange(seq)
    scale = dh ** -0.5

    def block(args):
        q_i, c_i, i = args
        s = jnp.einsum('bhqd,bhkd->bhqk', q_i, k, preferred_element_type=F32) * scale
        s = s + c_i[..., :, None] - cum[:, :, None, :]
        qpos = i * Q_BLOCK + jnp.arange(Q_BLOCK)
        s = jnp.where(kpos[None, :] <= qpos[:, None], s, NEG_INF)
        p = jax.nn.softmax(s, axis=-1)
        return jnp.einsum('bhqk,bhkd->bhqd', p.astype(v.dtype), v)

    o = lax.map(block, (qb, cb, jnp.arange(nblk)))
    return o.transpose(1, 2, 0, 3, 4).reshape(bsz, nh, seq, dh)


def fox_layer(h, g_pre, g_post, w_in, w_out, k, v, cum):
    bsz, seq, _ = h.shape
    qz = rmsnorm(h, g_pre) @ w_in
    q, z = jnp.split(qz, 2, axis=-1)
    q = q.reshape(bsz, seq, FOX_HEADS, FOX_HEAD_DIM).transpose(0, 2, 1, 3)
    o = fox_attention(q, k, v, cum)
    o = o.transpose(0, 2, 1, 3).reshape(bsz, seq, FOX_WIDTH)
    o = o.astype(h.dtype) * jax.nn.silu(z)
    return h + rmsnorm(o @ w_out, g_post)


def setup_inputs(seed: int = 0) -> dict:
    key = jax.random.key(seed)
    ks = jax.random.split(key, 24)
    nrm = lambda k, shp, s: jax.random.normal(k, shp, F32) * s
    n = jnp.arange(S5_STATE, dtype=F32)
    a_re = -0.5 + nrm(ks[4], (N_A_LAYERS, S5_GROUPS, S5_STATE), 0.01)
    a_im = math.pi * n + nrm(ks[5], (N_A_LAYERS, S5_GROUPS, S5_STATE), 0.01)
    log_dt = jax.random.uniform(ks[6], (N_A_LAYERS, S5_GROUPS), F32, math.log(DT_MIN), math.log(DT_MAX))
    return {
        "x": nrm(ks[0], (BATCH, SEQ, D_MODEL), 1.0),
        "norm_pre": 1.0 + nrm(ks[1], (DEPTH, D_MODEL), 0.02),
        "norm_post": 1.0 + nrm(ks[2], (DEPTH, D_MODEL), 0.02),
        "s5_w_in": nrm(ks[3], (N_A_LAYERS, D_MODEL, 2 * S5_WIDTH), D_MODEL ** -0.5),
        "s5_a_re": a_re,
        "s5_a_im": a_im,
        "s5_log_dt": log_dt,
        "s5_b_re": nrm(ks[7], (N_A_LAYERS, S5_GROUPS, S5_STATE, S5_GROUP), (2 * S5_GROUP) ** -0.5),
        "s5_b_im": nrm(ks[8], (N_A_LAYERS, S5_GROUPS, S5_STATE, S5_GROUP), (2 * S5_GROUP) ** -0.5),
        "s5_c_re": nrm(ks[9], (N_A_LAYERS, S5_GROUPS, S5_GROUP, S5_STATE), S5_STATE ** -0.5),
        "s5_c_im": nrm(ks[10], (N_A_LAYERS, S5_GROUPS, S5_GROUP, S5_STATE), S5_STATE ** -0.5),
        "s5_d": nrm(ks[11], (N_A_LAYERS, S5_WIDTH), 1.0),
        "s5_w_glu": nrm(ks[12], (N_A_LAYERS, S5_WIDTH, S5_WIDTH), S5_WIDTH ** -0.5),
        "s5_b_glu": nrm(ks[13], (N_A_LAYERS, S5_WIDTH), 0.01),
        "s5_w_out": nrm(ks[14], (N_A_LAYERS, S5_WIDTH, D_MODEL), S5_WIDTH ** -0.5),
        "kv_norm": 1.0 + nrm(ks[15], (D_MODEL,), 0.02),
        "kv_w": nrm(ks[16], (D_MODEL, 2 * FOX_WIDTH + FOX_HEADS), D_MODEL ** -0.5),
        "kv_b_f": jax.random.uniform(ks[17], (FOX_HEADS,), F32, 2.0, 7.0),
        "fox_w_in": nrm(ks[18], (N_B_LAYERS, D_MODEL, 2 * FOX_WIDTH), D_MODEL ** -0.5),
        "fox_w_out": nrm(ks[19], (N_B_LAYERS, FOX_WIDTH, D_MODEL), FOX_WIDTH ** -0.5),
    }


def reference(x, norm_pre, norm_post, s5_w_in, s5_a_re, s5_a_im, s5_log_dt, s5_b_re, s5_b_im,
              s5_c_re, s5_c_im, s5_d, s5_w_glu, s5_b_glu, s5_w_out, kv_norm, kv_w, kv_b_f,
              fox_w_in, fox_w_out):
    h = x
    k = v = cum = None
    for layer in range(DEPTH):
        if layer < N_A_LAYERS:
            i = layer
            h = s5_layer(h, norm_pre[layer], norm_post[layer], s5_w_in[i], s5_a_re[i], s5_a_im[i],
                         s5_log_dt[i], s5_b_re[i], s5_b_im[i], s5_c_re[i], s5_c_im[i], s5_d[i],
                         s5_w_glu[i], s5_b_glu[i], s5_w_out[i])
        else:
            if layer == N_A_LAYERS:
                k, v, cum = shared_kv(h, kv_norm, kv_w, kv_b_f)
            j = layer - N_A_LAYERS
            h = fox_layer(h, norm_pre[layer], norm_post[layer], fox_w_in[j], fox_w_out[j], k, v, cum)
    return h
```

```python
import functools
import math

import jax
import jax.numpy as jnp
from jax import lax
from jax.experimental import pallas as pl
from jax.experimental.pallas import tpu as pltpu

F32 = jnp.float32
BF16 = jnp.bfloat16

RMS_EPS = 1e-6
NEG_INF = -1e30

S5_GROUP = 16
S5_STATE = 64
FOX_HEAD_DIM = 128

V7X_LANES = 128
V7X_SUBLANES = 8
SSM_CHUNK = 16
GROUPS_PER_SLAB = V7X_LANES // S5_GROUP
VMEM_LIMIT = 48 * 1024 * 1024


def _params(*semantics):
    return pltpu.CompilerParams(dimension_semantics=semantics, vmem_limit_bytes=VMEM_LIMIT)


def _rms_scale(x, gain):
    ms = jnp.mean(x * x, axis=-1, keepdims=True)
    return x * lax.rsqrt(ms + RMS_EPS) * gain


def _norm_proj_kernel(x_ref, gain_ref, w1_ref, w2_ref, o1_ref, o2_ref, xn_ref, *, act1, act2):
    @pl.when(pl.program_id(1) == 0)
    def _():
        xn_ref[...] = _rms_scale(x_ref[...], gain_ref[...]).astype(BF16)

    xn = xn_ref[...]
    o1_ref[...] = act1(jnp.dot(xn, w1_ref[...], preferred_element_type=F32)).astype(o1_ref.dtype)
    o2_ref[...] = act2(jnp.dot(xn, w2_ref[...], preferred_element_type=F32)).astype(o2_ref.dtype)


def norm_proj(x, gain, w, act1, act2, dtype1, dtype2, *, tm=512, tn=512, name="norm_proj"):
    t, d = x.shape
    n = w.shape[1] // 2
    nj = n // tn
    return pl.pallas_call(
        functools.partial(_norm_proj_kernel, act1=act1, act2=act2),
        grid=(t // tm, nj),
        in_specs=[
            pl.BlockSpec((tm, d), lambda i, j: (i, 0)),
            pl.BlockSpec((1, d), lambda i, j: (0, 0)),
            pl.BlockSpec((d, tn), lambda i, j: (0, j)),
            pl.BlockSpec((d, tn), lambda i, j: (0, j + nj)),
        ],
        out_specs=[
            pl.BlockSpec((tm, tn), lambda i, j: (i, j)),
            pl.BlockSpec((tm, tn), lambda i, j: (i, j)),
        ],
        out_shape=[jax.ShapeDtypeStruct((t, n), dtype1), jax.ShapeDtypeStruct((t, n), dtype2)],
        scratch_shapes=[pltpu.VMEM((tm, d), BF16)],
        compiler_params=_params("parallel", "arbitrary"),
        name=name,
    )(x, gain, w, w)


def _out_proj_kernel(a_ref, w_ref, res_ref, gain_ref, o_ref):
    y = jnp.dot(a_ref[...], w_ref[...], preferred_element_type=F32)
    o_ref[...] = res_ref[...] + _rms_scale(y, gain_ref[...])


def out_proj(a, w, res, gain, *, tm=512, name="out_proj"):
    t, k = a.shape
    d = w.shape[1]
    return pl.pallas_call(
        _out_proj_kernel,
        grid=(t // tm,),
        in_specs=[
            pl.BlockSpec((tm, k), lambda i: (i, 0)),
            pl.BlockSpec((k, d), lambda i: (0, 0)),
            pl.BlockSpec((tm, d), lambda i: (i, 0)),
            pl.BlockSpec((1, d), lambda i: (0, 0)),
        ],
        out_specs=pl.BlockSpec((tm, d), lambda i: (i, 0)),
        out_shape=jax.ShapeDtypeStruct((t, d), F32),
        compiler_params=_params("parallel"),
        name=name,
    )(a, w, res, gain)


def _glu_kernel(yg_ref, w_ref, b_ref, sz_ref, o_ref, yb_ref, *, tn):
    j = pl.program_id(1)

    @pl.when(j == 0)
    def _():
        yb_ref[...] = yg_ref[...].astype(BF16)

    gate = jax.nn.sigmoid(jnp.dot(yb_ref[...], w_ref[...], preferred_element_type=F32) + b_ref[...])
    yg = yg_ref[:, pl.ds(pl.multiple_of(j * tn, tn), tn)]
    o_ref[...] = (yg * gate * sz_ref[...].astype(F32)).astype(o_ref.dtype)


def glu_gate(yg, w, b, sz, *, tm=512, tn=512, name="s5_glu"):
    t, k = yg.shape
    n = w.shape[1]
    return pl.pallas_call(
        functools.partial(_glu_kernel, tn=tn),
        grid=(t // tm, n // tn),
        in_specs=[
            pl.BlockSpec((tm, k), lambda i, j: (i, 0)),
            pl.BlockSpec((k, tn), lambda i, j: (0, j)),
            pl.BlockSpec((1, tn), lambda i, j: (0, j)),
            pl.BlockSpec((tm, tn), lambda i, j: (i, j)),
        ],
        out_specs=pl.BlockSpec((tm, tn), lambda i, j: (i, j)),
        out_shape=jax.ShapeDtypeStruct((t, n), BF16),
        scratch_shapes=[pltpu.VMEM((tm, k), BF16)],
        compiler_params=_params("parallel", "arbitrary"),
        name=name,
    )(yg, w, b, sz)


def _ssm_prep_kernel(are_ref, aim_ref, ldt_ref, b2_ref, c2_ref,
                     t0_ref, g_ref, h_ref, sr_ref, si_ref, *, groups, scan_steps):
    lane = lax.broadcasted_iota(jnp.int32, (1, 2 * S5_STATE), 1)
    minus_plus = jnp.where(lane < S5_STATE, -1.0, 1.0).astype(F32)
    gw = S5_GROUP
    width = SSM_CHUNK * gw

    for gi in range(groups):
        ar = are_ref[gi]
        ai = aim_ref[gi]
        dt = jnp.exp(ldt_ref[gi])
        mag = jnp.exp(ar * dt)
        lam_r = mag * jnp.cos(ai * dt)
        lam_i = mag * jnp.sin(ai * dt)
        den = ar * ar + ai * ai
        nr = lam_r - 1.0
        coef_r = (nr * ar + lam_i * ai) / den
        coef_i = (lam_i * ar - nr * ai) / den

        pow_r = [jnp.ones_like(lam_r)]
        pow_i = [jnp.zeros_like(lam_r)]
        for _ in range(SSM_CHUNK):
            pr, pi = pow_r[-1], pow_i[-1]
            pow_r.append(pr * lam_r - pi * lam_i)
            pow_i.append(pr * lam_i + pi * lam_r)

        b2 = b2_ref[gi]
        b2s = pltpu.roll(b2, S5_STATE, axis=1) * minus_plus
        c2 = c2_ref[gi]
        c2a = c2 * (-minus_plus)
        c2b = -pltpu.roll(c2, S5_STATE, axis=1)

        g_rows = []
        for step in range(SSM_CHUNK):
            pr, pi = pow_r[SSM_CHUNK - 1 - step], pow_i[SSM_CHUNK - 1 - step]
            wr = pr * coef_r - pi * coef_i
            wi = pr * coef_i + pi * coef_r
            g_rows.append(wr * b2 + wi * b2s)
        g_ref[gi] = jnp.concatenate(g_rows, axis=0).astype(g_ref.dtype)

        cl = [pow_r[tau] * c2a + pow_i[tau] * c2b for tau in range(SSM_CHUNK + 1)]
        h_t = jnp.concatenate(cl[1:], axis=0)
        h_ref[gi] = h_t.T.astype(h_ref.dtype)

        bbar2 = coef_r * b2 + coef_i * b2s
        cl_all = jnp.concatenate(cl[:-1], axis=0)
        krow = lax.dot_general(bbar2, cl_all, (((1,), (1,)), ((), ())),
                               precision=lax.Precision.HIGHEST,
                               preferred_element_type=F32)
        kpad = jnp.concatenate([jnp.zeros_like(krow), krow], axis=1)
        t0_rows = [kpad[:, width - gw * step: 2 * width - gw * step] for step in range(SSM_CHUNK)]
        t0_ref[gi] = jnp.concatenate(t0_rows, axis=0).astype(t0_ref.dtype)

        mu_r, mu_i = pow_r[SSM_CHUNK], pow_i[SSM_CHUNK]
        sr_rows, si_rows = [], []
        for _ in range(scan_steps):
            sr_rows.append(mu_r)
            si_rows.append(mu_i * minus_plus)
            mu_r, mu_i = mu_r * mu_r - mu_i * mu_i, 2.0 * mu_r * mu_i
        pad = [jnp.zeros_like(mu_r)] * (sr_ref.shape[1] - scan_steps)
        sr_ref[gi] = jnp.concatenate(sr_rows + pad, axis=0)
        si_ref[gi] = jnp.concatenate(si_rows + pad, axis=0)


def ssm_prep(a_re2, a_im2, log_dt2, b2, c2, *, scan_steps, groups_per_step=8, name="ssm_prep"):
    ng = a_re2.shape[0]
    width = SSM_CHUNK * S5_GROUP
    ns = 2 * S5_STATE
    rows = -(-scan_steps // V7X_SUBLANES) * V7X_SUBLANES
    gb = groups_per_step
    vec = pl.BlockSpec((gb, 1, ns), lambda i: (i, 0, 0))
    mat = pl.BlockSpec((gb, S5_GROUP, ns), lambda i: (i, 0, 0))
    return pl.pallas_call(
        functools.partial(_ssm_prep_kernel, groups=gb, scan_steps=scan_steps),
        grid=(ng // gb,),
        in_specs=[vec, vec, vec, mat, mat],
        out_specs=[
            pl.BlockSpec((gb, width, width), lambda i: (i, 0, 0)),
            pl.BlockSpec((gb, width, ns), lambda i: (i, 0, 0)),
            pl.BlockSpec((gb, ns, width), lambda i: (i, 0, 0)),
            pl.BlockSpec((gb, rows, ns), lambda i: (i, 0, 0)),
            pl.BlockSpec((gb, rows, ns), lambda i: (i, 0, 0)),
        ],
        out_shape=[
            jax.ShapeDtypeStruct((ng, width, width), BF16),
            jax.ShapeDtypeStruct((ng, width, ns), BF16),
            jax.ShapeDtypeStruct((ng, ns, width), BF16),
            jax.ShapeDtypeStruct((ng, rows, ns), F32),
            jax.ShapeDtypeStruct((ng, rows, ns), F32),
        ],
        compiler_params=_params("parallel"),
        name=name,
    )(a_re2, a_im2, log_dt2, b2, c2)


def _granule_transpose(vs, lane):
    vs = list(vs)
    n = len(vs)
    d = n // 2
    while d >= 1:
        keep = ((lane // S5_GROUP) & d) == 0
        nxt = list(vs)
        for i in range(n):
            if i & d:
                continue
            p = i + d
            nxt[i] = jnp.where(keep, vs[i], pltpu.roll(vs[p], d * S5_GROUP, axis=1))
            nxt[p] = jnp.where(keep, pltpu.roll(vs[i], V7X_LANES - d * S5_GROUP, axis=1), vs[p])
        vs = nxt
        d //= 2
    return vs


def _ssm_kernel(u_ref, t0_ref, g_ref, h_ref, sr_ref, si_ref, d5_ref, y_ref, u5_ref, y5_ref,
                *, chunks_per_seq, scan_steps):
    t = u_ref.shape[0]
    nchunk = t // SSM_CHUNK
    oct_rows = V7X_SUBLANES * SSM_CHUNK
    lane8 = lax.broadcasted_iota(jnp.int32, (V7X_SUBLANES, V7X_LANES), 1)
    halves = SSM_CHUNK // V7X_SUBLANES

    def relayout_in(it, carry):
        base = pl.multiple_of(it * 2 * oct_rows, 2 * oct_rows)
        per_oct = []
        for o in range(2):
            rows = [u_ref[pl.ds(base + o * oct_rows + step, V7X_SUBLANES, stride=SSM_CHUNK), :]
                    for step in range(SSM_CHUNK)]
            per_oct.append([_granule_transpose(rows[hf * 8:(hf + 1) * 8], lane8) for hf in range(halves)])
        c0 = pl.multiple_of(it * 2 * V7X_SUBLANES, 2 * V7X_SUBLANES)
        for g in range(GROUPS_PER_SLAB):
            for hf in range(halves):
                blk = jnp.concatenate([per_oct[0][hf][g], per_oct[1][hf][g]], axis=0)
                u5_ref[g, pl.ds(c0, 2 * V7X_SUBLANES), hf * V7X_LANES:(hf + 1) * V7X_LANES] = blk.astype(BF16)
        return carry

    lax.fori_loop(0, nchunk // (2 * V7X_SUBLANES), relayout_in, 0)

    row = lax.broadcasted_iota(jnp.int32, (nchunk, 2 * S5_STATE), 0) % chunks_per_seq

    def per_group(g, carry):
        u5 = u5_ref[g]
        hs = jnp.dot(u5, g_ref[g], preferred_element_type=F32)
        for k in range(scan_steps):
            d = 1 << k
            sh = jnp.where(row >= d, pltpu.roll(hs, d, axis=0), 0.0)
            hs = hs + sr_ref[g, k:k + 1, :] * sh + si_ref[g, k:k + 1, :] * pltpu.roll(sh, S5_STATE, axis=1)
        hprev = jnp.where(row >= 1, pltpu.roll(hs, 1, axis=0), 0.0).astype(BF16)
        y = (jnp.dot(u5, t0_ref[g], preferred_element_type=F32)
             + jnp.dot(hprev, h_ref[g], preferred_element_type=F32)
             + d5_ref[g] * u5.astype(F32))
        y5_ref[g] = jax.nn.gelu(y, approximate=True)
        return carry

    lax.fori_loop(0, GROUPS_PER_SLAB, per_group, 0)

    def relayout_out(it, carry):
        c0 = pl.multiple_of(it * V7X_SUBLANES, V7X_SUBLANES)
        base = pl.multiple_of(it * oct_rows, oct_rows)
        for hf in range(halves):
            vs = [y5_ref[g, pl.ds(c0, V7X_SUBLANES), hf * V7X_LANES:(hf + 1) * V7X_LANES]
                  for g in range(GROUPS_PER_SLAB)]
            out = _granule_transpose(vs, lane8)
            for s in range(V7X_SUBLANES):
                step = hf * V7X_SUBLANES + s
                y_ref[pl.ds(base + step, V7X_SUBLANES, stride=SSM_CHUNK), :] = out[s]
        return carry

    lax.fori_loop(0, nchunk // V7X_SUBLANES, relayout_out, 0)


def ssm_apply(u, t0, gm, hm, sr, si, d5, *, seq, name="s5_ssm"):
    t, w = u.shape
    nslab = w // V7X_LANES
    nchunk = t // SSM_CHUNK
    width = SSM_CHUNK * S5_GROUP
    ns = 2 * S5_STATE
    chunks_per_seq = seq // SSM_CHUNK
    scan_steps = (chunks_per_seq - 1).bit_length()
    gb = GROUPS_PER_SLAB
    rows = sr.shape[1]
    return pl.pallas_call(
        functools.partial(_ssm_kernel, chunks_per_seq=chunks_per_seq, scan_steps=scan_steps),
        grid=(nslab,),
        in_specs=[
            pl.BlockSpec((t, V7X_LANES), lambda s: (0, s)),
            pl.BlockSpec((gb, width, width), lambda s: (s, 0, 0)),
            pl.BlockSpec((gb, width, ns), lambda s: (s, 0, 0)),
            pl.BlockSpec((gb, ns, width), lambda s: (s, 0, 0)),
            pl.BlockSpec((gb, rows, ns), lambda s: (s, 0, 0)),
            pl.BlockSpec((gb, rows, ns), lambda s: (s, 0, 0)),
            pl.BlockSpec((gb, 1, width), lambda s: (s, 0, 0)),
        ],
        out_specs=pl.BlockSpec((t, V7X_LANES), lambda s: (0, s)),
        out_shape=jax.ShapeDtypeStruct((t, w), F32),
        scratch_shapes=[pltpu.VMEM((gb, nchunk, width), BF16), pltpu.VMEM((gb, nchunk, width), F32)],
        compiler_params=_params("parallel"),
        name=name,
    )(u, t0, gm, hm, sr, si, d5)


def _fgate_kernel(x_ref, gain_ref, wf_ref, bf_ref, cum_ref, cumt_ref, carry_ref, *, tiles_per_seq, heads):
    i = pl.program_id(0)

    @pl.when(i % tiles_per_seq == 0)
    def _():
        carry_ref[...] = jnp.zeros_like(carry_ref)

    xn = _rms_scale(x_ref[...], gain_ref[...]).astype(BF16)
    logit = jnp.dot(xn, wf_ref[...], preferred_element_type=F32) + bf_ref[...]
    c = -(jnp.maximum(-logit, 0.0) + jnp.log1p(jnp.exp(-jnp.abs(logit))))
    tm = c.shape[0]
    row = lax.broadcasted_iota(jnp.int32, c.shape, 0)
    d = 1
    while d < tm:
        c = c + jnp.where(row >= d, pltpu.roll(c, d, axis=0), 0.0)
        d *= 2
    c = c + carry_ref[...]
    carry_ref[...] = c[tm - 1:tm, :]
    cum_ref[...] = c
    cumt_ref[...] = c.T[:heads, :]


def forget_gate(x, gain, wf, bf, *, seq, heads, tm=512, name="fox_forget_gate"):
    t, d = x.shape
    tiles_per_seq = seq // tm
    return pl.pallas_call(
        functools.partial(_fgate_kernel, tiles_per_seq=tiles_per_seq, heads=heads),
        grid=(t // tm,),
        in_specs=[
            pl.BlockSpec((tm, d), lambda i: (i, 0)),
            pl.BlockSpec((1, d), lambda i: (0, 0)),
            pl.BlockSpec((d, V7X_LANES), lambda i: (0, 0)),
            pl.BlockSpec((1, V7X_LANES), lambda i: (0, 0)),
        ],
        out_specs=[
            pl.BlockSpec((tm, V7X_LANES), lambda i: (i, 0)),
            pl.BlockSpec((None, heads, tm), lambda i: (i // tiles_per_seq, 0, i % tiles_per_seq)),
        ],
        out_shape=[
            jax.ShapeDtypeStruct((t, V7X_LANES), F32),
            jax.ShapeDtypeStruct((t // seq, heads, seq), F32),
        ],
        scratch_shapes=[pltpu.VMEM((1, V7X_LANES), F32)],
        compiler_params=_params("arbitrary"),
        name=name,
    )(x, gain, wf, bf)


def _fox_attn_kernel(q_ref, k_ref, v_ref, cn_ref, ct_ref, sz_ref, o_ref, *, tq, tk):
    seq, dh = q_ref.shape
    head = pl.program_id(1)
    sub = tq // tk
    lane = lax.broadcasted_iota(jnp.int32, (tq, V7X_LANES), 1)
    rel = (lax.broadcasted_iota(jnp.int32, (tq, tk), 0) - lax.broadcasted_iota(jnp.int32, (tq, tk), 1))

    def q_tile(qi, carry):
        q0 = pl.multiple_of(qi * tq, tq)
        qb = q_ref[pl.ds(q0, tq), :]
        c_q = jnp.sum(jnp.where(lane == head, cn_ref[pl.ds(q0, tq), :], 0.0), axis=-1, keepdims=True)

        def step(k0, state, diag_offset):
            m, l, acc = state
            kb = k_ref[pl.ds(k0, tk), :]
            vb = v_ref[pl.ds(k0, tk), :]
            s = lax.dot_general(qb, kb, (((1,), (1,)), ((), ())), preferred_element_type=F32)
            s = s + c_q - ct_ref[:, pl.ds(k0, tk)]
            if diag_offset is not None:
                s = jnp.where(rel >= diag_offset, s, NEG_INF)
            m_new = jnp.maximum(m, jnp.max(s, axis=-1, keepdims=True))
            alpha = jnp.exp(m - m_new)
            p = jnp.exp(s - m_new)
            l = alpha * l + jnp.sum(p, axis=-1, keepdims=True)
            acc = alpha * acc + jnp.dot(p.astype(BF16), vb, preferred_element_type=F32)
            return m_new, l, acc

        state = (jnp.full((tq, 1), NEG_INF, F32), jnp.zeros((tq, 1), F32), jnp.zeros((tq, dh), F32))
        state = lax.fori_loop(0, qi * sub,
                              lambda ki, st: step(pl.multiple_of(ki * tk, tk), st, None), state)
        for dsub in range(sub):
            state = step(pl.multiple_of(q0 + dsub * tk, tk), state, dsub * tk)
        _, l, acc = state
        o = acc / l
        o_ref[pl.ds(q0, tq), :] = (o * sz_ref[pl.ds(q0, tq), :].astype(F32)).astype(o_ref.dtype)
        return carry

    lax.fori_loop(0, seq // tq, q_tile, 0)


def fox_attention(q, k, v, cum, cumt, sz, *, batch, seq, heads, tq=512, tk=512, name="fox_attention"):
    t, width = q.shape
    dh = width // heads
    return pl.pallas_call(
        functools.partial(_fox_attn_kernel, tq=tq, tk=tk),
        grid=(batch, heads),
        in_specs=[
            pl.BlockSpec((seq, dh), lambda b, h: (b, h)),
            pl.BlockSpec((seq, dh), lambda b, h: (b, h)),
            pl.BlockSpec((seq, dh), lambda b, h: (b, h)),
            pl.BlockSpec((seq, V7X_LANES), lambda b, h: (b, 0)),
            pl.BlockSpec((None, None, 1, seq), lambda b, h: (b, h, 0, 0)),
            pl.BlockSpec((seq, dh), lambda b, h: (b, h)),
        ],
        out_specs=pl.BlockSpec((seq, dh), lambda b, h: (b, h)),
        out_shape=jax.ShapeDtypeStruct((t, width), BF16),
        compiler_params=_params("parallel", "arbitrary"),
        name=name,
    )(q, k, v, cum, cumt, sz)


def _identity(v):
    return v


def _doubled(v):
    return jnp.concatenate([v, v], axis=-1)[:, None, :].astype(F32)


def kernel(x, norm_pre, norm_post, s5_w_in, s5_a_re, s5_a_im, s5_log_dt, s5_b_re, s5_b_im, s5_c_re, s5_c_im,
           s5_d, s5_w_glu, s5_b_glu, s5_w_out, kv_norm, kv_w, kv_b_f, fox_w_in, fox_w_out):
    batch, seq, d_model = x.shape
    t = batch * seq
    heads = kv_b_f.shape[0]
    fox_width = fox_w_in.shape[-1] // 2
    assert norm_pre.shape[0] == 2 and s5_w_in.shape[0] == 1 and fox_w_in.shape[0] == 1
    assert fox_width // heads == FOX_HEAD_DIM and heads <= V7X_LANES
    ngroups = s5_a_re.shape[1]
    chunks_per_seq = seq // SSM_CHUNK
    scan_steps = (chunks_per_seq - 1).bit_length()

    h0 = x.reshape(t, d_model)

    u, sz = norm_proj(h0, norm_pre[0][None, :], s5_w_in[0].astype(BF16), _identity, jax.nn.silu, F32, BF16,
                      name="s5_in_proj")
    ldt2 = jnp.broadcast_to(s5_log_dt[0][:, None, None], (ngroups, 1, 2 * S5_STATE)).astype(F32)
    b2 = jnp.concatenate([jnp.swapaxes(s5_b_re[0], 1, 2), jnp.swapaxes(s5_b_im[0], 1, 2)], axis=-1)
    c2 = jnp.concatenate([s5_c_re[0], s5_c_im[0]], axis=-1)
    t0, gm, hm, sr, si = ssm_prep(_doubled(s5_a_re[0]), _doubled(s5_a_im[0]), ldt2, b2.astype(F32),
                                  c2.astype(F32), scan_steps=scan_steps)
    d5 = jnp.tile(s5_d[0].reshape(ngroups, 1, S5_GROUP), (1, 1, SSM_CHUNK)).astype(F32)
    yg = ssm_apply(u, t0, gm, hm, sr, si, d5, seq=seq)
    y3 = glu_gate(yg, s5_w_glu[0].astype(BF16), s5_b_glu[0][None, :], sz)
    h1 = out_proj(y3, s5_w_out[0].astype(BF16), h0, norm_post[0][None, :], name="s5_out_proj")

    kv_cols = 2 * fox_width
    k, v = norm_proj(h1, kv_norm[None, :], kv_w[:, :kv_cols].astype(BF16), _identity, _identity, BF16, BF16,
                     name="fox_kv_proj")
    wf = jnp.pad(kv_w[:, kv_cols:], ((0, 0), (0, V7X_LANES - heads))).astype(BF16)
    bf = jnp.pad(kv_b_f, (0, V7X_LANES - heads))[None, :].astype(F32)
    cum, cumt = forget_gate(h1, kv_norm[None, :], wf, bf, seq=seq, heads=heads)
    scale = FOX_HEAD_DIM ** -0.5
    q, sz2 = norm_proj(h1, norm_pre[1][None, :], fox_w_in[0].astype(BF16), lambda v: v * scale, jax.nn.silu,
                       BF16, BF16, name="fox_in_proj")
    o = fox_attention(q, k, v, cum, cumt.reshape(batch, heads, 1, seq), sz2, batch=batch, seq=seq, heads=heads)
    h2 = out_proj(o, fox_w_out[0].astype(BF16), h1, norm_post[1][None, :], name="fox_out_proj")
    return h2.reshape(batch, seq, d_model)
```

```python
import functools
import math

import jax
import jax.numpy as jnp
from jax import lax
from jax.experimental import pallas as pl
from jax.experimental.pallas import tpu as pltpu

F32 = jnp.float32
BF16 = jnp.bfloat16

RMS_EPS = 1e-6
NEG_INF = -1e30
LOG2_E = math.log2(math.e)

S5_GROUP = 16
S5_STATE = 64
FOX_HEAD_DIM = 128

V7X_LANES = 128
V7X_SUBLANES = 8
SSM_CHUNK = 16
GROUPS_PER_SLAB = V7X_LANES // S5_GROUP
VMEM_LIMIT = 48 * 1024 * 1024


def _params(*semantics):
    return pltpu.CompilerParams(dimension_semantics=semantics, vmem_limit_bytes=VMEM_LIMIT)


def _rms_scale(x, gain):
    ms = jnp.mean(x * x, axis=-1, keepdims=True)
    return x * lax.rsqrt(ms + RMS_EPS) * gain


def _norm_proj_kernel(x_ref, gain_ref, w1_ref, w2_ref, o1_ref, o2_ref, xn_ref, *, act1, act2):
    @pl.when(pl.program_id(1) == 0)
    def _():
        xn_ref[...] = _rms_scale(x_ref[...], gain_ref[...]).astype(BF16)

    xn = xn_ref[...]
    o1_ref[...] = act1(jnp.dot(xn, w1_ref[...], preferred_element_type=F32)).astype(o1_ref.dtype)
    o2_ref[...] = act2(jnp.dot(xn, w2_ref[...], preferred_element_type=F32)).astype(o2_ref.dtype)


def norm_proj(x, gain, w, act1, act2, dtype1, dtype2, *, tm=1024, tn=512, name="norm_proj"):
    t, d = x.shape
    n = w.shape[1] // 2
    nj = n // tn
    return pl.pallas_call(
        functools.partial(_norm_proj_kernel, act1=act1, act2=act2),
        grid=(t // tm, nj),
        in_specs=[
            pl.BlockSpec((tm, d), lambda i, j: (i, 0)),
            pl.BlockSpec((1, d), lambda i, j: (0, 0)),
            pl.BlockSpec((d, tn), lambda i, j: (0, j)),
            pl.BlockSpec((d, tn), lambda i, j: (0, j + nj)),
        ],
        out_specs=[
            pl.BlockSpec((tm, tn), lambda i, j: (i, j)),
            pl.BlockSpec((tm, tn), lambda i, j: (i, j)),
        ],
        out_shape=[jax.ShapeDtypeStruct((t, n), dtype1), jax.ShapeDtypeStruct((t, n), dtype2)],
        scratch_shapes=[pltpu.VMEM((tm, d), BF16)],
        compiler_params=_params("parallel", "arbitrary"),
        name=name,
    )(x, gain, w, w)


def _out_proj_kernel(a_ref, w_ref, res_ref, gain_ref, o_ref):
    y = jnp.dot(a_ref[...], w_ref[...], preferred_element_type=F32)
    o_ref[...] = res_ref[...] + _rms_scale(y, gain_ref[...])


def out_proj(a, w, res, gain, *, tm=512, name="out_proj"):
    t, k = a.shape
    d = w.shape[1]
    return pl.pallas_call(
        _out_proj_kernel,
        grid=(t // tm,),
        in_specs=[
            pl.BlockSpec((tm, k), lambda i: (i, 0)),
            pl.BlockSpec((k, d), lambda i: (0, 0)),
            pl.BlockSpec((tm, d), lambda i: (i, 0)),
            pl.BlockSpec((1, d), lambda i: (0, 0)),
        ],
        out_specs=pl.BlockSpec((tm, d), lambda i: (i, 0)),
        out_shape=jax.ShapeDtypeStruct((t, d), F32),
        compiler_params=_params("parallel"),
        name=name,
    )(a, w, res, gain)


def _glu_kernel(yg_ref, w_ref, b_ref, sz_ref, o_ref, yb_ref, *, tn):
    j = pl.program_id(1)

    @pl.when(j == 0)
    def _():
        yb_ref[...] = yg_ref[...].astype(BF16)

    gate = jax.nn.sigmoid(jnp.dot(yb_ref[...], w_ref[...], preferred_element_type=F32) + b_ref[...])
    yg = yg_ref[:, pl.ds(pl.multiple_of(j * tn, tn), tn)]
    o_ref[...] = (yg * gate * sz_ref[...].astype(F32)).astype(o_ref.dtype)


def glu_gate(yg, w, b, sz, *, tm=1024, tn=512, name="s5_glu"):
    t, k = yg.shape
    n = w.shape[1]
    return pl.pallas_call(
        functools.partial(_glu_kernel, tn=tn),
        grid=(t // tm, n // tn),
        in_specs=[
            pl.BlockSpec((tm, k), lambda i, j: (i, 0)),
            pl.BlockSpec((k, tn), lambda i, j: (0, j)),
            pl.BlockSpec((1, tn), lambda i, j: (0, j)),
            pl.BlockSpec((tm, tn), lambda i, j: (i, j)),
        ],
        out_specs=pl.BlockSpec((tm, tn), lambda i, j: (i, j)),
        out_shape=jax.ShapeDtypeStruct((t, n), BF16),
        scratch_shapes=[pltpu.VMEM((tm, k), BF16)],
        compiler_params=_params("parallel", "arbitrary"),
        name=name,
    )(yg, w, b, sz)


def _ssm_prep_kernel(are_ref, aim_ref, ldt_ref, b2_ref, c2_ref,
                     t0_ref, g_ref, h_ref, sr_ref, si_ref, *, groups, scan_steps):
    lane = lax.broadcasted_iota(jnp.int32, (1, 2 * S5_STATE), 1)
    minus_plus = jnp.where(lane < S5_STATE, -1.0, 1.0).astype(F32)
    gw = S5_GROUP
    width = SSM_CHUNK * gw

    for gi in range(groups):
        ar = are_ref[gi]
        ai = aim_ref[gi]
        dt = jnp.exp(ldt_ref[gi])
        mag = jnp.exp(ar * dt)
        lam_r = mag * jnp.cos(ai * dt)
        lam_i = mag * jnp.sin(ai * dt)
        den = ar * ar + ai * ai
        nr = lam_r - 1.0
        coef_r = (nr * ar + lam_i * ai) / den
        coef_i = (lam_i * ar - nr * ai) / den

        pow_r = [jnp.ones_like(lam_r)]
        pow_i = [jnp.zeros_like(lam_r)]
        for _ in range(SSM_CHUNK):
            pr, pi = pow_r[-1], pow_i[-1]
            pow_r.append(pr * lam_r - pi * lam_i)
            pow_i.append(pr * lam_i + pi * lam_r)

        b2 = b2_ref[gi]
        b2s = pltpu.roll(b2, S5_STATE, axis=1) * minus_plus
        c2 = c2_ref[gi]
        c2a = c2 * (-minus_plus)
        c2b = -pltpu.roll(c2, S5_STATE, axis=1)

        g_rows = []
        for step in range(SSM_CHUNK):
            pr, pi = pow_r[SSM_CHUNK - 1 - step], pow_i[SSM_CHUNK - 1 - step]
            wr = pr * coef_r - pi * coef_i
            wi = pr * coef_i + pi * coef_r
            g_rows.append(wr * b2 + wi * b2s)
        g_ref[gi] = jnp.concatenate(g_rows, axis=0).astype(g_ref.dtype)

        cl = [pow_r[tau] * c2a + pow_i[tau] * c2b for tau in range(SSM_CHUNK + 1)]
        h_t = jnp.concatenate(cl[1:], axis=0)
        h_ref[gi] = h_t.T.astype(h_ref.dtype)

        bbar2 = coef_r * b2 + coef_i * b2s
        cl_all = jnp.concatenate(cl[:-1], axis=0)
        krow = lax.dot_general(bbar2, cl_all, (((1,), (1,)), ((), ())),
                               precision=lax.Precision.HIGHEST,
                               preferred_element_type=F32)
        kpad = jnp.concatenate([jnp.zeros_like(krow), krow], axis=1)
        t0_rows = [kpad[:, width - gw * step: 2 * width - gw * step] for step in range(SSM_CHUNK)]
        t0_ref[gi] = jnp.concatenate(t0_rows, axis=0).astype(t0_ref.dtype)

        mu_r, mu_i = pow_r[SSM_CHUNK], pow_i[SSM_CHUNK]
        sr_rows, si_rows = [], []
        for _ in range(scan_steps):
            sr_rows.append(mu_r)
            si_rows.append(mu_i * minus_plus)
            mu_r, mu_i = mu_r * mu_r - mu_i * mu_i, 2.0 * mu_r * mu_i
        pad = [jnp.zeros_like(mu_r)] * (sr_ref.shape[1] - scan_steps)
        sr_ref[gi] = jnp.concatenate(sr_rows + pad, axis=0)
        si_ref[gi] = jnp.concatenate(si_rows + pad, axis=0)


def ssm_prep(a_re2, a_im2, log_dt2, b2, c2, *, scan_steps, groups_per_step=8, name="ssm_prep"):
    ng = a_re2.shape[0]
    width = SSM_CHUNK * S5_GROUP
    ns = 2 * S5_STATE
    rows = -(-scan_steps // V7X_SUBLANES) * V7X_SUBLANES
    gb = groups_per_step
    vec = pl.BlockSpec((gb, 1, ns), lambda i: (i, 0, 0))
    mat = pl.BlockSpec((gb, S5_GROUP, ns), lambda i: (i, 0, 0))
    return pl.pallas_call(
        functools.partial(_ssm_prep_kernel, groups=gb, scan_steps=scan_steps),
        grid=(ng // gb,),
        in_specs=[vec, vec, vec, mat, mat],
        out_specs=[
            pl.BlockSpec((gb, width, width), lambda i: (i, 0, 0)),
            pl.BlockSpec((gb, width, ns), lambda i: (i, 0, 0)),
            pl.BlockSpec((gb, ns, width), lambda i: (i, 0, 0)),
            pl.BlockSpec((gb, rows, ns), lambda i: (i, 0, 0)),
            pl.BlockSpec((gb, rows, ns), lambda i: (i, 0, 0)),
        ],
        out_shape=[
            jax.ShapeDtypeStruct((ng, width, width), BF16),
            jax.ShapeDtypeStruct((ng, width, ns), BF16),
            jax.ShapeDtypeStruct((ng, ns, width), BF16),
            jax.ShapeDtypeStruct((ng, rows, ns), F32),
            jax.ShapeDtypeStruct((ng, rows, ns), F32),
        ],
        compiler_params=_params("parallel"),
        name=name,
    )(a_re2, a_im2, log_dt2, b2, c2)


def _granule_transpose(vs, lane):
    vs = list(vs)
    n = len(vs)
    d = n // 2
    while d >= 1:
        keep = ((lane // S5_GROUP) & d) == 0
        nxt = list(vs)
        for i in range(n):
            if i & d:
                continue
            p = i + d
            nxt[i] = jnp.where(keep, vs[i], pltpu.roll(vs[p], d * S5_GROUP, axis=1))
            nxt[p] = jnp.where(keep, pltpu.roll(vs[i], V7X_LANES - d * S5_GROUP, axis=1), vs[p])
        vs = nxt
        d //= 2
    return vs


def _ssm_kernel(u_ref, t0_ref, g_ref, h_ref, sr_ref, si_ref, d5_ref, y_ref, u5_ref, y5_ref,
                *, chunks_per_seq, scan_steps):
    t = u_ref.shape[0]
    nchunk = t // SSM_CHUNK
    oct_rows = V7X_SUBLANES * SSM_CHUNK
    lane8 = lax.broadcasted_iota(jnp.int32, (V7X_SUBLANES, V7X_LANES), 1)
    halves = SSM_CHUNK // V7X_SUBLANES

    def relayout_in(it, carry):
        base = pl.multiple_of(it * 2 * oct_rows, 2 * oct_rows)
        per_oct = []
        for o in range(2):
            rows = [u_ref[pl.ds(base + o * oct_rows + step, V7X_SUBLANES, stride=SSM_CHUNK), :]
                    for step in range(SSM_CHUNK)]
            per_oct.append([_granule_transpose(rows[hf * 8:(hf + 1) * 8], lane8) for hf in range(halves)])
        c0 = pl.multiple_of(it * 2 * V7X_SUBLANES, 2 * V7X_SUBLANES)
        for g in range(GROUPS_PER_SLAB):
            for hf in range(halves):
                blk = jnp.concatenate([per_oct[0][hf][g], per_oct[1][hf][g]], axis=0)
                u5_ref[g, pl.ds(c0, 2 * V7X_SUBLANES), hf * V7X_LANES:(hf + 1) * V7X_LANES] = blk.astype(BF16)
        return carry

    lax.fori_loop(0, nchunk // (2 * V7X_SUBLANES), relayout_in, 0, unroll=4)

    row = lax.broadcasted_iota(jnp.int32, (nchunk, 2 * S5_STATE), 0) % chunks_per_seq

    def per_group(g, carry):
        u5 = u5_ref[g]
        hs = jnp.dot(u5, g_ref[g], preferred_element_type=F32)
        for k in range(scan_steps):
            d = 1 << k
            sh = jnp.where(row >= d, pltpu.roll(hs, d, axis=0), 0.0)
            hs = hs + sr_ref[g, k:k + 1, :] * sh + si_ref[g, k:k + 1, :] * pltpu.roll(sh, S5_STATE, axis=1)
        hprev = jnp.where(row >= 1, pltpu.roll(hs, 1, axis=0), 0.0).astype(BF16)
        y = (jnp.dot(u5, t0_ref[g], preferred_element_type=F32)
             + jnp.dot(hprev, h_ref[g], preferred_element_type=F32)
             + d5_ref[g] * u5.astype(F32))
        y5_ref[g] = jax.nn.gelu(y, approximate=True)
        return carry

    lax.fori_loop(0, GROUPS_PER_SLAB, per_group, 0, unroll=2)

    def relayout_out(it, carry):
        c0 = pl.multiple_of(it * V7X_SUBLANES, V7X_SUBLANES)
        base = pl.multiple_of(it * oct_rows, oct_rows)
        for hf in range(halves):
            vs = [y5_ref[g, pl.ds(c0, V7X_SUBLANES), hf * V7X_LANES:(hf + 1) * V7X_LANES]
                  for g in range(GROUPS_PER_SLAB)]
            out = _granule_transpose(vs, lane8)
            for s in range(V7X_SUBLANES):
                step = hf * V7X_SUBLANES + s
                y_ref[pl.ds(base + step, V7X_SUBLANES, stride=SSM_CHUNK), :] = out[s]
        return carry

    lax.fori_loop(0, nchunk // V7X_SUBLANES, relayout_out, 0, unroll=8)


def ssm_apply(u, t0, gm, hm, sr, si, d5, *, seq, name="s5_ssm"):
    t, w = u.shape
    nslab = w // V7X_LANES
    nchunk = t // SSM_CHUNK
    width = SSM_CHUNK * S5_GROUP
    ns = 2 * S5_STATE
    chunks_per_seq = seq // SSM_CHUNK
    scan_steps = (chunks_per_seq - 1).bit_length()
    gb = GROUPS_PER_SLAB
    rows = sr.shape[1]
    return pl.pallas_call(
        functools.partial(_ssm_kernel, chunks_per_seq=chunks_per_seq, scan_steps=scan_steps),
        grid=(nslab,),
        in_specs=[
            pl.BlockSpec((t, V7X_LANES), lambda s: (0, s)),
            pl.BlockSpec((gb, width, width), lambda s: (s, 0, 0)),
            pl.BlockSpec((gb, width, ns), lambda s: (s, 0, 0)),
            pl.BlockSpec((gb, ns, width), lambda s: (s, 0, 0)),
            pl.BlockSpec((gb, rows, ns), lambda s: (s, 0, 0)),
            pl.BlockSpec((gb, rows, ns), lambda s: (s, 0, 0)),
            pl.BlockSpec((gb, 1, width), lambda s: (s, 0, 0)),
        ],
        out_specs=pl.BlockSpec((t, V7X_LANES), lambda s: (0, s)),
        out_shape=jax.ShapeDtypeStruct((t, w), F32),
        scratch_shapes=[pltpu.VMEM((gb, nchunk, width), BF16), pltpu.VMEM((gb, nchunk, width), F32)],
        compiler_params=_params("parallel"),
        name=name,
    )(u, t0, gm, hm, sr, si, d5)


def _fgate_kernel(x_ref, gain_ref, wf_ref, bf_ref, cum_ref, carry_ref, *, tiles_per_seq):
    i = pl.program_id(0)

    @pl.when(i % tiles_per_seq == 0)
    def _():
        carry_ref[...] = jnp.zeros_like(carry_ref)

    xn = _rms_scale(x_ref[...], gain_ref[...]).astype(BF16)
    logit = jnp.dot(xn, wf_ref[...], preferred_element_type=F32) + bf_ref[...]
    c = -(jnp.maximum(-logit, 0.0) + jnp.log1p(jnp.exp(-jnp.abs(logit))))
    tm = c.shape[0]
    row = lax.broadcasted_iota(jnp.int32, c.shape, 0)
    d = 1
    while d < tm:
        c = c + jnp.where(row >= d, pltpu.roll(c, d, axis=0), 0.0)
        d *= 2
    c = c + carry_ref[...]
    carry_ref[...] = c[tm - 1:tm, :]
    cum_ref[...] = c * LOG2_E


def forget_gate(x, gain, wf, bf, *, seq, tm=512, name="fox_forget_gate"):
    t, d = x.shape
    tiles_per_seq = seq // tm
    return pl.pallas_call(
        functools.partial(_fgate_kernel, tiles_per_seq=tiles_per_seq),
        grid=(t // tm,),
        in_specs=[
            pl.BlockSpec((tm, d), lambda i: (i, 0)),
            pl.BlockSpec((1, d), lambda i: (0, 0)),
            pl.BlockSpec((d, V7X_LANES), lambda i: (0, 0)),
            pl.BlockSpec((1, V7X_LANES), lambda i: (0, 0)),
        ],
        out_specs=pl.BlockSpec((tm, V7X_LANES), lambda i: (i, 0)),
        out_shape=jax.ShapeDtypeStruct((t, V7X_LANES), F32),
        scratch_shapes=[pltpu.VMEM((1, V7X_LANES), F32)],
        compiler_params=_params("arbitrary"),
        name=name,
    )(x, gain, wf, bf)


def _split3(c):
    hi = c.astype(BF16).astype(F32)
    mid = (c - hi).astype(BF16).astype(F32)
    return hi, mid, c - hi - mid


def _fox_attn_kernel(q_ref, k_ref, v_ref, cn_ref, sz_ref, o_ref, qa_ref, ka_ref, *, tq, tk):
    seq, dh = q_ref.shape
    head = pl.program_id(1)
    nsplit = 3

    def augment(r, carry):
        r0 = pl.multiple_of(r * tq, tq)
        lane = lax.broadcasted_iota(jnp.int32, (tq, V7X_LANES), 1)
        c = jnp.sum(jnp.where(lane == head, cn_ref[pl.ds(r0, tq), :], 0.0), axis=-1, keepdims=True)
        parts = _split3(c)
        ext_q = jnp.where(lane < 2 * nsplit, 1.0, 0.0)
        ext_k = jnp.where(lane < nsplit, 1.0, 0.0)
        for n, part in enumerate(parts):
            ext_q = jnp.where(lane == n, part, ext_q)
            ext_k = jnp.where(lane == nsplit + n, -part, ext_k)
        qa_ref[pl.ds(r0, tq), :dh] = q_ref[pl.ds(r0, tq), :]
        qa_ref[pl.ds(r0, tq), dh:] = ext_q.astype(BF16)
        ka_ref[pl.ds(r0, tq), :dh] = k_ref[pl.ds(r0, tq), :]
        ka_ref[pl.ds(r0, tq), dh:] = ext_k.astype(BF16)
        return carry

    lax.fori_loop(0, seq // tq, augment, 0)

    rel = (lax.broadcasted_iota(jnp.int32, (tq, tk), 0) - lax.broadcasted_iota(jnp.int32, (tq, tk), 1))

    def q_tile(qi, carry):
        q0 = pl.multiple_of(qi * tq, tq)
        qb = qa_ref[pl.ds(q0, tq), :]

        def scores(k0):
            return lax.dot_general(qb, ka_ref[pl.ds(k0, tk), :], (((1,), (1,)), ((), ())),
                                   preferred_element_type=F32)

        def update(s, k0, state):
            m, l, acc = state
            m_new = jnp.maximum(m, jnp.max(s, axis=-1, keepdims=True))
            alpha = jnp.exp2(m - m_new)
            p = jnp.exp2(s - m_new)
            l = alpha * l + jnp.sum(p, axis=-1, keepdims=True)
            acc = alpha * acc + jnp.dot(p.astype(BF16), v_ref[pl.ds(k0, tk), :], preferred_element_type=F32)
            return m_new, l, acc

        def body(ki, st):
            s, state = st
            k0 = pl.multiple_of(ki * tk, tk)
            s_next = scores(k0 + tk)
            return s_next, update(s, k0, state)

        state = (jnp.full((tq, 1), NEG_INF, F32), jnp.zeros((tq, 1), F32), jnp.zeros((tq, dh), F32))
        s, state = lax.fori_loop(0, qi * (tq // tk), body, (scores(0), state))
        for dsub in range(tq // tk):
            k0 = pl.multiple_of(q0 + dsub * tk, tk)
            s = jnp.where(rel >= dsub * tk, s, NEG_INF)
            s_next = scores(k0 + tk) if dsub + 1 < tq // tk else None
            state = update(s, k0, state)
            s = s_next
        _, l, acc = state
        o = acc / l
        o_ref[pl.ds(q0, tq), :] = (o * sz_ref[pl.ds(q0, tq), :].astype(F32)).astype(o_ref.dtype)
        return carry

    lax.fori_loop(0, seq // tq, q_tile, 0)


def fox_attention(q, k, v, cum2, sz, *, batch, seq, heads, tq=512, tk=512, name="fox_attention"):
    t, width = q.shape
    dh = width // heads
    head_block = pl.BlockSpec((seq, dh), lambda b, h: (b, h))
    return pl.pallas_call(
        functools.partial(_fox_attn_kernel, tq=tq, tk=tk),
        grid=(batch, heads),
        in_specs=[head_block, head_block, head_block,
                  pl.BlockSpec((seq, V7X_LANES), lambda b, h: (b, 0)),
                  head_block],
        out_specs=head_block,
        out_shape=jax.ShapeDtypeStruct((t, width), BF16),
        scratch_shapes=[pltpu.VMEM((seq, 2 * dh), BF16), pltpu.VMEM((seq, 2 * dh), BF16)],
        compiler_params=_params("parallel", "arbitrary"),
        name=name,
    )(q, k, v, cum2, sz)


def _identity(v):
    return v


def _doubled(v):
    return jnp.concatenate([v, v], axis=-1)[:, None, :].astype(F32)


def kernel(x, norm_pre, norm_post, s5_w_in, s5_a_re, s5_a_im, s5_log_dt, s5_b_re, s5_b_im, s5_c_re, s5_c_im,
           s5_d, s5_w_glu, s5_b_glu, s5_w_out, kv_norm, kv_w, kv_b_f, fox_w_in, fox_w_out):
    batch, seq, d_model = x.shape
    t = batch * seq
    heads = kv_b_f.shape[0]
    fox_width = fox_w_in.shape[-1] // 2
    assert norm_pre.shape[0] == 2 and s5_w_in.shape[0] == 1 and fox_w_in.shape[0] == 1
    assert fox_width // heads == FOX_HEAD_DIM and heads <= V7X_LANES
    ngroups = s5_a_re.shape[1]
    chunks_per_seq = seq // SSM_CHUNK
    scan_steps = (chunks_per_seq - 1).bit_length()

    h0 = x.reshape(t, d_model)

    u, sz = norm_proj(h0, norm_pre[0][None, :], s5_w_in[0].astype(BF16), _identity, jax.nn.silu, F32, BF16,
                      name="s5_in_proj")
    ldt2 = jnp.broadcast_to(s5_log_dt[0][:, None, None], (ngroups, 1, 2 * S5_STATE)).astype(F32)
    b2 = jnp.concatenate([jnp.swapaxes(s5_b_re[0], 1, 2), jnp.swapaxes(s5_b_im[0], 1, 2)], axis=-1)
    c2 = jnp.concatenate([s5_c_re[0], s5_c_im[0]], axis=-1)
    t0, gm, hm, sr, si = ssm_prep(_doubled(s5_a_re[0]), _doubled(s5_a_im[0]), ldt2, b2.astype(F32),
                                  c2.astype(F32), scan_steps=scan_steps)
    d5 = jnp.tile(s5_d[0].reshape(ngroups, 1, S5_GROUP), (1, 1, SSM_CHUNK)).astype(F32)
    yg = ssm_apply(u, t0, gm, hm, sr, si, d5, seq=seq)
    y3 = glu_gate(yg, s5_w_glu[0].astype(BF16), s5_b_glu[0][None, :], sz)
    h1 = out_proj(y3, s5_w_out[0].astype(BF16), h0, norm_post[0][None, :], name="s5_out_proj")

    kv_cols = 2 * fox_width
    k, v = norm_proj(h1, kv_norm[None, :], kv_w[:, :kv_cols].astype(BF16), _identity, _identity, BF16, BF16,
                     name="fox_kv_proj")
    wf = jnp.pad(kv_w[:, kv_cols:], ((0, 0), (0, V7X_LANES - heads))).astype(BF16)
    bf = jnp.pad(kv_b_f, (0, V7X_LANES - heads))[None, :].astype(F32)
    cum2 = forget_gate(h1, kv_norm[None, :], wf, bf, seq=seq)
    scale = FOX_HEAD_DIM ** -0.5 * LOG2_E
    q, sz2 = norm_proj(h1, norm_pre[1][None, :], fox_w_in[0].astype(BF16), lambda v: v * scale, jax.nn.silu,
                       BF16, BF16, name="fox_in_proj")
    o = fox_attention(q, k, v, cum2, sz2, batch=batch, seq=seq, heads=heads)
    h2 = out_proj(o, fox_w_out[0].astype(BF16), h1, norm_post[1][None, :], name="fox_out_proj")
    return h2.reshape(batch, seq, d_model)
```

```python
import functools
import math

import jax
import jax.numpy as jnp
from jax import lax
from jax.experimental import pallas as pl
from jax.experimental.pallas import tpu as pltpu

F32 = jnp.float32
BF16 = jnp.bfloat16

RMS_EPS = 1e-6
NEG_INF = -1e30
LOG2_E = math.log2(math.e)

S5_GROUP = 16
S5_STATE = 64
FOX_HEAD_DIM = 128

V7X_LANES = 128
V7X_SUBLANES = 8
SSM_CHUNK = 16
GROUPS_PER_SLAB = V7X_LANES // S5_GROUP
VMEM_LIMIT = 48 * 1024 * 1024


def _params(*semantics):
    return pltpu.CompilerParams(dimension_semantics=semantics, vmem_limit_bytes=VMEM_LIMIT)


def _rms_scale(x, gain):
    ms = jnp.mean(x * x, axis=-1, keepdims=True)
    return x * lax.rsqrt(ms + RMS_EPS) * gain


def _norm_proj_kernel(x_ref, gain_ref, w1_ref, w2_ref, o1_ref, o2_ref, xn_ref, *, acts, transposed):
    @pl.when(pl.program_id(1) == 0)
    def _():
        xn_ref[...] = _rms_scale(x_ref[...], gain_ref[...]).astype(BF16)

    xn = xn_ref[...]
    for w_ref, o_ref, act, tr in zip((w1_ref, w2_ref), (o1_ref, o2_ref), acts, transposed):
        if tr:
            y = lax.dot_general(w_ref[...], xn, (((1,), (1,)), ((), ())), preferred_element_type=F32)
        else:
            y = jnp.dot(xn, w_ref[...], preferred_element_type=F32)
        o_ref[...] = act(y).astype(o_ref.dtype)


def norm_proj(x, gain, weights, acts, dtypes, *, n, transposed=(False, False), tm=1024, tn=512,
              name="norm_proj"):
    t, d = x.shape
    nj = n // tn
    w_specs, out_specs, out_shapes = [], [], []
    for (w, col0), dtype, tr in zip(weights, dtypes, transposed):
        if tr:
            w_specs.append(pl.BlockSpec((tn, d), lambda i, j: (j, 0)))
            out_specs.append(pl.BlockSpec((tn, tm), lambda i, j: (j, i)))
            out_shapes.append(jax.ShapeDtypeStruct((n, t), dtype))
        else:
            w_specs.append(pl.BlockSpec((d, tn), lambda i, j, b0=col0 // tn: (0, j + b0)))
            out_specs.append(pl.BlockSpec((tm, tn), lambda i, j: (i, j)))
            out_shapes.append(jax.ShapeDtypeStruct((t, n), dtype))
    return pl.pallas_call(
        functools.partial(_norm_proj_kernel, acts=acts, transposed=transposed),
        grid=(t // tm, nj),
        in_specs=[pl.BlockSpec((tm, d), lambda i, j: (i, 0)), pl.BlockSpec((1, d), lambda i, j: (0, 0))] + w_specs,
        out_specs=out_specs,
        out_shape=out_shapes,
        scratch_shapes=[pltpu.VMEM((tm, d), BF16)],
        compiler_params=_params("parallel", "arbitrary"),
        name=name,
    )(x, gain, weights[0][0], weights[1][0])


def _out_proj_kernel(a_ref, w_ref, res_ref, gain_ref, o_ref):
    y = jnp.dot(a_ref[...], w_ref[...], preferred_element_type=F32)
    o_ref[...] = res_ref[...] + _rms_scale(y, gain_ref[...])


def out_proj(a, w, res, gain, *, tm=512, name="out_proj"):
    t, k = a.shape
    d = w.shape[1]
    return pl.pallas_call(
        _out_proj_kernel,
        grid=(t // tm,),
        in_specs=[
            pl.BlockSpec((tm, k), lambda i: (i, 0)),
            pl.BlockSpec((k, d), lambda i: (0, 0)),
            pl.BlockSpec((tm, d), lambda i: (i, 0)),
            pl.BlockSpec((1, d), lambda i: (0, 0)),
        ],
        out_specs=pl.BlockSpec((tm, d), lambda i: (i, 0)),
        out_shape=jax.ShapeDtypeStruct((t, d), F32),
        compiler_params=_params("parallel"),
        name=name,
    )(a, w, res, gain)


def _glu_kernel(yg_ref, w_ref, b_ref, sz_ref, o_ref, yb_ref, *, tn):
    j = pl.program_id(1)

    @pl.when(j == 0)
    def _():
        yb_ref[...] = yg_ref[...].astype(BF16)

    gate = jax.nn.sigmoid(jnp.dot(yb_ref[...], w_ref[...], preferred_element_type=F32) + b_ref[...])
    yg = yg_ref[:, pl.ds(pl.multiple_of(j * tn, tn), tn)]
    o_ref[...] = (yg * gate * sz_ref[...].astype(F32)).astype(o_ref.dtype)


def glu_gate(yg, w, b, sz, *, tm=1024, tn=512, name="s5_glu"):
    t, k = yg.shape
    n = w.shape[1]
    return pl.pallas_call(
        functools.partial(_glu_kernel, tn=tn),
        grid=(t // tm, n // tn),
        in_specs=[
            pl.BlockSpec((tm, k), lambda i, j: (i, 0)),
            pl.BlockSpec((k, tn), lambda i, j: (0, j)),
            pl.BlockSpec((1, tn), lambda i, j: (0, j)),
            pl.BlockSpec((tm, tn), lambda i, j: (i, j)),
        ],
        out_specs=pl.BlockSpec((tm, tn), lambda i, j: (i, j)),
        out_shape=jax.ShapeDtypeStruct((t, n), BF16),
        scratch_shapes=[pltpu.VMEM((tm, k), BF16)],
        compiler_params=_params("parallel", "arbitrary"),
        name=name,
    )(yg, w, b, sz)


def _ssm_prep_kernel(are_ref, aim_ref, ldt_ref, b2_ref, c2_ref,
                     t0_ref, g_ref, h_ref, sr_ref, si_ref, *, groups, scan_steps):
    lane = lax.broadcasted_iota(jnp.int32, (1, 2 * S5_STATE), 1)
    minus_plus = jnp.where(lane < S5_STATE, -1.0, 1.0).astype(F32)
    gw = S5_GROUP
    width = SSM_CHUNK * gw

    for gi in range(groups):
        ar = are_ref[gi]
        ai = aim_ref[gi]
        dt = jnp.exp(ldt_ref[gi])
        mag = jnp.exp(ar * dt)
        lam_r = mag * jnp.cos(ai * dt)
        lam_i = mag * jnp.sin(ai * dt)
        den = ar * ar + ai * ai
        nr = lam_r - 1.0
        coef_r = (nr * ar + lam_i * ai) / den
        coef_i = (lam_i * ar - nr * ai) / den

        pow_r = [jnp.ones_like(lam_r)]
        pow_i = [jnp.zeros_like(lam_r)]
        for _ in range(SSM_CHUNK):
            pr, pi = pow_r[-1], pow_i[-1]
            pow_r.append(pr * lam_r - pi * lam_i)
            pow_i.append(pr * lam_i + pi * lam_r)

        b2 = b2_ref[gi]
        b2s = pltpu.roll(b2, S5_STATE, axis=1) * minus_plus
        c2 = c2_ref[gi]
        c2a = c2 * (-minus_plus)
        c2b = -pltpu.roll(c2, S5_STATE, axis=1)

        g_rows = []
        for step in range(SSM_CHUNK):
            pr, pi = pow_r[SSM_CHUNK - 1 - step], pow_i[SSM_CHUNK - 1 - step]
            wr = pr * coef_r - pi * coef_i
            wi = pr * coef_i + pi * coef_r
            g_rows.append(wr * b2 + wi * b2s)
        g_ref[gi] = jnp.concatenate(g_rows, axis=0).astype(g_ref.dtype)

        cl = [pow_r[tau] * c2a + pow_i[tau] * c2b for tau in range(SSM_CHUNK + 1)]
        h_t = jnp.concatenate(cl[1:], axis=0)
        h_ref[gi] = h_t.T.astype(h_ref.dtype)

        bbar2 = coef_r * b2 + coef_i * b2s
        cl_all = jnp.concatenate(cl[:-1], axis=0)
        krow = lax.dot_general(bbar2, cl_all, (((1,), (1,)), ((), ())),
                               precision=lax.Precision.HIGHEST,
                               preferred_element_type=F32)
        kpad = jnp.concatenate([jnp.zeros_like(krow), krow], axis=1)
        t0_rows = [kpad[:, width - gw * step: 2 * width - gw * step] for step in range(SSM_CHUNK)]
        t0_ref[gi] = jnp.concatenate(t0_rows, axis=0).astype(t0_ref.dtype)

        mu_r, mu_i = pow_r[SSM_CHUNK], pow_i[SSM_CHUNK]
        sr_rows, si_rows = [], []
        for _ in range(scan_steps):
            sr_rows.append(mu_r)
            si_rows.append(mu_i * minus_plus)
            mu_r, mu_i = mu_r * mu_r - mu_i * mu_i, 2.0 * mu_r * mu_i
        pad = [jnp.zeros_like(mu_r)] * (sr_ref.shape[1] - scan_steps)
        sr_ref[gi] = jnp.concatenate(sr_rows + pad, axis=0)
        si_ref[gi] = jnp.concatenate(si_rows + pad, axis=0)


def ssm_prep(a_re2, a_im2, log_dt2, b2, c2, *, scan_steps, groups_per_step=8, name="ssm_prep"):
    ng = a_re2.shape[0]
    width = SSM_CHUNK * S5_GROUP
    ns = 2 * S5_STATE
    rows = -(-scan_steps // V7X_SUBLANES) * V7X_SUBLANES
    gb = groups_per_step
    vec = pl.BlockSpec((gb, 1, ns), lambda i: (i, 0, 0))
    mat = pl.BlockSpec((gb, S5_GROUP, ns), lambda i: (i, 0, 0))
    return pl.pallas_call(
        functools.partial(_ssm_prep_kernel, groups=gb, scan_steps=scan_steps),
        grid=(ng // gb,),
        in_specs=[vec, vec, vec, mat, mat],
        out_specs=[
            pl.BlockSpec((gb, width, width), lambda i: (i, 0, 0)),
            pl.BlockSpec((gb, width, ns), lambda i: (i, 0, 0)),
            pl.BlockSpec((gb, ns, width), lambda i: (i, 0, 0)),
            pl.BlockSpec((gb, rows, ns), lambda i: (i, 0, 0)),
            pl.BlockSpec((gb, rows, ns), lambda i: (i, 0, 0)),
        ],
        out_shape=[
            jax.ShapeDtypeStruct((ng, width, width), BF16),
            jax.ShapeDtypeStruct((ng, width, ns), BF16),
            jax.ShapeDtypeStruct((ng, ns, width), BF16),
            jax.ShapeDtypeStruct((ng, rows, ns), F32),
            jax.ShapeDtypeStruct((ng, rows, ns), F32),
        ],
        compiler_params=_params("parallel"),
        name=name,
    )(a_re2, a_im2, log_dt2, b2, c2)


def _granule_transpose(vs, lane):
    vs = list(vs)
    n = len(vs)
    d = n // 2
    while d >= 1:
        keep = ((lane // S5_GROUP) & d) == 0
        nxt = list(vs)
        for i in range(n):
            if i & d:
                continue
            p = i + d
            nxt[i] = jnp.where(keep, vs[i], pltpu.roll(vs[p], d * S5_GROUP, axis=1))
            nxt[p] = jnp.where(keep, pltpu.roll(vs[i], V7X_LANES - d * S5_GROUP, axis=1), vs[p])
        vs = nxt
        d //= 2
    return vs


def _ssm_kernel(u_ref, t0_ref, g_ref, h_ref, sr_ref, si_ref, d5_ref, y_ref, u5_ref, y5_ref,
                *, chunks_per_seq, scan_steps):
    t = u_ref.shape[0]
    nchunk = t // SSM_CHUNK
    oct_rows = V7X_SUBLANES * SSM_CHUNK
    lane8 = lax.broadcasted_iota(jnp.int32, (V7X_SUBLANES, V7X_LANES), 1)
    halves = SSM_CHUNK // V7X_SUBLANES

    def relayout_in(it, carry):
        base = pl.multiple_of(it * 2 * oct_rows, 2 * oct_rows)
        per_oct = []
        for o in range(2):
            rows = [u_ref[pl.ds(base + o * oct_rows + step, V7X_SUBLANES, stride=SSM_CHUNK), :]
                    for step in range(SSM_CHUNK)]
            per_oct.append([_granule_transpose(rows[hf * 8:(hf + 1) * 8], lane8) for hf in range(halves)])
        c0 = pl.multiple_of(it * 2 * V7X_SUBLANES, 2 * V7X_SUBLANES)
        for g in range(GROUPS_PER_SLAB):
            for hf in range(halves):
                blk = jnp.concatenate([per_oct[0][hf][g], per_oct[1][hf][g]], axis=0)
                u5_ref[g, pl.ds(c0, 2 * V7X_SUBLANES), hf * V7X_LANES:(hf + 1) * V7X_LANES] = blk.astype(BF16)
        return carry

    lax.fori_loop(0, nchunk // (2 * V7X_SUBLANES), relayout_in, 0, unroll=4)

    row = lax.broadcasted_iota(jnp.int32, (nchunk, 2 * S5_STATE), 0) % chunks_per_seq

    def per_group(g, carry):
        u5 = u5_ref[g]
        hs = jnp.dot(u5, g_ref[g], preferred_element_type=F32)
        for k in range(scan_steps):
            d = 1 << k
            sh = jnp.where(row >= d, pltpu.roll(hs, d, axis=0), 0.0)
            hs = hs + sr_ref[g, k:k + 1, :] * sh + si_ref[g, k:k + 1, :] * pltpu.roll(sh, S5_STATE, axis=1)
        hprev = jnp.where(row >= 1, pltpu.roll(hs, 1, axis=0), 0.0).astype(BF16)
        y = (jnp.dot(u5, t0_ref[g], preferred_element_type=F32)
             + jnp.dot(hprev, h_ref[g], preferred_element_type=F32)
             + d5_ref[g] * u5.astype(F32))
        y5_ref[g] = jax.nn.gelu(y, approximate=True)
        return carry

    lax.fori_loop(0, GROUPS_PER_SLAB, per_group, 0, unroll=2)

    def relayout_out(it, carry):
        c0 = pl.multiple_of(it * V7X_SUBLANES, V7X_SUBLANES)
        base = pl.multiple_of(it * oct_rows, oct_rows)
        for hf in range(halves):
            vs = [y5_ref[g, pl.ds(c0, V7X_SUBLANES), hf * V7X_LANES:(hf + 1) * V7X_LANES]
                  for g in range(GROUPS_PER_SLAB)]
            out = _granule_transpose(vs, lane8)
            for s in range(V7X_SUBLANES):
                step = hf * V7X_SUBLANES + s
                y_ref[pl.ds(base + step, V7X_SUBLANES, stride=SSM_CHUNK), :] = out[s]
        return carry

    lax.fori_loop(0, nchunk // V7X_SUBLANES, relayout_out, 0, unroll=8)


def ssm_apply(u, t0, gm, hm, sr, si, d5, *, seq, name="s5_ssm"):
    t, w = u.shape
    nslab = w // V7X_LANES
    nchunk = t // SSM_CHUNK
    width = SSM_CHUNK * S5_GROUP
    ns = 2 * S5_STATE
    chunks_per_seq = seq // SSM_CHUNK
    scan_steps = (chunks_per_seq - 1).bit_length()
    gb = GROUPS_PER_SLAB
    rows = sr.shape[1]
    return pl.pallas_call(
        functools.partial(_ssm_kernel, chunks_per_seq=chunks_per_seq, scan_steps=scan_steps),
        grid=(nslab,),
        in_specs=[
            pl.BlockSpec((t, V7X_LANES), lambda s: (0, s)),
            pl.BlockSpec((gb, width, width), lambda s: (s, 0, 0)),
            pl.BlockSpec((gb, width, ns), lambda s: (s, 0, 0)),
            pl.BlockSpec((gb, ns, width), lambda s: (s, 0, 0)),
            pl.BlockSpec((gb, rows, ns), lambda s: (s, 0, 0)),
            pl.BlockSpec((gb, rows, ns), lambda s: (s, 0, 0)),
            pl.BlockSpec((gb, 1, width), lambda s: (s, 0, 0)),
        ],
        out_specs=pl.BlockSpec((t, V7X_LANES), lambda s: (0, s)),
        out_shape=jax.ShapeDtypeStruct((t, w), F32),
        scratch_shapes=[pltpu.VMEM((gb, nchunk, width), BF16), pltpu.VMEM((gb, nchunk, width), F32)],
        compiler_params=_params("parallel"),
        name=name,
    )(u, t0, gm, hm, sr, si, d5)


def _fgate_kernel(x_ref, gain_ref, wf_ref, bf_ref, cum_ref, cumt_ref, carry_ref, *, tiles_per_seq, heads):
    i = pl.program_id(0)

    @pl.when(i % tiles_per_seq == 0)
    def _():
        carry_ref[...] = jnp.zeros_like(carry_ref)

    xn = _rms_scale(x_ref[...], gain_ref[...]).astype(BF16)
    logit = jnp.dot(xn, wf_ref[...], preferred_element_type=F32) + bf_ref[...]
    c = -(jnp.maximum(-logit, 0.0) + jnp.log1p(jnp.exp(-jnp.abs(logit))))
    tm = c.shape[0]
    row = lax.broadcasted_iota(jnp.int32, c.shape, 0)
    d = 1
    while d < tm:
        c = c + jnp.where(row >= d, pltpu.roll(c, d, axis=0), 0.0)
        d *= 2
    c = c + carry_ref[...]
    carry_ref[...] = c[tm - 1:tm, :]
    c2 = c * LOG2_E
    cum_ref[...] = c2
    cumt_ref[...] = c2.T[:heads, :]


def forget_gate(x, gain, wf, bf, *, seq, heads, tm=512, name="fox_forget_gate"):
    t, d = x.shape
    tiles_per_seq = seq // tm
    return pl.pallas_call(
        functools.partial(_fgate_kernel, tiles_per_seq=tiles_per_seq, heads=heads),
        grid=(t // tm,),
        in_specs=[
            pl.BlockSpec((tm, d), lambda i: (i, 0)),
            pl.BlockSpec((1, d), lambda i: (0, 0)),
            pl.BlockSpec((d, V7X_LANES), lambda i: (0, 0)),
            pl.BlockSpec((1, V7X_LANES), lambda i: (0, 0)),
        ],
        out_specs=[
            pl.BlockSpec((tm, V7X_LANES), lambda i: (i, 0)),
            pl.BlockSpec((None, heads, tm), lambda i: (i // tiles_per_seq, 0, i % tiles_per_seq)),
        ],
        out_shape=[
            jax.ShapeDtypeStruct((t, V7X_LANES), F32),
            jax.ShapeDtypeStruct((t // seq, heads, seq), F32),
        ],
        scratch_shapes=[pltpu.VMEM((1, V7X_LANES), F32)],
        compiler_params=_params("arbitrary"),
        name=name,
    )(x, gain, wf, bf)


ATTN_SPLIT = 3
ATTN_EXTRA_ROWS = 16


def _split3(c):
    hi = c.astype(BF16).astype(F32)
    mid = (c - hi).astype(BF16).astype(F32)
    return hi, mid, c - hi - mid


def _fox_attn_kernel(qt_ref, k_ref, vt_ref, cn_ref, ct_ref, sz_ref, o_ref,
                     qa_ref, ka_ref, va_ref, s0_ref, s1_ref, smax_ref, m_ref, acc_ref, *, tq, tk):
    dh, seq = qt_ref.shape
    head = pl.program_id(1)

    qa_ref[:dh, :] = qt_ref[...]
    rowi = lax.broadcasted_iota(jnp.int32, (dh, seq), 0)
    ext_q = jnp.where(rowi < 2 * ATTN_SPLIT, 1.0, 0.0)
    for n, part in enumerate(_split3(ct_ref[...])):
        ext_q = jnp.where(rowi == n, part, ext_q)
    qa_ref[dh:, :] = ext_q.astype(BF16)

    va_ref[:dh, :] = vt_ref[...]
    rowv = lax.broadcasted_iota(jnp.int32, (ATTN_EXTRA_ROWS, seq), 0)
    va_ref[dh:, :] = jnp.where(rowv == 0, 1.0, 0.0).astype(BF16)

    def augment_k(r, carry):
        r0 = pl.multiple_of(r * tk, tk)
        lane = lax.broadcasted_iota(jnp.int32, (tk, V7X_LANES), 1)
        c = jnp.sum(jnp.where(lane == head, cn_ref[pl.ds(r0, tk), :], 0.0), axis=-1, keepdims=True)
        ext_k = jnp.where(lane < ATTN_SPLIT, 1.0, 0.0)
        for n, part in enumerate(_split3(c)):
            ext_k = jnp.where(lane == ATTN_SPLIT + n, -part, ext_k)
        ka_ref[pl.ds(r0, tk), :dh] = k_ref[pl.ds(r0, tk), :]
        ka_ref[pl.ds(r0, tk), dh:] = ext_k.astype(BF16)
        return carry

    lax.fori_loop(0, seq // tk, augment_k, 0)

    causal = (lax.broadcasted_iota(jnp.int32, (tk, tq), 0) <= lax.broadcasted_iota(jnp.int32, (tk, tq), 1))

    s_slots = (s0_ref, s1_ref)

    def q_tile(qi, carry):
        q0 = pl.multiple_of(qi * tq, tq)

        def produce(slot, k0):
            st = jnp.dot(ka_ref[pl.ds(k0, tk), :], qa_ref[:, pl.ds(q0, tq)], preferred_element_type=F32)
            s_slots[slot][...] = st
            smax_ref[slot:slot + 1, :] = jnp.max(st, axis=0, keepdims=True)

        def consume(slot, k0, masked):
            st = s_slots[slot][...]
            if masked:
                st = jnp.where(causal, st, NEG_INF)
                st_max = jnp.max(st, axis=0, keepdims=True)
            else:
                st_max = smax_ref[slot:slot + 1, :]
            m = m_ref[...]
            m_new = jnp.maximum(m, st_max)
            pt = jnp.exp2(st - m_new).astype(BF16)
            acc_ref[...] = (jnp.exp2(m - m_new) * acc_ref[...]
                            + jnp.dot(va_ref[:, pl.ds(k0, tk)], pt, preferred_element_type=F32))
            m_ref[...] = m_new

        m_ref[...] = jnp.full(m_ref.shape, NEG_INF, F32)
        acc_ref[...] = jnp.zeros(acc_ref.shape, F32)
        produce(0, 0)

        def pair(j, c):
            k0 = pl.multiple_of(j * 2 * tk, 2 * tk)
            produce(1, k0 + tk)
            consume(0, k0, False)
            produce(0, k0 + 2 * tk)
            consume(1, k0 + tk, False)
            return c

        lax.fori_loop(0, qi // 2, pair, 0)
        kb = pl.multiple_of((qi // 2) * 2 * tk, 2 * tk)

        @pl.when(qi % 2 == 1)
        def _():
            produce(1, kb + tk)
            consume(0, kb, False)
            consume(1, kb + tk, True)

        @pl.when(qi % 2 == 0)
        def _():
            consume(0, kb, True)

        acc = acc_ref[...]
        o = (acc[:dh, :] / acc[dh:dh + 1, :]).T
        o_ref[pl.ds(q0, tq), :] = (o * sz_ref[pl.ds(q0, tq), :].astype(F32)).astype(o_ref.dtype)
        return carry

    lax.fori_loop(0, seq // tq, q_tile, 0)


def fox_attention(qt, k, vt, cum2, cumt2, sz, *, batch, seq, heads, tq=512, tk=512, name="fox_attention"):
    width, t = qt.shape
    dh = width // heads
    natural = pl.BlockSpec((seq, dh), lambda b, h: (b, h))
    transposed = pl.BlockSpec((dh, seq), lambda b, h: (h, b))
    return pl.pallas_call(
        functools.partial(_fox_attn_kernel, tq=tq, tk=tk),
        grid=(batch, heads),
        in_specs=[transposed, natural, transposed,
                  pl.BlockSpec((seq, V7X_LANES), lambda b, h: (b, 0)),
                  pl.BlockSpec((None, None, 1, seq), lambda b, h: (b, h, 0, 0)),
                  natural],
        out_specs=natural,
        out_shape=jax.ShapeDtypeStruct((t, width), BF16),
        scratch_shapes=[pltpu.VMEM((2 * dh, seq), BF16), pltpu.VMEM((seq, 2 * dh), BF16),
                        pltpu.VMEM((dh + ATTN_EXTRA_ROWS, seq), BF16),
                        pltpu.VMEM((tk, tq), F32), pltpu.VMEM((tk, tq), F32),
                        pltpu.VMEM((V7X_SUBLANES, tq), F32), pltpu.VMEM((1, tq), F32),
                        pltpu.VMEM((dh + ATTN_EXTRA_ROWS, tq), F32)],
        compiler_params=_params("parallel", "arbitrary"),
        name=name,
    )(qt, k, vt, cum2, cumt2, sz)


def _identity(v):
    return v


def _doubled(v):
    return jnp.concatenate([v, v], axis=-1)[:, None, :].astype(F32)


def kernel(x, norm_pre, norm_post, s5_w_in, s5_a_re, s5_a_im, s5_log_dt, s5_b_re, s5_b_im, s5_c_re, s5_c_im,
           s5_d, s5_w_glu, s5_b_glu, s5_w_out, kv_norm, kv_w, kv_b_f, fox_w_in, fox_w_out):
    batch, seq, d_model = x.shape
    t = batch * seq
    heads = kv_b_f.shape[0]
    fox_width = fox_w_in.shape[-1] // 2
    assert norm_pre.shape[0] == 2 and s5_w_in.shape[0] == 1 and fox_w_in.shape[0] == 1
    assert fox_width // heads == FOX_HEAD_DIM and heads <= V7X_LANES
    ngroups = s5_a_re.shape[1]
    chunks_per_seq = seq // SSM_CHUNK
    scan_steps = (chunks_per_seq - 1).bit_length()

    h0 = x.reshape(t, d_model)

    s5_width = s5_w_in.shape[-1] // 2
    w_in = s5_w_in[0].astype(BF16)
    u, sz = norm_proj(h0, norm_pre[0][None, :], [(w_in, 0), (w_in, s5_width)], (_identity, jax.nn.silu),
                      (F32, BF16), n=s5_width, name="s5_in_proj")
    ldt2 = jnp.broadcast_to(s5_log_dt[0][:, None, None], (ngroups, 1, 2 * S5_STATE)).astype(F32)
    b2 = jnp.concatenate([jnp.swapaxes(s5_b_re[0], 1, 2), jnp.swapaxes(s5_b_im[0], 1, 2)], axis=-1)
    c2 = jnp.concatenate([s5_c_re[0], s5_c_im[0]], axis=-1)
    t0, gm, hm, sr, si = ssm_prep(_doubled(s5_a_re[0]), _doubled(s5_a_im[0]), ldt2, b2.astype(F32),
                                  c2.astype(F32), scan_steps=scan_steps)
    d5 = jnp.tile(s5_d[0].reshape(ngroups, 1, S5_GROUP), (1, 1, SSM_CHUNK)).astype(F32)
    yg = ssm_apply(u, t0, gm, hm, sr, si, d5, seq=seq)
    y3 = glu_gate(yg, s5_w_glu[0].astype(BF16), s5_b_glu[0][None, :], sz)
    h1 = out_proj(y3, s5_w_out[0].astype(BF16), h0, norm_post[0][None, :], name="s5_out_proj")

    kv_cols = 2 * fox_width
    w_k = kv_w[:, :fox_width].astype(BF16)
    w_vt = kv_w[:, fox_width:kv_cols].T.astype(BF16)
    k, vt = norm_proj(h1, kv_norm[None, :], [(w_k, 0), (w_vt, 0)], (_identity, _identity), (BF16, BF16),
                      n=fox_width, transposed=(False, True), name="fox_kv_proj")
    wf = jnp.pad(kv_w[:, kv_cols:], ((0, 0), (0, V7X_LANES - heads))).astype(BF16)
    bf = jnp.pad(kv_b_f, (0, V7X_LANES - heads))[None, :].astype(F32)
    cum2, cumt2 = forget_gate(h1, kv_norm[None, :], wf, bf, seq=seq, heads=heads)
    scale = FOX_HEAD_DIM ** -0.5 * LOG2_E
    w_qt = fox_w_in[0][:, :fox_width].T.astype(BF16)
    w_z = fox_w_in[0][:, fox_width:].astype(BF16)
    qt, sz2 = norm_proj(h1, norm_pre[1][None, :], [(w_qt, 0), (w_z, 0)], (lambda v: v * scale, jax.nn.silu),
                        (BF16, BF16), n=fox_width, transposed=(True, False), name="fox_in_proj")
    o = fox_attention(qt, k, vt, cum2, cumt2.reshape(batch, heads, 1, seq), sz2,
                      batch=batch, seq=seq, heads=heads)
    h2 = out_proj(o, fox_w_out[0].astype(BF16), h1, norm_post[1][None, :], name="fox_out_proj")
    return h2.reshape(batch, seq, d_model)
```

```python
import functools
import math

import jax
import jax.numpy as jnp
from jax import lax
from jax.experimental import pallas as pl
from jax.experimental.pallas import tpu as pltpu

F32 = jnp.float32
BF16 = jnp.bfloat16

RMS_EPS = 1e-6
NEG_INF = -1e30
LOG2_E = math.log2(math.e)

S5_GROUP = 16
S5_STATE = 64
FOX_HEAD_DIM = 128

V7X_LANES = 128
V7X_SUBLANES = 8
SSM_CHUNK = 16
GROUPS_PER_SLAB = V7X_LANES // S5_GROUP
VMEM_LIMIT = 48 * 1024 * 1024
SSM_VMEM_LIMIT = 56 * 1024 * 1024


def _params(*semantics):
    return pltpu.CompilerParams(dimension_semantics=semantics, vmem_limit_bytes=VMEM_LIMIT)


def _rms_scale(x, gain):
    ms = jnp.mean(x * x, axis=-1, keepdims=True)
    return x * lax.rsqrt(ms + RMS_EPS) * gain


def _norm_proj_kernel(x_ref, gain_ref, w1_ref, w2_ref, o1_ref, o2_ref, xn_ref, *, acts, transposed):
    @pl.when(pl.program_id(1) == 0)
    def _():
        xn_ref[...] = _rms_scale(x_ref[...], gain_ref[...]).astype(BF16)

    xn = xn_ref[...]
    for w_ref, o_ref, act, tr in zip((w1_ref, w2_ref), (o1_ref, o2_ref), acts, transposed):
        if tr:
            y = lax.dot_general(w_ref[...], xn, (((1,), (1,)), ((), ())), preferred_element_type=F32)
        else:
            y = jnp.dot(xn, w_ref[...], preferred_element_type=F32)
        o_ref[...] = act(y).astype(o_ref.dtype)


def norm_proj(x, gain, weights, acts, dtypes, *, n, transposed=(False, False), tm=1024, tn=512,
              name="norm_proj"):
    t, d = x.shape
    nj = n // tn
    w_specs, out_specs, out_shapes = [], [], []
    for (w, col0), dtype, tr in zip(weights, dtypes, transposed):
        if tr:
            w_specs.append(pl.BlockSpec((tn, d), lambda i, j: (j, 0)))
            out_specs.append(pl.BlockSpec((tn, tm), lambda i, j: (j, i)))
            out_shapes.append(jax.ShapeDtypeStruct((n, t), dtype))
        else:
            w_specs.append(pl.BlockSpec((d, tn), lambda i, j, b0=col0 // tn: (0, j + b0)))
            out_specs.append(pl.BlockSpec((tm, tn), lambda i, j: (i, j)))
            out_shapes.append(jax.ShapeDtypeStruct((t, n), dtype))
    return pl.pallas_call(
        functools.partial(_norm_proj_kernel, acts=acts, transposed=transposed),
        grid=(t // tm, nj),
        in_specs=[pl.BlockSpec((tm, d), lambda i, j: (i, 0)), pl.BlockSpec((1, d), lambda i, j: (0, 0))] + w_specs,
        out_specs=out_specs,
        out_shape=out_shapes,
        scratch_shapes=[pltpu.VMEM((tm, d), BF16)],
        compiler_params=_params("parallel", "arbitrary"),
        name=name,
    )(x, gain, weights[0][0], weights[1][0])


def _out_proj_kernel(a_ref, w_ref, res_ref, gain_ref, o_ref):
    y = jnp.dot(a_ref[...], w_ref[...], preferred_element_type=F32)
    o_ref[...] = res_ref[...] + _rms_scale(y, gain_ref[...])


def out_proj(a, w, res, gain, *, tm=512, name="out_proj"):
    t, k = a.shape
    d = w.shape[1]
    return pl.pallas_call(
        _out_proj_kernel,
        grid=(t // tm,),
        in_specs=[
            pl.BlockSpec((tm, k), lambda i: (i, 0)),
            pl.BlockSpec((k, d), lambda i: (0, 0)),
            pl.BlockSpec((tm, d), lambda i: (i, 0)),
            pl.BlockSpec((1, d), lambda i: (0, 0)),
        ],
        out_specs=pl.BlockSpec((tm, d), lambda i: (i, 0)),
        out_shape=jax.ShapeDtypeStruct((t, d), F32),
        compiler_params=_params("parallel"),
        name=name,
    )(a, w, res, gain)


def _glu_kernel(yg_ref, w_ref, b_ref, sz_ref, o_ref, yb_ref, *, tn):
    j = pl.program_id(1)

    @pl.when(j == 0)
    def _():
        yb_ref[...] = yg_ref[...].astype(BF16)

    gate = jax.nn.sigmoid(jnp.dot(yb_ref[...], w_ref[...], preferred_element_type=F32) + b_ref[...])
    yg = yg_ref[:, pl.ds(pl.multiple_of(j * tn, tn), tn)]
    o_ref[...] = (yg * gate * sz_ref[...].astype(F32)).astype(o_ref.dtype)


def glu_gate(yg, w, b, sz, *, tm=1024, tn=512, name="s5_glu"):
    t, k = yg.shape
    n = w.shape[1]
    return pl.pallas_call(
        functools.partial(_glu_kernel, tn=tn),
        grid=(t // tm, n // tn),
        in_specs=[
            pl.BlockSpec((tm, k), lambda i, j: (i, 0)),
            pl.BlockSpec((k, tn), lambda i, j: (0, j)),
            pl.BlockSpec((1, tn), lambda i, j: (0, j)),
            pl.BlockSpec((tm, tn), lambda i, j: (i, j)),
        ],
        out_specs=pl.BlockSpec((tm, tn), lambda i, j: (i, j)),
        out_shape=jax.ShapeDtypeStruct((t, n), BF16),
        scratch_shapes=[pltpu.VMEM((tm, k), BF16)],
        compiler_params=_params("parallel", "arbitrary"),
        name=name,
    )(yg, w, b, sz)


def _ssm_prep_kernel(are_ref, aim_ref, ldt_ref, b2_ref, c2_ref, rep_ref,
                     r_ref, g_ref, h_ref, sr_ref, si_ref, *, groups, scan_steps):
    lane = lax.broadcasted_iota(jnp.int32, (1, 2 * S5_STATE), 1)
    minus_plus = jnp.where(lane < S5_STATE, -1.0, 1.0).astype(F32)
    gw = S5_GROUP
    ns = 2 * S5_STATE
    krows = []

    for gi in range(groups):
        ar = are_ref[gi]
        ai = aim_ref[gi]
        dt = jnp.exp(ldt_ref[gi])
        mag = jnp.exp(ar * dt)
        lam_r = mag * jnp.cos(ai * dt)
        lam_i = mag * jnp.sin(ai * dt)
        den = ar * ar + ai * ai
        nr = lam_r - 1.0
        coef_r = (nr * ar + lam_i * ai) / den
        coef_i = (lam_i * ar - nr * ai) / den

        pow_r = [jnp.ones_like(lam_r)]
        pow_i = [jnp.zeros_like(lam_r)]
        for _ in range(SSM_CHUNK):
            pr, pi = pow_r[-1], pow_i[-1]
            pow_r.append(pr * lam_r - pi * lam_i)
            pow_i.append(pr * lam_i + pi * lam_r)

        b2 = b2_ref[gi]
        b2s = pltpu.roll(b2, S5_STATE, axis=1) * minus_plus
        c2 = c2_ref[gi]
        c2a = c2 * (-minus_plus)
        c2b = -pltpu.roll(c2, S5_STATE, axis=1)

        for step in range(SSM_CHUNK):
            pr, pi = pow_r[SSM_CHUNK - 1 - step], pow_i[SSM_CHUNK - 1 - step]
            wr = pr * coef_r - pi * coef_i
            wi = pr * coef_i + pi * coef_r
            g_ref[step, gi * gw:(gi + 1) * gw, :] = (wr * b2 + wi * b2s).astype(g_ref.dtype)

        cl = [pow_r[tau] * c2a + pow_i[tau] * c2b for tau in range(SSM_CHUNK + 1)]
        h_t = jnp.concatenate(cl[1:], axis=0)
        h_ref[gi * ns:(gi + 1) * ns, :] = h_t.T.astype(h_ref.dtype)

        bbar2 = coef_r * b2 + coef_i * b2s
        cl_all = jnp.concatenate(cl[:-1], axis=0)
        krows.append(lax.dot_general(bbar2, cl_all, (((1,), (1,)), ((), ())),
                                     precision=lax.Precision.HIGHEST,
                                     preferred_element_type=F32))

        mu_r, mu_i = pow_r[SSM_CHUNK], pow_i[SSM_CHUNK]
        sr_rows, si_rows = [], []
        for _ in range(scan_steps):
            sr_rows.append(mu_r)
            si_rows.append(mu_i * minus_plus)
            mu_r, mu_i = mu_r * mu_r - mu_i * mu_i, 2.0 * mu_r * mu_i
        pad = [jnp.zeros_like(mu_r)] * (sr_ref.shape[1] - scan_steps)
        sr_ref[gi] = jnp.concatenate(sr_rows + pad, axis=0)
        si_ref[gi] = jnp.concatenate(si_rows + pad, axis=0)

    kst = jnp.concatenate(krows, axis=0).astype(BF16)
    spread = jnp.dot(kst, rep_ref[...], preferred_element_type=F32)
    rows_g = lax.broadcasted_iota(jnp.int32, spread.shape, 0) // gw
    cols_g = _group_of_lane(lax.broadcasted_iota(jnp.int32, spread.shape, 1))
    spread = jnp.where(rows_g == cols_g, spread, 0.0).astype(r_ref.dtype)
    blocks = [spread[:, tau * V7X_LANES:(tau + 1) * V7X_LANES] for tau in range(SSM_CHUNK)]
    zero = jnp.zeros_like(blocks[0])
    npairs = SSM_CHUNK // 2
    for d in range(npairs):
        base = (npairs - 1 - d) * 2 * V7X_LANES
        top = jnp.concatenate([blocks[2 * d], blocks[2 * d + 1]], axis=1)
        bottom = jnp.concatenate([blocks[2 * d - 1] if d else zero, blocks[2 * d]], axis=1)
        r_ref[base:base + V7X_LANES, :] = top
        r_ref[base + V7X_LANES:base + 2 * V7X_LANES, :] = bottom


def ssm_prep(a_re2, a_im2, log_dt2, b2, c2, rep, *, scan_steps, name="ssm_prep"):
    ng = a_re2.shape[0]
    gb = GROUPS_PER_SLAB
    nslab = ng // gb
    cat = SSM_CHUNK * V7X_LANES
    ns = 2 * S5_STATE
    rows = -(-scan_steps // V7X_SUBLANES) * V7X_SUBLANES
    vec = pl.BlockSpec((gb, 1, ns), lambda i: (i, 0, 0))
    mat = pl.BlockSpec((gb, S5_GROUP, ns), lambda i: (i, 0, 0))
    return pl.pallas_call(
        functools.partial(_ssm_prep_kernel, groups=gb, scan_steps=scan_steps),
        grid=(nslab,),
        in_specs=[vec, vec, vec, mat, mat, pl.BlockSpec(rep.shape, lambda i: (0, 0))],
        out_specs=[
            pl.BlockSpec((None, cat, 2 * V7X_LANES), lambda i: (i, 0, 0)),
            pl.BlockSpec((None, SSM_CHUNK, V7X_LANES, ns), lambda i: (i, 0, 0, 0)),
            pl.BlockSpec((None, gb * ns, SSM_CHUNK * S5_GROUP), lambda i: (i, 0, 0)),
            pl.BlockSpec((gb, rows, ns), lambda i: (i, 0, 0)),
            pl.BlockSpec((gb, rows, ns), lambda i: (i, 0, 0)),
        ],
        out_shape=[
            jax.ShapeDtypeStruct((nslab, cat, 2 * V7X_LANES), BF16),
            jax.ShapeDtypeStruct((nslab, SSM_CHUNK, V7X_LANES, ns), BF16),
            jax.ShapeDtypeStruct((nslab, gb * ns, SSM_CHUNK * S5_GROUP), BF16),
            jax.ShapeDtypeStruct((ng, rows, ns), F32),
            jax.ShapeDtypeStruct((ng, rows, ns), F32),
        ],
        compiler_params=_params("parallel"),
        name=name,
    )(a_re2, a_im2, log_dt2, b2, c2, rep)


def replication_matrix():
    src = jnp.arange(SSM_CHUNK * S5_GROUP)
    dst = jnp.arange(SSM_CHUNK * V7X_LANES)
    same_step = (src[:, None] // S5_GROUP) == (dst[None, :] // V7X_LANES)
    same_chan = (src[:, None] % S5_GROUP) == (dst[None, :] % S5_GROUP)
    return (same_step & same_chan).astype(BF16)


def _group_of_lane(idx):
    return (idx % V7X_LANES) // S5_GROUP


def _ssm_kernel(u_ref, r_ref, gst_ref, hst_ref, sr_ref, si_ref, d_ref, rep_ref, y_ref,
                ucat_ref, gexp_ref, hexp_ref, hprev_ref, intra_ref, *, chunks_per_seq, scan_steps):
    t = u_ref.shape[0]
    nchunk = t // SSM_CHUNK
    ns = 2 * S5_STATE
    pair = 2 * V7X_LANES
    npairs = SSM_CHUNK // 2

    for step in range(SSM_CHUNK):
        x = u_ref[pl.ds(step, nchunk, stride=SSM_CHUNK), :]
        ucat_ref[:, step * V7X_LANES:(step + 1) * V7X_LANES] = x.astype(BF16)

    g_rows = lax.broadcasted_iota(jnp.int32, (V7X_LANES, GROUPS_PER_SLAB * ns), 0) // S5_GROUP
    g_cols = lax.broadcasted_iota(jnp.int32, (V7X_LANES, GROUPS_PER_SLAB * ns), 1) // ns
    for step in range(SSM_CHUNK):
        tiled = jnp.concatenate([gst_ref[step]] * GROUPS_PER_SLAB, axis=1)
        gexp_ref[step * V7X_LANES:(step + 1) * V7X_LANES, :] = jnp.where(g_rows == g_cols, tiled, 0.0).astype(BF16)

    h_rows = lax.broadcasted_iota(jnp.int32, (GROUPS_PER_SLAB * ns, pair), 0) // ns
    h_cols = _group_of_lane(lax.broadcasted_iota(jnp.int32, (GROUPS_PER_SLAB * ns, pair), 1))
    for b in range(npairs):
        spread = jnp.dot(hst_ref[...], rep_ref[:, b * pair:(b + 1) * pair], preferred_element_type=F32)
        hexp_ref[:, b * pair:(b + 1) * pair] = jnp.where(h_rows == h_cols, spread, 0.0).astype(BF16)

    hs_all = jnp.dot(ucat_ref[...], gexp_ref[...], preferred_element_type=F32)
    for b in range(npairs):
        intra_ref[:, b * pair:(b + 1) * pair] = jnp.dot(
            ucat_ref[:, :(b + 1) * pair], r_ref[(npairs - 1 - b) * pair:, :], preferred_element_type=F32)

    row = lax.broadcasted_iota(jnp.int32, (nchunk, ns), 0) % chunks_per_seq
    for g in range(GROUPS_PER_SLAB):
        hs = hs_all[:, g * ns:(g + 1) * ns]
        for k in range(scan_steps):
            d = 1 << k
            sh = jnp.where(row >= d, pltpu.roll(hs, d, axis=0), 0.0)
            hs = hs + sr_ref[g, k:k + 1, :] * sh + si_ref[g, k:k + 1, :] * pltpu.roll(sh, S5_STATE, axis=1)
        hprev_ref[:, g * ns:(g + 1) * ns] = jnp.where(row >= 1, pltpu.roll(hs, 1, axis=0), 0.0).astype(BF16)

    for b in range(npairs):
        y = intra_ref[:, b * pair:(b + 1) * pair] + jnp.dot(
            hprev_ref[...], hexp_ref[:, b * pair:(b + 1) * pair], preferred_element_type=F32)
        for bit in range(2):
            step = 2 * b + bit
            lanes = slice(step * V7X_LANES, (step + 1) * V7X_LANES)
            yl = y[:, bit * V7X_LANES:(bit + 1) * V7X_LANES] + d_ref[...] * ucat_ref[:, lanes].astype(F32)
            y_ref[pl.ds(step, nchunk, stride=SSM_CHUNK), :] = jax.nn.gelu(yl, approximate=True)


def ssm_apply(u, r_all, gst, hst, sr, si, d_skip, rep, *, seq, name="s5_ssm"):
    t, w = u.shape
    nslab = w // V7X_LANES
    nchunk = t // SSM_CHUNK
    cat = SSM_CHUNK * V7X_LANES
    ns = 2 * S5_STATE
    chunks_per_seq = seq // SSM_CHUNK
    scan_steps = (chunks_per_seq - 1).bit_length()
    gb = GROUPS_PER_SLAB
    rows = sr.shape[1]
    return pl.pallas_call(
        functools.partial(_ssm_kernel, chunks_per_seq=chunks_per_seq, scan_steps=scan_steps),
        grid=(nslab,),
        in_specs=[
            pl.BlockSpec((t, V7X_LANES), lambda s: (0, s)),
            pl.BlockSpec((None, cat, 2 * V7X_LANES), lambda s: (s, 0, 0)),
            pl.BlockSpec((None, SSM_CHUNK, V7X_LANES, ns), lambda s: (s, 0, 0, 0)),
            pl.BlockSpec((None, gb * ns, SSM_CHUNK * S5_GROUP), lambda s: (s, 0, 0)),
            pl.BlockSpec((gb, rows, ns), lambda s: (s, 0, 0)),
            pl.BlockSpec((gb, rows, ns), lambda s: (s, 0, 0)),
            pl.BlockSpec((1, V7X_LANES), lambda s: (0, s)),
            pl.BlockSpec((SSM_CHUNK * S5_GROUP, cat), lambda s: (0, 0)),
        ],
        out_specs=pl.BlockSpec((t, V7X_LANES), lambda s: (0, s)),
        out_shape=jax.ShapeDtypeStruct((t, w), F32),
        scratch_shapes=[pltpu.VMEM((nchunk, cat), BF16), pltpu.VMEM((cat, gb * ns), BF16),
                        pltpu.VMEM((gb * ns, cat), BF16), pltpu.VMEM((nchunk, gb * ns), BF16),
                        pltpu.VMEM((nchunk, cat), F32)],
        compiler_params=pltpu.CompilerParams(dimension_semantics=("parallel",),
                                             vmem_limit_bytes=SSM_VMEM_LIMIT),
        name=name,
    )(u, r_all, gst, hst, sr, si, d_skip, rep)


def _fgate_kernel(x_ref, gain_ref, wf_ref, bf_ref, cum_ref, cumt_ref, carry_ref, *, tiles_per_seq, heads):
    i = pl.program_id(0)

    @pl.when(i % tiles_per_seq == 0)
    def _():
        carry_ref[...] = jnp.zeros_like(carry_ref)

    xn = _rms_scale(x_ref[...], gain_ref[...]).astype(BF16)
    logit = jnp.dot(xn, wf_ref[...], preferred_element_type=F32) + bf_ref[...]
    c = -(jnp.maximum(-logit, 0.0) + jnp.log1p(jnp.exp(-jnp.abs(logit))))
    tm = c.shape[0]
    row = lax.broadcasted_iota(jnp.int32, c.shape, 0)
    d = 1
    while d < tm:
        c = c + jnp.where(row >= d, pltpu.roll(c, d, axis=0), 0.0)
        d *= 2
    c = c + carry_ref[...]
    carry_ref[...] = c[tm - 1:tm, :]
    c2 = c * LOG2_E
    cum_ref[...] = c2
    cumt_ref[...] = c2.T[:heads, :]


def forget_gate(x, gain, wf, bf, *, seq, heads, tm=512, name="fox_forget_gate"):
    t, d = x.shape
    tiles_per_seq = seq // tm
    return pl.pallas_call(
        functools.partial(_fgate_kernel, tiles_per_seq=tiles_per_seq, heads=heads),
        grid=(t // tm,),
        in_specs=[
            pl.BlockSpec((tm, d), lambda i: (i, 0)),
            pl.BlockSpec((1, d), lambda i: (0, 0)),
            pl.BlockSpec((d, V7X_LANES), lambda i: (0, 0)),
            pl.BlockSpec((1, V7X_LANES), lambda i: (0, 0)),
        ],
        out_specs=[
            pl.BlockSpec((tm, V7X_LANES), lambda i: (i, 0)),
            pl.BlockSpec((None, heads, tm), lambda i: (i // tiles_per_seq, 0, i % tiles_per_seq)),
        ],
        out_shape=[
            jax.ShapeDtypeStruct((t, V7X_LANES), F32),
            jax.ShapeDtypeStruct((t // seq, heads, seq), F32),
        ],
        scratch_shapes=[pltpu.VMEM((1, V7X_LANES), F32)],
        compiler_params=_params("arbitrary"),
        name=name,
    )(x, gain, wf, bf)


ATTN_SPLIT = 3
ATTN_EXTRA_ROWS = 16


def _split3(c):
    hi = c.astype(BF16).astype(F32)
    mid = (c - hi).astype(BF16).astype(F32)
    return hi, mid, c - hi - mid


def _fox_attn_kernel(qt_ref, k_ref, vt_ref, cn_ref, ct_ref, sz_ref, o_ref,
                     qa_ref, ka_ref, va_ref, s0_ref, s1_ref, smax_ref, m_ref, acc_ref, *, tq, tk):
    dh, seq = qt_ref.shape
    head = pl.program_id(1)

    qa_ref[:dh, :] = qt_ref[...]
    rowi = lax.broadcasted_iota(jnp.int32, (dh, seq), 0)
    ext_q = jnp.where(rowi < 2 * ATTN_SPLIT, 1.0, 0.0)
    for n, part in enumerate(_split3(ct_ref[...])):
        ext_q = jnp.where(rowi == n, part, ext_q)
    qa_ref[dh:, :] = ext_q.astype(BF16)

    va_ref[:dh, :] = vt_ref[...]
    rowv = lax.broadcasted_iota(jnp.int32, (ATTN_EXTRA_ROWS, seq), 0)
    va_ref[dh:, :] = jnp.where(rowv == 0, 1.0, 0.0).astype(BF16)

    def augment_k(r, carry):
        r0 = pl.multiple_of(r * tk, tk)
        lane = lax.broadcasted_iota(jnp.int32, (tk, V7X_LANES), 1)
        c = jnp.sum(jnp.where(lane == head, cn_ref[pl.ds(r0, tk), :], 0.0), axis=-1, keepdims=True)
        ext_k = jnp.where(lane < ATTN_SPLIT, 1.0, 0.0)
        for n, part in enumerate(_split3(c)):
            ext_k = jnp.where(lane == ATTN_SPLIT + n, -part, ext_k)
        ka_ref[pl.ds(r0, tk), :dh] = k_ref[pl.ds(r0, tk), :]
        ka_ref[pl.ds(r0, tk), dh:] = ext_k.astype(BF16)
        return carry

    lax.fori_loop(0, seq // tk, augment_k, 0)

    causal = (lax.broadcasted_iota(jnp.int32, (tk, tq), 0) <= lax.broadcasted_iota(jnp.int32, (tk, tq), 1))

    s_slots = (s0_ref, s1_ref)

    def q_tile(qi, carry):
        q0 = pl.multiple_of(qi * tq, tq)

        def produce(slot, k0):
            st = jnp.dot(ka_ref[pl.ds(k0, tk), :], qa_ref[:, pl.ds(q0, tq)], preferred_element_type=F32)
            s_slots[slot][...] = st
            smax_ref[slot:slot + 1, :] = jnp.max(st, axis=0, keepdims=True)

        def consume(slot, k0, masked):
            st = s_slots[slot][...]
            if masked:
                st = jnp.where(causal, st, NEG_INF)
                st_max = jnp.max(st, axis=0, keepdims=True)
            else:
                st_max = smax_ref[slot:slot + 1, :]
            m = m_ref[...]
            m_new = jnp.maximum(m, st_max)
            pt = jnp.exp2(st - m_new).astype(BF16)
            acc_ref[...] = (jnp.exp2(m - m_new) * acc_ref[...]
                            + jnp.dot(va_ref[:, pl.ds(k0, tk)], pt, preferred_element_type=F32))
            m_ref[...] = m_new

        m_ref[...] = jnp.full(m_ref.shape, NEG_INF, F32)
        acc_ref[...] = jnp.zeros(acc_ref.shape, F32)
        produce(0, 0)

        def pair(j, c):
            k0 = pl.multiple_of(j * 2 * tk, 2 * tk)
            produce(1, k0 + tk)
            consume(0, k0, False)
            produce(0, k0 + 2 * tk)
            consume(1, k0 + tk, False)
            return c

        lax.fori_loop(0, qi // 2, pair, 0)
        kb = pl.multiple_of((qi // 2) * 2 * tk, 2 * tk)

        @pl.when(qi % 2 == 1)
        def _():
            produce(1, kb + tk)
            consume(0, kb, False)
            consume(1, kb + tk, True)

        @pl.when(qi % 2 == 0)
        def _():
            consume(0, kb, True)

        acc = acc_ref[...]
        o = (acc[:dh, :] / acc[dh:dh + 1, :]).T
        o_ref[pl.ds(q0, tq), :] = (o * sz_ref[pl.ds(q0, tq), :].astype(F32)).astype(o_ref.dtype)
        return carry

    lax.fori_loop(0, seq // tq, q_tile, 0)


def fox_attention(qt, k, vt, cum2, cumt2, sz, *, batch, seq, heads, tq=512, tk=512, name="fox_attention"):
    width, t = qt.shape
    dh = width // heads
    natural = pl.BlockSpec((seq, dh), lambda b, h: (b, h))
    transposed = pl.BlockSpec((dh, seq), lambda b, h: (h, b))
    return pl.pallas_call(
        functools.partial(_fox_attn_kernel, tq=tq, tk=tk),
        grid=(batch, heads),
        in_specs=[transposed, natural, transposed,
                  pl.BlockSpec((seq, V7X_LANES), lambda b, h: (b, 0)),
                  pl.BlockSpec((None, None, 1, seq), lambda b, h: (b, h, 0, 0)),
                  natural],
        out_specs=natural,
        out_shape=jax.ShapeDtypeStruct((t, width), BF16),
        scratch_shapes=[pltpu.VMEM((2 * dh, seq), BF16), pltpu.VMEM((seq, 2 * dh), BF16),
                        pltpu.VMEM((dh + ATTN_EXTRA_ROWS, seq), BF16),
                        pltpu.VMEM((tk, tq), F32), pltpu.VMEM((tk, tq), F32),
                        pltpu.VMEM((V7X_SUBLANES, tq), F32), pltpu.VMEM((1, tq), F32),
                        pltpu.VMEM((dh + ATTN_EXTRA_ROWS, tq), F32)],
        compiler_params=_params("parallel", "arbitrary"),
        name=name,
    )(qt, k, vt, cum2, cumt2, sz)


def _identity(v):
    return v


def _doubled(v):
    return jnp.concatenate([v, v], axis=-1)[:, None, :].astype(F32)


def kernel(x, norm_pre, norm_post, s5_w_in, s5_a_re, s5_a_im, s5_log_dt, s5_b_re, s5_b_im, s5_c_re, s5_c_im,
           s5_d, s5_w_glu, s5_b_glu, s5_w_out, kv_norm, kv_w, kv_b_f, fox_w_in, fox_w_out):
    batch, seq, d_model = x.shape
    t = batch * seq
    heads = kv_b_f.shape[0]
    fox_width = fox_w_in.shape[-1] // 2
    assert norm_pre.shape[0] == 2 and s5_w_in.shape[0] == 1 and fox_w_in.shape[0] == 1
    assert fox_width // heads == FOX_HEAD_DIM and heads <= V7X_LANES
    ngroups = s5_a_re.shape[1]
    chunks_per_seq = seq // SSM_CHUNK
    scan_steps = (chunks_per_seq - 1).bit_length()

    h0 = x.reshape(t, d_model)

    s5_width = s5_w_in.shape[-1] // 2
    w_in = s5_w_in[0].astype(BF16)
    u, sz = norm_proj(h0, norm_pre[0][None, :], [(w_in, 0), (w_in, s5_width)], (_identity, jax.nn.silu),
                      (F32, BF16), n=s5_width, name="s5_in_proj")
    ldt2 = jnp.broadcast_to(s5_log_dt[0][:, None, None], (ngroups, 1, 2 * S5_STATE)).astype(F32)
    b2 = jnp.concatenate([jnp.swapaxes(s5_b_re[0], 1, 2), jnp.swapaxes(s5_b_im[0], 1, 2)], axis=-1)
    c2 = jnp.concatenate([s5_c_re[0], s5_c_im[0]], axis=-1)
    rep = replication_matrix()
    r_all, gst, hst, sr, si = ssm_prep(_doubled(s5_a_re[0]), _doubled(s5_a_im[0]), ldt2, b2.astype(F32),
                                       c2.astype(F32), rep, scan_steps=scan_steps)
    yg = ssm_apply(u, r_all, gst, hst, sr, si, s5_d[0][None, :].astype(F32), rep, seq=seq)
    y3 = glu_gate(yg, s5_w_glu[0].astype(BF16), s5_b_glu[0][None, :], sz)
    h1 = out_proj(y3, s5_w_out[0].astype(BF16), h0, norm_post[0][None, :], name="s5_out_proj")

    kv_cols = 2 * fox_width
    w_k = kv_w[:, :fox_width].astype(BF16)
    w_vt = kv_w[:, fox_width:kv_cols].T.astype(BF16)
    k, vt = norm_proj(h1, kv_norm[None, :], [(w_k, 0), (w_vt, 0)], (_identity, _identity), (BF16, BF16),
                      n=fox_width, transposed=(False, True), name="fox_kv_proj")
    wf = jnp.pad(kv_w[:, kv_cols:], ((0, 0), (0, V7X_LANES - heads))).astype(BF16)
    bf = jnp.pad(kv_b_f, (0, V7X_LANES - heads))[None, :].astype(F32)
    cum2, cumt2 = forget_gate(h1, kv_norm[None, :], wf, bf, seq=seq, heads=heads)
    scale = FOX_HEAD_DIM ** -0.5 * LOG2_E
    w_qt = fox_w_in[0][:, :fox_width].T.astype(BF16)
    w_z = fox_w_in[0][:, fox_width:].astype(BF16)
    qt, sz2 = norm_proj(h1, norm_pre[1][None, :], [(w_qt, 0), (w_z, 0)], (lambda v: v * scale, jax.nn.silu),
                        (BF16, BF16), n=fox_width, transposed=(True, False), name="fox_in_proj")
    o = fox_attention(qt, k, vt, cum2, cumt2.reshape(batch, heads, 1, seq), sz2,
                      batch=batch, seq=seq, heads=heads)
    h2 = out_proj(o, fox_w_out[0].astype(BF16), h1, norm_post[1][None, :], name="fox_out_proj")
    return h2.reshape(batch, seq, d_model)
```

```python
import functools
import math

import jax
import jax.numpy as jnp
from jax import lax
from jax.experimental import pallas as pl
from jax.experimental.pallas import tpu as pltpu

F32 = jnp.float32
BF16 = jnp.bfloat16

RMS_EPS = 1e-6
NEG_INF = -1e30
LOG2_E = math.log2(math.e)

S5_GROUP = 16
S5_STATE = 64
FOX_HEAD_DIM = 128

V7X_LANES = 128
V7X_SUBLANES = 8
SSM_CHUNK = 16
GROUPS_PER_SLAB = V7X_LANES // S5_GROUP
VMEM_LIMIT = 48 * 1024 * 1024
SSM_VMEM_LIMIT = 56 * 1024 * 1024


def _params(*semantics):
    return pltpu.CompilerParams(dimension_semantics=semantics, vmem_limit_bytes=VMEM_LIMIT)


def _rms_scale(x, gain):
    ms = jnp.mean(x * x, axis=-1, keepdims=True)
    return x * lax.rsqrt(ms + RMS_EPS) * gain


def _norm_proj_kernel(x_ref, gain_ref, w1_ref, w2_ref, o1_ref, o2_ref, xn_ref, *, acts, transposed):
    @pl.when(pl.program_id(1) == 0)
    def _():
        xn_ref[...] = _rms_scale(x_ref[...], gain_ref[...]).astype(BF16)

    xn = xn_ref[...]
    for w_ref, o_ref, act, tr in zip((w1_ref, w2_ref), (o1_ref, o2_ref), acts, transposed):
        if tr:
            y = lax.dot_general(w_ref[...], xn, (((1,), (1,)), ((), ())), preferred_element_type=F32)
        else:
            y = jnp.dot(xn, w_ref[...], preferred_element_type=F32)
        o_ref[...] = act(y).astype(o_ref.dtype)


def norm_proj(x, gain, weights, acts, dtypes, *, n, transposed=(False, False), tm=1024, tn=512,
              name="norm_proj"):
    t, d = x.shape
    nj = n // tn
    w_specs, out_specs, out_shapes = [], [], []
    for (w, col0), dtype, tr in zip(weights, dtypes, transposed):
        if tr:
            w_specs.append(pl.BlockSpec((tn, d), lambda i, j: (j, 0)))
            out_specs.append(pl.BlockSpec((tn, tm), lambda i, j: (j, i)))
            out_shapes.append(jax.ShapeDtypeStruct((n, t), dtype))
        else:
            w_specs.append(pl.BlockSpec((d, tn), lambda i, j, b0=col0 // tn: (0, j + b0)))
            out_specs.append(pl.BlockSpec((tm, tn), lambda i, j: (i, j)))
            out_shapes.append(jax.ShapeDtypeStruct((t, n), dtype))
    return pl.pallas_call(
        functools.partial(_norm_proj_kernel, acts=acts, transposed=transposed),
        grid=(t // tm, nj),
        in_specs=[pl.BlockSpec((tm, d), lambda i, j: (i, 0)), pl.BlockSpec((1, d), lambda i, j: (0, 0))] + w_specs,
        out_specs=out_specs,
        out_shape=out_shapes,
        scratch_shapes=[pltpu.VMEM((tm, d), BF16)],
        compiler_params=_params("parallel", "arbitrary"),
        name=name,
    )(x, gain, weights[0][0], weights[1][0])


def _out_proj_kernel(a_ref, w_ref, res_ref, gain_ref, o_ref):
    y = jnp.dot(a_ref[...], w_ref[...], preferred_element_type=F32)
    o_ref[...] = res_ref[...] + _rms_scale(y, gain_ref[...])


def out_proj(a, w, res, gain, *, tm=512, name="out_proj"):
    t, k = a.shape
    d = w.shape[1]
    return pl.pallas_call(
        _out_proj_kernel,
        grid=(t // tm,),
        in_specs=[
            pl.BlockSpec((tm, k), lambda i: (i, 0)),
            pl.BlockSpec((k, d), lambda i: (0, 0)),
            pl.BlockSpec((tm, d), lambda i: (i, 0)),
            pl.BlockSpec((1, d), lambda i: (0, 0)),
        ],
        out_specs=pl.BlockSpec((tm, d), lambda i: (i, 0)),
        out_shape=jax.ShapeDtypeStruct((t, d), F32),
        compiler_params=_params("parallel"),
        name=name,
    )(a, w, res, gain)


def _glu_kernel(yg_ref, w_ref, b_ref, sz_ref, o_ref, yb_ref, *, tn):
    j = pl.program_id(1)

    @pl.when(j == 0)
    def _():
        yb_ref[...] = yg_ref[...].astype(BF16)

    gate = jax.nn.sigmoid(jnp.dot(yb_ref[...], w_ref[...], preferred_element_type=F32) + b_ref[...])
    yg = yg_ref[:, pl.ds(pl.multiple_of(j * tn, tn), tn)]
    o_ref[...] = (yg * gate * sz_ref[...].astype(F32)).astype(o_ref.dtype)


def glu_gate(yg, w, b, sz, *, tm=1024, tn=512, name="s5_glu"):
    t, k = yg.shape
    n = w.shape[1]
    return pl.pallas_call(
        functools.partial(_glu_kernel, tn=tn),
        grid=(t // tm, n // tn),
        in_specs=[
            pl.BlockSpec((tm, k), lambda i, j: (i, 0)),
            pl.BlockSpec((k, tn), lambda i, j: (0, j)),
            pl.BlockSpec((1, tn), lambda i, j: (0, j)),
            pl.BlockSpec((tm, tn), lambda i, j: (i, j)),
        ],
        out_specs=pl.BlockSpec((tm, tn), lambda i, j: (i, j)),
        out_shape=jax.ShapeDtypeStruct((t, n), BF16),
        scratch_shapes=[pltpu.VMEM((tm, k), BF16)],
        compiler_params=_params("parallel", "arbitrary"),
        name=name,
    )(yg, w, b, sz)


def _ssm_prep_kernel(are_ref, aim_ref, ldt_ref, b2_ref, c2_ref, rep_ref,
                     r_ref, g_ref, h_ref, sr_ref, si_ref, *, groups, scan_steps):
    lane = lax.broadcasted_iota(jnp.int32, (1, 2 * S5_STATE), 1)
    minus_plus = jnp.where(lane < S5_STATE, -1.0, 1.0).astype(F32)
    gw = S5_GROUP
    ns = 2 * S5_STATE
    krows = []

    for gi in range(groups):
        ar = are_ref[gi]
        ai = aim_ref[gi]
        dt = jnp.exp(ldt_ref[gi])
        mag = jnp.exp(ar * dt)
        lam_r = mag * jnp.cos(ai * dt)
        lam_i = mag * jnp.sin(ai * dt)
        den = ar * ar + ai * ai
        nr = lam_r - 1.0
        coef_r = (nr * ar + lam_i * ai) / den
        coef_i = (lam_i * ar - nr * ai) / den

        pow_r = [jnp.ones_like(lam_r)]
        pow_i = [jnp.zeros_like(lam_r)]
        for _ in range(SSM_CHUNK):
            pr, pi = pow_r[-1], pow_i[-1]
            pow_r.append(pr * lam_r - pi * lam_i)
            pow_i.append(pr * lam_i + pi * lam_r)

        b2 = b2_ref[gi]
        b2s = pltpu.roll(b2, S5_STATE, axis=1) * minus_plus
        c2 = c2_ref[gi]
        c2a = c2 * (-minus_plus)
        c2b = -pltpu.roll(c2, S5_STATE, axis=1)

        for step in range(SSM_CHUNK):
            pr, pi = pow_r[SSM_CHUNK - 1 - step], pow_i[SSM_CHUNK - 1 - step]
            wr = pr * coef_r - pi * coef_i
            wi = pr * coef_i + pi * coef_r
            g_ref[step, gi * gw:(gi + 1) * gw, :] = (wr * b2 + wi * b2s).astype(g_ref.dtype)

        cl = [pow_r[tau] * c2a + pow_i[tau] * c2b for tau in range(SSM_CHUNK + 1)]
        h_t = jnp.concatenate(cl[1:], axis=0)
        h_ref[gi * ns:(gi + 1) * ns, :] = h_t.T.astype(h_ref.dtype)

        bbar2 = coef_r * b2 + coef_i * b2s
        cl_all = jnp.concatenate(cl[:-1], axis=0)
        krows.append(lax.dot_general(bbar2, cl_all, (((1,), (1,)), ((), ())),
                                     precision=lax.Precision.HIGHEST,
                                     preferred_element_type=F32))

        mu_r, mu_i = pow_r[SSM_CHUNK], pow_i[SSM_CHUNK]
        sr_rows, si_rows = [], []
        for _ in range(scan_steps):
            sr_rows.append(mu_r)
            si_rows.append(mu_i * minus_plus)
            mu_r, mu_i = mu_r * mu_r - mu_i * mu_i, 2.0 * mu_r * mu_i
        pad = [jnp.zeros_like(mu_r)] * (sr_ref.shape[1] - scan_steps)
        sr_ref[gi] = jnp.concatenate(sr_rows + pad, axis=0)
        si_ref[gi] = jnp.concatenate(si_rows + pad, axis=0)

    kst = jnp.concatenate(krows, axis=0).astype(BF16)
    spread = jnp.dot(kst, rep_ref[...], preferred_element_type=F32)
    rows_g = lax.broadcasted_iota(jnp.int32, spread.shape, 0) // gw
    cols_g = _group_of_lane(lax.broadcasted_iota(jnp.int32, spread.shape, 1))
    spread = jnp.where(rows_g == cols_g, spread, 0.0).astype(r_ref.dtype)
    blocks = [spread[:, tau * V7X_LANES:(tau + 1) * V7X_LANES] for tau in range(SSM_CHUNK)]
    zero = jnp.zeros_like(blocks[0])
    npairs = SSM_CHUNK // 2
    for d in range(npairs):
        base = (npairs - 1 - d) * 2 * V7X_LANES
        top = jnp.concatenate([blocks[2 * d], blocks[2 * d + 1]], axis=1)
        bottom = jnp.concatenate([blocks[2 * d - 1] if d else zero, blocks[2 * d]], axis=1)
        r_ref[base:base + V7X_LANES, :] = top
        r_ref[base + V7X_LANES:base + 2 * V7X_LANES, :] = bottom


def ssm_prep(a_re2, a_im2, log_dt2, b2, c2, rep, *, scan_steps, name="ssm_prep"):
    ng = a_re2.shape[0]
    gb = GROUPS_PER_SLAB
    nslab = ng // gb
    cat = SSM_CHUNK * V7X_LANES
    ns = 2 * S5_STATE
    rows = -(-scan_steps // V7X_SUBLANES) * V7X_SUBLANES
    vec = pl.BlockSpec((gb, 1, ns), lambda i: (i, 0, 0))
    mat = pl.BlockSpec((gb, S5_GROUP, ns), lambda i: (i, 0, 0))
    return pl.pallas_call(
        functools.partial(_ssm_prep_kernel, groups=gb, scan_steps=scan_steps),
        grid=(nslab,),
        in_specs=[vec, vec, vec, mat, mat, pl.BlockSpec(rep.shape, lambda i: (0, 0))],
        out_specs=[
            pl.BlockSpec((None, cat, 2 * V7X_LANES), lambda i: (i, 0, 0)),
            pl.BlockSpec((None, SSM_CHUNK, V7X_LANES, ns), lambda i: (i, 0, 0, 0)),
            pl.BlockSpec((None, gb * ns, SSM_CHUNK * S5_GROUP), lambda i: (i, 0, 0)),
            pl.BlockSpec((gb, rows, ns), lambda i: (i, 0, 0)),
            pl.BlockSpec((gb, rows, ns), lambda i: (i, 0, 0)),
        ],
        out_shape=[
            jax.ShapeDtypeStruct((nslab, cat, 2 * V7X_LANES), BF16),
            jax.ShapeDtypeStruct((nslab, SSM_CHUNK, V7X_LANES, ns), BF16),
            jax.ShapeDtypeStruct((nslab, gb * ns, SSM_CHUNK * S5_GROUP), BF16),
            jax.ShapeDtypeStruct((ng, rows, ns), F32),
            jax.ShapeDtypeStruct((ng, rows, ns), F32),
        ],
        compiler_params=_params("parallel"),
        name=name,
    )(a_re2, a_im2, log_dt2, b2, c2, rep)


def replication_matrix():
    src = jnp.arange(SSM_CHUNK * S5_GROUP)
    dst = jnp.arange(SSM_CHUNK * V7X_LANES)
    same_step = (src[:, None] // S5_GROUP) == (dst[None, :] // V7X_LANES)
    same_chan = (src[:, None] % S5_GROUP) == (dst[None, :] % S5_GROUP)
    return (same_step & same_chan).astype(BF16)


def _group_of_lane(idx):
    return (idx % V7X_LANES) // S5_GROUP


def _ssm_kernel(u_ref, r_ref, gst_ref, hst_ref, sr_ref, si_ref, d_ref, rep_ref, y_ref,
                ucat_ref, gexp_ref, hexp_ref, hprev_ref, intra_ref, *, chunks_per_seq, scan_steps):
    t = u_ref.shape[0]
    nchunk = t // SSM_CHUNK
    ns = 2 * S5_STATE
    pair = 2 * V7X_LANES
    npairs = SSM_CHUNK // 2

    for step in range(SSM_CHUNK):
        x = u_ref[pl.ds(step, nchunk, stride=SSM_CHUNK), :]
        ucat_ref[:, step * V7X_LANES:(step + 1) * V7X_LANES] = x.astype(BF16)

    g_rows = lax.broadcasted_iota(jnp.int32, (V7X_LANES, GROUPS_PER_SLAB * ns), 0) // S5_GROUP
    g_cols = lax.broadcasted_iota(jnp.int32, (V7X_LANES, GROUPS_PER_SLAB * ns), 1) // ns
    for step in range(SSM_CHUNK):
        tiled = jnp.concatenate([gst_ref[step]] * GROUPS_PER_SLAB, axis=1)
        gexp_ref[step * V7X_LANES:(step + 1) * V7X_LANES, :] = jnp.where(g_rows == g_cols, tiled, 0.0).astype(BF16)

    h_rows = lax.broadcasted_iota(jnp.int32, (GROUPS_PER_SLAB * ns, pair), 0) // ns
    h_cols = _group_of_lane(lax.broadcasted_iota(jnp.int32, (GROUPS_PER_SLAB * ns, pair), 1))
    for b in range(npairs):
        spread = jnp.dot(hst_ref[...], rep_ref[:, b * pair:(b + 1) * pair], preferred_element_type=F32)
        hexp_ref[:, b * pair:(b + 1) * pair] = jnp.where(h_rows == h_cols, spread, 0.0).astype(BF16)

    hs_all = jnp.dot(ucat_ref[...], gexp_ref[...], preferred_element_type=F32)
    for b in range(npairs):
        intra_ref[:, b * pair:(b + 1) * pair] = jnp.dot(
            ucat_ref[:, :(b + 1) * pair], r_ref[(npairs - 1 - b) * pair:, :], preferred_element_type=F32)

    row = lax.broadcasted_iota(jnp.int32, (nchunk, ns), 0) % chunks_per_seq
    for g in range(GROUPS_PER_SLAB):
        hs = hs_all[:, g * ns:(g + 1) * ns]
        for k in range(scan_steps):
            d = 1 << k
            sh = jnp.where(row >= d, pltpu.roll(hs, d, axis=0), 0.0)
            hs = hs + sr_ref[g, k:k + 1, :] * sh + si_ref[g, k:k + 1, :] * pltpu.roll(sh, S5_STATE, axis=1)
        hprev_ref[:, g * ns:(g + 1) * ns] = jnp.where(row >= 1, pltpu.roll(hs, 1, axis=0), 0.0).astype(BF16)

    for b in range(npairs):
        y = intra_ref[:, b * pair:(b + 1) * pair] + jnp.dot(
            hprev_ref[...], hexp_ref[:, b * pair:(b + 1) * pair], preferred_element_type=F32)
        for bit in range(2):
            step = 2 * b + bit
            lanes = slice(step * V7X_LANES, (step + 1) * V7X_LANES)
            yl = y[:, bit * V7X_LANES:(bit + 1) * V7X_LANES] + d_ref[...] * ucat_ref[:, lanes].astype(F32)
            y_ref[pl.ds(step, nchunk, stride=SSM_CHUNK), :] = jax.nn.gelu(yl, approximate=True)


def ssm_apply(u, r_all, gst, hst, sr, si, d_skip, rep, *, seq, name="s5_ssm"):
    t, w = u.shape
    nslab = w // V7X_LANES
    nchunk = t // SSM_CHUNK
    cat = SSM_CHUNK * V7X_LANES
    ns = 2 * S5_STATE
    chunks_per_seq = seq // SSM_CHUNK
    scan_steps = (chunks_per_seq - 1).bit_length()
    gb = GROUPS_PER_SLAB
    rows = sr.shape[1]
    return pl.pallas_call(
        functools.partial(_ssm_kernel, chunks_per_seq=chunks_per_seq, scan_steps=scan_steps),
        grid=(nslab,),
        in_specs=[
            pl.BlockSpec((t, V7X_LANES), lambda s: (0, s)),
            pl.BlockSpec((None, cat, 2 * V7X_LANES), lambda s: (s, 0, 0)),
            pl.BlockSpec((None, SSM_CHUNK, V7X_LANES, ns), lambda s: (s, 0, 0, 0)),
            pl.BlockSpec((None, gb * ns, SSM_CHUNK * S5_GROUP), lambda s: (s, 0, 0)),
            pl.BlockSpec((gb, rows, ns), lambda s: (s, 0, 0)),
            pl.BlockSpec((gb, rows, ns), lambda s: (s, 0, 0)),
            pl.BlockSpec((1, V7X_LANES), lambda s: (0, s)),
            pl.BlockSpec((SSM_CHUNK * S5_GROUP, cat), lambda s: (0, 0)),
        ],
        out_specs=pl.BlockSpec((t, V7X_LANES), lambda s: (0, s)),
        out_shape=jax.ShapeDtypeStruct((t, w), F32),
        scratch_shapes=[pltpu.VMEM((nchunk, cat), BF16), pltpu.VMEM((cat, gb * ns), BF16),
                        pltpu.VMEM((gb * ns, cat), BF16), pltpu.VMEM((nchunk, gb * ns), BF16),
                        pltpu.VMEM((nchunk, cat), F32)],
        compiler_params=pltpu.CompilerParams(dimension_semantics=("parallel",),
                                             vmem_limit_bytes=SSM_VMEM_LIMIT),
        name=name,
    )(u, r_all, gst, hst, sr, si, d_skip, rep)


def _fgate_kernel(x_ref, gain_ref, wf_ref, bf_ref, cumt_ref, carry_ref, *, tiles_per_seq, heads):
    i = pl.program_id(0)

    @pl.when(i % tiles_per_seq == 0)
    def _():
        carry_ref[...] = jnp.zeros_like(carry_ref)

    xn = _rms_scale(x_ref[...], gain_ref[...]).astype(BF16)
    logit = jnp.dot(xn, wf_ref[...], preferred_element_type=F32) + bf_ref[...]
    c = -(jnp.maximum(-logit, 0.0) + jnp.log1p(jnp.exp(-jnp.abs(logit))))
    tm = c.shape[0]
    row = lax.broadcasted_iota(jnp.int32, c.shape, 0)
    d = 1
    while d < tm:
        c = c + jnp.where(row >= d, pltpu.roll(c, d, axis=0), 0.0)
        d *= 2
    c = c + carry_ref[...]
    carry_ref[...] = c[tm - 1:tm, :]
    cumt_ref[...] = (c * LOG2_E).T[:heads, :]


def forget_gate(x, gain, wf, bf, *, seq, heads, tm=512, name="fox_forget_gate"):
    t, d = x.shape
    tiles_per_seq = seq // tm
    return pl.pallas_call(
        functools.partial(_fgate_kernel, tiles_per_seq=tiles_per_seq, heads=heads),
        grid=(t // tm,),
        in_specs=[
            pl.BlockSpec((tm, d), lambda i: (i, 0)),
            pl.BlockSpec((1, d), lambda i: (0, 0)),
            pl.BlockSpec((d, V7X_LANES), lambda i: (0, 0)),
            pl.BlockSpec((1, V7X_LANES), lambda i: (0, 0)),
        ],
        out_specs=pl.BlockSpec((None, heads, tm), lambda i: (i // tiles_per_seq, 0, i % tiles_per_seq)),
        out_shape=jax.ShapeDtypeStruct((t // seq, heads, seq), F32),
        scratch_shapes=[pltpu.VMEM((1, V7X_LANES), F32)],
        compiler_params=_params("arbitrary"),
        name=name,
    )(x, gain, wf, bf)


ATTN_SPLIT = 3
ATTN_EXTRA_ROWS = 16


def _split3(c):
    hi = c.astype(BF16).astype(F32)
    mid = (c - hi).astype(BF16).astype(F32)
    return hi, mid, c - hi - mid


def _fox_attn_kernel(qt_ref, k_ref, vt_ref, ct_ref, sz_ref, o_ref,
                     qa_ref, ka_ref, va_ref, s0_ref, s1_ref, smax_ref, m_ref, acc_ref, *, tile):
    dh, seq = qt_ref.shape
    ntiles = seq // tile
    ext = ATTN_EXTRA_ROWS

    @pl.when((pl.program_id(0) == 0) & (pl.program_id(1) == 0))
    def _():
        qa_ref[dh + ext:, :] = jnp.zeros((dh - ext, seq), BF16)
        rowv = lax.broadcasted_iota(jnp.int32, (ext, seq), 0)
        va_ref[dh:, :] = jnp.where(rowv == 0, 1.0, 0.0).astype(BF16)

    parts = _split3(ct_ref[...])
    qa_ref[:dh, :] = qt_ref[...]
    row16 = lax.broadcasted_iota(jnp.int32, (ext, seq), 0)
    ext_q = jnp.where(row16 < 2 * ATTN_SPLIT, 1.0, 0.0)
    for n, part in enumerate(parts):
        ext_q = jnp.where(row16 == n, part, ext_q)
    qa_ref[dh:dh + ext, :] = ext_q.astype(BF16)
    va_ref[:dh, :] = vt_ref[...]

    rowk = lax.broadcasted_iota(jnp.int32, (dh, tile), 0)
    for r in range(ntiles):
        cols = slice(r * tile, (r + 1) * tile)
        ext_kt = jnp.where(rowk < ATTN_SPLIT, 1.0, 0.0)
        for n, part in enumerate(parts):
            ext_kt = jnp.where(rowk == ATTN_SPLIT + n, -part[:, cols], ext_kt)
        ka_ref[cols, :dh] = k_ref[cols, :]
        ka_ref[cols, dh:] = ext_kt.T.astype(BF16)

    causal = (lax.broadcasted_iota(jnp.int32, (tile, tile), 0) <= lax.broadcasted_iota(jnp.int32, (tile, tile), 1))
    s_slots = (s0_ref, s1_ref)

    def produce(slot, qi, kb):
        st = jnp.dot(ka_ref[kb * tile:(kb + 1) * tile, :], qa_ref[:, qi * tile:(qi + 1) * tile],
                     preferred_element_type=F32)
        s_slots[slot][...] = st
        smax_ref[slot:slot + 1, :] = jnp.max(st, axis=0, keepdims=True)

    def consume(slot, qi, kb):
        par = qi % 2
        st = s_slots[slot][...]
        if kb == qi:
            st = jnp.where(causal, st, NEG_INF)
            st_max = jnp.max(st, axis=0, keepdims=True)
        else:
            st_max = smax_ref[slot:slot + 1, :]
        pv = functools.partial(jnp.dot, va_ref[:, kb * tile:(kb + 1) * tile], preferred_element_type=F32)
        if kb == 0:
            m_ref[par:par + 1, :] = st_max
            acc_ref[par] = pv(jnp.exp2(st - st_max).astype(BF16))
        else:
            m = m_ref[par:par + 1, :]
            m_new = jnp.maximum(m, st_max)
            acc_ref[par] = jnp.exp2(m - m_new) * acc_ref[par] + pv(jnp.exp2(st - m_new).astype(BF16))
            m_ref[par:par + 1, :] = m_new

    def finalize(qi):
        acc = acc_ref[qi % 2]
        rows = slice(qi * tile, (qi + 1) * tile)
        o = (acc[:dh, :] / acc[dh:dh + 1, :]).T
        o_ref[rows, :] = (o * sz_ref[rows, :].astype(F32)).astype(o_ref.dtype)

    blocks = [(qi, kb) for qi in range(ntiles) for kb in range(qi + 1)]
    produce(0, *blocks[0])
    for n, (qi, kb) in enumerate(blocks):
        if n + 1 < len(blocks):
            produce((n + 1) % 2, *blocks[n + 1])
        consume(n % 2, qi, kb)
        if kb == qi:
            finalize(qi)


def fox_attention(qt, k, vt, cumt2, sz, *, batch, seq, heads, tile=512, name="fox_attention"):
    width, t = qt.shape
    dh = width // heads
    natural = pl.BlockSpec((seq, dh), lambda b, h: (b, h))
    transposed = pl.BlockSpec((dh, seq), lambda b, h: (h, b))
    return pl.pallas_call(
        functools.partial(_fox_attn_kernel, tile=tile),
        grid=(batch, heads),
        in_specs=[transposed, natural, transposed,
                  pl.BlockSpec((None, None, 1, seq), lambda b, h: (b, h, 0, 0)),
                  natural],
        out_specs=natural,
        out_shape=jax.ShapeDtypeStruct((t, width), BF16),
        scratch_shapes=[pltpu.VMEM((2 * dh, seq), BF16), pltpu.VMEM((seq, 2 * dh), BF16),
                        pltpu.VMEM((dh + ATTN_EXTRA_ROWS, seq), BF16),
                        pltpu.VMEM((tile, tile), F32), pltpu.VMEM((tile, tile), F32),
                        pltpu.VMEM((V7X_SUBLANES, tile), F32), pltpu.VMEM((V7X_SUBLANES, tile), F32),
                        pltpu.VMEM((2, dh + ATTN_EXTRA_ROWS, tile), F32)],
        compiler_params=_params("arbitrary", "arbitrary"),
        name=name,
    )(qt, k, vt, cumt2, sz)


def _identity(v):
    return v


def _doubled(v):
    return jnp.concatenate([v, v], axis=-1)[:, None, :].astype(F32)


def kernel(x, norm_pre, norm_post, s5_w_in, s5_a_re, s5_a_im, s5_log_dt, s5_b_re, s5_b_im, s5_c_re, s5_c_im,
           s5_d, s5_w_glu, s5_b_glu, s5_w_out, kv_norm, kv_w, kv_b_f, fox_w_in, fox_w_out):
    batch, seq, d_model = x.shape
    t = batch * seq
    heads = kv_b_f.shape[0]
    fox_width = fox_w_in.shape[-1] // 2
    assert norm_pre.shape[0] == 2 and s5_w_in.shape[0] == 1 and fox_w_in.shape[0] == 1
    assert fox_width // heads == FOX_HEAD_DIM and heads <= V7X_LANES
    ngroups = s5_a_re.shape[1]
    chunks_per_seq = seq // SSM_CHUNK
    scan_steps = (chunks_per_seq - 1).bit_length()

    h0 = x.reshape(t, d_model)

    s5_width = s5_w_in.shape[-1] // 2
    w_in = s5_w_in[0].astype(BF16)
    u, sz = norm_proj(h0, norm_pre[0][None, :], [(w_in, 0), (w_in, s5_width)], (_identity, jax.nn.silu),
                      (F32, BF16), n=s5_width, name="s5_in_proj")
    ldt2 = jnp.broadcast_to(s5_log_dt[0][:, None, None], (ngroups, 1, 2 * S5_STATE)).astype(F32)
    b2 = jnp.concatenate([jnp.swapaxes(s5_b_re[0], 1, 2), jnp.swapaxes(s5_b_im[0], 1, 2)], axis=-1)
    c2 = jnp.concatenate([s5_c_re[0], s5_c_im[0]], axis=-1)
    rep = replication_matrix()
    r_all, gst, hst, sr, si = ssm_prep(_doubled(s5_a_re[0]), _doubled(s5_a_im[0]), ldt2, b2.astype(F32),
                                       c2.astype(F32), rep, scan_steps=scan_steps)
    yg = ssm_apply(u, r_all, gst, hst, sr, si, s5_d[0][None, :].astype(F32), rep, seq=seq)
    y3 = glu_gate(yg, s5_w_glu[0].astype(BF16), s5_b_glu[0][None, :], sz)
    h1 = out_proj(y3, s5_w_out[0].astype(BF16), h0, norm_post[0][None, :], name="s5_out_proj")

    kv_cols = 2 * fox_width
    w_k = kv_w[:, :fox_width].astype(BF16)
    w_vt = kv_w[:, fox_width:kv_cols].T.astype(BF16)
    k, vt = norm_proj(h1, kv_norm[None, :], [(w_k, 0), (w_vt, 0)], (_identity, _identity), (BF16, BF16),
                      n=fox_width, transposed=(False, True), name="fox_kv_proj")
    wf = jnp.pad(kv_w[:, kv_cols:], ((0, 0), (0, V7X_LANES - heads))).astype(BF16)
    bf = jnp.pad(kv_b_f, (0, V7X_LANES - heads))[None, :].astype(F32)
    cumt2 = forget_gate(h1, kv_norm[None, :], wf, bf, seq=seq, heads=heads)
    scale = FOX_HEAD_DIM ** -0.5 * LOG2_E
    w_qt = fox_w_in[0][:, :fox_width].T.astype(BF16)
    w_z = fox_w_in[0][:, fox_width:].astype(BF16)
    qt, sz2 = norm_proj(h1, norm_pre[1][None, :], [(w_qt, 0), (w_z, 0)], (lambda v: v * scale, jax.nn.silu),
                        (BF16, BF16), n=fox_width, transposed=(True, False), name="fox_in_proj")
    o = fox_attention(qt, k, vt, cumt2.reshape(batch, heads, 1, seq), sz2,
                      batch=batch, seq=seq, heads=heads)
    h2 = out_proj(o, fox_w_out[0].astype(BF16), h1, norm_post[1][None, :], name="fox_out_proj")
    return h2.reshape(batch, seq, d_model)
```

```python
import functools
import math
from typing import Callable, NamedTuple

import jax
import jax.numpy as jnp
from jax import lax
from jax.experimental import pallas as pl
from jax.experimental.pallas import tpu as pltpu

F32 = jnp.float32
BF16 = jnp.bfloat16

RMS_EPS = 1e-6
NEG_INF = -1e30
LOG2_E = math.log2(math.e)

S5_GROUP = 16
S5_STATE = 64
FOX_HEAD_DIM = 128

V7X_LANES = 128
V7X_SUBLANES = 8
SSM_CHUNK = 16
GROUPS_PER_SLAB = V7X_LANES // S5_GROUP
VMEM_LIMIT = 48 * 1024 * 1024
SSM_VMEM_LIMIT = 56 * 1024 * 1024
FOX_PROJ_VMEM_LIMIT = 56 * 1024 * 1024


def _params(*semantics):
    return pltpu.CompilerParams(dimension_semantics=semantics, vmem_limit_bytes=VMEM_LIMIT)


def _rms_scale(x, gain):
    ms = jnp.mean(x * x, axis=-1, keepdims=True)
    return x * lax.rsqrt(ms + RMS_EPS) * gain


class Proj(NamedTuple):
    weight_t: jax.Array
    first_row: int
    gain: int
    act: Callable
    dtype: jnp.dtype
    transposed: bool


def _log_sigmoid(x):
    return -(jnp.maximum(-x, 0.0) + jnp.log1p(jnp.exp(-jnp.abs(x))))


def _cumsum_rows(c):
    rows = c.shape[0]
    row = lax.broadcasted_iota(jnp.int32, c.shape, 0)
    d = 1
    while d < rows:
        c = c + jnp.where(row >= d, pltpu.roll(c, d, axis=0), 0.0)
        d *= 2
    return c


def _fused_proj_kernel(*refs, projs, ngains, gate):
    nproj = len(projs)
    x_ref, gains_ref = refs[:2]
    w_refs = refs[2:2 + nproj]
    pos = 2 + nproj
    if gate is not None:
        wf_ref, bf_ref = refs[pos:pos + 2]
        pos += 2
    o_refs = refs[pos:pos + nproj]
    pos += nproj
    if gate is not None:
        cumt_ref = refs[pos]
        pos += 1
    xn_ref = refs[pos]
    nt = (((1,), (1,)), ((), ()))

    @pl.when(pl.program_id(1) == 0)
    def _():
        x = x_ref[...]
        base = x * lax.rsqrt(jnp.mean(x * x, axis=-1, keepdims=True) + RMS_EPS)
        for g in range(ngains):
            xn_ref[g] = (base * gains_ref[g:g + 1, :]).astype(BF16)
        if gate is not None:
            carry_ref = refs[pos + 1]
            i = pl.program_id(0)

            @pl.when(i % gate.tiles_per_seq == 0)
            def _():
                carry_ref[...] = jnp.zeros_like(carry_ref)

            logit = lax.dot_general(xn_ref[gate.gain], wf_ref[...], nt, preferred_element_type=F32) + bf_ref[...]
            c = _cumsum_rows(_log_sigmoid(logit)) + carry_ref[...]
            carry_ref[...] = c[c.shape[0] - 1:, :]
            cumt_ref[...] = (c * LOG2_E).T[:gate.heads, :]

    for p, w_ref, o_ref in zip(projs, w_refs, o_refs):
        xn = xn_ref[p.gain]
        if p.transposed:
            y = lax.dot_general(w_ref[...], xn, nt, preferred_element_type=F32)
        else:
            y = lax.dot_general(xn, w_ref[...], nt, preferred_element_type=F32)
        o_ref[...] = p.act(y).astype(o_ref.dtype)


class Gate(NamedTuple):
    wf_t: jax.Array
    bias: jax.Array
    gain: int
    heads: int
    seq: int
    tiles_per_seq: int = 0


def fused_proj(x, gains, projs, *, n, gate=None, tm=1024, tn=512, vmem_limit=VMEM_LIMIT, name="fused_proj"):
    t, d = x.shape
    ngains = gains.shape[0]
    w_specs, out_specs, out_shapes = [], [], []
    for p in projs:
        w_specs.append(pl.BlockSpec((tn, d), lambda i, j, b0=p.first_row // tn: (j + b0, 0)))
        if p.transposed:
            out_specs.append(pl.BlockSpec((tn, tm), lambda i, j: (j, i)))
            out_shapes.append(jax.ShapeDtypeStruct((n, t), p.dtype))
        else:
            out_specs.append(pl.BlockSpec((tm, tn), lambda i, j: (i, j)))
            out_shapes.append(jax.ShapeDtypeStruct((t, n), p.dtype))
    operands = [x, gains] + [p.weight_t for p in projs]
    in_specs = [pl.BlockSpec((tm, d), lambda i, j: (i, 0)), pl.BlockSpec((ngains, d), lambda i, j: (0, 0))] + w_specs
    scratch = [pltpu.VMEM((ngains, tm, d), BF16)]
    kernel_projs = tuple(p._replace(weight_t=None) for p in projs)
    kernel_gate = None
    if gate is not None:
        tps = gate.seq // tm
        kernel_gate = gate._replace(wf_t=None, bias=None, tiles_per_seq=tps)
        operands += [gate.wf_t, gate.bias]
        in_specs += [pl.BlockSpec((V7X_LANES, d), lambda i, j: (0, 0)), pl.BlockSpec((1, V7X_LANES), lambda i, j: (0, 0))]
        out_specs.append(pl.BlockSpec((None, gate.heads, tm), lambda i, j: (i // tps, 0, i % tps)))
        out_shapes.append(jax.ShapeDtypeStruct((t // gate.seq, gate.heads, gate.seq), F32))
        scratch.append(pltpu.VMEM((1, V7X_LANES), F32))
    return pl.pallas_call(
        functools.partial(_fused_proj_kernel, projs=kernel_projs, ngains=ngains, gate=kernel_gate),
        grid=(t // tm, n // tn),
        in_specs=in_specs,
        out_specs=out_specs,
        out_shape=out_shapes,
        scratch_shapes=scratch,
        compiler_params=pltpu.CompilerParams(dimension_semantics=("arbitrary", "arbitrary"),
                                             vmem_limit_bytes=vmem_limit),
        name=name,
    )(*operands)


def _out_proj_kernel(a_ref, w_ref, res_ref, gain_ref, o_ref):
    y = jnp.dot(a_ref[...], w_ref[...], preferred_element_type=F32)
    o_ref[...] = res_ref[...] + _rms_scale(y, gain_ref[...])


def out_proj(a, w, res, gain, *, tm=512, name="out_proj"):
    t, k = a.shape
    d = w.shape[1]
    return pl.pallas_call(
        _out_proj_kernel,
        grid=(t // tm,),
        in_specs=[
            pl.BlockSpec((tm, k), lambda i: (i, 0)),
            pl.BlockSpec((k, d), lambda i: (0, 0)),
            pl.BlockSpec((tm, d), lambda i: (i, 0)),
            pl.BlockSpec((1, d), lambda i: (0, 0)),
        ],
        out_specs=pl.BlockSpec((tm, d), lambda i: (i, 0)),
        out_shape=jax.ShapeDtypeStruct((t, d), F32),
        compiler_params=_params("parallel"),
        name=name,
    )(a, w, res, gain)


def _glu_kernel(yg_ref, w_ref, b_ref, sz_ref, o_ref, yb_ref, *, tn):
    j = pl.program_id(1)

    @pl.when(j == 0)
    def _():
        yb_ref[...] = yg_ref[...].astype(BF16)

    gate = jax.nn.sigmoid(jnp.dot(yb_ref[...], w_ref[...], preferred_element_type=F32) + b_ref[...])
    yg = yg_ref[:, pl.ds(pl.multiple_of(j * tn, tn), tn)]
    o_ref[...] = (yg * gate * sz_ref[...].astype(F32)).astype(o_ref.dtype)


def glu_gate(yg, w, b, sz, *, tm=1024, tn=512, name="s5_glu"):
    t, k = yg.shape
    n = w.shape[1]
    return pl.pallas_call(
        functools.partial(_glu_kernel, tn=tn),
        grid=(t // tm, n // tn),
        in_specs=[
            pl.BlockSpec((tm, k), lambda i, j: (i, 0)),
            pl.BlockSpec((k, tn), lambda i, j: (0, j)),
            pl.BlockSpec((1, tn), lambda i, j: (0, j)),
            pl.BlockSpec((tm, tn), lambda i, j: (i, j)),
        ],
        out_specs=pl.BlockSpec((tm, tn), lambda i, j: (i, j)),
        out_shape=jax.ShapeDtypeStruct((t, n), BF16),
        scratch_shapes=[pltpu.VMEM((tm, k), BF16)],
        compiler_params=_params("parallel", "arbitrary"),
        name=name,
    )(yg, w, b, sz)


def _ssm_prep_kernel(are_ref, aim_ref, ldt_ref, b2_ref, c2_ref, rep_ref,
                     r_ref, g_ref, h_ref, sr_ref, si_ref, *, groups, scan_steps):
    lane = lax.broadcasted_iota(jnp.int32, (1, 2 * S5_STATE), 1)
    minus_plus = jnp.where(lane < S5_STATE, -1.0, 1.0).astype(F32)
    gw = S5_GROUP
    ns = 2 * S5_STATE
    krows = []

    for gi in range(groups):
        ar = are_ref[gi]
        ai = aim_ref[gi]
        dt = jnp.exp(ldt_ref[gi])
        mag = jnp.exp(ar * dt)
        lam_r = mag * jnp.cos(ai * dt)
        lam_i = mag * jnp.sin(ai * dt)
        den = ar * ar + ai * ai
        nr = lam_r - 1.0
        coef_r = (nr * ar + lam_i * ai) / den
        coef_i = (lam_i * ar - nr * ai) / den

        pow_r = [jnp.ones_like(lam_r)]
        pow_i = [jnp.zeros_like(lam_r)]
        for _ in range(SSM_CHUNK):
            pr, pi = pow_r[-1], pow_i[-1]
            pow_r.append(pr * lam_r - pi * lam_i)
            pow_i.append(pr * lam_i + pi * lam_r)

        b2 = b2_ref[gi]
        b2s = pltpu.roll(b2, S5_STATE, axis=1) * minus_plus
        c2 = c2_ref[gi]
        c2a = c2 * (-minus_plus)
        c2b = -pltpu.roll(c2, S5_STATE, axis=1)

        for step in range(SSM_CHUNK):
            pr, pi = pow_r[SSM_CHUNK - 1 - step], pow_i[SSM_CHUNK - 1 - step]
            wr = pr * coef_r - pi * coef_i
            wi = pr * coef_i + pi * coef_r
            g_ref[step, gi * gw:(gi + 1) * gw, :] = (wr * b2 + wi * b2s).astype(g_ref.dtype)

        cl = [pow_r[tau] * c2a + pow_i[tau] * c2b for tau in range(SSM_CHUNK + 1)]
        h_t = jnp.concatenate(cl[1:], axis=0)
        h_ref[gi * ns:(gi + 1) * ns, :] = h_t.T.astype(h_ref.dtype)

        bbar2 = coef_r * b2 + coef_i * b2s
        cl_all = jnp.concatenate(cl[:-1], axis=0)
        krows.append(lax.dot_general(bbar2, cl_all, (((1,), (1,)), ((), ())),
                                     precision=lax.Precision.HIGHEST,
                                     preferred_element_type=F32))

        mu_r, mu_i = pow_r[SSM_CHUNK], pow_i[SSM_CHUNK]
        sr_rows, si_rows = [], []
        for _ in range(scan_steps):
            sr_rows.append(mu_r)
            si_rows.append(mu_i * minus_plus)
            mu_r, mu_i = mu_r * mu_r - mu_i * mu_i, 2.0 * mu_r * mu_i
        pad = [jnp.zeros_like(mu_r)] * (sr_ref.shape[1] - scan_steps)
        sr_ref[gi] = jnp.concatenate(sr_rows + pad, axis=0)
        si_ref[gi] = jnp.concatenate(si_rows + pad, axis=0)

    kst = jnp.concatenate(krows, axis=0).astype(BF16)
    spread = jnp.dot(kst, rep_ref[...], preferred_element_type=F32)
    rows_g = lax.broadcasted_iota(jnp.int32, spread.shape, 0) // gw
    cols_g = _group_of_lane(lax.broadcasted_iota(jnp.int32, spread.shape, 1))
    spread = jnp.where(rows_g == cols_g, spread, 0.0).astype(r_ref.dtype)
    blocks = [spread[:, tau * V7X_LANES:(tau + 1) * V7X_LANES] for tau in range(SSM_CHUNK)]
    zero = jnp.zeros_like(blocks[0])
    npairs = SSM_CHUNK // 2
    for d in range(npairs):
        base = (npairs - 1 - d) * 2 * V7X_LANES
        top = jnp.concatenate([blocks[2 * d], blocks[2 * d + 1]], axis=1)
        bottom = jnp.concatenate([blocks[2 * d - 1] if d else zero, blocks[2 * d]], axis=1)
        r_ref[base:base + V7X_LANES, :] = top
        r_ref[base + V7X_LANES:base + 2 * V7X_LANES, :] = bottom


def ssm_prep(a_re2, a_im2, log_dt2, b2, c2, rep, *, scan_steps, name="ssm_prep"):
    ng = a_re2.shape[0]
    gb = GROUPS_PER_SLAB
    nslab = ng // gb
    cat = SSM_CHUNK * V7X_LANES
    ns = 2 * S5_STATE
    rows = -(-scan_steps // V7X_SUBLANES) * V7X_SUBLANES
    vec = pl.BlockSpec((gb, 1, ns), lambda i: (i, 0, 0))
    mat = pl.BlockSpec((gb, S5_GROUP, ns), lambda i: (i, 0, 0))
    return pl.pallas_call(
        functools.partial(_ssm_prep_kernel, groups=gb, scan_steps=scan_steps),
        grid=(nslab,),
        in_specs=[vec, vec, vec, mat, mat, pl.BlockSpec(rep.shape, lambda i: (0, 0))],
        out_specs=[
            pl.BlockSpec((None, cat, 2 * V7X_LANES), lambda i: (i, 0, 0)),
            pl.BlockSpec((None, SSM_CHUNK, V7X_LANES, ns), lambda i: (i, 0, 0, 0)),
            pl.BlockSpec((None, gb * ns, SSM_CHUNK * S5_GROUP), lambda i: (i, 0, 0)),
            pl.BlockSpec((gb, rows, ns), lambda i: (i, 0, 0)),
            pl.BlockSpec((gb, rows, ns), lambda i: (i, 0, 0)),
        ],
        out_shape=[
            jax.ShapeDtypeStruct((nslab, cat, 2 * V7X_LANES), BF16),
            jax.ShapeDtypeStruct((nslab, SSM_CHUNK, V7X_LANES, ns), BF16),
            jax.ShapeDtypeStruct((nslab, gb * ns, SSM_CHUNK * S5_GROUP), BF16),
            jax.ShapeDtypeStruct((ng, rows, ns), F32),
            jax.ShapeDtypeStruct((ng, rows, ns), F32),
        ],
        compiler_params=_params("parallel"),
        name=name,
    )(a_re2, a_im2, log_dt2, b2, c2, rep)


def replication_matrix():
    src = jnp.arange(SSM_CHUNK * S5_GROUP)
    dst = jnp.arange(SSM_CHUNK * V7X_LANES)
    same_step = (src[:, None] // S5_GROUP) == (dst[None, :] // V7X_LANES)
    same_chan = (src[:, None] % S5_GROUP) == (dst[None, :] % S5_GROUP)
    return (same_step & same_chan).astype(BF16)


def _group_of_lane(idx):
    return (idx % V7X_LANES) // S5_GROUP


def _ssm_kernel(u_ref, r_ref, gst_ref, hst_ref, sr_ref, si_ref, d_ref, rep_ref, y_ref,
                ucat_ref, gexp_ref, hexp_ref, hprev_ref, intra_ref, *, chunks_per_seq, scan_steps):
    t = u_ref.shape[0]
    nchunk = t // SSM_CHUNK
    ns = 2 * S5_STATE
    pair = 2 * V7X_LANES
    npairs = SSM_CHUNK // 2

    for step in range(SSM_CHUNK):
        x = u_ref[pl.ds(step, nchunk, stride=SSM_CHUNK), :]
        ucat_ref[:, step * V7X_LANES:(step + 1) * V7X_LANES] = x.astype(BF16)

    g_rows = lax.broadcasted_iota(jnp.int32, (V7X_LANES, GROUPS_PER_SLAB * ns), 0) // S5_GROUP
    g_cols = lax.broadcasted_iota(jnp.int32, (V7X_LANES, GROUPS_PER_SLAB * ns), 1) // ns
    for step in range(SSM_CHUNK):
        tiled = jnp.concatenate([gst_ref[step]] * GROUPS_PER_SLAB, axis=1)
        gexp_ref[step * V7X_LANES:(step + 1) * V7X_LANES, :] = jnp.where(g_rows == g_cols, tiled, 0.0).astype(BF16)

    h_rows = lax.broadcasted_iota(jnp.int32, (GROUPS_PER_SLAB * ns, pair), 0) // ns
    h_cols = _group_of_lane(lax.broadcasted_iota(jnp.int32, (GROUPS_PER_SLAB * ns, pair), 1))
    for b in range(npairs):
        spread = jnp.dot(hst_ref[...], rep_ref[:, b * pair:(b + 1) * pair], preferred_element_type=F32)
        hexp_ref[:, b * pair:(b + 1) * pair] = jnp.where(h_rows == h_cols, spread, 0.0).astype(BF16)

    hs_all = jnp.dot(ucat_ref[...], gexp_ref[...], preferred_element_type=F32)
    for b in range(npairs):
        intra_ref[:, b * pair:(b + 1) * pair] = jnp.dot(
            ucat_ref[:, :(b + 1) * pair], r_ref[(npairs - 1 - b) * pair:, :], preferred_element_type=F32)

    row = lax.broadcasted_iota(jnp.int32, (nchunk, ns), 0) % chunks_per_seq
    for g in range(GROUPS_PER_SLAB):
        hs = hs_all[:, g * ns:(g + 1) * ns]
        for k in range(scan_steps):
            d = 1 << k
            sh = jnp.where(row >= d, pltpu.roll(hs, d, axis=0), 0.0)
            hs = hs + sr_ref[g, k:k + 1, :] * sh + si_ref[g, k:k + 1, :] * pltpu.roll(sh, S5_STATE, axis=1)
        hprev_ref[:, g * ns:(g + 1) * ns] = jnp.where(row >= 1, pltpu.roll(hs, 1, axis=0), 0.0).astype(BF16)

    for b in range(npairs):
        y = intra_ref[:, b * pair:(b + 1) * pair] + jnp.dot(
            hprev_ref[...], hexp_ref[:, b * pair:(b + 1) * pair], preferred_element_type=F32)
        for bit in range(2):
            step = 2 * b + bit
            lanes = slice(step * V7X_LANES, (step + 1) * V7X_LANES)
            yl = y[:, bit * V7X_LANES:(bit + 1) * V7X_LANES] + d_ref[...] * ucat_ref[:, lanes].astype(F32)
            y_ref[pl.ds(step, nchunk, stride=SSM_CHUNK), :] = jax.nn.gelu(yl, approximate=True)


def ssm_apply(u, r_all, gst, hst, sr, si, d_skip, rep, *, seq, name="s5_ssm"):
    t, w = u.shape
    nslab = w // V7X_LANES
    nchunk = t // SSM_CHUNK
    cat = SSM_CHUNK * V7X_LANES
    ns = 2 * S5_STATE
    chunks_per_seq = seq // SSM_CHUNK
    scan_steps = (chunks_per_seq - 1).bit_length()
    gb = GROUPS_PER_SLAB
    rows = sr.shape[1]
    return pl.pallas_call(
        functools.partial(_ssm_kernel, chunks_per_seq=chunks_per_seq, scan_steps=scan_steps),
        grid=(nslab,),
        in_specs=[
            pl.BlockSpec((t, V7X_LANES), lambda s: (0, s)),
            pl.BlockSpec((None, cat, 2 * V7X_LANES), lambda s: (s, 0, 0)),
            pl.BlockSpec((None, SSM_CHUNK, V7X_LANES, ns), lambda s: (s, 0, 0, 0)),
            pl.BlockSpec((None, gb * ns, SSM_CHUNK * S5_GROUP), lambda s: (s, 0, 0)),
            pl.BlockSpec((gb, rows, ns), lambda s: (s, 0, 0)),
            pl.BlockSpec((gb, rows, ns), lambda s: (s, 0, 0)),
            pl.BlockSpec((1, V7X_LANES), lambda s: (0, s)),
            pl.BlockSpec((SSM_CHUNK * S5_GROUP, cat), lambda s: (0, 0)),
        ],
        out_specs=pl.BlockSpec((t, V7X_LANES), lambda s: (0, s)),
        out_shape=jax.ShapeDtypeStruct((t, w), F32),
        scratch_shapes=[pltpu.VMEM((nchunk, cat), BF16), pltpu.VMEM((cat, gb * ns), BF16),
                        pltpu.VMEM((gb * ns, cat), BF16), pltpu.VMEM((nchunk, gb * ns), BF16),
                        pltpu.VMEM((nchunk, cat), F32)],
        compiler_params=pltpu.CompilerParams(dimension_semantics=("parallel",),
                                             vmem_limit_bytes=SSM_VMEM_LIMIT),
        name=name,
    )(u, r_all, gst, hst, sr, si, d_skip, rep)


ATTN_SPLIT = 3
ATTN_EXTRA_ROWS = 16


def _split3(c):
    hi = c.astype(BF16).astype(F32)
    mid = (c - hi).astype(BF16).astype(F32)
    return hi, mid, c - hi - mid


def _fox_attn_kernel(qt_ref, k_ref, vt_ref, ct_ref, sz_ref, o_ref,
                     qa_ref, ka_ref, va_ref, s0_ref, s1_ref, smax_ref, m_ref, acc_ref, *, tile):
    dh, seq = qt_ref.shape
    ntiles = seq // tile
    ext = ATTN_EXTRA_ROWS

    @pl.when((pl.program_id(0) == 0) & (pl.program_id(1) == 0))
    def _():
        qa_ref[dh + ext:, :] = jnp.zeros((dh - ext, seq), BF16)
        rowv = lax.broadcasted_iota(jnp.int32, (ext, seq), 0)
        va_ref[dh:, :] = jnp.where(rowv == 0, 1.0, 0.0).astype(BF16)

    parts = _split3(ct_ref[...])
    qa_ref[:dh, :] = qt_ref[...]
    row16 = lax.broadcasted_iota(jnp.int32, (ext, seq), 0)
    ext_q = jnp.where(row16 < 2 * ATTN_SPLIT, 1.0, 0.0)
    for n, part in enumerate(parts):
        ext_q = jnp.where(row16 == n, part, ext_q)
    qa_ref[dh:dh + ext, :] = ext_q.astype(BF16)
    va_ref[:dh, :] = vt_ref[...]

    rowk = lax.broadcasted_iota(jnp.int32, (dh, tile), 0)
    for r in range(ntiles):
        cols = slice(r * tile, (r + 1) * tile)
        ext_kt = jnp.where(rowk < ATTN_SPLIT, 1.0, 0.0)
        for n, part in enumerate(parts):
            ext_kt = jnp.where(rowk == ATTN_SPLIT + n, -part[:, cols], ext_kt)
        ka_ref[cols, :dh] = k_ref[cols, :]
        ka_ref[cols, dh:] = ext_kt.T.astype(BF16)

    causal = (lax.broadcasted_iota(jnp.int32, (tile, tile), 0) <= lax.broadcasted_iota(jnp.int32, (tile, tile), 1))
    s_slots = (s0_ref, s1_ref)

    def produce(slot, qi, kb):
        st = jnp.dot(ka_ref[kb * tile:(kb + 1) * tile, :], qa_ref[:, qi * tile:(qi + 1) * tile],
                     preferred_element_type=F32)
        s_slots[slot][...] = st
        smax_ref[slot:slot + 1, :] = jnp.max(st, axis=0, keepdims=True)

    def consume(slot, qi, kb):
        par = qi % 2
        st = s_slots[slot][...]
        if kb == qi:
            st = jnp.where(causal, st, NEG_INF)
            st_max = jnp.max(st, axis=0, keepdims=True)
        else:
            st_max = smax_ref[slot:slot + 1, :]
        pv = functools.partial(jnp.dot, va_ref[:, kb * tile:(kb + 1) * tile], preferred_element_type=F32)
        if kb == 0:
            m_ref[par:par + 1, :] = st_max
            acc_ref[par] = pv(jnp.exp2(st - st_max).astype(BF16))
        else:
            m = m_ref[par:par + 1, :]
            m_new = jnp.maximum(m, st_max)
            acc_ref[par] = jnp.exp2(m - m_new) * acc_ref[par] + pv(jnp.exp2(st - m_new).astype(BF16))
            m_ref[par:par + 1, :] = m_new

    def finalize(qi):
        acc = acc_ref[qi % 2]
        rows = slice(qi * tile, (qi + 1) * tile)
        o = (acc[:dh, :] / acc[dh:dh + 1, :]).T
        o_ref[rows, :] = (o * sz_ref[rows, :].astype(F32)).astype(o_ref.dtype)

    blocks = [(qi, kb) for qi in range(ntiles) for kb in range(qi + 1)]
    produce(0, *blocks[0])
    for n, (qi, kb) in enumerate(blocks):
        if n + 1 < len(blocks):
            produce((n + 1) % 2, *blocks[n + 1])
        consume(n % 2, qi, kb)
        if kb == qi:
            finalize(qi)


def fox_attention(qt, k, vt, cumt2, sz, *, batch, seq, heads, tile=512, name="fox_attention"):
    width, t = qt.shape
    dh = width // heads
    natural = pl.BlockSpec((seq, dh), lambda b, h: (b, h))
    transposed = pl.BlockSpec((dh, seq), lambda b, h: (h, b))
    return pl.pallas_call(
        functools.partial(_fox_attn_kernel, tile=tile),
        grid=(batch, heads),
        in_specs=[transposed, natural, transposed,
                  pl.BlockSpec((None, None, 1, seq), lambda b, h: (b, h, 0, 0)),
                  natural],
        out_specs=natural,
        out_shape=jax.ShapeDtypeStruct((t, width), BF16),
        scratch_shapes=[pltpu.VMEM((2 * dh, seq), BF16), pltpu.VMEM((seq, 2 * dh), BF16),
                        pltpu.VMEM((dh + ATTN_EXTRA_ROWS, seq), BF16),
                        pltpu.VMEM((tile, tile), F32), pltpu.VMEM((tile, tile), F32),
                        pltpu.VMEM((V7X_SUBLANES, tile), F32), pltpu.VMEM((V7X_SUBLANES, tile), F32),
                        pltpu.VMEM((2, dh + ATTN_EXTRA_ROWS, tile), F32)],
        compiler_params=_params("arbitrary", "arbitrary"),
        name=name,
    )(qt, k, vt, cumt2, sz)


def _identity(v):
    return v


def _doubled(v):
    return jnp.concatenate([v, v], axis=-1)[:, None, :].astype(F32)


def kernel(x, norm_pre, norm_post, s5_w_in, s5_a_re, s5_a_im, s5_log_dt, s5_b_re, s5_b_im, s5_c_re, s5_c_im,
           s5_d, s5_w_glu, s5_b_glu, s5_w_out, kv_norm, kv_w, kv_b_f, fox_w_in, fox_w_out):
    batch, seq, d_model = x.shape
    t = batch * seq
    heads = kv_b_f.shape[0]
    fox_width = fox_w_in.shape[-1] // 2
    assert norm_pre.shape[0] == 2 and s5_w_in.shape[0] == 1 and fox_w_in.shape[0] == 1
    assert fox_width // heads == FOX_HEAD_DIM and heads <= V7X_LANES
    ngroups = s5_a_re.shape[1]
    chunks_per_seq = seq // SSM_CHUNK
    scan_steps = (chunks_per_seq - 1).bit_length()

    h0 = x.reshape(t, d_model)

    s5_width = s5_w_in.shape[-1] // 2
    w_in_t = s5_w_in[0].T.astype(BF16)
    u, sz = fused_proj(h0, norm_pre[0][None, :],
                       [Proj(w_in_t, 0, 0, _identity, F32, False), Proj(w_in_t, s5_width, 0, jax.nn.silu, BF16, False)],
                       n=s5_width, name="s5_in_proj")
    ldt2 = jnp.broadcast_to(s5_log_dt[0][:, None, None], (ngroups, 1, 2 * S5_STATE)).astype(F32)
    b2 = jnp.concatenate([jnp.swapaxes(s5_b_re[0], 1, 2), jnp.swapaxes(s5_b_im[0], 1, 2)], axis=-1)
    c2 = jnp.concatenate([s5_c_re[0], s5_c_im[0]], axis=-1)
    rep = replication_matrix()
    r_all, gst, hst, sr, si = ssm_prep(_doubled(s5_a_re[0]), _doubled(s5_a_im[0]), ldt2, b2.astype(F32),
                                       c2.astype(F32), rep, scan_steps=scan_steps)
    yg = ssm_apply(u, r_all, gst, hst, sr, si, s5_d[0][None, :].astype(F32), rep, seq=seq)
    y3 = glu_gate(yg, s5_w_glu[0].astype(BF16), s5_b_glu[0][None, :], sz)
    h1 = out_proj(y3, s5_w_out[0].astype(BF16), h0, norm_post[0][None, :], name="s5_out_proj")

    kv_cols = 2 * fox_width
    w_kv_t = kv_w.T.astype(BF16)
    wf_t = jnp.pad(w_kv_t[kv_cols:], ((0, V7X_LANES - heads), (0, 0)))
    bf = jnp.pad(kv_b_f, (0, V7X_LANES - heads))[None, :].astype(F32)
    w_in_fox_t = fox_w_in[0].T.astype(BF16)
    scale = FOX_HEAD_DIM ** -0.5 * LOG2_E
    k, vt, qt, sz2, cumt2 = fused_proj(
        h1, jnp.stack([kv_norm, norm_pre[1]]),
        [Proj(w_kv_t, 0, 0, _identity, BF16, False), Proj(w_kv_t, fox_width, 0, _identity, BF16, True),
         Proj(w_in_fox_t, 0, 1, lambda v: v * scale, BF16, True), Proj(w_in_fox_t, fox_width, 1, jax.nn.silu, BF16, False)],
        n=fox_width, gate=Gate(wf_t, bf, 0, heads, seq), tm=512, vmem_limit=FOX_PROJ_VMEM_LIMIT, name="fox_proj")
    o = fox_attention(qt, k, vt, cumt2.reshape(batch, heads, 1, seq), sz2,
                      batch=batch, seq=seq, heads=heads)
    h2 = out_proj(o, fox_w_out[0].astype(BF16), h1, norm_post[1][None, :], name="fox_out_proj")
    return h2.reshape(batch, seq, d_model)
```

```python
import functools
import math
from typing import Callable, NamedTuple

import jax
import jax.numpy as jnp
from jax import lax
from jax.experimental import pallas as pl
from jax.experimental.pallas import tpu as pltpu

F32 = jnp.float32
BF16 = jnp.bfloat16

RMS_EPS = 1e-6
NEG_INF = -1e30
LOG2_E = math.log2(math.e)

S5_GROUP = 16
S5_STATE = 64
FOX_HEAD_DIM = 128

V7X_LANES = 128
V7X_SUBLANES = 8
SSM_CHUNK = 16
GROUPS_PER_SLAB = V7X_LANES // S5_GROUP
VMEM_LIMIT = 48 * 1024 * 1024
SSM_VMEM_LIMIT = 56 * 1024 * 1024
FOX_PROJ_VMEM_LIMIT = 56 * 1024 * 1024
S5_TAIL_VMEM_LIMIT = 56 * 1024 * 1024


def _params(*semantics):
    return pltpu.CompilerParams(dimension_semantics=semantics, vmem_limit_bytes=VMEM_LIMIT)


def _rms_scale(x, gain):
    ms = jnp.mean(x * x, axis=-1, keepdims=True)
    return x * lax.rsqrt(ms + RMS_EPS) * gain


class Proj(NamedTuple):
    weight: jax.Array
    weight_is_t: bool
    first: int
    gain: int
    act: Callable
    dtype: jnp.dtype
    transposed: bool


def _log_sigmoid(x):
    return -(jnp.maximum(-x, 0.0) + jnp.log1p(jnp.exp(-jnp.abs(x))))


def _cumsum_rows(c):
    rows = c.shape[0]
    row = lax.broadcasted_iota(jnp.int32, c.shape, 0)
    d = 1
    while d < rows:
        c = c + jnp.where(row >= d, pltpu.roll(c, d, axis=0), 0.0)
        d *= 2
    return c


def _fused_proj_kernel(*refs, projs, ngains, gate):
    nproj = len(projs)
    x_ref, gains_ref = refs[:2]
    w_refs = refs[2:2 + nproj]
    pos = 2 + nproj
    if gate is not None:
        wf_ref, bf_ref = refs[pos:pos + 2]
        pos += 2
    o_refs = refs[pos:pos + nproj]
    pos += nproj
    if gate is not None:
        cumt_ref = refs[pos]
        pos += 1
    xn_ref = refs[pos]
    nt = (((1,), (1,)), ((), ()))

    @pl.when(pl.program_id(1) == 0)
    def _():
        x = x_ref[...]
        base = x * lax.rsqrt(jnp.mean(x * x, axis=-1, keepdims=True) + RMS_EPS)
        for g in range(ngains):
            xn_ref[g] = (base * gains_ref[g:g + 1, :]).astype(BF16)
        if gate is not None:
            carry_ref = refs[pos + 1]

            @pl.when(pl.program_id(0) % gate.tiles_per_seq == 0)
            def _():
                carry_ref[...] = jnp.zeros_like(carry_ref)

            logit = lax.dot_general(xn_ref[gate.gain], wf_ref[...], nt, preferred_element_type=F32) + bf_ref[...]
            c = _cumsum_rows(_log_sigmoid(logit)) + carry_ref[...]
            carry_ref[...] = c[c.shape[0] - 1:, :]
            cumt_ref[...] = (c * LOG2_E).T[:gate.heads, :]

    for p, w_ref, o_ref in zip(projs, w_refs, o_refs):
        xn = xn_ref[p.gain]
        if p.weight_is_t and p.transposed:
            y = lax.dot_general(w_ref[...], xn, nt, preferred_element_type=F32)
        elif p.weight_is_t:
            y = lax.dot_general(xn, w_ref[...], nt, preferred_element_type=F32)
        else:
            y = jnp.dot(xn, w_ref[...], preferred_element_type=F32)
            y = y.T if p.transposed else y
        o_ref[...] = p.act(y).astype(o_ref.dtype)


class Gate(NamedTuple):
    wf_t: jax.Array
    bias: jax.Array
    gain: int
    heads: int
    seq: int
    tiles_per_seq: int = 0


def fused_proj(x, gains, projs, *, n, gate=None, tm=512, tn=512, vmem_limit=VMEM_LIMIT, name="fused_proj"):
    t, d = x.shape
    ngains = gains.shape[0]
    w_specs, out_specs, out_shapes = [], [], []
    for p in projs:
        if p.weight_is_t:
            w_specs.append(pl.BlockSpec((tn, d), lambda i, j, b0=p.first // tn: (j + b0, 0)))
        else:
            w_specs.append(pl.BlockSpec((d, tn), lambda i, j, b0=p.first // tn: (0, j + b0)))
        if p.transposed:
            out_specs.append(pl.BlockSpec((tn, tm), lambda i, j: (j, i)))
            out_shapes.append(jax.ShapeDtypeStruct((n, t), p.dtype))
        else:
            out_specs.append(pl.BlockSpec((tm, tn), lambda i, j: (i, j)))
            out_shapes.append(jax.ShapeDtypeStruct((t, n), p.dtype))
    operands = [x, gains] + [p.weight for p in projs]
    in_specs = [pl.BlockSpec((tm, d), lambda i, j: (i, 0)), pl.BlockSpec((ngains, d), lambda i, j: (0, 0))] + w_specs
    scratch = [pltpu.VMEM((ngains, tm, d), BF16)]
    kernel_projs = tuple(p._replace(weight=None) for p in projs)
    kernel_gate = None
    if gate is not None:
        tps = gate.seq // tm
        kernel_gate = gate._replace(wf_t=None, bias=None, tiles_per_seq=tps)
        operands += [gate.wf_t, gate.bias]
        in_specs += [pl.BlockSpec((V7X_LANES, d), lambda i, j: (0, 0)), pl.BlockSpec((1, V7X_LANES), lambda i, j: (0, 0))]
        out_specs.append(pl.BlockSpec((None, gate.heads, tm), lambda i, j: (i // tps, 0, i % tps)))
        out_shapes.append(jax.ShapeDtypeStruct((t // gate.seq, gate.heads, gate.seq), F32))
        scratch.append(pltpu.VMEM((1, V7X_LANES), F32))
    return pl.pallas_call(
        functools.partial(_fused_proj_kernel, projs=kernel_projs, ngains=ngains, gate=kernel_gate),
        grid=(t // tm, n // tn),
        in_specs=in_specs,
        out_specs=out_specs,
        out_shape=out_shapes,
        scratch_shapes=scratch,
        compiler_params=pltpu.CompilerParams(dimension_semantics=("arbitrary", "arbitrary"),
                                             vmem_limit_bytes=vmem_limit),
        name=name,
    )(*operands)


def _out_proj_kernel(a_ref, w_ref, res_ref, gain_ref, o_ref):
    y = jnp.dot(a_ref[...], w_ref[...], preferred_element_type=F32)
    o_ref[...] = res_ref[...] + _rms_scale(y, gain_ref[...])


def out_proj(a, w, res, gain, *, tm=512, name="out_proj"):
    t, k = a.shape
    d = w.shape[1]
    return pl.pallas_call(
        _out_proj_kernel,
        grid=(t // tm,),
        in_specs=[
            pl.BlockSpec((tm, k), lambda i: (i, 0)),
            pl.BlockSpec((k, d), lambda i: (0, 0)),
            pl.BlockSpec((tm, d), lambda i: (i, 0)),
            pl.BlockSpec((1, d), lambda i: (0, 0)),
        ],
        out_specs=pl.BlockSpec((tm, d), lambda i: (i, 0)),
        out_shape=jax.ShapeDtypeStruct((t, d), F32),
        compiler_params=_params("parallel"),
        name=name,
    )(a, w, res, gain)


def _s5_tail_kernel(yg_ref, wg_ref, b_ref, sz_ref, wo_ref, res_ref, gain_ref, o_ref, *, halves):
    rows = yg_ref.shape[0] // halves
    for h in range(halves):
        r = slice(h * rows, (h + 1) * rows)
        yg = yg_ref[r, :]
        gate = jax.nn.sigmoid(jnp.dot(yg.astype(BF16), wg_ref[...], preferred_element_type=F32) + b_ref[...])
        y3 = (yg * gate * sz_ref[r, :].astype(F32)).astype(BF16)
        y = jnp.dot(y3, wo_ref[...], preferred_element_type=F32)
        o_ref[r, :] = res_ref[r, :] + _rms_scale(y, gain_ref[...])


def s5_tail(yg, w_glu, b_glu, sz, w_out, res, gain, *, tm=512, halves=2, name="s5_tail"):
    t, k = yg.shape
    d = w_out.shape[1]
    rows = pl.BlockSpec((tm, k), lambda i: (i, 0))
    resident = functools.partial(pl.BlockSpec, index_map=lambda i: (0, 0), pipeline_mode=pl.Buffered(1))
    return pl.pallas_call(
        functools.partial(_s5_tail_kernel, halves=halves),
        grid=(t // tm,),
        in_specs=[rows, resident((k, k)), resident((1, k)), rows, resident((k, d)),
                  pl.BlockSpec((tm, d), lambda i: (i, 0)), resident((1, d))],
        out_specs=pl.BlockSpec((tm, d), lambda i: (i, 0)),
        out_shape=jax.ShapeDtypeStruct((t, d), F32),
        compiler_params=pltpu.CompilerParams(dimension_semantics=("parallel",),
                                             vmem_limit_bytes=S5_TAIL_VMEM_LIMIT),
        name=name,
    )(yg, w_glu, b_glu, sz, w_out, res, gain)


def _glu_kernel(yg_ref, w_ref, b_ref, sz_ref, o_ref, yb_ref, *, tn):
    j = pl.program_id(1)

    @pl.when(j == 0)
    def _():
        yb_ref[...] = yg_ref[...].astype(BF16)

    gate = jax.nn.sigmoid(jnp.dot(yb_ref[...], w_ref[...], preferred_element_type=F32) + b_ref[...])
    yg = yg_ref[:, pl.ds(pl.multiple_of(j * tn, tn), tn)]
    o_ref[...] = (yg * gate * sz_ref[...].astype(F32)).astype(o_ref.dtype)


def glu_gate(yg, w, b, sz, *, tm=1024, tn=512, name="s5_glu"):
    t, k = yg.shape
    n = w.shape[1]
    return pl.pallas_call(
        functools.partial(_glu_kernel, tn=tn),
        grid=(t // tm, n // tn),
        in_specs=[
            pl.BlockSpec((tm, k), lambda i, j: (i, 0)),
            pl.BlockSpec((k, tn), lambda i, j: (0, j)),
            pl.BlockSpec((1, tn), lambda i, j: (0, j)),
            pl.BlockSpec((tm, tn), lambda i, j: (i, j)),
        ],
        out_specs=pl.BlockSpec((tm, tn), lambda i, j: (i, j)),
        out_shape=jax.ShapeDtypeStruct((t, n), BF16),
        scratch_shapes=[pltpu.VMEM((tm, k), BF16)],
        compiler_params=_params("parallel", "arbitrary"),
        name=name,
    )(yg, w, b, sz)


def _ssm_prep_kernel(are_ref, aim_ref, ldt_ref, b2_ref, c2_ref, rep_ref,
                     r_ref, g_ref, h_ref, sr_ref, si_ref, *, groups, scan_steps):
    lane = lax.broadcasted_iota(jnp.int32, (1, 2 * S5_STATE), 1)
    minus_plus = jnp.where(lane < S5_STATE, -1.0, 1.0).astype(F32)
    gw = S5_GROUP
    ns = 2 * S5_STATE
    krows = []

    for gi in range(groups):
        ar = are_ref[gi]
        ai = aim_ref[gi]
        dt = jnp.exp(ldt_ref[gi])
        mag = jnp.exp(ar * dt)
        lam_r = mag * jnp.cos(ai * dt)
        lam_i = mag * jnp.sin(ai * dt)
        den = ar * ar + ai * ai
        nr = lam_r - 1.0
        coef_r = (nr * ar + lam_i * ai) / den
        coef_i = (lam_i * ar - nr * ai) / den

        pow_r = [jnp.ones_like(lam_r)]
        pow_i = [jnp.zeros_like(lam_r)]
        for _ in range(SSM_CHUNK):
            pr, pi = pow_r[-1], pow_i[-1]
            pow_r.append(pr * lam_r - pi * lam_i)
            pow_i.append(pr * lam_i + pi * lam_r)

        b2 = b2_ref[gi]
        b2s = pltpu.roll(b2, S5_STATE, axis=1) * minus_plus
        c2 = c2_ref[gi]
        c2a = c2 * (-minus_plus)
        c2b = -pltpu.roll(c2, S5_STATE, axis=1)

        for step in range(SSM_CHUNK):
            pr, pi = pow_r[SSM_CHUNK - 1 - step], pow_i[SSM_CHUNK - 1 - step]
            wr = pr * coef_r - pi * coef_i
            wi = pr * coef_i + pi * coef_r
            g_ref[step, gi * gw:(gi + 1) * gw, :] = (wr * b2 + wi * b2s).astype(g_ref.dtype)

        cl = [pow_r[tau] * c2a + pow_i[tau] * c2b for tau in range(SSM_CHUNK + 1)]
        h_t = jnp.concatenate(cl[1:], axis=0)
        h_ref[gi * ns:(gi + 1) * ns, :] = h_t.T.astype(h_ref.dtype)

        bbar2 = coef_r * b2 + coef_i * b2s
        cl_all = jnp.concatenate(cl[:-1], axis=0)
        krows.append(lax.dot_general(bbar2, cl_all, (((1,), (1,)), ((), ())),
                                     precision=lax.Precision.HIGHEST,
                                     preferred_element_type=F32))

        mu_r, mu_i = pow_r[SSM_CHUNK], pow_i[SSM_CHUNK]
        sr_rows, si_rows = [], []
        for _ in range(scan_steps):
            sr_rows.append(mu_r)
            si_rows.append(mu_i * minus_plus)
            mu_r, mu_i = mu_r * mu_r - mu_i * mu_i, 2.0 * mu_r * mu_i
        pad = [jnp.zeros_like(mu_r)] * (sr_ref.shape[1] - scan_steps)
        sr_ref[gi] = jnp.concatenate(sr_rows + pad, axis=0)
        si_ref[gi] = jnp.concatenate(si_rows + pad, axis=0)

    kst = jnp.concatenate(krows, axis=0).astype(BF16)
    spread = jnp.dot(kst, rep_ref[...], preferred_element_type=F32)
    rows_g = lax.broadcasted_iota(jnp.int32, spread.shape, 0) // gw
    cols_g = _group_of_lane(lax.broadcasted_iota(jnp.int32, spread.shape, 1))
    spread = jnp.where(rows_g == cols_g, spread, 0.0).astype(r_ref.dtype)
    blocks = [spread[:, tau * V7X_LANES:(tau + 1) * V7X_LANES] for tau in range(SSM_CHUNK)]
    zero = jnp.zeros_like(blocks[0])
    npairs = SSM_CHUNK // 2
    for d in range(npairs):
        base = (npairs - 1 - d) * 2 * V7X_LANES
        top = jnp.concatenate([blocks[2 * d], blocks[2 * d + 1]], axis=1)
        bottom = jnp.concatenate([blocks[2 * d - 1] if d else zero, blocks[2 * d]], axis=1)
        r_ref[base:base + V7X_LANES, :] = top
        r_ref[base + V7X_LANES:base + 2 * V7X_LANES, :] = bottom


def ssm_prep(a_re2, a_im2, log_dt2, b2, c2, rep, *, scan_steps, name="ssm_prep"):
    ng = a_re2.shape[0]
    gb = GROUPS_PER_SLAB
    nslab = ng // gb
    cat = SSM_CHUNK * V7X_LANES
    ns = 2 * S5_STATE
    rows = -(-scan_steps // V7X_SUBLANES) * V7X_SUBLANES
    vec = pl.BlockSpec((gb, 1, ns), lambda i: (i, 0, 0))
    mat = pl.BlockSpec((gb, S5_GROUP, ns), lambda i: (i, 0, 0))
    return pl.pallas_call(
        functools.partial(_ssm_prep_kernel, groups=gb, scan_steps=scan_steps),
        grid=(nslab,),
        in_specs=[vec, vec, vec, mat, mat, pl.BlockSpec(rep.shape, lambda i: (0, 0))],
        out_specs=[
            pl.BlockSpec((None, cat, 2 * V7X_LANES), lambda i: (i, 0, 0)),
            pl.BlockSpec((None, SSM_CHUNK, V7X_LANES, ns), lambda i: (i, 0, 0, 0)),
            pl.BlockSpec((None, gb * ns, SSM_CHUNK * S5_GROUP), lambda i: (i, 0, 0)),
            pl.BlockSpec((gb, rows, ns), lambda i: (i, 0, 0)),
            pl.BlockSpec((gb, rows, ns), lambda i: (i, 0, 0)),
        ],
        out_shape=[
            jax.ShapeDtypeStruct((nslab, cat, 2 * V7X_LANES), BF16),
            jax.ShapeDtypeStruct((nslab, SSM_CHUNK, V7X_LANES, ns), BF16),
            jax.ShapeDtypeStruct((nslab, gb * ns, SSM_CHUNK * S5_GROUP), BF16),
            jax.ShapeDtypeStruct((ng, rows, ns), F32),
            jax.ShapeDtypeStruct((ng, rows, ns), F32),
        ],
        compiler_params=_params("parallel"),
        name=name,
    )(a_re2, a_im2, log_dt2, b2, c2, rep)


def replication_matrix():
    src = jnp.arange(SSM_CHUNK * S5_GROUP)
    dst = jnp.arange(SSM_CHUNK * V7X_LANES)
    same_step = (src[:, None] // S5_GROUP) == (dst[None, :] // V7X_LANES)
    same_chan = (src[:, None] % S5_GROUP) == (dst[None, :] % S5_GROUP)
    return (same_step & same_chan).astype(BF16)


def _group_of_lane(idx):
    return (idx % V7X_LANES) // S5_GROUP


def _ssm_kernel(u_ref, r_ref, gst_ref, hst_ref, sr_ref, si_ref, d_ref, rep_ref, y_ref,
                ucat_ref, gexp_ref, hexp_ref, hprev_ref, intra_ref, *, chunks_per_seq, scan_steps):
    t = u_ref.shape[0]
    nchunk = t // SSM_CHUNK
    ns = 2 * S5_STATE
    pair = 2 * V7X_LANES
    npairs = SSM_CHUNK // 2

    for step in range(SSM_CHUNK):
        x = u_ref[pl.ds(step, nchunk, stride=SSM_CHUNK), :]
        ucat_ref[:, step * V7X_LANES:(step + 1) * V7X_LANES] = x.astype(BF16)

    g_rows = lax.broadcasted_iota(jnp.int32, (V7X_LANES, GROUPS_PER_SLAB * ns), 0) // S5_GROUP
    g_cols = lax.broadcasted_iota(jnp.int32, (V7X_LANES, GROUPS_PER_SLAB * ns), 1) // ns
    for step in range(SSM_CHUNK):
        tiled = jnp.concatenate([gst_ref[step]] * GROUPS_PER_SLAB, axis=1)
        gexp_ref[step * V7X_LANES:(step + 1) * V7X_LANES, :] = jnp.where(g_rows == g_cols, tiled, 0.0).astype(BF16)

    h_rows = lax.broadcasted_iota(jnp.int32, (GROUPS_PER_SLAB * ns, pair), 0) // ns
    h_cols = _group_of_lane(lax.broadcasted_iota(jnp.int32, (GROUPS_PER_SLAB * ns, pair), 1))
    for b in range(npairs):
        spread = jnp.dot(hst_ref[...], rep_ref[:, b * pair:(b + 1) * pair], preferred_element_type=F32)
        hexp_ref[:, b * pair:(b + 1) * pair] = jnp.where(h_rows == h_cols, spread, 0.0).astype(BF16)

    hs_all = jnp.dot(ucat_ref[...], gexp_ref[...], preferred_element_type=F32)
    for b in range(npairs):
        intra_ref[:, b * pair:(b + 1) * pair] = jnp.dot(
            ucat_ref[:, :(b + 1) * pair], r_ref[(npairs - 1 - b) * pair:, :], preferred_element_type=F32)

    row = lax.broadcasted_iota(jnp.int32, (nchunk, ns), 0) % chunks_per_seq
    for g in range(GROUPS_PER_SLAB):
        hs = hs_all[:, g * ns:(g + 1) * ns]
        for k in range(scan_steps):
            d = 1 << k
            sh = jnp.where(row >= d, pltpu.roll(hs, d, axis=0), 0.0)
            hs = hs + sr_ref[g, k:k + 1, :] * sh + si_ref[g, k:k + 1, :] * pltpu.roll(sh, S5_STATE, axis=1)
        hprev_ref[:, g * ns:(g + 1) * ns] = jnp.where(row >= 1, pltpu.roll(hs, 1, axis=0), 0.0).astype(BF16)

    for b in range(npairs):
        y = intra_ref[:, b * pair:(b + 1) * pair] + jnp.dot(
            hprev_ref[...], hexp_ref[:, b * pair:(b + 1) * pair], preferred_element_type=F32)
        for bit in range(2):
            step = 2 * b + bit
            lanes = slice(step * V7X_LANES, (step + 1) * V7X_LANES)
            yl = y[:, bit * V7X_LANES:(bit + 1) * V7X_LANES] + d_ref[...] * ucat_ref[:, lanes].astype(F32)
            y_ref[pl.ds(step, nchunk, stride=SSM_CHUNK), :] = jax.nn.gelu(yl, approximate=True)


def ssm_apply(u, r_all, gst, hst, sr, si, d_skip, rep, *, seq, name="s5_ssm"):
    t, w = u.shape
    nslab = w // V7X_LANES
    nchunk = t // SSM_CHUNK
    cat = SSM_CHUNK * V7X_LANES
    ns = 2 * S5_STATE
    chunks_per_seq = seq // SSM_CHUNK
    scan_steps = (chunks_per_seq - 1).bit_length()
    gb = GROUPS_PER_SLAB
    rows = sr.shape[1]
    return pl.pallas_call(
        functools.partial(_ssm_kernel, chunks_per_seq=chunks_per_seq, scan_steps=scan_steps),
        grid=(nslab,),
        in_specs=[
            pl.BlockSpec((t, V7X_LANES), lambda s: (0, s)),
            pl.BlockSpec((None, cat, 2 * V7X_LANES), lambda s: (s, 0, 0)),
            pl.BlockSpec((None, SSM_CHUNK, V7X_LANES, ns), lambda s: (s, 0, 0, 0)),
            pl.BlockSpec((None, gb * ns, SSM_CHUNK * S5_GROUP), lambda s: (s, 0, 0)),
            pl.BlockSpec((gb, rows, ns), lambda s: (s, 0, 0)),
            pl.BlockSpec((gb, rows, ns), lambda s: (s, 0, 0)),
            pl.BlockSpec((1, V7X_LANES), lambda s: (0, s)),
            pl.BlockSpec((SSM_CHUNK * S5_GROUP, cat), lambda s: (0, 0)),
        ],
        out_specs=pl.BlockSpec((t, V7X_LANES), lambda s: (0, s)),
        out_shape=jax.ShapeDtypeStruct((t, w), F32),
        scratch_shapes=[pltpu.VMEM((nchunk, cat), BF16), pltpu.VMEM((cat, gb * ns), BF16),
                        pltpu.VMEM((gb * ns, cat), BF16), pltpu.VMEM((nchunk, gb * ns), BF16),
                        pltpu.VMEM((nchunk, cat), F32)],
        compiler_params=pltpu.CompilerParams(dimension_semantics=("parallel",),
                                             vmem_limit_bytes=SSM_VMEM_LIMIT),
        name=name,
    )(u, r_all, gst, hst, sr, si, d_skip, rep)


ATTN_SPLIT = 3
ATTN_EXTRA_ROWS = 16


def _split3(c):
    hi = c.astype(BF16).astype(F32)
    mid = (c - hi).astype(BF16).astype(F32)
    return hi, mid, c - hi - mid


def _fox_attn_kernel(qt_ref, k_ref, vt_ref, ct_ref, sz_ref, o_ref,
                     qa_ref, ka_ref, va_ref, s0_ref, s1_ref, smax_ref, m_ref, acc_ref, *, tile):
    dh, seq = qt_ref.shape
    ntiles = seq // tile
    ext = ATTN_EXTRA_ROWS

    @pl.when((pl.program_id(0) == 0) & (pl.program_id(1) == 0))
    def _():
        qa_ref[dh + ext:, :] = jnp.zeros((dh - ext, seq), BF16)
        rowv = lax.broadcasted_iota(jnp.int32, (ext, seq), 0)
        va_ref[dh:, :] = jnp.where(rowv == 0, 1.0, 0.0).astype(BF16)

    parts = _split3(ct_ref[...])
    qa_ref[:dh, :] = qt_ref[...]
    row16 = lax.broadcasted_iota(jnp.int32, (ext, seq), 0)
    ext_q = jnp.where(row16 < 2 * ATTN_SPLIT, 1.0, 0.0)
    for n, part in enumerate(parts):
        ext_q = jnp.where(row16 == n, part, ext_q)
    qa_ref[dh:dh + ext, :] = ext_q.astype(BF16)
    va_ref[:dh, :] = vt_ref[...]

    rowk = lax.broadcasted_iota(jnp.int32, (dh, tile), 0)
    for r in range(ntiles):
        cols = slice(r * tile, (r + 1) * tile)
        ext_kt = jnp.where(rowk < ATTN_SPLIT, 1.0, 0.0)
        for n, part in enumerate(parts):
            ext_kt = jnp.where(rowk == ATTN_SPLIT + n, -part[:, cols], ext_kt)
        ka_ref[cols, :dh] = k_ref[cols, :]
        ka_ref[cols, dh:] = ext_kt.T.astype(BF16)

    causal = (lax.broadcasted_iota(jnp.int32, (tile, tile), 0) <= lax.broadcasted_iota(jnp.int32, (tile, tile), 1))
    s_slots = (s0_ref, s1_ref)

    def produce(slot, qi, kb):
        st = jnp.dot(ka_ref[kb * tile:(kb + 1) * tile, :], qa_ref[:, qi * tile:(qi + 1) * tile],
                     preferred_element_type=F32)
        s_slots[slot][...] = st
        smax_ref[slot:slot + 1, :] = jnp.max(st, axis=0, keepdims=True)

    def consume(slot, qi, kb):
        par = qi % 2
        st = s_slots[slot][...]
        if kb == qi:
            st = jnp.where(causal, st, NEG_INF)
            st_max = jnp.max(st, axis=0, keepdims=True)
        else:
            st_max = smax_ref[slot:slot + 1, :]
        pv = functools.partial(jnp.dot, va_ref[:, kb * tile:(kb + 1) * tile], preferred_element_type=F32)
        if kb == 0:
            m_ref[par:par + 1, :] = st_max
            acc_ref[par] = pv(jnp.exp2(st - st_max).astype(BF16))
        else:
            m = m_ref[par:par + 1, :]
            m_new = jnp.maximum(m, st_max)
            acc_ref[par] = jnp.exp2(m - m_new) * acc_ref[par] + pv(jnp.exp2(st - m_new).astype(BF16))
            m_ref[par:par + 1, :] = m_new

    def finalize(qi):
        acc = acc_ref[qi % 2]
        rows = slice(qi * tile, (qi + 1) * tile)
        o = (acc[:dh, :] / acc[dh:dh + 1, :]).T
        o_ref[rows, :] = (o * sz_ref[rows, :].astype(F32)).astype(o_ref.dtype)

    blocks = [(qi, kb) for qi in range(ntiles) for kb in range(qi + 1)]
    produce(0, *blocks[0])
    for n, (qi, kb) in enumerate(blocks):
        if n + 1 < len(blocks):
            produce((n + 1) % 2, *blocks[n + 1])
        consume(n % 2, qi, kb)
        if kb == qi:
            finalize(qi)


def fox_attention(qt, k, vt, cumt2, sz, *, batch, seq, heads, tile=512, name="fox_attention"):
    width, t = qt.shape
    dh = width // heads
    natural = pl.BlockSpec((seq, dh), lambda b, h: (b, h))
    transposed = pl.BlockSpec((dh, seq), lambda b, h: (h, b))
    return pl.pallas_call(
        functools.partial(_fox_attn_kernel, tile=tile),
        grid=(batch, heads),
        in_specs=[transposed, natural, transposed,
                  pl.BlockSpec((None, None, 1, seq), lambda b, h: (b, h, 0, 0)),
                  natural],
        out_specs=natural,
        out_shape=jax.ShapeDtypeStruct((t, width), BF16),
        scratch_shapes=[pltpu.VMEM((2 * dh, seq), BF16), pltpu.VMEM((seq, 2 * dh), BF16),
                        pltpu.VMEM((dh + ATTN_EXTRA_ROWS, seq), BF16),
                        pltpu.VMEM((tile, tile), F32), pltpu.VMEM((tile, tile), F32),
                        pltpu.VMEM((V7X_SUBLANES, tile), F32), pltpu.VMEM((V7X_SUBLANES, tile), F32),
                        pltpu.VMEM((2, dh + ATTN_EXTRA_ROWS, tile), F32)],
        compiler_params=_params("arbitrary", "arbitrary"),
        name=name,
    )(qt, k, vt, cumt2, sz)


def _identity(v):
    return v


def _doubled(v):
    return jnp.concatenate([v, v], axis=-1)[:, None, :].astype(F32)


def kernel(x, norm_pre, norm_post, s5_w_in, s5_a_re, s5_a_im, s5_log_dt, s5_b_re, s5_b_im, s5_c_re, s5_c_im,
           s5_d, s5_w_glu, s5_b_glu, s5_w_out, kv_norm, kv_w, kv_b_f, fox_w_in, fox_w_out):
    batch, seq, d_model = x.shape
    t = batch * seq
    heads = kv_b_f.shape[0]
    fox_width = fox_w_in.shape[-1] // 2
    assert norm_pre.shape[0] == 2 and s5_w_in.shape[0] == 1 and fox_w_in.shape[0] == 1
    assert fox_width // heads == FOX_HEAD_DIM and heads <= V7X_LANES
    ngroups = s5_a_re.shape[1]
    chunks_per_seq = seq // SSM_CHUNK
    scan_steps = (chunks_per_seq - 1).bit_length()

    h0 = x.reshape(t, d_model)

    s5_width = s5_w_in.shape[-1] // 2
    w_in = s5_w_in[0].astype(BF16)
    u, sz = fused_proj(h0, norm_pre[0][None, :],
                       [Proj(w_in, False, 0, 0, _identity, F32, False),
                        Proj(w_in, False, s5_width, 0, jax.nn.silu, BF16, False)],
                       n=s5_width, tm=1024, name="s5_in_proj")
    ldt2 = jnp.broadcast_to(s5_log_dt[0][:, None, None], (ngroups, 1, 2 * S5_STATE)).astype(F32)
    b2 = jnp.concatenate([jnp.swapaxes(s5_b_re[0], 1, 2), jnp.swapaxes(s5_b_im[0], 1, 2)], axis=-1)
    c2 = jnp.concatenate([s5_c_re[0], s5_c_im[0]], axis=-1)
    rep = replication_matrix()
    r_all, gst, hst, sr, si = ssm_prep(_doubled(s5_a_re[0]), _doubled(s5_a_im[0]), ldt2, b2.astype(F32),
                                       c2.astype(F32), rep, scan_steps=scan_steps)
    yg = ssm_apply(u, r_all, gst, hst, sr, si, s5_d[0][None, :].astype(F32), rep, seq=seq)
    h1 = s5_tail(yg, s5_w_glu[0].astype(BF16), s5_b_glu[0][None, :], sz, s5_w_out[0].astype(BF16), h0,
                 norm_post[0][None, :])

    kv_cols = 2 * fox_width
    w_kv_t = kv_w.T.astype(BF16)
    wf_t = jnp.pad(w_kv_t[kv_cols:], ((0, V7X_LANES - heads), (0, 0)))
    bf = jnp.pad(kv_b_f, (0, V7X_LANES - heads))[None, :].astype(F32)
    w_in_fox = fox_w_in[0].astype(BF16)
    scale = FOX_HEAD_DIM ** -0.5 * LOG2_E
    k, vt, qt, sz2, cumt2 = fused_proj(
        h1, jnp.stack([kv_norm, norm_pre[1]]),
        [Proj(w_kv_t, True, 0, 0, _identity, BF16, False),
         Proj(w_kv_t, True, fox_width, 0, _identity, BF16, True),
         Proj(w_in_fox, False, 0, 1, lambda v: v * scale, BF16, True),
         Proj(w_in_fox, False, fox_width, 1, jax.nn.silu, BF16, False)],
        n=fox_width, gate=Gate(wf_t, bf, 0, heads, seq), vmem_limit=FOX_PROJ_VMEM_LIMIT, name="fox_proj")
    o = fox_attention(qt, k, vt, cumt2.reshape(batch, heads, 1, seq), sz2,
                      batch=batch, seq=seq, heads=heads)
    h2 = out_proj(o, fox_w_out[0].astype(BF16), h1, norm_post[1][None, :], name="fox_out_proj")
    return h2.reshape(batch, seq, d_model)
```

```python
import functools
import math
from typing import Callable, NamedTuple

import jax
import jax.numpy as jnp
from jax import lax
from jax.experimental import pallas as pl
from jax.experimental.pallas import tpu as pltpu

F32 = jnp.float32
BF16 = jnp.bfloat16

RMS_EPS = 1e-6
NEG_INF = -1e30
LOG2_E = math.log2(math.e)

S5_GROUP = 16
S5_STATE = 64
FOX_HEAD_DIM = 128

V7X_LANES = 128
V7X_SUBLANES = 8
SSM_CHUNK = 16
GROUPS_PER_SLAB = V7X_LANES // S5_GROUP
VMEM_LIMIT = 48 * 1024 * 1024
SSM_VMEM_LIMIT = 56 * 1024 * 1024
FOX_PROJ_VMEM_LIMIT = 56 * 1024 * 1024
S5_TAIL_VMEM_LIMIT = 56 * 1024 * 1024


def _params(*semantics):
    return pltpu.CompilerParams(dimension_semantics=semantics, vmem_limit_bytes=VMEM_LIMIT)


def _rms_scale(x, gain):
    ms = jnp.mean(x * x, axis=-1, keepdims=True)
    return x * lax.rsqrt(ms + RMS_EPS) * gain


class Proj(NamedTuple):
    weight: jax.Array
    weight_is_t: bool
    first: int
    gain: int
    act: Callable
    dtype: jnp.dtype
    transposed: bool


def _log_sigmoid(x):
    return -(jnp.maximum(-x, 0.0) + jnp.log1p(jnp.exp(-jnp.abs(x))))


def _cumsum_rows(c):
    rows = c.shape[0]
    row = lax.broadcasted_iota(jnp.int32, c.shape, 0)
    d = 1
    while d < rows:
        c = c + jnp.where(row >= d, pltpu.roll(c, d, axis=0), 0.0)
        d *= 2
    return c


def _fused_proj_kernel(*refs, projs, ngains, gate):
    nproj = len(projs)
    x_ref, gains_ref = refs[:2]
    w_refs = refs[2:2 + nproj]
    pos = 2 + nproj
    if gate is not None:
        wf_ref, bf_ref = refs[pos:pos + 2]
        pos += 2
    o_refs = refs[pos:pos + nproj]
    pos += nproj
    if gate is not None:
        cumt_ref = refs[pos]
        pos += 1
    xn_ref = refs[pos]
    nt = (((1,), (1,)), ((), ()))

    @pl.when(pl.program_id(1) == 0)
    def _():
        x = x_ref[...]
        base = x * lax.rsqrt(jnp.mean(x * x, axis=-1, keepdims=True) + RMS_EPS)
        for g in range(ngains):
            xn_ref[g] = (base * gains_ref[g:g + 1, :]).astype(BF16)
        if gate is not None:
            carry_ref = refs[pos + 1]

            @pl.when(pl.program_id(0) % gate.tiles_per_seq == 0)
            def _():
                carry_ref[...] = jnp.zeros_like(carry_ref)

            logit = lax.dot_general(xn_ref[gate.gain], wf_ref[...], nt, preferred_element_type=F32) + bf_ref[...]
            c = _cumsum_rows(_log_sigmoid(logit)) + carry_ref[...]
            carry_ref[...] = c[c.shape[0] - 1:, :]
            cumt_ref[...] = (c * LOG2_E).T[:gate.heads, :]

    for p, w_ref, o_ref in zip(projs, w_refs, o_refs):
        xn = xn_ref[p.gain]
        if p.weight_is_t and p.transposed:
            y = lax.dot_general(w_ref[...], xn, nt, preferred_element_type=F32)
        elif p.weight_is_t:
            y = lax.dot_general(xn, w_ref[...], nt, preferred_element_type=F32)
        else:
            y = jnp.dot(xn, w_ref[...], preferred_element_type=F32)
            y = y.T if p.transposed else y
        o_ref[...] = p.act(y).astype(o_ref.dtype)


class Gate(NamedTuple):
    wf_t: jax.Array
    bias: jax.Array
    gain: int
    heads: int
    seq: int
    tiles_per_seq: int = 0


def fused_proj(x, gains, projs, *, n, gate=None, tm=512, tn=512, vmem_limit=VMEM_LIMIT, name="fused_proj"):
    t, d = x.shape
    ngains = gains.shape[0]
    w_specs, out_specs, out_shapes = [], [], []
    for p in projs:
        if p.weight_is_t:
            w_specs.append(pl.BlockSpec((tn, d), lambda i, j, b0=p.first // tn: (j + b0, 0)))
        else:
            w_specs.append(pl.BlockSpec((d, tn), lambda i, j, b0=p.first // tn: (0, j + b0)))
        if p.transposed:
            out_specs.append(pl.BlockSpec((tn, tm), lambda i, j: (j, i)))
            out_shapes.append(jax.ShapeDtypeStruct((n, t), p.dtype))
        else:
            out_specs.append(pl.BlockSpec((tm, tn), lambda i, j: (i, j)))
            out_shapes.append(jax.ShapeDtypeStruct((t, n), p.dtype))
    operands = [x, gains] + [p.weight for p in projs]
    in_specs = [pl.BlockSpec((tm, d), lambda i, j: (i, 0)), pl.BlockSpec((ngains, d), lambda i, j: (0, 0))] + w_specs
    scratch = [pltpu.VMEM((ngains, tm, d), BF16)]
    kernel_projs = tuple(p._replace(weight=None) for p in projs)
    kernel_gate = None
    if gate is not None:
        tps = gate.seq // tm
        kernel_gate = gate._replace(wf_t=None, bias=None, tiles_per_seq=tps)
        operands += [gate.wf_t, gate.bias]
        in_specs += [pl.BlockSpec((V7X_LANES, d), lambda i, j: (0, 0)), pl.BlockSpec((1, V7X_LANES), lambda i, j: (0, 0))]
        out_specs.append(pl.BlockSpec((None, gate.heads, tm), lambda i, j: (i // tps, 0, i % tps)))
        out_shapes.append(jax.ShapeDtypeStruct((t // gate.seq, gate.heads, gate.seq), F32))
        scratch.append(pltpu.VMEM((1, V7X_LANES), F32))
    return pl.pallas_call(
        functools.partial(_fused_proj_kernel, projs=kernel_projs, ngains=ngains, gate=kernel_gate),
        grid=(t // tm, n // tn),
        in_specs=in_specs,
        out_specs=out_specs,
        out_shape=out_shapes,
        scratch_shapes=scratch,
        compiler_params=pltpu.CompilerParams(dimension_semantics=("arbitrary", "arbitrary"),
                                             vmem_limit_bytes=vmem_limit),
        name=name,
    )(*operands)


def _out_proj_kernel(a_ref, w_ref, res_ref, gain_ref, o_ref):
    y = jnp.dot(a_ref[...], w_ref[...], preferred_element_type=F32)
    o_ref[...] = res_ref[...] + _rms_scale(y, gain_ref[...])


def out_proj(a, w, res, gain, *, tm=512, name="out_proj"):
    t, k = a.shape
    d = w.shape[1]
    return pl.pallas_call(
        _out_proj_kernel,
        grid=(t // tm,),
        in_specs=[
            pl.BlockSpec((tm, k), lambda i: (i, 0)),
            pl.BlockSpec((k, d), lambda i: (0, 0)),
            pl.BlockSpec((tm, d), lambda i: (i, 0)),
            pl.BlockSpec((1, d), lambda i: (0, 0)),
        ],
        out_specs=pl.BlockSpec((tm, d), lambda i: (i, 0)),
        out_shape=jax.ShapeDtypeStruct((t, d), F32),
        compiler_params=_params("parallel"),
        name=name,
    )(a, w, res, gain)


def _s5_tail_kernel(yg_ref, wg_ref, b_ref, sz_ref, wo_ref, res_ref, gain_ref, o_ref, *, halves):
    rows = yg_ref.shape[0] // halves
    for h in range(halves):
        r = slice(h * rows, (h + 1) * rows)
        yg = yg_ref[r, :]
        gate = jax.nn.sigmoid(jnp.dot(yg.astype(BF16), wg_ref[...], preferred_element_type=F32) + b_ref[...])
        y3 = (yg * gate * sz_ref[r, :].astype(F32)).astype(BF16)
        y = jnp.dot(y3, wo_ref[...], preferred_element_type=F32)
        o_ref[r, :] = res_ref[r, :] + _rms_scale(y, gain_ref[...])


def s5_tail(yg, w_glu, b_glu, sz, w_out, res, gain, *, tm=512, halves=2, name="s5_tail"):
    t, k = yg.shape
    d = w_out.shape[1]
    rows = pl.BlockSpec((tm, k), lambda i: (i, 0))
    resident = functools.partial(pl.BlockSpec, index_map=lambda i: (0, 0), pipeline_mode=pl.Buffered(1))
    return pl.pallas_call(
        functools.partial(_s5_tail_kernel, halves=halves),
        grid=(t // tm,),
        in_specs=[rows, resident((k, k)), resident((1, k)), rows, resident((k, d)),
                  pl.BlockSpec((tm, d), lambda i: (i, 0)), resident((1, d))],
        out_specs=pl.BlockSpec((tm, d), lambda i: (i, 0)),
        out_shape=jax.ShapeDtypeStruct((t, d), F32),
        compiler_params=pltpu.CompilerParams(dimension_semantics=("parallel",),
                                             vmem_limit_bytes=S5_TAIL_VMEM_LIMIT),
        name=name,
    )(yg, w_glu, b_glu, sz, w_out, res, gain)


def _glu_kernel(yg_ref, w_ref, b_ref, sz_ref, o_ref, yb_ref, *, tn):
    j = pl.program_id(1)

    @pl.when(j == 0)
    def _():
        yb_ref[...] = yg_ref[...].astype(BF16)

    gate = jax.nn.sigmoid(jnp.dot(yb_ref[...], w_ref[...], preferred_element_type=F32) + b_ref[...])
    yg = yg_ref[:, pl.ds(pl.multiple_of(j * tn, tn), tn)]
    o_ref[...] = (yg * gate * sz_ref[...].astype(F32)).astype(o_ref.dtype)


def glu_gate(yg, w, b, sz, *, tm=1024, tn=512, name="s5_glu"):
    t, k = yg.shape
    n = w.shape[1]
    return pl.pallas_call(
        functools.partial(_glu_kernel, tn=tn),
        grid=(t // tm, n // tn),
        in_specs=[
            pl.BlockSpec((tm, k), lambda i, j: (i, 0)),
            pl.BlockSpec((k, tn), lambda i, j: (0, j)),
            pl.BlockSpec((1, tn), lambda i, j: (0, j)),
            pl.BlockSpec((tm, tn), lambda i, j: (i, j)),
        ],
        out_specs=pl.BlockSpec((tm, tn), lambda i, j: (i, j)),
        out_shape=jax.ShapeDtypeStruct((t, n), BF16),
        scratch_shapes=[pltpu.VMEM((tm, k), BF16)],
        compiler_params=_params("parallel", "arbitrary"),
        name=name,
    )(yg, w, b, sz)


def _ssm_prep_kernel(are_ref, aim_ref, ldt_ref, b2_ref, c2_ref, rep_ref,
                     r_ref, g_ref, h_ref, sr_ref, si_ref, *, groups, scan_steps):
    lane = lax.broadcasted_iota(jnp.int32, (1, 2 * S5_STATE), 1)
    minus_plus = jnp.where(lane < S5_STATE, -1.0, 1.0).astype(F32)
    gw = S5_GROUP
    ns = 2 * S5_STATE
    krows = []

    for gi in range(groups):
        ar = are_ref[gi]
        ai = aim_ref[gi]
        dt = jnp.exp(ldt_ref[gi])
        mag = jnp.exp(ar * dt)
        lam_r = mag * jnp.cos(ai * dt)
        lam_i = mag * jnp.sin(ai * dt)
        den = ar * ar + ai * ai
        nr = lam_r - 1.0
        coef_r = (nr * ar + lam_i * ai) / den
        coef_i = (lam_i * ar - nr * ai) / den

        pow_r = [jnp.ones_like(lam_r)]
        pow_i = [jnp.zeros_like(lam_r)]
        for _ in range(SSM_CHUNK):
            pr, pi = pow_r[-1], pow_i[-1]
            pow_r.append(pr * lam_r - pi * lam_i)
            pow_i.append(pr * lam_i + pi * lam_r)

        b2 = b2_ref[gi]
        b2s = pltpu.roll(b2, S5_STATE, axis=1) * minus_plus
        c2 = c2_ref[gi]
        c2a = c2 * (-minus_plus)
        c2b = -pltpu.roll(c2, S5_STATE, axis=1)

        for step in range(SSM_CHUNK):
            pr, pi = pow_r[SSM_CHUNK - 1 - step], pow_i[SSM_CHUNK - 1 - step]
            wr = pr * coef_r - pi * coef_i
            wi = pr * coef_i + pi * coef_r
            g_ref[step, gi * gw:(gi + 1) * gw, :] = (wr * b2 + wi * b2s).astype(g_ref.dtype)

        cl = [pow_r[tau] * c2a + pow_i[tau] * c2b for tau in range(SSM_CHUNK + 1)]
        h_t = jnp.concatenate(cl[1:], axis=0)
        h_ref[gi * ns:(gi + 1) * ns, :] = h_t.T.astype(h_ref.dtype)

        bbar2 = coef_r * b2 + coef_i * b2s
        cl_all = jnp.concatenate(cl[:-1], axis=0)
        krows.append(lax.dot_general(bbar2, cl_all, (((1,), (1,)), ((), ())),
                                     precision=lax.Precision.HIGHEST,
                                     preferred_element_type=F32))

        mu_r, mu_i = pow_r[SSM_CHUNK], pow_i[SSM_CHUNK]
        sr_rows, si_rows = [], []
        for _ in range(scan_steps):
            sr_rows.append(mu_r)
            si_rows.append(mu_i * minus_plus)
            mu_r, mu_i = mu_r * mu_r - mu_i * mu_i, 2.0 * mu_r * mu_i
        pad = [jnp.zeros_like(mu_r)] * (sr_ref.shape[1] - scan_steps)
        sr_ref[gi] = jnp.concatenate(sr_rows + pad, axis=0)
        si_ref[gi] = jnp.concatenate(si_rows + pad, axis=0)

    kst = jnp.concatenate(krows, axis=0).astype(BF16)
    spread = jnp.dot(kst, rep_ref[...], preferred_element_type=F32)
    rows_g = lax.broadcasted_iota(jnp.int32, spread.shape, 0) // gw
    cols_g = _group_of_lane(lax.broadcasted_iota(jnp.int32, spread.shape, 1))
    spread = jnp.where(rows_g == cols_g, spread, 0.0).astype(r_ref.dtype)
    blocks = [spread[:, tau * V7X_LANES:(tau + 1) * V7X_LANES] for tau in range(SSM_CHUNK)]
    zero = jnp.zeros_like(blocks[0])
    npairs = SSM_CHUNK // 2
    for d in range(npairs):
        base = (npairs - 1 - d) * 2 * V7X_LANES
        top = jnp.concatenate([blocks[2 * d], blocks[2 * d + 1]], axis=1)
        bottom = jnp.concatenate([blocks[2 * d - 1] if d else zero, blocks[2 * d]], axis=1)
        r_ref[base:base + V7X_LANES, :] = top
        r_ref[base + V7X_LANES:base + 2 * V7X_LANES, :] = bottom


def ssm_prep(a_re2, a_im2, log_dt2, b2, c2, rep, *, scan_steps, name="ssm_prep"):
    ng = a_re2.shape[0]
    gb = GROUPS_PER_SLAB
    nslab = ng // gb
    cat = SSM_CHUNK * V7X_LANES
    ns = 2 * S5_STATE
    rows = -(-scan_steps // V7X_SUBLANES) * V7X_SUBLANES
    vec = pl.BlockSpec((gb, 1, ns), lambda i: (i, 0, 0))
    mat = pl.BlockSpec((gb, S5_GROUP, ns), lambda i: (i, 0, 0))
    return pl.pallas_call(
        functools.partial(_ssm_prep_kernel, groups=gb, scan_steps=scan_steps),
        grid=(nslab,),
        in_specs=[vec, vec, vec, mat, mat, pl.BlockSpec(rep.shape, lambda i: (0, 0))],
        out_specs=[
            pl.BlockSpec((None, cat, 2 * V7X_LANES), lambda i: (i, 0, 0)),
            pl.BlockSpec((None, SSM_CHUNK, V7X_LANES, ns), lambda i: (i, 0, 0, 0)),
            pl.BlockSpec((None, gb * ns, SSM_CHUNK * S5_GROUP), lambda i: (i, 0, 0)),
            pl.BlockSpec((gb, rows, ns), lambda i: (i, 0, 0)),
            pl.BlockSpec((gb, rows, ns), lambda i: (i, 0, 0)),
        ],
        out_shape=[
            jax.ShapeDtypeStruct((nslab, cat, 2 * V7X_LANES), BF16),
            jax.ShapeDtypeStruct((nslab, SSM_CHUNK, V7X_LANES, ns), BF16),
            jax.ShapeDtypeStruct((nslab, gb * ns, SSM_CHUNK * S5_GROUP), BF16),
            jax.ShapeDtypeStruct((ng, rows, ns), F32),
            jax.ShapeDtypeStruct((ng, rows, ns), F32),
        ],
        compiler_params=_params("parallel"),
        name=name,
    )(a_re2, a_im2, log_dt2, b2, c2, rep)


def replication_matrix():
    src = jnp.arange(SSM_CHUNK * S5_GROUP)
    dst = jnp.arange(SSM_CHUNK * V7X_LANES)
    same_step = (src[:, None] // S5_GROUP) == (dst[None, :] // V7X_LANES)
    same_chan = (src[:, None] % S5_GROUP) == (dst[None, :] % S5_GROUP)
    return (same_step & same_chan).astype(BF16)


def _group_of_lane(idx):
    return (idx % V7X_LANES) // S5_GROUP


def _ssm_kernel(u_ref, r_ref, gst_ref, hst_ref, sr_ref, si_ref, d_ref, rep_ref, y_ref,
                ucat_ref, gexp_ref, hexp_ref, hprev_ref, intra_ref, *, chunks_per_seq, scan_steps):
    t = u_ref.shape[0]
    nchunk = t // SSM_CHUNK
    ns = 2 * S5_STATE
    pair = 2 * V7X_LANES
    npairs = SSM_CHUNK // 2

    for step in range(SSM_CHUNK):
        x = u_ref[pl.ds(step, nchunk, stride=SSM_CHUNK), :]
        ucat_ref[:, step * V7X_LANES:(step + 1) * V7X_LANES] = x.astype(BF16)

    g_rows = lax.broadcasted_iota(jnp.int32, (V7X_LANES, GROUPS_PER_SLAB * ns), 0) // S5_GROUP
    g_cols = lax.broadcasted_iota(jnp.int32, (V7X_LANES, GROUPS_PER_SLAB * ns), 1) // ns
    for step in range(SSM_CHUNK):
        tiled = jnp.concatenate([gst_ref[step]] * GROUPS_PER_SLAB, axis=1)
        gexp_ref[step * V7X_LANES:(step + 1) * V7X_LANES, :] = jnp.where(g_rows == g_cols, tiled, 0.0).astype(BF16)

    h_rows = lax.broadcasted_iota(jnp.int32, (GROUPS_PER_SLAB * ns, pair), 0) // ns
    h_cols = _group_of_lane(lax.broadcasted_iota(jnp.int32, (GROUPS_PER_SLAB * ns, pair), 1))
    for b in range(npairs):
        spread = jnp.dot(hst_ref[...], rep_ref[:, b * pair:(b + 1) * pair], preferred_element_type=F32)
        hexp_ref[:, b * pair:(b + 1) * pair] = jnp.where(h_rows == h_cols, spread, 0.0).astype(BF16)

    hs_all = jnp.dot(ucat_ref[...], gexp_ref[...], preferred_element_type=F32)
    for b in range(npairs):
        intra_ref[:, b * pair:(b + 1) * pair] = jnp.dot(
            ucat_ref[:, :(b + 1) * pair], r_ref[(npairs - 1 - b) * pair:, :], preferred_element_type=F32)

    row = lax.broadcasted_iota(jnp.int32, (nchunk, ns), 0) % chunks_per_seq
    for g in range(GROUPS_PER_SLAB):
        hs = hs_all[:, g * ns:(g + 1) * ns]
        for k in range(scan_steps):
            d = 1 << k
            sh = jnp.where(row >= d, pltpu.roll(hs, d, axis=0), 0.0)
            hs = hs + sr_ref[g, k:k + 1, :] * sh + si_ref[g, k:k + 1, :] * pltpu.roll(sh, S5_STATE, axis=1)
        hprev_ref[:, g * ns:(g + 1) * ns] = jnp.where(row >= 1, pltpu.roll(hs, 1, axis=0), 0.0).astype(BF16)

    for b in range(npairs):
        y = intra_ref[:, b * pair:(b + 1) * pair] + jnp.dot(
            hprev_ref[...], hexp_ref[:, b * pair:(b + 1) * pair], preferred_element_type=F32)
        for bit in range(2):
            step = 2 * b + bit
            lanes = slice(step * V7X_LANES, (step + 1) * V7X_LANES)
            yl = y[:, bit * V7X_LANES:(bit + 1) * V7X_LANES] + d_ref[...] * ucat_ref[:, lanes].astype(F32)
            y_ref[pl.ds(step, nchunk, stride=SSM_CHUNK), :] = jax.nn.gelu(yl, approximate=True)


def ssm_apply(u, r_all, gst, hst, sr, si, d_skip, rep, *, seq, name="s5_ssm"):
    t, w = u.shape
    nslab = w // V7X_LANES
    nchunk = t // SSM_CHUNK
    cat = SSM_CHUNK * V7X_LANES
    ns = 2 * S5_STATE
    chunks_per_seq = seq // SSM_CHUNK
    scan_steps = (chunks_per_seq - 1).bit_length()
    gb = GROUPS_PER_SLAB
    rows = sr.shape[1]
    return pl.pallas_call(
        functools.partial(_ssm_kernel, chunks_per_seq=chunks_per_seq, scan_steps=scan_steps),
        grid=(nslab,),
        in_specs=[
            pl.BlockSpec((t, V7X_LANES), lambda s: (0, s)),
            pl.BlockSpec((None, cat, 2 * V7X_LANES), lambda s: (s, 0, 0)),
            pl.BlockSpec((None, SSM_CHUNK, V7X_LANES, ns), lambda s: (s, 0, 0, 0)),
            pl.BlockSpec((None, gb * ns, SSM_CHUNK * S5_GROUP), lambda s: (s, 0, 0)),
            pl.BlockSpec((gb, rows, ns), lambda s: (s, 0, 0)),
            pl.BlockSpec((gb, rows, ns), lambda s: (s, 0, 0)),
            pl.BlockSpec((1, V7X_LANES), lambda s: (0, s)),
            pl.BlockSpec((SSM_CHUNK * S5_GROUP, cat), lambda s: (0, 0)),
        ],
        out_specs=pl.BlockSpec((t, V7X_LANES), lambda s: (0, s)),
        out_shape=jax.ShapeDtypeStruct((t, w), F32),
        scratch_shapes=[pltpu.VMEM((nchunk, cat), BF16), pltpu.VMEM((cat, gb * ns), BF16),
                        pltpu.VMEM((gb * ns, cat), BF16), pltpu.VMEM((nchunk, gb * ns), BF16),
                        pltpu.VMEM((nchunk, cat), F32)],
        compiler_params=pltpu.CompilerParams(dimension_semantics=("parallel",),
                                             vmem_limit_bytes=SSM_VMEM_LIMIT),
        name=name,
    )(u, r_all, gst, hst, sr, si, d_skip, rep)


ATTN_SPLIT = 3
ATTN_EXTRA_ROWS = 16


def _split3(c):
    hi = c.astype(BF16).astype(F32)
    mid = (c - hi).astype(BF16).astype(F32)
    return hi, mid, c - hi - mid


def _fox_attn_kernel(qt_ref, k_ref, vt_ref, ct_ref, sz_ref, o_ref,
                     qa_ref, ka_ref, va_ref, s0_ref, s1_ref, smax_ref, m_ref, acc_ref, *, tile):
    dh, seq = qt_ref.shape
    ntiles = seq // tile
    ext = ATTN_EXTRA_ROWS

    @pl.when((pl.program_id(0) == 0) & (pl.program_id(1) == 0))
    def _():
        qa_ref[dh + ext:, :] = jnp.zeros((dh - ext, seq), BF16)
        rowv = lax.broadcasted_iota(jnp.int32, (ext, seq), 0)
        va_ref[dh:, :] = jnp.where(rowv == 0, 1.0, 0.0).astype(BF16)

    parts = _split3(ct_ref[...])
    qa_ref[:dh, :] = qt_ref[...]
    row16 = lax.broadcasted_iota(jnp.int32, (ext, seq), 0)
    ext_q = jnp.where(row16 < 2 * ATTN_SPLIT, 1.0, 0.0)
    for n, part in enumerate(parts):
        ext_q = jnp.where(row16 == n, part, ext_q)
    qa_ref[dh:dh + ext, :] = ext_q.astype(BF16)
    va_ref[:dh, :] = vt_ref[...]

    rowk = lax.broadcasted_iota(jnp.int32, (dh, tile), 0)
    for r in range(ntiles):
        cols = slice(r * tile, (r + 1) * tile)
        ext_kt = jnp.where(rowk < ATTN_SPLIT, 1.0, 0.0)
        for n, part in enumerate(parts):
            ext_kt = jnp.where(rowk == ATTN_SPLIT + n, -part[:, cols], ext_kt)
        ka_ref[cols, :dh] = k_ref[cols, :]
        ka_ref[cols, dh:] = ext_kt.T.astype(BF16)

    sub = tile // 2
    causal = (lax.broadcasted_iota(jnp.int32, (sub, sub), 0) <= lax.broadcasted_iota(jnp.int32, (sub, sub), 1))
    s_slots = (s0_ref, s1_ref)

    def scores(k0, nk, q0, nq):
        return jnp.dot(ka_ref[k0:k0 + nk, :], qa_ref[:, q0:q0 + nq], preferred_element_type=F32)

    def produce(slot, qi, kb):
        k0, q0 = kb * tile, qi * tile
        if kb == qi:
            s_slots[slot][:sub, :] = scores(k0, sub, q0, tile)
            s_slots[slot][sub:, sub:] = scores(k0 + sub, sub, q0 + sub, sub)
        else:
            st = scores(k0, tile, q0, tile)
            s_slots[slot][...] = st
            smax_ref[slot:slot + 1, :] = jnp.max(st, axis=0, keepdims=True)

    def update(par, st, st_max, k0, c0, first):
        nk, nq = st.shape
        cols = slice(c0, c0 + nq)
        pv = functools.partial(jnp.dot, va_ref[:, k0:k0 + nk], preferred_element_type=F32)
        if first:
            m_ref[par:par + 1, cols] = st_max
            acc_ref[par, :, cols] = pv(jnp.exp2(st - st_max).astype(BF16))
        else:
            m = m_ref[par:par + 1, cols]
            m_new = jnp.maximum(m, st_max)
            acc_ref[par, :, cols] = (jnp.exp2(m - m_new) * acc_ref[par, :, cols]
                                     + pv(jnp.exp2(st - m_new).astype(BF16)))
            m_ref[par:par + 1, cols] = m_new

    def consume(slot, qi, kb):
        par, k0, s_ref = qi % 2, kb * tile, s_slots[slot]
        if kb != qi:
            update(par, s_ref[...], smax_ref[slot:slot + 1, :], k0, 0, kb == 0)
            return
        early = jnp.where(causal, s_ref[:sub, :sub], NEG_INF)
        update(par, early, jnp.max(early, axis=0, keepdims=True), k0, 0, kb == 0)
        late = jnp.concatenate([s_ref[:sub, sub:], jnp.where(causal, s_ref[sub:, sub:], NEG_INF)], axis=0)
        update(par, late, jnp.max(late, axis=0, keepdims=True), k0, sub, kb == 0)

    def finalize(qi):
        acc = acc_ref[qi % 2]
        rows = slice(qi * tile, (qi + 1) * tile)
        o = (acc[:dh, :] / acc[dh:dh + 1, :]).T
        o_ref[rows, :] = (o * sz_ref[rows, :].astype(F32)).astype(o_ref.dtype)

    blocks = [(qi, kb) for qi in range(ntiles) for kb in range(qi + 1)]
    produce(0, *blocks[0])
    for n, (qi, kb) in enumerate(blocks):
        if n + 1 < len(blocks):
            produce((n + 1) % 2, *blocks[n + 1])
        consume(n % 2, qi, kb)
        if kb == qi:
            finalize(qi)


def fox_attention(qt, k, vt, cumt2, sz, *, batch, seq, heads, tile=1024, name="fox_attention"):
    width, t = qt.shape
    dh = width // heads
    natural = pl.BlockSpec((seq, dh), lambda b, h: (b, h))
    transposed = pl.BlockSpec((dh, seq), lambda b, h: (h, b))
    return pl.pallas_call(
        functools.partial(_fox_attn_kernel, tile=tile),
        grid=(batch, heads),
        in_specs=[transposed, natural, transposed,
                  pl.BlockSpec((None, None, 1, seq), lambda b, h: (b, h, 0, 0)),
                  natural],
        out_specs=natural,
        out_shape=jax.ShapeDtypeStruct((t, width), BF16),
        scratch_shapes=[pltpu.VMEM((2 * dh, seq), BF16), pltpu.VMEM((seq, 2 * dh), BF16),
                        pltpu.VMEM((dh + ATTN_EXTRA_ROWS, seq), BF16),
                        pltpu.VMEM((tile, tile), F32), pltpu.VMEM((tile, tile), F32),
                        pltpu.VMEM((V7X_SUBLANES, tile), F32), pltpu.VMEM((V7X_SUBLANES, tile), F32),
                        pltpu.VMEM((2, dh + ATTN_EXTRA_ROWS, tile), F32)],
        compiler_params=_params("arbitrary", "arbitrary"),
        name=name,
    )(qt, k, vt, cumt2, sz)


def _identity(v):
    return v


def _doubled(v):
    return jnp.concatenate([v, v], axis=-1)[:, None, :].astype(F32)


def kernel(x, norm_pre, norm_post, s5_w_in, s5_a_re, s5_a_im, s5_log_dt, s5_b_re, s5_b_im, s5_c_re, s5_c_im,
           s5_d, s5_w_glu, s5_b_glu, s5_w_out, kv_norm, kv_w, kv_b_f, fox_w_in, fox_w_out):
    batch, seq, d_model = x.shape
    t = batch * seq
    heads = kv_b_f.shape[0]
    fox_width = fox_w_in.shape[-1] // 2
    assert norm_pre.shape[0] == 2 and s5_w_in.shape[0] == 1 and fox_w_in.shape[0] == 1
    assert fox_width // heads == FOX_HEAD_DIM and heads <= V7X_LANES
    ngroups = s5_a_re.shape[1]
    chunks_per_seq = seq // SSM_CHUNK
    scan_steps = (chunks_per_seq - 1).bit_length()

    h0 = x.reshape(t, d_model)

    s5_width = s5_w_in.shape[-1] // 2
    w_in = s5_w_in[0].astype(BF16)
    u, sz = fused_proj(h0, norm_pre[0][None, :],
                       [Proj(w_in, False, 0, 0, _identity, F32, False),
                        Proj(w_in, False, s5_width, 0, jax.nn.silu, BF16, False)],
                       n=s5_width, tm=1024, name="s5_in_proj")
    ldt2 = jnp.broadcast_to(s5_log_dt[0][:, None, None], (ngroups, 1, 2 * S5_STATE)).astype(F32)
    b2 = jnp.concatenate([jnp.swapaxes(s5_b_re[0], 1, 2), jnp.swapaxes(s5_b_im[0], 1, 2)], axis=-1)
    c2 = jnp.concatenate([s5_c_re[0], s5_c_im[0]], axis=-1)
    rep = replication_matrix()
    r_all, gst, hst, sr, si = ssm_prep(_doubled(s5_a_re[0]), _doubled(s5_a_im[0]), ldt2, b2.astype(F32),
                                       c2.astype(F32), rep, scan_steps=scan_steps)
    yg = ssm_apply(u, r_all, gst, hst, sr, si, s5_d[0][None, :].astype(F32), rep, seq=seq)
    h1 = s5_tail(yg, s5_w_glu[0].astype(BF16), s5_b_glu[0][None, :], sz, s5_w_out[0].astype(BF16), h0,
                 norm_post[0][None, :])

    kv_cols = 2 * fox_width
    w_kv_t = kv_w.T.astype(BF16)
    wf_t = jnp.pad(w_kv_t[kv_cols:], ((0, V7X_LANES - heads), (0, 0)))
    bf = jnp.pad(kv_b_f, (0, V7X_LANES - heads))[None, :].astype(F32)
    w_in_fox = fox_w_in[0].astype(BF16)
    scale = FOX_HEAD_DIM ** -0.5 * LOG2_E
    k, vt, qt, sz2, cumt2 = fused_proj(
        h1, jnp.stack([kv_norm, norm_pre[1]]),
        [Proj(w_kv_t, True, 0, 0, _identity, BF16, False),
         Proj(w_kv_t, True, fox_width, 0, _identity, BF16, True),
         Proj(w_in_fox, False, 0, 1, lambda v: v * scale, BF16, True),
         Proj(w_in_fox, False, fox_width, 1, jax.nn.silu, BF16, False)],
        n=fox_width, gate=Gate(wf_t, bf, 0, heads, seq), vmem_limit=FOX_PROJ_VMEM_LIMIT, name="fox_proj")
    o = fox_attention(qt, k, vt, cumt2.reshape(batch, heads, 1, seq), sz2,
                      batch=batch, seq=seq, heads=heads)
    h2 = out_proj(o, fox_w_out[0].astype(BF16), h1, norm_post[1][None, :], name="fox_out_proj")
    return h2.reshape(batch, seq, d_model)
```

```python
import functools
import math
from typing import Callable, NamedTuple

import jax
import jax.numpy as jnp
from jax import lax
from jax.experimental import pallas as pl
from jax.experimental.pallas import tpu as pltpu

F32 = jnp.float32
BF16 = jnp.bfloat16

RMS_EPS = 1e-6
NEG_INF = -1e30
LOG2_E = math.log2(math.e)

S5_GROUP = 16
S5_STATE = 64
FOX_HEAD_DIM = 128

V7X_LANES = 128
V7X_SUBLANES = 8
SSM_CHUNK = 16
GROUPS_PER_SLAB = V7X_LANES // S5_GROUP
VMEM_LIMIT = 48 * 1024 * 1024
SSM_VMEM_LIMIT = 56 * 1024 * 1024
FOX_PROJ_VMEM_LIMIT = 56 * 1024 * 1024
S5_TAIL_VMEM_LIMIT = 56 * 1024 * 1024


def _params(*semantics):
    return pltpu.CompilerParams(dimension_semantics=semantics, vmem_limit_bytes=VMEM_LIMIT)


def _rms_scale(x, gain):
    ms = jnp.mean(x * x, axis=-1, keepdims=True)
    return x * lax.rsqrt(ms + RMS_EPS) * gain


class Proj(NamedTuple):
    weight: jax.Array
    weight_is_t: bool
    first: int
    gain: int
    act: Callable
    dtype: jnp.dtype
    transposed: bool


def _log_sigmoid(x):
    return -(jnp.maximum(-x, 0.0) + jnp.log1p(jnp.exp(-jnp.abs(x))))


def _cumsum_rows(c):
    rows = c.shape[0]
    row = lax.broadcasted_iota(jnp.int32, c.shape, 0)
    d = 1
    while d < rows:
        c = c + jnp.where(row >= d, pltpu.roll(c, d, axis=0), 0.0)
        d *= 2
    return c


class SideCast(NamedTuple):
    src: jax.Array
    rows_used: int


def _side_cast_specs(casts, steps, flat_step):
    in_specs, out_specs, out_shapes = [], [], []
    for c in casts:
        block = (c.rows_used // steps, c.src.shape[1])
        in_specs.append(pl.BlockSpec(block, lambda *idx: (flat_step(*idx), 0)))
        out_specs.append(pl.BlockSpec(block, lambda *idx: (flat_step(*idx), 0)))
        out_shapes.append(jax.ShapeDtypeStruct((c.rows_used, c.src.shape[1]), BF16))
    return in_specs, out_specs, out_shapes


def _do_side_casts(src_refs, dst_refs):
    for src_ref, dst_ref in zip(src_refs, dst_refs):
        dst_ref[...] = src_ref[...].astype(BF16)


def _fused_proj_kernel(*refs, projs, ngains, gate, nside):
    nproj = len(projs)
    x_ref, gains_ref = refs[:2]
    w_refs = refs[2:2 + nproj]
    pos = 2 + nproj
    if gate is not None:
        wf_ref, bf_ref = refs[pos:pos + 2]
        pos += 2
    side_src = refs[pos:pos + nside]
    pos += nside
    o_refs = refs[pos:pos + nproj]
    pos += nproj
    if gate is not None:
        cumt_ref = refs[pos]
        pos += 1
    _do_side_casts(side_src, refs[pos:pos + nside])
    pos += nside
    xn_ref = refs[pos]
    nt = (((1,), (1,)), ((), ()))

    @pl.when(pl.program_id(1) == 0)
    def _():
        x = x_ref[...]
        base = x * lax.rsqrt(jnp.mean(x * x, axis=-1, keepdims=True) + RMS_EPS)
        for g in range(ngains):
            xn_ref[g] = (base * gains_ref[g:g + 1, :]).astype(BF16)
        if gate is not None:
            carry_ref = refs[pos + 1]

            @pl.when(pl.program_id(0) % gate.tiles_per_seq == 0)
            def _():
                carry_ref[...] = jnp.zeros_like(carry_ref)

            logit = lax.dot_general(xn_ref[gate.gain], wf_ref[...], nt, preferred_element_type=F32) + bf_ref[...]
            c = _cumsum_rows(_log_sigmoid(logit)) + carry_ref[...]
            carry_ref[...] = c[c.shape[0] - 1:, :]
            cumt_ref[...] = (c * LOG2_E).T[:gate.heads, :]

    for p, w_ref, o_ref in zip(projs, w_refs, o_refs):
        xn = xn_ref[p.gain]
        if p.weight_is_t and p.transposed:
            y = lax.dot_general(w_ref[...], xn, nt, preferred_element_type=F32)
        elif p.weight_is_t:
            y = lax.dot_general(xn, w_ref[...], nt, preferred_element_type=F32)
        else:
            y = jnp.dot(xn, w_ref[...], preferred_element_type=F32)
            y = y.T if p.transposed else y
        o_ref[...] = p.act(y).astype(o_ref.dtype)


class Gate(NamedTuple):
    wf_t: jax.Array
    bias: jax.Array
    gain: int
    heads: int
    seq: int
    tiles_per_seq: int = 0


def fused_proj(x, gains, projs, *, n, gate=None, side=(), tm=512, tn=512, vmem_limit=VMEM_LIMIT,
               name="fused_proj"):
    t, d = x.shape
    ngains = gains.shape[0]
    nj = n // tn
    w_specs, out_specs, out_shapes = [], [], []
    for p in projs:
        if p.weight_is_t:
            w_specs.append(pl.BlockSpec((tn, d), lambda i, j, b0=p.first // tn: (j + b0, 0)))
        else:
            w_specs.append(pl.BlockSpec((d, tn), lambda i, j, b0=p.first // tn: (0, j + b0)))
        if p.transposed:
            out_specs.append(pl.BlockSpec((tn, tm), lambda i, j: (j, i)))
            out_shapes.append(jax.ShapeDtypeStruct((n, t), p.dtype))
        else:
            out_specs.append(pl.BlockSpec((tm, tn), lambda i, j: (i, j)))
            out_shapes.append(jax.ShapeDtypeStruct((t, n), p.dtype))
    operands = [x, gains] + [p.weight for p in projs]
    in_specs = [pl.BlockSpec((tm, d), lambda i, j: (i, 0)), pl.BlockSpec((ngains, d), lambda i, j: (0, 0))] + w_specs
    scratch = [pltpu.VMEM((ngains, tm, d), BF16)]
    kernel_projs = tuple(p._replace(weight=None) for p in projs)
    kernel_gate = None
    if gate is not None:
        tps = gate.seq // tm
        kernel_gate = gate._replace(wf_t=None, bias=None, tiles_per_seq=tps)
        operands += [gate.wf_t, gate.bias]
        in_specs += [pl.BlockSpec((V7X_LANES, d), lambda i, j: (0, 0)), pl.BlockSpec((1, V7X_LANES), lambda i, j: (0, 0))]
        out_specs.append(pl.BlockSpec((None, gate.heads, tm), lambda i, j: (i // tps, 0, i % tps)))
        out_shapes.append(jax.ShapeDtypeStruct((t // gate.seq, gate.heads, gate.seq), F32))
        scratch.append(pltpu.VMEM((1, V7X_LANES), F32))
    side_in, side_out, side_shapes = _side_cast_specs(side, (t // tm) * nj, lambda i, j: i * nj + j)
    operands += [c.src for c in side]
    in_specs += side_in
    out_specs += side_out
    out_shapes += side_shapes
    return pl.pallas_call(
        functools.partial(_fused_proj_kernel, projs=kernel_projs, ngains=ngains, gate=kernel_gate,
                          nside=len(side)),
        grid=(t // tm, nj),
        in_specs=in_specs,
        out_specs=out_specs,
        out_shape=out_shapes,
        scratch_shapes=scratch,
        compiler_params=pltpu.CompilerParams(dimension_semantics=("arbitrary", "arbitrary"),
                                             vmem_limit_bytes=vmem_limit),
        name=name,
    )(*operands)


def _out_proj_kernel(a_ref, w_ref, res_ref, gain_ref, o_ref, *, halves):
    rows = a_ref.shape[0] // halves
    for h in range(halves):
        r = slice(h * rows, (h + 1) * rows)
        y = jnp.dot(a_ref[r, :], w_ref[...], preferred_element_type=F32)
        o_ref[r, :] = res_ref[r, :] + _rms_scale(y, gain_ref[...])


def out_proj(a, w, res, gain, *, tm=512, halves=2, name="out_proj"):
    t, k = a.shape
    d = w.shape[1]
    return pl.pallas_call(
        functools.partial(_out_proj_kernel, halves=halves),
        grid=(t // tm,),
        in_specs=[
            pl.BlockSpec((tm, k), lambda i: (i, 0)),
            pl.BlockSpec((k, d), lambda i: (0, 0), pipeline_mode=pl.Buffered(1)),
            pl.BlockSpec((tm, d), lambda i: (i, 0)),
            pl.BlockSpec((1, d), lambda i: (0, 0)),
        ],
        out_specs=pl.BlockSpec((tm, d), lambda i: (i, 0)),
        out_shape=jax.ShapeDtypeStruct((t, d), F32),
        compiler_params=_params("parallel"),
        name=name,
    )(a, w, res, gain)


def _s5_tail_kernel(yg_ref, wg_ref, b_ref, sz_ref, wo_ref, res_ref, gain_ref, *rest, halves):
    nside = (len(rest) - 1) // 2
    o_ref = rest[nside]
    _do_side_casts(rest[:nside], rest[nside + 1:])
    rows = yg_ref.shape[0] // halves
    for h in range(halves):
        r = slice(h * rows, (h + 1) * rows)
        yg = yg_ref[r, :]
        gate = jax.nn.sigmoid(jnp.dot(yg.astype(BF16), wg_ref[...], preferred_element_type=F32) + b_ref[...])
        y3 = (yg * gate * sz_ref[r, :].astype(F32)).astype(BF16)
        y = jnp.dot(y3, wo_ref[...], preferred_element_type=F32)
        o_ref[r, :] = res_ref[r, :] + _rms_scale(y, gain_ref[...])


def s5_tail(yg, w_glu, b_glu, sz, w_out, res, gain, *, side=(), tm=512, halves=2, name="s5_tail"):
    t, k = yg.shape
    d = w_out.shape[1]
    rows = pl.BlockSpec((tm, k), lambda i: (i, 0))
    resident = functools.partial(pl.BlockSpec, index_map=lambda i: (0, 0), pipeline_mode=pl.Buffered(1))
    side_in, side_out, side_shapes = _side_cast_specs(side, t // tm, lambda i: i)
    return pl.pallas_call(
        functools.partial(_s5_tail_kernel, halves=halves),
        grid=(t // tm,),
        in_specs=[rows, resident((k, k)), resident((1, k)), rows, resident((k, d)),
                  pl.BlockSpec((tm, d), lambda i: (i, 0)), resident((1, d))] + side_in,
        out_specs=[pl.BlockSpec((tm, d), lambda i: (i, 0))] + side_out,
        out_shape=[jax.ShapeDtypeStruct((t, d), F32)] + side_shapes,
        compiler_params=pltpu.CompilerParams(dimension_semantics=("parallel",),
                                             vmem_limit_bytes=S5_TAIL_VMEM_LIMIT),
        name=name,
    )(yg, w_glu, b_glu, sz, w_out, res, gain, *[c.src for c in side])


def _glu_kernel(yg_ref, w_ref, b_ref, sz_ref, o_ref, yb_ref, *, tn):
    j = pl.program_id(1)

    @pl.when(j == 0)
    def _():
        yb_ref[...] = yg_ref[...].astype(BF16)

    gate = jax.nn.sigmoid(jnp.dot(yb_ref[...], w_ref[...], preferred_element_type=F32) + b_ref[...])
    yg = yg_ref[:, pl.ds(pl.multiple_of(j * tn, tn), tn)]
    o_ref[...] = (yg * gate * sz_ref[...].astype(F32)).astype(o_ref.dtype)


def glu_gate(yg, w, b, sz, *, tm=1024, tn=512, name="s5_glu"):
    t, k = yg.shape
    n = w.shape[1]
    return pl.pallas_call(
        functools.partial(_glu_kernel, tn=tn),
        grid=(t // tm, n // tn),
        in_specs=[
            pl.BlockSpec((tm, k), lambda i, j: (i, 0)),
            pl.BlockSpec((k, tn), lambda i, j: (0, j)),
            pl.BlockSpec((1, tn), lambda i, j: (0, j)),
            pl.BlockSpec((tm, tn), lambda i, j: (i, j)),
        ],
        out_specs=pl.BlockSpec((tm, tn), lambda i, j: (i, j)),
        out_shape=jax.ShapeDtypeStruct((t, n), BF16),
        scratch_shapes=[pltpu.VMEM((tm, k), BF16)],
        compiler_params=_params("parallel", "arbitrary"),
        name=name,
    )(yg, w, b, sz)


def _ssm_prep_kernel(are_ref, aim_ref, ldt_ref, b2_ref, c2_ref, rep_ref,
                     r_ref, g_ref, h_ref, sr_ref, si_ref, *, groups, scan_steps):
    lane = lax.broadcasted_iota(jnp.int32, (1, 2 * S5_STATE), 1)
    minus_plus = jnp.where(lane < S5_STATE, -1.0, 1.0).astype(F32)
    gw = S5_GROUP
    ns = 2 * S5_STATE
    krows = []

    for gi in range(groups):
        ar = are_ref[gi]
        ai = aim_ref[gi]
        dt = jnp.exp(ldt_ref[gi])
        mag = jnp.exp(ar * dt)
        lam_r = mag * jnp.cos(ai * dt)
        lam_i = mag * jnp.sin(ai * dt)
        den = ar * ar + ai * ai
        nr = lam_r - 1.0
        coef_r = (nr * ar + lam_i * ai) / den
        coef_i = (lam_i * ar - nr * ai) / den

        pow_r = [jnp.ones_like(lam_r)]
        pow_i = [jnp.zeros_like(lam_r)]
        for _ in range(SSM_CHUNK):
            pr, pi = pow_r[-1], pow_i[-1]
            pow_r.append(pr * lam_r - pi * lam_i)
            pow_i.append(pr * lam_i + pi * lam_r)

        b2 = b2_ref[gi]
        b2s = pltpu.roll(b2, S5_STATE, axis=1) * minus_plus
        c2 = c2_ref[gi]
        c2a = c2 * (-minus_plus)
        c2b = -pltpu.roll(c2, S5_STATE, axis=1)

        for step in range(SSM_CHUNK):
            pr, pi = pow_r[SSM_CHUNK - 1 - step], pow_i[SSM_CHUNK - 1 - step]
            wr = pr * coef_r - pi * coef_i
            wi = pr * coef_i + pi * coef_r
            g_ref[step, gi * gw:(gi + 1) * gw, :] = (wr * b2 + wi * b2s).astype(g_ref.dtype)

        cl = [pow_r[tau] * c2a + pow_i[tau] * c2b for tau in range(SSM_CHUNK + 1)]
        h_t = jnp.concatenate(cl[1:], axis=0)
        h_ref[gi * ns:(gi + 1) * ns, :] = h_t.T.astype(h_ref.dtype)

        bbar2 = coef_r * b2 + coef_i * b2s
        cl_all = jnp.concatenate(cl[:-1], axis=0)
        krows.append(lax.dot_general(bbar2, cl_all, (((1,), (1,)), ((), ())),
                                     precision=lax.Precision.HIGHEST,
                                     preferred_element_type=F32))

        mu_r, mu_i = pow_r[SSM_CHUNK], pow_i[SSM_CHUNK]
        sr_rows, si_rows = [], []
        for _ in range(scan_steps):
            sr_rows.append(mu_r)
            si_rows.append(mu_i * minus_plus)
            mu_r, mu_i = mu_r * mu_r - mu_i * mu_i, 2.0 * mu_r * mu_i
        pad = [jnp.zeros_like(mu_r)] * (sr_ref.shape[1] - scan_steps)
        sr_ref[gi] = jnp.concatenate(sr_rows + pad, axis=0)
        si_ref[gi] = jnp.concatenate(si_rows + pad, axis=0)

    kst = jnp.concatenate(krows, axis=0).astype(BF16)
    spread = jnp.dot(kst, rep_ref[...], preferred_element_type=F32)
    rows_g = lax.broadcasted_iota(jnp.int32, spread.shape, 0) // gw
    cols_g = _group_of_lane(lax.broadcasted_iota(jnp.int32, spread.shape, 1))
    spread = jnp.where(rows_g == cols_g, spread, 0.0).astype(r_ref.dtype)
    blocks = [spread[:, tau * V7X_LANES:(tau + 1) * V7X_LANES] for tau in range(SSM_CHUNK)]
    zero = jnp.zeros_like(blocks[0])
    npairs = SSM_CHUNK // 2
    for d in range(npairs):
        base = (npairs - 1 - d) * 2 * V7X_LANES
        top = jnp.concatenate([blocks[2 * d], blocks[2 * d + 1]], axis=1)
        bottom = jnp.concatenate([blocks[2 * d - 1] if d else zero, blocks[2 * d]], axis=1)
        r_ref[base:base + V7X_LANES, :] = top
        r_ref[base + V7X_LANES:base + 2 * V7X_LANES, :] = bottom


def ssm_prep(a_re2, a_im2, log_dt2, b2, c2, rep, *, scan_steps, name="ssm_prep"):
    ng = a_re2.shape[0]
    gb = GROUPS_PER_SLAB
    nslab = ng // gb
    cat = SSM_CHUNK * V7X_LANES
    ns = 2 * S5_STATE
    rows = -(-scan_steps // V7X_SUBLANES) * V7X_SUBLANES
    vec = pl.BlockSpec((gb, 1, ns), lambda i: (i, 0, 0))
    mat = pl.BlockSpec((gb, S5_GROUP, ns), lambda i: (i, 0, 0))
    return pl.pallas_call(
        functools.partial(_ssm_prep_kernel, groups=gb, scan_steps=scan_steps),
        grid=(nslab,),
        in_specs=[vec, vec, vec, mat, mat, pl.BlockSpec(rep.shape, lambda i: (0, 0))],
        out_specs=[
            pl.BlockSpec((None, cat, 2 * V7X_LANES), lambda i: (i, 0, 0)),
            pl.BlockSpec((None, SSM_CHUNK, V7X_LANES, ns), lambda i: (i, 0, 0, 0)),
            pl.BlockSpec((None, gb * ns, SSM_CHUNK * S5_GROUP), lambda i: (i, 0, 0)),
            pl.BlockSpec((gb, rows, ns), lambda i: (i, 0, 0)),
            pl.BlockSpec((gb, rows, ns), lambda i: (i, 0, 0)),
        ],
        out_shape=[
            jax.ShapeDtypeStruct((nslab, cat, 2 * V7X_LANES), BF16),
            jax.ShapeDtypeStruct((nslab, SSM_CHUNK, V7X_LANES, ns), BF16),
            jax.ShapeDtypeStruct((nslab, gb * ns, SSM_CHUNK * S5_GROUP), BF16),
            jax.ShapeDtypeStruct((ng, rows, ns), F32),
            jax.ShapeDtypeStruct((ng, rows, ns), F32),
        ],
        compiler_params=_params("parallel"),
        name=name,
    )(a_re2, a_im2, log_dt2, b2, c2, rep)


def replication_matrix():
    src = jnp.arange(SSM_CHUNK * S5_GROUP)
    dst = jnp.arange(SSM_CHUNK * V7X_LANES)
    same_step = (src[:, None] // S5_GROUP) == (dst[None, :] // V7X_LANES)
    same_chan = (src[:, None] % S5_GROUP) == (dst[None, :] % S5_GROUP)
    return (same_step & same_chan).astype(BF16)


def _group_of_lane(idx):
    return (idx % V7X_LANES) // S5_GROUP


def _ssm_kernel(u_ref, r_ref, gst_ref, hst_ref, sr_ref, si_ref, d_ref, rep_ref, *rest,
                chunks_per_seq, scan_steps, nside):
    y_ref = rest[nside]
    _do_side_casts(rest[:nside], rest[nside + 1:2 * nside + 1])
    ucat_ref, gexp_ref, hexp_ref, hprev_ref, intra_ref = rest[2 * nside + 1:]
    t = u_ref.shape[0]
    nchunk = t // SSM_CHUNK
    ns = 2 * S5_STATE
    pair = 2 * V7X_LANES
    npairs = SSM_CHUNK // 2

    for step in range(SSM_CHUNK):
        x = u_ref[pl.ds(step, nchunk, stride=SSM_CHUNK), :]
        ucat_ref[:, step * V7X_LANES:(step + 1) * V7X_LANES] = x.astype(BF16)

    g_rows = lax.broadcasted_iota(jnp.int32, (V7X_LANES, GROUPS_PER_SLAB * ns), 0) // S5_GROUP
    g_cols = lax.broadcasted_iota(jnp.int32, (V7X_LANES, GROUPS_PER_SLAB * ns), 1) // ns
    for step in range(SSM_CHUNK):
        tiled = jnp.concatenate([gst_ref[step]] * GROUPS_PER_SLAB, axis=1)
        gexp_ref[step * V7X_LANES:(step + 1) * V7X_LANES, :] = jnp.where(g_rows == g_cols, tiled, 0.0).astype(BF16)

    h_rows = lax.broadcasted_iota(jnp.int32, (GROUPS_PER_SLAB * ns, pair), 0) // ns
    h_cols = _group_of_lane(lax.broadcasted_iota(jnp.int32, (GROUPS_PER_SLAB * ns, pair), 1))
    for b in range(npairs):
        spread = jnp.dot(hst_ref[...], rep_ref[:, b * pair:(b + 1) * pair], preferred_element_type=F32)
        hexp_ref[:, b * pair:(b + 1) * pair] = jnp.where(h_rows == h_cols, spread, 0.0).astype(BF16)

    hs_all = jnp.dot(ucat_ref[...], gexp_ref[...], preferred_element_type=F32)
    for b in range(npairs):
        intra_ref[:, b * pair:(b + 1) * pair] = jnp.dot(
            ucat_ref[:, :(b + 1) * pair], r_ref[(npairs - 1 - b) * pair:, :], preferred_element_type=F32)

    row = lax.broadcasted_iota(jnp.int32, (nchunk, ns), 0) % chunks_per_seq
    for g in range(GROUPS_PER_SLAB):
        hs = hs_all[:, g * ns:(g + 1) * ns]
        for k in range(scan_steps):
            d = 1 << k
            sh = jnp.where(row >= d, pltpu.roll(hs, d, axis=0), 0.0)
            hs = hs + sr_ref[g, k:k + 1, :] * sh + si_ref[g, k:k + 1, :] * pltpu.roll(sh, S5_STATE, axis=1)
        hprev_ref[:, g * ns:(g + 1) * ns] = jnp.where(row >= 1, pltpu.roll(hs, 1, axis=0), 0.0).astype(BF16)

    for b in range(npairs):
        y = intra_ref[:, b * pair:(b + 1) * pair] + jnp.dot(
            hprev_ref[...], hexp_ref[:, b * pair:(b + 1) * pair], preferred_element_type=F32)
        for bit in range(2):
            step = 2 * b + bit
            lanes = slice(step * V7X_LANES, (step + 1) * V7X_LANES)
            yl = y[:, bit * V7X_LANES:(bit + 1) * V7X_LANES] + d_ref[...] * ucat_ref[:, lanes].astype(F32)
            y_ref[pl.ds(step, nchunk, stride=SSM_CHUNK), :] = jax.nn.gelu(yl, approximate=True)


def ssm_apply(u, r_all, gst, hst, sr, si, d_skip, rep, *, seq, side=(), name="s5_ssm"):
    t, w = u.shape
    nslab = w // V7X_LANES
    side_in, side_out, side_shapes = _side_cast_specs(side, nslab, lambda s: s)
    nchunk = t // SSM_CHUNK
    cat = SSM_CHUNK * V7X_LANES
    ns = 2 * S5_STATE
    chunks_per_seq = seq // SSM_CHUNK
    scan_steps = (chunks_per_seq - 1).bit_length()
    gb = GROUPS_PER_SLAB
    rows = sr.shape[1]
    return pl.pallas_call(
        functools.partial(_ssm_kernel, chunks_per_seq=chunks_per_seq, scan_steps=scan_steps, nside=len(side)),
        grid=(nslab,),
        in_specs=[
            pl.BlockSpec((t, V7X_LANES), lambda s: (0, s)),
            pl.BlockSpec((None, cat, 2 * V7X_LANES), lambda s: (s, 0, 0)),
            pl.BlockSpec((None, SSM_CHUNK, V7X_LANES, ns), lambda s: (s, 0, 0, 0)),
            pl.BlockSpec((None, gb * ns, SSM_CHUNK * S5_GROUP), lambda s: (s, 0, 0)),
            pl.BlockSpec((gb, rows, ns), lambda s: (s, 0, 0)),
            pl.BlockSpec((gb, rows, ns), lambda s: (s, 0, 0)),
            pl.BlockSpec((1, V7X_LANES), lambda s: (0, s)),
            pl.BlockSpec((SSM_CHUNK * S5_GROUP, cat), lambda s: (0, 0)),
        ] + side_in,
        out_specs=[pl.BlockSpec((t, V7X_LANES), lambda s: (0, s))] + side_out,
        out_shape=[jax.ShapeDtypeStruct((t, w), F32)] + side_shapes,
        scratch_shapes=[pltpu.VMEM((nchunk, cat), BF16), pltpu.VMEM((cat, gb * ns), BF16),
                        pltpu.VMEM((gb * ns, cat), BF16), pltpu.VMEM((nchunk, gb * ns), BF16),
                        pltpu.VMEM((nchunk, cat), F32)],
        compiler_params=pltpu.CompilerParams(dimension_semantics=("parallel",),
                                             vmem_limit_bytes=SSM_VMEM_LIMIT),
        name=name,
    )(u, r_all, gst, hst, sr, si, d_skip, rep, *[c.src for c in side])


ATTN_SPLIT = 3
ATTN_EXTRA_ROWS = 16


def _split3(c):
    hi = c.astype(BF16).astype(F32)
    mid = (c - hi).astype(BF16).astype(F32)
    return hi, mid, c - hi - mid


def _fox_attn_kernel(qt_ref, k_ref, vt_ref, ct_ref, sz_ref, o_ref,
                     qa_ref, ka_ref, va_ref, s0_ref, s1_ref, smax_ref, m_ref, acc_ref, *, tile):
    dh, seq = qt_ref.shape
    ntiles = seq // tile
    ext = ATTN_EXTRA_ROWS

    @pl.when((pl.program_id(0) == 0) & (pl.program_id(1) == 0))
    def _():
        qa_ref[dh + ext:, :] = jnp.zeros((dh - ext, seq), BF16)
        rowv = lax.broadcasted_iota(jnp.int32, (ext, seq), 0)
        va_ref[dh:, :] = jnp.where(rowv == 0, 1.0, 0.0).astype(BF16)

    parts = _split3(ct_ref[...])
    qa_ref[:dh, :] = qt_ref[...]
    row16 = lax.broadcasted_iota(jnp.int32, (ext, seq), 0)
    ext_q = jnp.where(row16 < 2 * ATTN_SPLIT, 1.0, 0.0)
    for n, part in enumerate(parts):
        ext_q = jnp.where(row16 == n, part, ext_q)
    qa_ref[dh:dh + ext, :] = ext_q.astype(BF16)
    va_ref[:dh, :] = vt_ref[...]

    rowk = lax.broadcasted_iota(jnp.int32, (dh, tile), 0)
    for r in range(ntiles):
        cols = slice(r * tile, (r + 1) * tile)
        ext_kt = jnp.where(rowk < ATTN_SPLIT, 1.0, 0.0)
        for n, part in enumerate(parts):
            ext_kt = jnp.where(rowk == ATTN_SPLIT + n, -part[:, cols], ext_kt)
        ka_ref[cols, :dh] = k_ref[cols, :]
        ka_ref[cols, dh:] = ext_kt.T.astype(BF16)

    sub = tile // 2
    causal = (lax.broadcasted_iota(jnp.int32, (sub, sub), 0) <= lax.broadcasted_iota(jnp.int32, (sub, sub), 1))
    s_slots = (s0_ref, s1_ref)

    def scores(k0, nk, q0, nq):
        return jnp.dot(ka_ref[k0:k0 + nk, :], qa_ref[:, q0:q0 + nq], preferred_element_type=F32)

    def produce(slot, qi, kb):
        k0, q0 = kb * tile, qi * tile
        if kb == qi:
            s_slots[slot][:sub, :] = scores(k0, sub, q0, tile)
            s_slots[slot][sub:, sub:] = scores(k0 + sub, sub, q0 + sub, sub)
        else:
            st = scores(k0, tile, q0, tile)
            s_slots[slot][...] = st
            smax_ref[slot:slot + 1, :] = jnp.max(st, axis=0, keepdims=True)

    def update(par, st, st_max, k0, c0, first):
        nk, nq = st.shape
        cols = slice(c0, c0 + nq)
        pv = functools.partial(jnp.dot, va_ref[:, k0:k0 + nk], preferred_element_type=F32)
        if first:
            m_ref[par:par + 1, cols] = st_max
            acc_ref[par, :, cols] = pv(jnp.exp2(st - st_max).astype(BF16))
        else:
            m = m_ref[par:par + 1, cols]
            m_new = jnp.maximum(m, st_max)
            acc_ref[par, :, cols] = (jnp.exp2(m - m_new) * acc_ref[par, :, cols]
                                     + pv(jnp.exp2(st - m_new).astype(BF16)))
            m_ref[par:par + 1, cols] = m_new

    def consume(slot, qi, kb):
        par, k0, s_ref = qi % 2, kb * tile, s_slots[slot]
        if kb != qi:
            update(par, s_ref[...], smax_ref[slot:slot + 1, :], k0, 0, kb == 0)
            return
        early = jnp.where(causal, s_ref[:sub, :sub], NEG_INF)
        update(par, early, jnp.max(early, axis=0, keepdims=True), k0, 0, kb == 0)
        late = jnp.concatenate([s_ref[:sub, sub:], jnp.where(causal, s_ref[sub:, sub:], NEG_INF)], axis=0)
        update(par, late, jnp.max(late, axis=0, keepdims=True), k0, sub, kb == 0)

    def finalize(qi):
        acc = acc_ref[qi % 2]
        rows = slice(qi * tile, (qi + 1) * tile)
        o = (acc[:dh, :] / acc[dh:dh + 1, :]).T
        o_ref[rows, :] = (o * sz_ref[rows, :].astype(F32)).astype(o_ref.dtype)

    blocks = [(qi, kb) for qi in range(ntiles) for kb in range(qi + 1)]
    produce(0, *blocks[0])
    for n, (qi, kb) in enumerate(blocks):
        if n + 1 < len(blocks):
            produce((n + 1) % 2, *blocks[n + 1])
        consume(n % 2, qi, kb)
        if kb == qi:
            finalize(qi)


def fox_attention(qt, k, vt, cumt2, sz, *, batch, seq, heads, tile=1024, name="fox_attention"):
    width, t = qt.shape
    dh = width // heads
    natural = pl.BlockSpec((seq, dh), lambda b, h: (b, h))
    transposed = pl.BlockSpec((dh, seq), lambda b, h: (h, b))
    return pl.pallas_call(
        functools.partial(_fox_attn_kernel, tile=tile),
        grid=(batch, heads),
        in_specs=[transposed, natural, transposed,
                  pl.BlockSpec((None, None, 1, seq), lambda b, h: (b, h, 0, 0)),
                  natural],
        out_specs=natural,
        out_shape=jax.ShapeDtypeStruct((t, width), BF16),
        scratch_shapes=[pltpu.VMEM((2 * dh, seq), BF16), pltpu.VMEM((seq, 2 * dh), BF16),
                        pltpu.VMEM((dh + ATTN_EXTRA_ROWS, seq), BF16),
                        pltpu.VMEM((tile, tile), F32), pltpu.VMEM((tile, tile), F32),
                        pltpu.VMEM((V7X_SUBLANES, tile), F32), pltpu.VMEM((V7X_SUBLANES, tile), F32),
                        pltpu.VMEM((2, dh + ATTN_EXTRA_ROWS, tile), F32)],
        compiler_params=_params("arbitrary", "arbitrary"),
        name=name,
    )(qt, k, vt, cumt2, sz)


def _identity(v):
    return v


def _doubled(v):
    return jnp.concatenate([v, v], axis=-1)[:, None, :].astype(F32)


def kernel(x, norm_pre, norm_post, s5_w_in, s5_a_re, s5_a_im, s5_log_dt, s5_b_re, s5_b_im, s5_c_re, s5_c_im,
           s5_d, s5_w_glu, s5_b_glu, s5_w_out, kv_norm, kv_w, kv_b_f, fox_w_in, fox_w_out):
    batch, seq, d_model = x.shape
    t = batch * seq
    heads = kv_b_f.shape[0]
    fox_width = fox_w_in.shape[-1] // 2
    assert norm_pre.shape[0] == 2 and s5_w_in.shape[0] == 1 and fox_w_in.shape[0] == 1
    assert fox_width // heads == FOX_HEAD_DIM and heads <= V7X_LANES
    ngroups = s5_a_re.shape[1]
    chunks_per_seq = seq // SSM_CHUNK
    scan_steps = (chunks_per_seq - 1).bit_length()

    h0 = x.reshape(t, d_model)

    s5_width = s5_w_in.shape[-1] // 2
    w_in = s5_w_in[0].astype(BF16)
    kv_cols = 2 * fox_width
    kv_w_t = kv_w.T
    u, sz, w_glu, w_out, w_kv_t = fused_proj(
        h0, norm_pre[0][None, :],
        [Proj(w_in, False, 0, 0, _identity, F32, False), Proj(w_in, False, s5_width, 0, jax.nn.silu, BF16, False)],
        n=s5_width, tm=1024, name="s5_in_proj",
        side=[SideCast(s5_w_glu[0], s5_w_glu.shape[1]), SideCast(s5_w_out[0], s5_w_out.shape[1]),
              SideCast(kv_w_t, kv_cols)])
    ldt2 = jnp.broadcast_to(s5_log_dt[0][:, None, None], (ngroups, 1, 2 * S5_STATE)).astype(F32)
    b2 = jnp.concatenate([jnp.swapaxes(s5_b_re[0], 1, 2), jnp.swapaxes(s5_b_im[0], 1, 2)], axis=-1)
    c2 = jnp.concatenate([s5_c_re[0], s5_c_im[0]], axis=-1)
    rep = replication_matrix()
    r_all, gst, hst, sr, si = ssm_prep(_doubled(s5_a_re[0]), _doubled(s5_a_im[0]), ldt2, b2.astype(F32),
                                       c2.astype(F32), rep, scan_steps=scan_steps)
    yg, w_in_fox = ssm_apply(u, r_all, gst, hst, sr, si, s5_d[0][None, :].astype(F32), rep, seq=seq,
                             side=[SideCast(fox_w_in[0], fox_w_in.shape[1])])
    h1, w_out_fox = s5_tail(yg, w_glu, s5_b_glu[0][None, :], sz, w_out, h0, norm_post[0][None, :],
                            side=[SideCast(fox_w_out[0], fox_w_out.shape[1])])

    wf_t = jnp.pad(kv_w_t[kv_cols:].astype(BF16), ((0, V7X_LANES - heads), (0, 0)))
    bf = jnp.pad(kv_b_f, (0, V7X_LANES - heads))[None, :].astype(F32)
    scale = FOX_HEAD_DIM ** -0.5 * LOG2_E
    k, vt, qt, sz2, cumt2 = fused_proj(
        h1, jnp.stack([kv_norm, norm_pre[1]]),
        [Proj(w_kv_t, True, 0, 0, _identity, BF16, False),
         Proj(w_kv_t, True, fox_width, 0, _identity, BF16, True),
         Proj(w_in_fox, False, 0, 1, lambda v: v * scale, BF16, True),
         Proj(w_in_fox, False, fox_width, 1, jax.nn.silu, BF16, False)],
        n=fox_width, gate=Gate(wf_t, bf, 0, heads, seq), vmem_limit=FOX_PROJ_VMEM_LIMIT, name="fox_proj")
    o = fox_attention(qt, k, vt, cumt2.reshape(batch, heads, 1, seq), sz2,
                      batch=batch, seq=seq, heads=heads)
    h2 = out_proj(o, w_out_fox, h1, norm_post[1][None, :], name="fox_out_proj")
    return h2.reshape(batch, seq, d_model)
```

```python
import functools
import math
from typing import Callable, NamedTuple

import jax
import jax.numpy as jnp
from jax import lax
from jax.experimental import pallas as pl
from jax.experimental.pallas import tpu as pltpu

F32 = jnp.float32
BF16 = jnp.bfloat16

RMS_EPS = 1e-6
NEG_INF = -1e30
LOG2_E = math.log2(math.e)

S5_GROUP = 16
S5_STATE = 64
FOX_HEAD_DIM = 128

V7X_LANES = 128
V7X_SUBLANES = 8
SSM_CHUNK = 16
GROUPS_PER_SLAB = V7X_LANES // S5_GROUP
VMEM_LIMIT = 48 * 1024 * 1024
SSM_VMEM_LIMIT = 56 * 1024 * 1024
FOX_PROJ_VMEM_LIMIT = 56 * 1024 * 1024
S5_TAIL_VMEM_LIMIT = 56 * 1024 * 1024


def _params(*semantics):
    return pltpu.CompilerParams(dimension_semantics=semantics, vmem_limit_bytes=VMEM_LIMIT)


def _rms_scale(x, gain):
    ms = jnp.mean(x * x, axis=-1, keepdims=True)
    return x * lax.rsqrt(ms + RMS_EPS) * gain


class Proj(NamedTuple):
    weight: jax.Array
    weight_is_t: bool
    first: int
    gain: int
    act: Callable
    dtype: jnp.dtype
    transposed: bool


def _log_sigmoid(x):
    return -(jnp.maximum(-x, 0.0) + jnp.log1p(jnp.exp(-jnp.abs(x))))


def _cumsum_rows(c):
    rows = c.shape[0]
    row = lax.broadcasted_iota(jnp.int32, c.shape, 0)
    d = 1
    while d < rows:
        c = c + jnp.where(row >= d, pltpu.roll(c, d, axis=0), 0.0)
        d *= 2
    return c


class SideCast(NamedTuple):
    src: jax.Array
    rows_used: int


def _side_cast_specs(casts, steps, flat_step):
    in_specs, out_specs, out_shapes = [], [], []
    for c in casts:
        block = (c.rows_used // steps, c.src.shape[1])
        in_specs.append(pl.BlockSpec(block, lambda *idx: (flat_step(*idx), 0)))
        out_specs.append(pl.BlockSpec(block, lambda *idx: (flat_step(*idx), 0)))
        out_shapes.append(jax.ShapeDtypeStruct((c.rows_used, c.src.shape[1]), BF16))
    return in_specs, out_specs, out_shapes


def _do_side_casts(src_refs, dst_refs):
    for src_ref, dst_ref in zip(src_refs, dst_refs):
        dst_ref[...] = src_ref[...].astype(BF16)


def _fused_proj_kernel(*refs, projs, ngains, gate, nside):
    nproj = len(projs)
    x_ref, gains_ref = refs[:2]
    w_refs = refs[2:2 + nproj]
    pos = 2 + nproj
    if gate is not None:
        wf_ref, bf_ref = refs[pos:pos + 2]
        pos += 2
    side_src = refs[pos:pos + nside]
    pos += nside
    o_refs = refs[pos:pos + nproj]
    pos += nproj
    if gate is not None:
        cumt_ref = refs[pos]
        pos += 1
    _do_side_casts(side_src, refs[pos:pos + nside])
    pos += nside
    xn_ref = refs[pos]
    nt = (((1,), (1,)), ((), ()))

    @pl.when(pl.program_id(1) == 0)
    def _():
        x = x_ref[...]
        base = x * lax.rsqrt(jnp.mean(x * x, axis=-1, keepdims=True) + RMS_EPS)
        for g in range(ngains):
            xn_ref[g] = (base * gains_ref[g:g + 1, :]).astype(BF16)
        if gate is not None:
            carry_ref = refs[pos + 1]

            @pl.when(pl.program_id(0) % gate.tiles_per_seq == 0)
            def _():
                carry_ref[...] = jnp.zeros_like(carry_ref)

            logit = lax.dot_general(xn_ref[gate.gain], wf_ref[...], nt, preferred_element_type=F32) + bf_ref[...]
            c = _cumsum_rows(_log_sigmoid(logit)) + carry_ref[...]
            carry_ref[...] = c[c.shape[0] - 1:, :]
            cumt_ref[...] = (c * LOG2_E).T[:gate.heads, :]

    for p, w_ref, o_ref in zip(projs, w_refs, o_refs):
        xn = xn_ref[p.gain]
        if p.weight_is_t and p.transposed:
            y = lax.dot_general(w_ref[...], xn, nt, preferred_element_type=F32)
        elif p.weight_is_t:
            y = lax.dot_general(xn, w_ref[...], nt, preferred_element_type=F32)
        else:
            y = jnp.dot(xn, w_ref[...], preferred_element_type=F32)
            y = y.T if p.transposed else y
        o_ref[...] = p.act(y).astype(o_ref.dtype)


class Gate(NamedTuple):
    wf_t: jax.Array
    bias: jax.Array
    gain: int
    heads: int
    seq: int
    tiles_per_seq: int = 0


def fused_proj(x, gains, projs, *, n, gate=None, side=(), tm=512, tn=512, vmem_limit=VMEM_LIMIT,
               name="fused_proj"):
    t, d = x.shape
    ngains = gains.shape[0]
    nj = n // tn
    w_specs, out_specs, out_shapes = [], [], []
    for p in projs:
        if p.weight_is_t:
            w_specs.append(pl.BlockSpec((tn, d), lambda i, j, b0=p.first // tn: (j + b0, 0)))
        else:
            w_specs.append(pl.BlockSpec((d, tn), lambda i, j, b0=p.first // tn: (0, j + b0)))
        if p.transposed:
            out_specs.append(pl.BlockSpec((tn, tm), lambda i, j: (j, i)))
            out_shapes.append(jax.ShapeDtypeStruct((n, t), p.dtype))
        else:
            out_specs.append(pl.BlockSpec((tm, tn), lambda i, j: (i, j)))
            out_shapes.append(jax.ShapeDtypeStruct((t, n), p.dtype))
    operands = [x, gains] + [p.weight for p in projs]
    in_specs = [pl.BlockSpec((tm, d), lambda i, j: (i, 0)), pl.BlockSpec((ngains, d), lambda i, j: (0, 0))] + w_specs
    scratch = [pltpu.VMEM((ngains, tm, d), BF16)]
    kernel_projs = tuple(p._replace(weight=None) for p in projs)
    kernel_gate = None
    if gate is not None:
        tps = gate.seq // tm
        kernel_gate = gate._replace(wf_t=None, bias=None, tiles_per_seq=tps)
        operands += [gate.wf_t, gate.bias]
        in_specs += [pl.BlockSpec((V7X_LANES, d), lambda i, j: (0, 0)), pl.BlockSpec((1, V7X_LANES), lambda i, j: (0, 0))]
        out_specs.append(pl.BlockSpec((None, gate.heads, tm), lambda i, j: (i // tps, 0, i % tps)))
        out_shapes.append(jax.ShapeDtypeStruct((t // gate.seq, gate.heads, gate.seq), F32))
        scratch.append(pltpu.VMEM((1, V7X_LANES), F32))
    side_in, side_out, side_shapes = _side_cast_specs(side, (t // tm) * nj, lambda i, j: i * nj + j)
    operands += [c.src for c in side]
    in_specs += side_in
    out_specs += side_out
    out_shapes += side_shapes
    return pl.pallas_call(
        functools.partial(_fused_proj_kernel, projs=kernel_projs, ngains=ngains, gate=kernel_gate,
                          nside=len(side)),
        grid=(t // tm, nj),
        in_specs=in_specs,
        out_specs=out_specs,
        out_shape=out_shapes,
        scratch_shapes=scratch,
        compiler_params=pltpu.CompilerParams(dimension_semantics=("arbitrary", "arbitrary"),
                                             vmem_limit_bytes=vmem_limit),
        name=name,
    )(*operands)


def _out_proj_kernel(a_ref, w_ref, res_ref, gain_ref, o_ref, *, halves):
    rows = a_ref.shape[0] // halves
    for h in range(halves):
        r = slice(h * rows, (h + 1) * rows)
        y = jnp.dot(a_ref[r, :], w_ref[...], preferred_element_type=F32)
        o_ref[r, :] = res_ref[r, :] + _rms_scale(y, gain_ref[...])


def out_proj(a, w, res, gain, *, tm=512, halves=1, name="out_proj"):
    t, k = a.shape
    d = w.shape[1]
    return pl.pallas_call(
        functools.partial(_out_proj_kernel, halves=halves),
        grid=(t // tm,),
        in_specs=[
            pl.BlockSpec((tm, k), lambda i: (i, 0)),
            pl.BlockSpec((k, d), lambda i: (0, 0), pipeline_mode=pl.Buffered(1)),
            pl.BlockSpec((tm, d), lambda i: (i, 0)),
            pl.BlockSpec((1, d), lambda i: (0, 0)),
        ],
        out_specs=pl.BlockSpec((tm, d), lambda i: (i, 0)),
        out_shape=jax.ShapeDtypeStruct((t, d), F32),
        compiler_params=_params("parallel"),
        name=name,
    )(a, w, res, gain)


def _s5_tail_kernel(yg_ref, wg_ref, b_ref, sz_ref, wo_ref, res_ref, gain_ref, *rest, halves):
    nside = (len(rest) - 1) // 2
    o_ref = rest[nside]
    _do_side_casts(rest[:nside], rest[nside + 1:])
    rows = yg_ref.shape[0] // halves
    for h in range(halves):
        r = slice(h * rows, (h + 1) * rows)
        yg = yg_ref[r, :]
        gate = jax.nn.sigmoid(jnp.dot(yg.astype(BF16), wg_ref[...], preferred_element_type=F32) + b_ref[...])
        y3 = (yg * gate * sz_ref[r, :].astype(F32)).astype(BF16)
        y = jnp.dot(y3, wo_ref[...], preferred_element_type=F32)
        o_ref[r, :] = res_ref[r, :] + _rms_scale(y, gain_ref[...])


def s5_tail(yg, w_glu, b_glu, sz, w_out, res, gain, *, side=(), tm=512, halves=2, name="s5_tail"):
    t, k = yg.shape
    d = w_out.shape[1]
    rows = pl.BlockSpec((tm, k), lambda i: (i, 0))
    resident = functools.partial(pl.BlockSpec, index_map=lambda i: (0, 0), pipeline_mode=pl.Buffered(1))
    side_in, side_out, side_shapes = _side_cast_specs(side, t // tm, lambda i: i)
    return pl.pallas_call(
        functools.partial(_s5_tail_kernel, halves=halves),
        grid=(t // tm,),
        in_specs=[rows, resident((k, k)), resident((1, k)), rows, resident((k, d)),
                  pl.BlockSpec((tm, d), lambda i: (i, 0)), resident((1, d))] + side_in,
        out_specs=[pl.BlockSpec((tm, d), lambda i: (i, 0))] + side_out,
        out_shape=[jax.ShapeDtypeStruct((t, d), F32)] + side_shapes,
        compiler_params=pltpu.CompilerParams(dimension_semantics=("parallel",),
                                             vmem_limit_bytes=S5_TAIL_VMEM_LIMIT),
        name=name,
    )(yg, w_glu, b_glu, sz, w_out, res, gain, *[c.src for c in side])


def _ssm_prep_kernel(are_ref, aim_ref, ldt_ref, b2_ref, c2_ref, rep_ref, *rest, groups, scan_steps):
    nside = (len(rest) - 5) // 2
    r_ref, g_ref, h_ref, sr_ref, si_ref = rest[nside:nside + 5]
    _do_side_casts(rest[:nside], rest[nside + 5:])
    lane = lax.broadcasted_iota(jnp.int32, (1, 2 * S5_STATE), 1)
    minus_plus = jnp.where(lane < S5_STATE, -1.0, 1.0).astype(F32)
    gw = S5_GROUP
    ns = 2 * S5_STATE
    krows = []

    for gi in range(groups):
        ar = are_ref[gi]
        ai = aim_ref[gi]
        dt = jnp.exp(ldt_ref[gi])
        mag = jnp.exp(ar * dt)
        lam_r = mag * jnp.cos(ai * dt)
        lam_i = mag * jnp.sin(ai * dt)
        den = ar * ar + ai * ai
        nr = lam_r - 1.0
        coef_r = (nr * ar + lam_i * ai) / den
        coef_i = (lam_i * ar - nr * ai) / den

        pow_r = [jnp.ones_like(lam_r)]
        pow_i = [jnp.zeros_like(lam_r)]
        for _ in range(SSM_CHUNK):
            pr, pi = pow_r[-1], pow_i[-1]
            pow_r.append(pr * lam_r - pi * lam_i)
            pow_i.append(pr * lam_i + pi * lam_r)

        b2 = b2_ref[gi]
        b2s = pltpu.roll(b2, S5_STATE, axis=1) * minus_plus
        c2 = c2_ref[gi]
        c2a = c2 * (-minus_plus)
        c2b = -pltpu.roll(c2, S5_STATE, axis=1)

        for step in range(SSM_CHUNK):
            pr, pi = pow_r[SSM_CHUNK - 1 - step], pow_i[SSM_CHUNK - 1 - step]
            wr = pr * coef_r - pi * coef_i
            wi = pr * coef_i + pi * coef_r
            g_ref[step, gi * gw:(gi + 1) * gw, :] = (wr * b2 + wi * b2s).astype(g_ref.dtype)

        cl = [pow_r[tau] * c2a + pow_i[tau] * c2b for tau in range(SSM_CHUNK + 1)]
        h_t = jnp.concatenate(cl[1:], axis=0)
        h_ref[gi * ns:(gi + 1) * ns, :] = h_t.T.astype(h_ref.dtype)

        bbar2 = coef_r * b2 + coef_i * b2s
        cl_all = jnp.concatenate(cl[:-1], axis=0)
        krows.append(lax.dot_general(bbar2, cl_all, (((1,), (1,)), ((), ())),
                                     precision=lax.Precision.HIGHEST,
                                     preferred_element_type=F32))

        mu_r, mu_i = pow_r[SSM_CHUNK], pow_i[SSM_CHUNK]
        sr_rows, si_rows = [], []
        for _ in range(scan_steps):
            sr_rows.append(mu_r)
            si_rows.append(mu_i * minus_plus)
            mu_r, mu_i = mu_r * mu_r - mu_i * mu_i, 2.0 * mu_r * mu_i
        pad = [jnp.zeros_like(mu_r)] * (sr_ref.shape[1] - scan_steps)
        sr_ref[gi] = jnp.concatenate(sr_rows + pad, axis=0)
        si_ref[gi] = jnp.concatenate(si_rows + pad, axis=0)

    kst = jnp.concatenate(krows, axis=0).astype(BF16)
    spread = jnp.dot(kst, rep_ref[...], preferred_element_type=F32)
    rows_g = lax.broadcasted_iota(jnp.int32, spread.shape, 0) // gw
    cols_g = _group_of_lane(lax.broadcasted_iota(jnp.int32, spread.shape, 1))
    spread = jnp.where(rows_g == cols_g, spread, 0.0).astype(r_ref.dtype)
    blocks = [spread[:, tau * V7X_LANES:(tau + 1) * V7X_LANES] for tau in range(SSM_CHUNK)]
    zero = jnp.zeros_like(blocks[0])
    npairs = SSM_CHUNK // 2
    for d in range(npairs):
        base = (npairs - 1 - d) * 2 * V7X_LANES
        top = jnp.concatenate([blocks[2 * d], blocks[2 * d + 1]], axis=1)
        bottom = jnp.concatenate([blocks[2 * d - 1] if d else zero, blocks[2 * d]], axis=1)
        r_ref[base:base + V7X_LANES, :] = top
        r_ref[base + V7X_LANES:base + 2 * V7X_LANES, :] = bottom


def ssm_prep(a_re2, a_im2, log_dt2, b2, c2, rep, *, scan_steps, side=(), name="ssm_prep"):
    ng = a_re2.shape[0]
    gb = GROUPS_PER_SLAB
    nslab = ng // gb
    cat = SSM_CHUNK * V7X_LANES
    ns = 2 * S5_STATE
    rows = -(-scan_steps // V7X_SUBLANES) * V7X_SUBLANES
    vec = pl.BlockSpec((gb, 1, ns), lambda i: (i, 0, 0))
    mat = pl.BlockSpec((gb, S5_GROUP, ns), lambda i: (i, 0, 0))
    side_in, side_out, side_shapes = _side_cast_specs(side, nslab, lambda i: i)
    return pl.pallas_call(
        functools.partial(_ssm_prep_kernel, groups=gb, scan_steps=scan_steps),
        grid=(nslab,),
        in_specs=[vec, vec, vec, mat, mat, pl.BlockSpec(rep.shape, lambda i: (0, 0))] + side_in,
        out_specs=[
            pl.BlockSpec((None, cat, 2 * V7X_LANES), lambda i: (i, 0, 0)),
            pl.BlockSpec((None, SSM_CHUNK, V7X_LANES, ns), lambda i: (i, 0, 0, 0)),
            pl.BlockSpec((None, gb * ns, SSM_CHUNK * S5_GROUP), lambda i: (i, 0, 0)),
            pl.BlockSpec((gb, rows, ns), lambda i: (i, 0, 0)),
            pl.BlockSpec((gb, rows, ns), lambda i: (i, 0, 0)),
        ] + side_out,
        out_shape=[
            jax.ShapeDtypeStruct((nslab, cat, 2 * V7X_LANES), BF16),
            jax.ShapeDtypeStruct((nslab, SSM_CHUNK, V7X_LANES, ns), BF16),
            jax.ShapeDtypeStruct((nslab, gb * ns, SSM_CHUNK * S5_GROUP), BF16),
            jax.ShapeDtypeStruct((ng, rows, ns), F32),
            jax.ShapeDtypeStruct((ng, rows, ns), F32),
        ] + side_shapes,
        compiler_params=_params("parallel"),
        name=name,
    )(a_re2, a_im2, log_dt2, b2, c2, rep, *[c.src for c in side])


def replication_matrix():
    src = jnp.arange(SSM_CHUNK * S5_GROUP)
    dst = jnp.arange(SSM_CHUNK * V7X_LANES)
    same_step = (src[:, None] // S5_GROUP) == (dst[None, :] // V7X_LANES)
    same_chan = (src[:, None] % S5_GROUP) == (dst[None, :] % S5_GROUP)
    return (same_step & same_chan).astype(BF16)


def _group_of_lane(idx):
    return (idx % V7X_LANES) // S5_GROUP


def _ssm_kernel(u_ref, r_ref, gst_ref, hst_ref, sr_ref, si_ref, d_ref, rep_ref, *rest,
                chunks_per_seq, scan_steps, nside):
    y_ref = rest[nside]
    _do_side_casts(rest[:nside], rest[nside + 1:2 * nside + 1])
    ucat_ref, gexp_ref, hexp_ref, hprev_ref, intra_ref = rest[2 * nside + 1:]
    t = u_ref.shape[0]
    nchunk = t // SSM_CHUNK
    ns = 2 * S5_STATE
    pair = 2 * V7X_LANES
    npairs = SSM_CHUNK // 2

    for step in range(SSM_CHUNK):
        x = u_ref[pl.ds(step, nchunk, stride=SSM_CHUNK), :]
        ucat_ref[:, step * V7X_LANES:(step + 1) * V7X_LANES] = x.astype(BF16)

    g_rows = lax.broadcasted_iota(jnp.int32, (V7X_LANES, GROUPS_PER_SLAB * ns), 0) // S5_GROUP
    g_cols = lax.broadcasted_iota(jnp.int32, (V7X_LANES, GROUPS_PER_SLAB * ns), 1) // ns
    for step in range(SSM_CHUNK):
        tiled = jnp.concatenate([gst_ref[step]] * GROUPS_PER_SLAB, axis=1)
        gexp_ref[step * V7X_LANES:(step + 1) * V7X_LANES, :] = jnp.where(g_rows == g_cols, tiled, 0.0).astype(BF16)

    h_rows = lax.broadcasted_iota(jnp.int32, (GROUPS_PER_SLAB * ns, pair), 0) // ns
    h_cols = _group_of_lane(lax.broadcasted_iota(jnp.int32, (GROUPS_PER_SLAB * ns, pair), 1))
    for b in range(npairs):
        spread = jnp.dot(hst_ref[...], rep_ref[:, b * pair:(b + 1) * pair], preferred_element_type=F32)
        hexp_ref[:, b * pair:(b + 1) * pair] = jnp.where(h_rows == h_cols, spread, 0.0).astype(BF16)

    hs_all = jnp.dot(ucat_ref[...], gexp_ref[...], preferred_element_type=F32)
    for b in range(npairs):
        intra_ref[:, b * pair:(b + 1) * pair] = jnp.dot(
            ucat_ref[:, :(b + 1) * pair], r_ref[(npairs - 1 - b) * pair:, :], preferred_element_type=F32)

    row = lax.broadcasted_iota(jnp.int32, (nchunk, ns), 0) % chunks_per_seq
    for g in range(GROUPS_PER_SLAB):
        hs = hs_all[:, g * ns:(g + 1) * ns]
        for k in range(scan_steps):
            d = 1 << k
            sh = jnp.where(row >= d, pltpu.roll(hs, d, axis=0), 0.0)
            hs = hs + sr_ref[g, k:k + 1, :] * sh + si_ref[g, k:k + 1, :] * pltpu.roll(sh, S5_STATE, axis=1)
        hprev_ref[:, g * ns:(g + 1) * ns] = jnp.where(row >= 1, pltpu.roll(hs, 1, axis=0), 0.0).astype(BF16)

    for b in range(npairs):
        y = intra_ref[:, b * pair:(b + 1) * pair] + jnp.dot(
            hprev_ref[...], hexp_ref[:, b * pair:(b + 1) * pair], preferred_element_type=F32)
        for bit in range(2):
            step = 2 * b + bit
            lanes = slice(step * V7X_LANES, (step + 1) * V7X_LANES)
            yl = y[:, bit * V7X_LANES:(bit + 1) * V7X_LANES] + d_ref[...] * ucat_ref[:, lanes].astype(F32)
            y_ref[pl.ds(step, nchunk, stride=SSM_CHUNK), :] = jax.nn.gelu(yl, approximate=True)


def ssm_apply(u, r_all, gst, hst, sr, si, d_skip, rep, *, seq, side=(), name="s5_ssm"):
    t, w = u.shape
    nslab = w // V7X_LANES
    side_in, side_out, side_shapes = _side_cast_specs(side, nslab, lambda s: s)
    nchunk = t // SSM_CHUNK
    cat = SSM_CHUNK * V7X_LANES
    ns = 2 * S5_STATE
    chunks_per_seq = seq // SSM_CHUNK
    scan_steps = (chunks_per_seq - 1).bit_length()
    gb = GROUPS_PER_SLAB
    rows = sr.shape[1]
    return pl.pallas_call(
        functools.partial(_ssm_kernel, chunks_per_seq=chunks_per_seq, scan_steps=scan_steps, nside=len(side)),
        grid=(nslab,),
        in_specs=[
            pl.BlockSpec((t, V7X_LANES), lambda s: (0, s)),
            pl.BlockSpec((None, cat, 2 * V7X_LANES), lambda s: (s, 0, 0)),
            pl.BlockSpec((None, SSM_CHUNK, V7X_LANES, ns), lambda s: (s, 0, 0, 0)),
            pl.BlockSpec((None, gb * ns, SSM_CHUNK * S5_GROUP), lambda s: (s, 0, 0)),
            pl.BlockSpec((gb, rows, ns), lambda s: (s, 0, 0)),
            pl.BlockSpec((gb, rows, ns), lambda s: (s, 0, 0)),
            pl.BlockSpec((1, V7X_LANES), lambda s: (0, s)),
            pl.BlockSpec((SSM_CHUNK * S5_GROUP, cat), lambda s: (0, 0)),
        ] + side_in,
        out_specs=[pl.BlockSpec((t, V7X_LANES), lambda s: (0, s))] + side_out,
        out_shape=[jax.ShapeDtypeStruct((t, w), F32)] + side_shapes,
        scratch_shapes=[pltpu.VMEM((nchunk, cat), BF16), pltpu.VMEM((cat, gb * ns), BF16),
                        pltpu.VMEM((gb * ns, cat), BF16), pltpu.VMEM((nchunk, gb * ns), BF16),
                        pltpu.VMEM((nchunk, cat), F32)],
        compiler_params=pltpu.CompilerParams(dimension_semantics=("parallel",),
                                             vmem_limit_bytes=SSM_VMEM_LIMIT),
        name=name,
    )(u, r_all, gst, hst, sr, si, d_skip, rep, *[c.src for c in side])


ATTN_SPLIT = 3
ATTN_EXTRA_ROWS = 16


def _split3(c):
    hi = c.astype(BF16).astype(F32)
    mid = (c - hi).astype(BF16).astype(F32)
    return hi, mid, c - hi - mid


def _fox_attn_kernel(qt_ref, k_ref, vt_ref, ct_ref, sz_ref, o_ref,
                     qa_ref, ka_ref, va_ref, s0_ref, s1_ref, smax_ref, m_ref, acc_ref, *, tile):
    dh, seq = qt_ref.shape
    ntiles = seq // tile
    ext = ATTN_EXTRA_ROWS

    @pl.when((pl.program_id(0) == 0) & (pl.program_id(1) == 0))
    def _():
        qa_ref[dh + ext:, :] = jnp.zeros((dh - ext, seq), BF16)
        rowv = lax.broadcasted_iota(jnp.int32, (ext, seq), 0)
        va_ref[dh:, :] = jnp.where(rowv == 0, 1.0, 0.0).astype(BF16)

    parts = _split3(ct_ref[...])
    qa_ref[:dh, :] = qt_ref[...]
    row16 = lax.broadcasted_iota(jnp.int32, (ext, seq), 0)
    ext_q = jnp.where(row16 < 2 * ATTN_SPLIT, 1.0, 0.0)
    for n, part in enumerate(parts):
        ext_q = jnp.where(row16 == n, part, ext_q)
    qa_ref[dh:dh + ext, :] = ext_q.astype(BF16)
    va_ref[:dh, :] = vt_ref[...]

    rowk = lax.broadcasted_iota(jnp.int32, (dh, tile), 0)
    for r in range(ntiles):
        cols = slice(r * tile, (r + 1) * tile)
        ext_kt = jnp.where(rowk < ATTN_SPLIT, 1.0, 0.0)
        for n, part in enumerate(parts):
            ext_kt = jnp.where(rowk == ATTN_SPLIT + n, -part[:, cols], ext_kt)
        ka_ref[cols, :dh] = k_ref[cols, :]
        ka_ref[cols, dh:] = ext_kt.T.astype(BF16)

    sub = tile // 2
    causal = (lax.broadcasted_iota(jnp.int32, (sub, sub), 0) <= lax.broadcasted_iota(jnp.int32, (sub, sub), 1))
    s_slots = (s0_ref, s1_ref)

    def scores(k0, nk, q0, nq):
        return jnp.dot(ka_ref[k0:k0 + nk, :], qa_ref[:, q0:q0 + nq], preferred_element_type=F32)

    def produce(slot, qi, kb):
        k0, q0 = kb * tile, qi * tile
        if kb == qi:
            s_slots[slot][:sub, :] = scores(k0, sub, q0, tile)
            s_slots[slot][sub:, sub:] = scores(k0 + sub, sub, q0 + sub, sub)
        else:
            st = scores(k0, tile, q0, tile)
            s_slots[slot][...] = st
            smax_ref[slot:slot + 1, :] = jnp.max(st, axis=0, keepdims=True)

    def update(par, st, st_max, k0, c0, first):
        nk, nq = st.shape
        cols = slice(c0, c0 + nq)
        pv = functools.partial(jnp.dot, va_ref[:, k0:k0 + nk], preferred_element_type=F32)
        if first:
            m_ref[par:par + 1, cols] = st_max
            acc_ref[par, :, cols] = pv(jnp.exp2(st - st_max).astype(BF16))
        else:
            m = m_ref[par:par + 1, cols]
            m_new = jnp.maximum(m, st_max)
            acc_ref[par, :, cols] = (jnp.exp2(m - m_new) * acc_ref[par, :, cols]
                                     + pv(jnp.exp2(st - m_new).astype(BF16)))
            m_ref[par:par + 1, cols] = m_new

    def consume(slot, qi, kb):
        par, k0, s_ref = qi % 2, kb * tile, s_slots[slot]
        if kb != qi:
            update(par, s_ref[...], smax_ref[slot:slot + 1, :], k0, 0, kb == 0)
            return
        early = jnp.where(causal, s_ref[:sub, :sub], NEG_INF)
        update(par, early, jnp.max(early, axis=0, keepdims=True), k0, 0, kb == 0)
        late = jnp.concatenate([s_ref[:sub, sub:], jnp.where(causal, s_ref[sub:, sub:], NEG_INF)], axis=0)
        update(par, late, jnp.max(late, axis=0, keepdims=True), k0, sub, kb == 0)

    def finalize(qi):
        acc = acc_ref[qi % 2]
        rows = slice(qi * tile, (qi + 1) * tile)
        o = (acc[:dh, :] / acc[dh:dh + 1, :]).T
        o_ref[rows, :] = (o * sz_ref[rows, :].astype(F32)).astype(o_ref.dtype)

    blocks = [(qi, kb) for qi in range(ntiles) for kb in range(qi + 1)]
    produce(0, *blocks[0])
    for n, (qi, kb) in enumerate(blocks):
        if n + 1 < len(blocks):
            produce((n + 1) % 2, *blocks[n + 1])
        consume(n % 2, qi, kb)
        if kb == qi:
            finalize(qi)


def fox_attention(qt, k, vt, cumt2, sz, *, batch, seq, heads, tile=1024, name="fox_attention"):
    width, t = qt.shape
    dh = width // heads
    natural = pl.BlockSpec((seq, dh), lambda b, h: (b, h))
    transposed = pl.BlockSpec((dh, seq), lambda b, h: (h, b))
    return pl.pallas_call(
        functools.partial(_fox_attn_kernel, tile=tile),
        grid=(batch, heads),
        in_specs=[transposed, natural, transposed,
                  pl.BlockSpec((None, None, 1, seq), lambda b, h: (b, h, 0, 0)),
                  natural],
        out_specs=natural,
        out_shape=jax.ShapeDtypeStruct((t, width), BF16),
        scratch_shapes=[pltpu.VMEM((2 * dh, seq), BF16), pltpu.VMEM((seq, 2 * dh), BF16),
                        pltpu.VMEM((dh + ATTN_EXTRA_ROWS, seq), BF16),
                        pltpu.VMEM((tile, tile), F32), pltpu.VMEM((tile, tile), F32),
                        pltpu.VMEM((V7X_SUBLANES, tile), F32), pltpu.VMEM((V7X_SUBLANES, tile), F32),
                        pltpu.VMEM((2, dh + ATTN_EXTRA_ROWS, tile), F32)],
        compiler_params=_params("arbitrary", "arbitrary"),
        name=name,
    )(qt, k, vt, cumt2, sz)


def _identity(v):
    return v


def _doubled(v):
    return jnp.concatenate([v, v], axis=-1)[:, None, :].astype(F32)


def kernel(x, norm_pre, norm_post, s5_w_in, s5_a_re, s5_a_im, s5_log_dt, s5_b_re, s5_b_im, s5_c_re, s5_c_im,
           s5_d, s5_w_glu, s5_b_glu, s5_w_out, kv_norm, kv_w, kv_b_f, fox_w_in, fox_w_out):
    batch, seq, d_model = x.shape
    t = batch * seq
    heads = kv_b_f.shape[0]
    fox_width = fox_w_in.shape[-1] // 2
    assert norm_pre.shape[0] == 2 and s5_w_in.shape[0] == 1 and fox_w_in.shape[0] == 1
    assert fox_width // heads == FOX_HEAD_DIM and heads <= V7X_LANES
    ngroups = s5_a_re.shape[1]
    chunks_per_seq = seq // SSM_CHUNK
    scan_steps = (chunks_per_seq - 1).bit_length()

    h0 = x.reshape(t, d_model)

    s5_width = s5_w_in.shape[-1] // 2
    ldt2 = jnp.broadcast_to(s5_log_dt[0][:, None, None], (ngroups, 1, 2 * S5_STATE)).astype(F32)
    b2 = jnp.concatenate([jnp.swapaxes(s5_b_re[0], 1, 2), jnp.swapaxes(s5_b_im[0], 1, 2)], axis=-1)
    c2 = jnp.concatenate([s5_c_re[0], s5_c_im[0]], axis=-1)
    rep = replication_matrix()
    r_all, gst, hst, sr, si, w_in = ssm_prep(_doubled(s5_a_re[0]), _doubled(s5_a_im[0]), ldt2, b2.astype(F32),
                                             c2.astype(F32), rep, scan_steps=scan_steps,
                                             side=[SideCast(s5_w_in[0], s5_w_in.shape[1])])
    kv_cols = 2 * fox_width
    kv_w_t = kv_w.T
    u, sz, w_glu, w_out, w_kv_t = fused_proj(
        h0, norm_pre[0][None, :],
        [Proj(w_in, False, 0, 0, _identity, F32, False), Proj(w_in, False, s5_width, 0, jax.nn.silu, BF16, False)],
        n=s5_width, tm=1024, name="s5_in_proj",
        side=[SideCast(s5_w_glu[0], s5_w_glu.shape[1]), SideCast(s5_w_out[0], s5_w_out.shape[1]),
              SideCast(kv_w_t, kv_cols)])
    yg, w_in_fox = ssm_apply(u, r_all, gst, hst, sr, si, s5_d[0][None, :].astype(F32), rep, seq=seq,
                             side=[SideCast(fox_w_in[0], fox_w_in.shape[1])])
    h1, w_out_fox = s5_tail(yg, w_glu, s5_b_glu[0][None, :], sz, w_out, h0, norm_post[0][None, :],
                            side=[SideCast(fox_w_out[0], fox_w_out.shape[1])])

    wf_t = jnp.pad(kv_w_t[kv_cols:].astype(BF16), ((0, V7X_LANES - heads), (0, 0)))
    bf = jnp.pad(kv_b_f, (0, V7X_LANES - heads))[None, :].astype(F32)
    scale = FOX_HEAD_DIM ** -0.5 * LOG2_E
    k, vt, qt, sz2, cumt2 = fused_proj(
        h1, jnp.stack([kv_norm, norm_pre[1]]),
        [Proj(w_kv_t, True, 0, 0, _identity, BF16, False),
         Proj(w_kv_t, True, fox_width, 0, _identity, BF16, True),
         Proj(w_in_fox, False, 0, 1, lambda v: v * scale, BF16, True),
         Proj(w_in_fox, False, fox_width, 1, jax.nn.silu, BF16, False)],
        n=fox_width, gate=Gate(wf_t, bf, 0, heads, seq), vmem_limit=FOX_PROJ_VMEM_LIMIT, name="fox_proj")
    o = fox_attention(qt, k, vt, cumt2.reshape(batch, heads, 1, seq), sz2,
                      batch=batch, seq=seq, heads=heads)
    h2 = out_proj(o, w_out_fox, h1, norm_post[1][None, :], name="fox_out_proj")
    return h2.reshape(batch, seq, d_model)
```

```python
import functools
import math
from typing import Callable, NamedTuple

import jax
import jax.numpy as jnp
from jax import lax
from jax.experimental import pallas as pl
from jax.experimental.pallas import tpu as pltpu

F32 = jnp.float32
BF16 = jnp.bfloat16

RMS_EPS = 1e-6
NEG_INF = -1e30
LOG2_E = math.log2(math.e)

S5_GROUP = 16
S5_STATE = 64
FOX_HEAD_DIM = 128

V7X_LANES = 128
V7X_SUBLANES = 8
SSM_CHUNK = 16
GROUPS_PER_SLAB = V7X_LANES // S5_GROUP
VMEM_LIMIT = 48 * 1024 * 1024
SSM_VMEM_LIMIT = 56 * 1024 * 1024
FOX_PROJ_VMEM_LIMIT = 56 * 1024 * 1024
S5_TAIL_VMEM_LIMIT = 56 * 1024 * 1024


def _params(*semantics):
    return pltpu.CompilerParams(dimension_semantics=semantics, vmem_limit_bytes=VMEM_LIMIT)


def _rms_scale(x, gain):
    ms = jnp.mean(x * x, axis=-1, keepdims=True)
    return x * lax.rsqrt(ms + RMS_EPS) * gain


class Proj(NamedTuple):
    weight: jax.Array
    weight_is_t: bool
    first: int
    act: Callable
    dtype: jnp.dtype
    transposed: bool


def _log_sigmoid(x):
    return -(jnp.maximum(-x, 0.0) + jnp.log1p(jnp.exp(-jnp.abs(x))))


def _cumsum_rows(c):
    rows = c.shape[0]
    row = lax.broadcasted_iota(jnp.int32, c.shape, 0)
    d = 1
    while d < rows:
        c = c + jnp.where(row >= d, pltpu.roll(c, d, axis=0), 0.0)
        d *= 2
    return c


class SideCast(NamedTuple):
    src: jax.Array
    rows_used: int
    scale: jax.Array


def side_cast(src, rows_used=None, *, row_scale=None, col_scale=None):
    rows_used = src.shape[0] if rows_used is None else rows_used
    if row_scale is not None:
        scale = row_scale.reshape(rows_used, 1)
    elif col_scale is not None:
        scale = col_scale.reshape(1, src.shape[1])
    else:
        scale = jnp.ones((1, src.shape[1]), F32)
    return SideCast(src, rows_used, scale.astype(F32))


def _side_cast_specs(casts, steps, flat_step):
    in_specs, out_specs, out_shapes, operands = [], [], [], []
    for c in casts:
        block = (c.rows_used // steps, c.src.shape[1])
        by_step = lambda *idx: (flat_step(*idx), 0)
        in_specs.append(pl.BlockSpec(block, by_step))
        if c.scale.shape[0] == 1:
            in_specs.append(pl.BlockSpec(c.scale.shape, lambda *idx: (0, 0)))
        else:
            in_specs.append(pl.BlockSpec((block[0], 1), by_step))
        operands += [c.src, c.scale]
        out_specs.append(pl.BlockSpec(block, by_step))
        out_shapes.append(jax.ShapeDtypeStruct((c.rows_used, c.src.shape[1]), BF16))
    return in_specs, out_specs, out_shapes, operands


def _do_side_casts(in_refs, dst_refs):
    for n, dst_ref in enumerate(dst_refs):
        dst_ref[...] = (in_refs[2 * n][...] * in_refs[2 * n + 1][...]).astype(BF16)


def _inv_rms(x):
    inv = lax.rsqrt(jnp.mean(x * x, axis=-1, keepdims=True) + RMS_EPS)
    return jnp.broadcast_to(inv, (x.shape[0], V7X_LANES))


def _fused_proj_kernel(*refs, projs, gate, nside, prenormed):
    nproj = len(projs)
    x_ref = refs[0]
    pos = 1
    if prenormed:
        inv_in_ref = refs[1]
        pos = 2
    w_refs = refs[pos:pos + nproj]
    pos += nproj
    if gate is not None:
        wf_ref, bf_ref = refs[pos:pos + 2]
        pos += 2
    side_in = refs[pos:pos + 2 * nside]
    pos += 2 * nside
    o_refs = refs[pos:pos + nproj]
    pos += nproj
    if gate is not None:
        cumt_ref = refs[pos]
        pos += 1
    _do_side_casts(side_in, refs[pos:pos + nside])
    pos += nside
    if prenormed:
        xb_ref, inv_ref = x_ref, inv_in_ref
    else:
        xb_ref, inv_ref = refs[pos:pos + 2]
        pos += 2
    invt_ref = refs[pos]
    pos += 1
    nt = (((1,), (1,)), ((), ()))
    any_transposed = any(p.transposed for p in projs)

    @pl.when(pl.program_id(1) == 0)
    def _():
        if not prenormed:
            x = x_ref[...]
            xb_ref[...] = x.astype(BF16)
            inv_ref[...] = _inv_rms(x)
        if any_transposed:
            invt_ref[...] = inv_ref[...].T[:V7X_SUBLANES, :]
        if gate is not None:
            carry_ref = refs[pos]

            @pl.when(pl.program_id(0) % gate.tiles_per_seq == 0)
            def _():
                carry_ref[...] = jnp.zeros_like(carry_ref)

            logit = lax.dot_general(xb_ref[...], wf_ref[...], nt, preferred_element_type=F32)
            c = _cumsum_rows(_log_sigmoid(logit * inv_ref[...] + bf_ref[...])) + carry_ref[...]
            carry_ref[...] = c[c.shape[0] - 1:, :]
            cumt_ref[...] = (c * LOG2_E).T[:gate.heads, :]

    xb = xb_ref[...]
    for p, w_ref, o_ref in zip(projs, w_refs, o_refs):
        if p.weight_is_t and p.transposed:
            y = lax.dot_general(w_ref[...], xb, nt, preferred_element_type=F32)
        elif p.weight_is_t:
            y = lax.dot_general(xb, w_ref[...], nt, preferred_element_type=F32)
        else:
            y = jnp.dot(xb, w_ref[...], preferred_element_type=F32)
            y = y.T if p.transposed else y
        if p.transposed:
            y = y * invt_ref[0:1, :]
        else:
            y = y * jnp.concatenate([inv_ref[...]] * (y.shape[1] // V7X_LANES), axis=1)
        o_ref[...] = p.act(y).astype(o_ref.dtype)


class Gate(NamedTuple):
    wf_t: jax.Array
    bias: jax.Array
    heads: int
    seq: int
    tiles_per_seq: int = 0


def fused_proj(x, projs, *, n, inv=None, gate=None, side=(), tm=512, tn=512, vmem_limit=VMEM_LIMIT,
               name="fused_proj"):
    t, d = x.shape
    prenormed = inv is not None
    nj = n // tn
    w_specs, out_specs, out_shapes = [], [], []
    for p in projs:
        if p.weight_is_t:
            w_specs.append(pl.BlockSpec((tn, d), lambda i, j, b0=p.first // tn: (j + b0, 0)))
        else:
            w_specs.append(pl.BlockSpec((d, tn), lambda i, j, b0=p.first // tn: (0, j + b0)))
        if p.transposed:
            out_specs.append(pl.BlockSpec((tn, tm), lambda i, j: (j, i)))
            out_shapes.append(jax.ShapeDtypeStruct((n, t), p.dtype))
        else:
            out_specs.append(pl.BlockSpec((tm, tn), lambda i, j: (i, j)))
            out_shapes.append(jax.ShapeDtypeStruct((t, n), p.dtype))
    operands = [x] + ([inv] if prenormed else []) + [p.weight for p in projs]
    in_specs = [pl.BlockSpec((tm, d), lambda i, j: (i, 0))]
    if prenormed:
        in_specs.append(pl.BlockSpec((tm, V7X_LANES), lambda i, j: (i, 0)))
    in_specs += w_specs
    scratch = [] if prenormed else [pltpu.VMEM((tm, d), BF16), pltpu.VMEM((tm, V7X_LANES), F32)]
    scratch.append(pltpu.VMEM((V7X_SUBLANES, tm), F32))
    kernel_projs = tuple(p._replace(weight=None) for p in projs)
    kernel_gate = None
    if gate is not None:
        tps = gate.seq // tm
        kernel_gate = gate._replace(wf_t=None, bias=None, tiles_per_seq=tps)
        operands += [gate.wf_t, gate.bias]
        in_specs += [pl.BlockSpec((V7X_LANES, d), lambda i, j: (0, 0)), pl.BlockSpec((1, V7X_LANES), lambda i, j: (0, 0))]
        out_specs.append(pl.BlockSpec((None, gate.heads, tm), lambda i, j: (i // tps, 0, i % tps)))
        out_shapes.append(jax.ShapeDtypeStruct((t // gate.seq, gate.heads, gate.seq), F32))
        scratch.append(pltpu.VMEM((1, V7X_LANES), F32))
    side_in, side_out, side_shapes, side_ops = _side_cast_specs(side, (t // tm) * nj, lambda i, j: i * nj + j)
    operands += side_ops
    in_specs += side_in
    out_specs += side_out
    out_shapes += side_shapes
    return pl.pallas_call(
        functools.partial(_fused_proj_kernel, projs=kernel_projs, gate=kernel_gate, nside=len(side),
                          prenormed=prenormed),
        grid=(t // tm, nj),
        in_specs=in_specs,
        out_specs=out_specs,
        out_shape=out_shapes,
        scratch_shapes=scratch,
        compiler_params=pltpu.CompilerParams(dimension_semantics=("arbitrary", "arbitrary"),
                                             vmem_limit_bytes=vmem_limit),
        name=name,
    )(*operands)


def _out_proj_kernel(a_ref, w_ref, res_ref, gain_ref, o_ref, *, halves):
    rows = a_ref.shape[0] // halves
    for h in range(halves):
        r = slice(h * rows, (h + 1) * rows)
        y = jnp.dot(a_ref[r, :], w_ref[...], preferred_element_type=F32)
        o_ref[r, :] = res_ref[r, :] + _rms_scale(y, gain_ref[...])


def out_proj(a, w, res, gain, *, tm=512, halves=1, name="out_proj"):
    t, k = a.shape
    d = w.shape[1]
    return pl.pallas_call(
        functools.partial(_out_proj_kernel, halves=halves),
        grid=(t // tm,),
        in_specs=[
            pl.BlockSpec((tm, k), lambda i: (i, 0)),
            pl.BlockSpec((k, d), lambda i: (0, 0), pipeline_mode=pl.Buffered(1)),
            pl.BlockSpec((tm, d), lambda i: (i, 0)),
            pl.BlockSpec((1, d), lambda i: (0, 0)),
        ],
        out_specs=pl.BlockSpec((tm, d), lambda i: (i, 0)),
        out_shape=jax.ShapeDtypeStruct((t, d), F32),
        compiler_params=_params("parallel"),
        name=name,
    )(a, w, res, gain)


def _s5_tail_kernel(yg_ref, wg_ref, b_ref, sz_ref, wo_ref, res_ref, gain_ref, o_ref, ob_ref, inv_ref, *, halves):
    rows = yg_ref.shape[0] // halves
    for h in range(halves):
        r = slice(h * rows, (h + 1) * rows)
        yg = yg_ref[r, :]
        gate = jax.nn.sigmoid(jnp.dot(yg.astype(BF16), wg_ref[...], preferred_element_type=F32) + b_ref[...])
        y3 = (yg * gate * sz_ref[r, :].astype(F32)).astype(BF16)
        y = jnp.dot(y3, wo_ref[...], preferred_element_type=F32)
        out = res_ref[r, :] + _rms_scale(y, gain_ref[...])
        o_ref[r, :] = out
        ob_ref[r, :] = out.astype(BF16)
        inv_ref[r, :] = _inv_rms(out)


def s5_tail(yg, w_glu, b_glu, sz, w_out, res, gain, *, tm=512, halves=2, name="s5_tail"):
    t, k = yg.shape
    d = w_out.shape[1]
    rows = pl.BlockSpec((tm, k), lambda i: (i, 0))
    resident = functools.partial(pl.BlockSpec, index_map=lambda i: (0, 0), pipeline_mode=pl.Buffered(1))
    return pl.pallas_call(
        functools.partial(_s5_tail_kernel, halves=halves),
        grid=(t // tm,),
        in_specs=[rows, resident((k, k)), resident((1, k)), rows, resident((k, d)),
                  pl.BlockSpec((tm, d), lambda i: (i, 0)), resident((1, d))],
        out_specs=[pl.BlockSpec((tm, d), lambda i: (i, 0)), pl.BlockSpec((tm, d), lambda i: (i, 0)),
                   pl.BlockSpec((tm, V7X_LANES), lambda i: (i, 0))],
        out_shape=[jax.ShapeDtypeStruct((t, d), F32), jax.ShapeDtypeStruct((t, d), BF16),
                   jax.ShapeDtypeStruct((t, V7X_LANES), F32)],
        compiler_params=pltpu.CompilerParams(dimension_semantics=("parallel",),
                                             vmem_limit_bytes=S5_TAIL_VMEM_LIMIT),
        name=name,
    )(yg, w_glu, b_glu, sz, w_out, res, gain)


def _ssm_prep_kernel(are_ref, aim_ref, ldt_ref, b2_ref, c2_ref, rep_ref, *rest, groups, scan_steps):
    nside = (len(rest) - 5) // 3
    r_ref, g_ref, h_ref, sr_ref, si_ref = rest[2 * nside:2 * nside + 5]
    _do_side_casts(rest[:2 * nside], rest[2 * nside + 5:])
    lane = lax.broadcasted_iota(jnp.int32, (1, 2 * S5_STATE), 1)
    minus_plus = jnp.where(lane < S5_STATE, -1.0, 1.0).astype(F32)
    gw = S5_GROUP
    ns = 2 * S5_STATE
    krows = []

    for gi in range(groups):
        ar = are_ref[gi]
        ai = aim_ref[gi]
        dt = jnp.exp(ldt_ref[gi])
        mag = jnp.exp(ar * dt)
        lam_r = mag * jnp.cos(ai * dt)
        lam_i = mag * jnp.sin(ai * dt)
        den = ar * ar + ai * ai
        nr = lam_r - 1.0
        coef_r = (nr * ar + lam_i * ai) / den
        coef_i = (lam_i * ar - nr * ai) / den

        pow_r = [jnp.ones_like(lam_r)]
        pow_i = [jnp.zeros_like(lam_r)]
        for _ in range(SSM_CHUNK):
            pr, pi = pow_r[-1], pow_i[-1]
            pow_r.append(pr * lam_r - pi * lam_i)
            pow_i.append(pr * lam_i + pi * lam_r)

        b2 = b2_ref[gi]
        b2s = pltpu.roll(b2, S5_STATE, axis=1) * minus_plus
        c2 = c2_ref[gi]
        c2a = c2 * (-minus_plus)
        c2b = -pltpu.roll(c2, S5_STATE, axis=1)

        for step in range(SSM_CHUNK):
            pr, pi = pow_r[SSM_CHUNK - 1 - step], pow_i[SSM_CHUNK - 1 - step]
            wr = pr * coef_r - pi * coef_i
            wi = pr * coef_i + pi * coef_r
            g_ref[step, gi * gw:(gi + 1) * gw, :] = (wr * b2 + wi * b2s).astype(g_ref.dtype)

        cl = [pow_r[tau] * c2a + pow_i[tau] * c2b for tau in range(SSM_CHUNK + 1)]
        h_t = jnp.concatenate(cl[1:], axis=0)
        h_ref[gi * ns:(gi + 1) * ns, :] = h_t.T.astype(h_ref.dtype)

        bbar2 = coef_r * b2 + coef_i * b2s
        cl_all = jnp.concatenate(cl[:-1], axis=0)
        krows.append(lax.dot_general(bbar2, cl_all, (((1,), (1,)), ((), ())),
                                     precision=lax.Precision.HIGHEST,
                                     preferred_element_type=F32))

        mu_r, mu_i = pow_r[SSM_CHUNK], pow_i[SSM_CHUNK]
        sr_rows, si_rows = [], []
        for _ in range(scan_steps):
            sr_rows.append(mu_r)
            si_rows.append(mu_i * minus_plus)
            mu_r, mu_i = mu_r * mu_r - mu_i * mu_i, 2.0 * mu_r * mu_i
        pad = [jnp.zeros_like(mu_r)] * (sr_ref.shape[1] - scan_steps)
        sr_ref[gi] = jnp.concatenate(sr_rows + pad, axis=0)
        si_ref[gi] = jnp.concatenate(si_rows + pad, axis=0)

    kst = jnp.concatenate(krows, axis=0).astype(BF16)
    spread = jnp.dot(kst, rep_ref[...], preferred_element_type=F32)
    rows_g = lax.broadcasted_iota(jnp.int32, spread.shape, 0) // gw
    cols_g = _group_of_lane(lax.broadcasted_iota(jnp.int32, spread.shape, 1))
    spread = jnp.where(rows_g == cols_g, spread, 0.0).astype(r_ref.dtype)
    blocks = [spread[:, tau * V7X_LANES:(tau + 1) * V7X_LANES] for tau in range(SSM_CHUNK)]
    zero = jnp.zeros_like(blocks[0])
    npairs = SSM_CHUNK // 2
    for d in range(npairs):
        base = (npairs - 1 - d) * 2 * V7X_LANES
        top = jnp.concatenate([blocks[2 * d], blocks[2 * d + 1]], axis=1)
        bottom = jnp.concatenate([blocks[2 * d - 1] if d else zero, blocks[2 * d]], axis=1)
        r_ref[base:base + V7X_LANES, :] = top
        r_ref[base + V7X_LANES:base + 2 * V7X_LANES, :] = bottom


def ssm_prep(a_re2, a_im2, log_dt2, b2, c2, rep, *, scan_steps, side=(), name="ssm_prep"):
    ng = a_re2.shape[0]
    gb = GROUPS_PER_SLAB
    nslab = ng // gb
    cat = SSM_CHUNK * V7X_LANES
    ns = 2 * S5_STATE
    rows = -(-scan_steps // V7X_SUBLANES) * V7X_SUBLANES
    vec = pl.BlockSpec((gb, 1, ns), lambda i: (i, 0, 0))
    mat = pl.BlockSpec((gb, S5_GROUP, ns), lambda i: (i, 0, 0))
    side_in, side_out, side_shapes, side_ops = _side_cast_specs(side, nslab, lambda i: i)
    return pl.pallas_call(
        functools.partial(_ssm_prep_kernel, groups=gb, scan_steps=scan_steps),
        grid=(nslab,),
        in_specs=[vec, vec, vec, mat, mat, pl.BlockSpec(rep.shape, lambda i: (0, 0))] + side_in,
        out_specs=[
            pl.BlockSpec((None, cat, 2 * V7X_LANES), lambda i: (i, 0, 0)),
            pl.BlockSpec((None, SSM_CHUNK, V7X_LANES, ns), lambda i: (i, 0, 0, 0)),
            pl.BlockSpec((None, gb * ns, SSM_CHUNK * S5_GROUP), lambda i: (i, 0, 0)),
            pl.BlockSpec((gb, rows, ns), lambda i: (i, 0, 0)),
            pl.BlockSpec((gb, rows, ns), lambda i: (i, 0, 0)),
        ] + side_out,
        out_shape=[
            jax.ShapeDtypeStruct((nslab, cat, 2 * V7X_LANES), BF16),
            jax.ShapeDtypeStruct((nslab, SSM_CHUNK, V7X_LANES, ns), BF16),
            jax.ShapeDtypeStruct((nslab, gb * ns, SSM_CHUNK * S5_GROUP), BF16),
            jax.ShapeDtypeStruct((ng, rows, ns), F32),
            jax.ShapeDtypeStruct((ng, rows, ns), F32),
        ] + side_shapes,
        compiler_params=_params("parallel"),
        name=name,
    )(a_re2, a_im2, log_dt2, b2, c2, rep, *side_ops)


def replication_matrix():
    src = jnp.arange(SSM_CHUNK * S5_GROUP)
    dst = jnp.arange(SSM_CHUNK * V7X_LANES)
    same_step = (src[:, None] // S5_GROUP) == (dst[None, :] // V7X_LANES)
    same_chan = (src[:, None] % S5_GROUP) == (dst[None, :] % S5_GROUP)
    return (same_step & same_chan).astype(BF16)


def _group_of_lane(idx):
    return (idx % V7X_LANES) // S5_GROUP


def _ssm_kernel(u_ref, r_ref, gst_ref, hst_ref, sr_ref, si_ref, d_ref, rep_ref, *rest,
                chunks_per_seq, scan_steps, nside):
    y_ref = rest[2 * nside]
    _do_side_casts(rest[:2 * nside], rest[2 * nside + 1:3 * nside + 1])
    ucat_ref, gexp_ref, hexp_ref, hprev_ref, intra_ref = rest[3 * nside + 1:]
    t = u_ref.shape[0]
    nchunk = t // SSM_CHUNK
    ns = 2 * S5_STATE
    pair = 2 * V7X_LANES
    npairs = SSM_CHUNK // 2

    for step in range(SSM_CHUNK):
        x = u_ref[pl.ds(step, nchunk, stride=SSM_CHUNK), :]
        ucat_ref[:, step * V7X_LANES:(step + 1) * V7X_LANES] = x.astype(BF16)

    g_rows = lax.broadcasted_iota(jnp.int32, (V7X_LANES, GROUPS_PER_SLAB * ns), 0) // S5_GROUP
    g_cols = lax.broadcasted_iota(jnp.int32, (V7X_LANES, GROUPS_PER_SLAB * ns), 1) // ns
    for step in range(SSM_CHUNK):
        tiled = jnp.concatenate([gst_ref[step]] * GROUPS_PER_SLAB, axis=1)
        gexp_ref[step * V7X_LANES:(step + 1) * V7X_LANES, :] = jnp.where(g_rows == g_cols, tiled, 0.0).astype(BF16)

    h_rows = lax.broadcasted_iota(jnp.int32, (GROUPS_PER_SLAB * ns, pair), 0) // ns
    h_cols = _group_of_lane(lax.broadcasted_iota(jnp.int32, (GROUPS_PER_SLAB * ns, pair), 1))
    for b in range(npairs):
        spread = jnp.dot(hst_ref[...], rep_ref[:, b * pair:(b + 1) * pair], preferred_element_type=F32)
        hexp_ref[:, b * pair:(b + 1) * pair] = jnp.where(h_rows == h_cols, spread, 0.0).astype(BF16)

    hs_all = jnp.dot(ucat_ref[...], gexp_ref[...], preferred_element_type=F32)
    for b in range(npairs):
        intra_ref[:, b * pair:(b + 1) * pair] = jnp.dot(
            ucat_ref[:, :(b + 1) * pair], r_ref[(npairs - 1 - b) * pair:, :], preferred_element_type=F32)

    row = lax.broadcasted_iota(jnp.int32, (nchunk, ns), 0) % chunks_per_seq
    for g in range(GROUPS_PER_SLAB):
        hs = hs_all[:, g * ns:(g + 1) * ns]
        for k in range(scan_steps):
            d = 1 << k
            sh = jnp.where(row >= d, pltpu.roll(hs, d, axis=0), 0.0)
            hs = hs + sr_ref[g, k:k + 1, :] * sh + si_ref[g, k:k + 1, :] * pltpu.roll(sh, S5_STATE, axis=1)
        hprev_ref[:, g * ns:(g + 1) * ns] = jnp.where(row >= 1, pltpu.roll(hs, 1, axis=0), 0.0).astype(BF16)

    for b in range(npairs):
        y = intra_ref[:, b * pair:(b + 1) * pair] + jnp.dot(
            hprev_ref[...], hexp_ref[:, b * pair:(b + 1) * pair], preferred_element_type=F32)
        for bit in range(2):
            step = 2 * b + bit
            lanes = slice(step * V7X_LANES, (step + 1) * V7X_LANES)
            yl = y[:, bit * V7X_LANES:(bit + 1) * V7X_LANES] + d_ref[...] * ucat_ref[:, lanes].astype(F32)
            y_ref[pl.ds(step, nchunk, stride=SSM_CHUNK), :] = jax.nn.gelu(yl, approximate=True)


def ssm_apply(u, r_all, gst, hst, sr, si, d_skip, rep, *, seq, side=(), name="s5_ssm"):
    t, w = u.shape
    nslab = w // V7X_LANES
    side_in, side_out, side_shapes, side_ops = _side_cast_specs(side, nslab, lambda s: s)
    nchunk = t // SSM_CHUNK
    cat = SSM_CHUNK * V7X_LANES
    ns = 2 * S5_STATE
    chunks_per_seq = seq // SSM_CHUNK
    scan_steps = (chunks_per_seq - 1).bit_length()
    gb = GROUPS_PER_SLAB
    rows = sr.shape[1]
    return pl.pallas_call(
        functools.partial(_ssm_kernel, chunks_per_seq=chunks_per_seq, scan_steps=scan_steps, nside=len(side)),
        grid=(nslab,),
        in_specs=[
            pl.BlockSpec((t, V7X_LANES), lambda s: (0, s)),
            pl.BlockSpec((None, cat, 2 * V7X_LANES), lambda s: (s, 0, 0)),
            pl.BlockSpec((None, SSM_CHUNK, V7X_LANES, ns), lambda s: (s, 0, 0, 0)),
            pl.BlockSpec((None, gb * ns, SSM_CHUNK * S5_GROUP), lambda s: (s, 0, 0)),
            pl.BlockSpec((gb, rows, ns), lambda s: (s, 0, 0)),
            pl.BlockSpec((gb, rows, ns), lambda s: (s, 0, 0)),
            pl.BlockSpec((1, V7X_LANES), lambda s: (0, s)),
            pl.BlockSpec((SSM_CHUNK * S5_GROUP, cat), lambda s: (0, 0)),
        ] + side_in,
        out_specs=[pl.BlockSpec((t, V7X_LANES), lambda s: (0, s))] + side_out,
        out_shape=[jax.ShapeDtypeStruct((t, w), F32)] + side_shapes,
        scratch_shapes=[pltpu.VMEM((nchunk, cat), BF16), pltpu.VMEM((cat, gb * ns), BF16),
                        pltpu.VMEM((gb * ns, cat), BF16), pltpu.VMEM((nchunk, gb * ns), BF16),
                        pltpu.VMEM((nchunk, cat), F32)],
        compiler_params=pltpu.CompilerParams(dimension_semantics=("parallel",),
                                             vmem_limit_bytes=SSM_VMEM_LIMIT),
        name=name,
    )(u, r_all, gst, hst, sr, si, d_skip, rep, *side_ops)


ATTN_SPLIT = 3
ATTN_EXTRA_ROWS = 16


def _split3(c):
    hi = c.astype(BF16).astype(F32)
    mid = (c - hi).astype(BF16).astype(F32)
    return hi, mid, c - hi - mid


def _fox_attn_kernel(qt_ref, k_ref, vt_ref, ct_ref, sz_ref, o_ref,
                     qa_ref, ka_ref, va_ref, s0_ref, s1_ref, smax_ref, m_ref, acc_ref, *, tile):
    dh, seq = qt_ref.shape
    ntiles = seq // tile
    ext = ATTN_EXTRA_ROWS

    @pl.when((pl.program_id(0) == 0) & (pl.program_id(1) == 0))
    def _():
        qa_ref[dh + ext:, :] = jnp.zeros((dh - ext, seq), BF16)
        rowv = lax.broadcasted_iota(jnp.int32, (ext, seq), 0)
        va_ref[dh:, :] = jnp.where(rowv == 0, 1.0, 0.0).astype(BF16)

    parts = _split3(ct_ref[...])
    qa_ref[:dh, :] = qt_ref[...]
    row16 = lax.broadcasted_iota(jnp.int32, (ext, seq), 0)
    ext_q = jnp.where(row16 < 2 * ATTN_SPLIT, 1.0, 0.0)
    for n, part in enumerate(parts):
        ext_q = jnp.where(row16 == n, part, ext_q)
    qa_ref[dh:dh + ext, :] = ext_q.astype(BF16)
    va_ref[:dh, :] = vt_ref[...]

    rowk = lax.broadcasted_iota(jnp.int32, (dh, tile), 0)
    for r in range(ntiles):
        cols = slice(r * tile, (r + 1) * tile)
        ext_kt = jnp.where(rowk < ATTN_SPLIT, 1.0, 0.0)
        for n, part in enumerate(parts):
            ext_kt = jnp.where(rowk == ATTN_SPLIT + n, -part[:, cols], ext_kt)
        ka_ref[cols, :dh] = k_ref[cols, :]
        ka_ref[cols, dh:] = ext_kt.T.astype(BF16)

    sub = tile // 2
    causal = (lax.broadcasted_iota(jnp.int32, (sub, sub), 0) <= lax.broadcasted_iota(jnp.int32, (sub, sub), 1))
    s_slots = (s0_ref, s1_ref)

    def scores(k0, nk, q0, nq):
        return jnp.dot(ka_ref[k0:k0 + nk, :], qa_ref[:, q0:q0 + nq], preferred_element_type=F32)

    def produce(slot, qi, kb):
        k0, q0 = kb * tile, qi * tile
        if kb == qi:
            s_slots[slot][:sub, :] = scores(k0, sub, q0, tile)
            s_slots[slot][sub:, sub:] = scores(k0 + sub, sub, q0 + sub, sub)
        else:
            st = scores(k0, tile, q0, tile)
            s_slots[slot][...] = st
            smax_ref[slot:slot + 1, :] = jnp.max(st, axis=0, keepdims=True)

    def update(par, st, st_max, k0, c0, first):
        nk, nq = st.shape
        cols = slice(c0, c0 + nq)
        pv = functools.partial(jnp.dot, va_ref[:, k0:k0 + nk], preferred_element_type=F32)
        if first:
            m_ref[par:par + 1, cols] = st_max
            acc_ref[par, :, cols] = pv(jnp.exp2(st - st_max).astype(BF16))
        else:
            m = m_ref[par:par + 1, cols]
            m_new = jnp.maximum(m, st_max)
            acc_ref[par, :, cols] = (jnp.exp2(m - m_new) * acc_ref[par, :, cols]
                                     + pv(jnp.exp2(st - m_new).astype(BF16)))
            m_ref[par:par + 1, cols] = m_new

    def consume(slot, qi, kb):
        par, k0, s_ref = qi % 2, kb * tile, s_slots[slot]
        if kb != qi:
            update(par, s_ref[...], smax_ref[slot:slot + 1, :], k0, 0, kb == 0)
            return
        early = jnp.where(causal, s_ref[:sub, :sub], NEG_INF)
        update(par, early, jnp.max(early, axis=0, keepdims=True), k0, 0, kb == 0)
        late = jnp.concatenate([s_ref[:sub, sub:], jnp.where(causal, s_ref[sub:, sub:], NEG_INF)], axis=0)
        update(par, late, jnp.max(late, axis=0, keepdims=True), k0, sub, kb == 0)

    def finalize(qi):
        acc = acc_ref[qi % 2]
        rows = slice(qi * tile, (qi + 1) * tile)
        o = (acc[:dh, :] / acc[dh:dh + 1, :]).T
        o_ref[rows, :] = (o * sz_ref[rows, :].astype(F32)).astype(o_ref.dtype)

    blocks = [(qi, kb) for qi in range(ntiles) for kb in range(qi + 1)]
    produce(0, *blocks[0])
    for n, (qi, kb) in enumerate(blocks):
        if n + 1 < len(blocks):
            produce((n + 1) % 2, *blocks[n + 1])
        consume(n % 2, qi, kb)
        if kb == qi:
            finalize(qi)


def fox_attention(qt, k, vt, cumt2, sz, *, batch, seq, heads, tile=1024, name="fox_attention"):
    width, t = qt.shape
    dh = width // heads
    natural = pl.BlockSpec((seq, dh), lambda b, h: (b, h))
    transposed = pl.BlockSpec((dh, seq), lambda b, h: (h, b))
    return pl.pallas_call(
        functools.partial(_fox_attn_kernel, tile=tile),
        grid=(batch, heads),
        in_specs=[transposed, natural, transposed,
                  pl.BlockSpec((None, None, 1, seq), lambda b, h: (b, h, 0, 0)),
                  natural],
        out_specs=natural,
        out_shape=jax.ShapeDtypeStruct((t, width), BF16),
        scratch_shapes=[pltpu.VMEM((2 * dh, seq), BF16), pltpu.VMEM((seq, 2 * dh), BF16),
                        pltpu.VMEM((dh + ATTN_EXTRA_ROWS, seq), BF16),
                        pltpu.VMEM((tile, tile), F32), pltpu.VMEM((tile, tile), F32),
                        pltpu.VMEM((V7X_SUBLANES, tile), F32), pltpu.VMEM((V7X_SUBLANES, tile), F32),
                        pltpu.VMEM((2, dh + ATTN_EXTRA_ROWS, tile), F32)],
        compiler_params=_params("arbitrary", "arbitrary"),
        name=name,
    )(qt, k, vt, cumt2, sz)


def _identity(v):
    return v


def _doubled(v):
    return jnp.concatenate([v, v], axis=-1)[:, None, :].astype(F32)


def kernel(x, norm_pre, norm_post, s5_w_in, s5_a_re, s5_a_im, s5_log_dt, s5_b_re, s5_b_im, s5_c_re, s5_c_im,
           s5_d, s5_w_glu, s5_b_glu, s5_w_out, kv_norm, kv_w, kv_b_f, fox_w_in, fox_w_out):
    batch, seq, d_model = x.shape
    t = batch * seq
    heads = kv_b_f.shape[0]
    fox_width = fox_w_in.shape[-1] // 2
    assert norm_pre.shape[0] == 2 and s5_w_in.shape[0] == 1 and fox_w_in.shape[0] == 1
    assert fox_width // heads == FOX_HEAD_DIM and heads <= V7X_LANES
    ngroups = s5_a_re.shape[1]
    chunks_per_seq = seq // SSM_CHUNK
    scan_steps = (chunks_per_seq - 1).bit_length()

    h0 = x.reshape(t, d_model)

    s5_width = s5_w_in.shape[-1] // 2
    ldt2 = jnp.broadcast_to(s5_log_dt[0][:, None, None], (ngroups, 1, 2 * S5_STATE)).astype(F32)
    b2 = jnp.concatenate([jnp.swapaxes(s5_b_re[0], 1, 2), jnp.swapaxes(s5_b_im[0], 1, 2)], axis=-1)
    c2 = jnp.concatenate([s5_c_re[0], s5_c_im[0]], axis=-1)
    rep = replication_matrix()
    r_all, gst, hst, sr, si, w_in = ssm_prep(_doubled(s5_a_re[0]), _doubled(s5_a_im[0]), ldt2, b2.astype(F32),
                                             c2.astype(F32), rep, scan_steps=scan_steps,
                                             side=[side_cast(s5_w_in[0], row_scale=norm_pre[0])])
    kv_cols = 2 * fox_width
    kv_w_t = kv_w.T
    u, sz, w_glu, w_out, w_kv_t = fused_proj(
        h0, [Proj(w_in, False, 0, _identity, F32, False), Proj(w_in, False, s5_width, jax.nn.silu, BF16, False)],
        n=s5_width, tm=1024, name="s5_in_proj",
        side=[side_cast(s5_w_glu[0]), side_cast(s5_w_out[0]), side_cast(kv_w_t, kv_cols, col_scale=kv_norm)])
    yg, w_in_fox, w_out_fox = ssm_apply(u, r_all, gst, hst, sr, si, s5_d[0][None, :].astype(F32), rep, seq=seq,
                                        side=[side_cast(fox_w_in[0], row_scale=norm_pre[1]),
                                              side_cast(fox_w_out[0])])
    h1, h1_bf16, h1_inv_rms = s5_tail(yg, w_glu, s5_b_glu[0][None, :], sz, w_out, h0, norm_post[0][None, :])

    wf_t = jnp.pad((kv_w_t[kv_cols:] * kv_norm[None, :]).astype(BF16), ((0, V7X_LANES - heads), (0, 0)))
    bf = jnp.pad(kv_b_f, (0, V7X_LANES - heads))[None, :].astype(F32)
    scale = FOX_HEAD_DIM ** -0.5 * LOG2_E
    k, vt, qt, sz2, cumt2 = fused_proj(
        h1_bf16,
        [Proj(w_kv_t, True, 0, _identity, BF16, False),
         Proj(w_kv_t, True, fox_width, _identity, BF16, True),
         Proj(w_in_fox, False, 0, lambda v: v * scale, BF16, True),
         Proj(w_in_fox, False, fox_width, jax.nn.silu, BF16, False)],
        n=fox_width, inv=h1_inv_rms, gate=Gate(wf_t, bf, heads, seq), vmem_limit=FOX_PROJ_VMEM_LIMIT,
        name="fox_proj")
    o = fox_attention(qt, k, vt, cumt2.reshape(batch, heads, 1, seq), sz2,
                      batch=batch, seq=seq, heads=heads)
    h2 = out_proj(o, w_out_fox, h1, norm_post[1][None, :], name="fox_out_proj")
    return h2.reshape(batch, seq, d_model)
```

```python
import functools
import math
from typing import Callable, NamedTuple

import jax
import jax.numpy as jnp
from jax import lax
from jax.experimental import pallas as pl
from jax.experimental.pallas import tpu as pltpu

F32 = jnp.float32
BF16 = jnp.bfloat16

RMS_EPS = 1e-6
NEG_INF = -1e30
LOG2_E = math.log2(math.e)

S5_GROUP = 16
S5_STATE = 64
FOX_HEAD_DIM = 128

V7X_LANES = 128
V7X_SUBLANES = 8
SSM_CHUNK = 16
GROUPS_PER_SLAB = V7X_LANES // S5_GROUP
VMEM_LIMIT = 48 * 1024 * 1024
SSM_VMEM_LIMIT = 56 * 1024 * 1024
FOX_PROJ_VMEM_LIMIT = 56 * 1024 * 1024
S5_TAIL_VMEM_LIMIT = 56 * 1024 * 1024


def _params(*semantics):
    return pltpu.CompilerParams(dimension_semantics=semantics, vmem_limit_bytes=VMEM_LIMIT)


def _rms_scale(x, gain):
    ms = jnp.mean(x * x, axis=-1, keepdims=True)
    return x * lax.rsqrt(ms + RMS_EPS) * gain


class Proj(NamedTuple):
    weight: jax.Array
    weight_is_t: bool
    first: int
    act: Callable
    dtype: jnp.dtype
    transposed: bool


def _log_sigmoid(x):
    return -(jnp.maximum(-x, 0.0) + jnp.log1p(jnp.exp(-jnp.abs(x))))


def _cumsum_rows(c):
    rows = c.shape[0]
    row = lax.broadcasted_iota(jnp.int32, c.shape, 0)
    d = 1
    while d < rows:
        c = c + jnp.where(row >= d, pltpu.roll(c, d, axis=0), 0.0)
        d *= 2
    return c


class SideCast(NamedTuple):
    src: jax.Array
    rows_used: int
    scale: jax.Array


def side_cast(src, rows_used=None, *, row_scale=None, col_scale=None):
    rows_used = src.shape[0] if rows_used is None else rows_used
    if row_scale is not None:
        scale = row_scale.reshape(rows_used, 1)
    elif col_scale is not None:
        scale = col_scale.reshape(1, src.shape[1])
    else:
        scale = jnp.ones((1, src.shape[1]), F32)
    return SideCast(src, rows_used, scale.astype(F32))


def _side_cast_specs(casts, steps, flat_step):
    in_specs, out_specs, out_shapes, operands = [], [], [], []
    for c in casts:
        block = (c.rows_used // steps, c.src.shape[1])
        by_step = lambda *idx: (flat_step(*idx), 0)
        in_specs.append(pl.BlockSpec(block, by_step))
        if c.scale.shape[0] == 1:
            in_specs.append(pl.BlockSpec(c.scale.shape, lambda *idx: (0, 0)))
        else:
            in_specs.append(pl.BlockSpec((block[0], 1), by_step))
        operands += [c.src, c.scale]
        out_specs.append(pl.BlockSpec(block, by_step))
        out_shapes.append(jax.ShapeDtypeStruct((c.rows_used, c.src.shape[1]), BF16))
    return in_specs, out_specs, out_shapes, operands


def _do_side_casts(in_refs, dst_refs):
    for n, dst_ref in enumerate(dst_refs):
        dst_ref[...] = (in_refs[2 * n][...] * in_refs[2 * n + 1][...]).astype(BF16)


def _inv_rms(x):
    inv = lax.rsqrt(jnp.mean(x * x, axis=-1, keepdims=True) + RMS_EPS)
    return jnp.broadcast_to(inv, (x.shape[0], V7X_LANES))


def _fused_proj_kernel(*refs, projs, gate, nside, prenormed):
    nproj = len(projs)
    x_ref = refs[0]
    pos = 1
    if prenormed:
        inv_in_ref = refs[1]
        pos = 2
    w_refs = refs[pos:pos + nproj]
    pos += nproj
    if gate is not None:
        wf_ref, bf_ref = refs[pos:pos + 2]
        pos += 2
    side_in = refs[pos:pos + 2 * nside]
    pos += 2 * nside
    o_refs = refs[pos:pos + nproj]
    pos += nproj
    if gate is not None:
        cumt_ref = refs[pos]
        pos += 1
    _do_side_casts(side_in, refs[pos:pos + nside])
    pos += nside
    if prenormed:
        xb_ref, inv_ref = x_ref, inv_in_ref
    else:
        xb_ref, inv_ref = refs[pos:pos + 2]
        pos += 2
    invt_ref = refs[pos]
    pos += 1
    nt = (((1,), (1,)), ((), ()))
    any_transposed = any(p.transposed for p in projs)

    @pl.when(pl.program_id(1) == 0)
    def _():
        if not prenormed:
            x = x_ref[...]
            xb_ref[...] = x.astype(BF16)
            inv_ref[...] = _inv_rms(x)
        if any_transposed:
            invt_ref[...] = inv_ref[...].T[:V7X_SUBLANES, :]
        if gate is not None:
            carry_ref = refs[pos]

            @pl.when(pl.program_id(0) % gate.tiles_per_seq == 0)
            def _():
                carry_ref[...] = jnp.zeros_like(carry_ref)

            logit = lax.dot_general(xb_ref[...], wf_ref[...], nt, preferred_element_type=F32)
            c = _cumsum_rows(_log_sigmoid(logit * inv_ref[...] + bf_ref[...])) + carry_ref[...]
            carry_ref[...] = c[c.shape[0] - 1:, :]
            cumt_ref[...] = (c * LOG2_E).T[:gate.heads, :]

    xb = xb_ref[...]
    for p, w_ref, o_ref in zip(projs, w_refs, o_refs):
        if p.weight_is_t and p.transposed:
            y = lax.dot_general(w_ref[...], xb, nt, preferred_element_type=F32)
        elif p.weight_is_t:
            y = lax.dot_general(xb, w_ref[...], nt, preferred_element_type=F32)
        else:
            y = jnp.dot(xb, w_ref[...], preferred_element_type=F32)
            y = y.T if p.transposed else y
        if p.transposed:
            y = y * invt_ref[0:1, :]
        else:
            y = y * jnp.concatenate([inv_ref[...]] * (y.shape[1] // V7X_LANES), axis=1)
        o_ref[...] = p.act(y).astype(o_ref.dtype)


class Gate(NamedTuple):
    wf_t: jax.Array
    bias: jax.Array
    heads: int
    seq: int
    tiles_per_seq: int = 0


def fused_proj(x, projs, *, n, inv=None, gate=None, side=(), tm=512, tn=512, vmem_limit=VMEM_LIMIT,
               name="fused_proj"):
    t, d = x.shape
    prenormed = inv is not None
    nj = n // tn
    w_specs, out_specs, out_shapes = [], [], []
    for p in projs:
        if p.weight_is_t:
            w_specs.append(pl.BlockSpec((tn, d), lambda i, j, b0=p.first // tn: (j + b0, 0)))
        else:
            w_specs.append(pl.BlockSpec((d, tn), lambda i, j, b0=p.first // tn: (0, j + b0)))
        if p.transposed:
            out_specs.append(pl.BlockSpec((tn, tm), lambda i, j: (j, i)))
            out_shapes.append(jax.ShapeDtypeStruct((n, t), p.dtype))
        else:
            out_specs.append(pl.BlockSpec((tm, tn), lambda i, j: (i, j)))
            out_shapes.append(jax.ShapeDtypeStruct((t, n), p.dtype))
    operands = [x] + ([inv] if prenormed else []) + [p.weight for p in projs]
    in_specs = [pl.BlockSpec((tm, d), lambda i, j: (i, 0))]
    if prenormed:
        in_specs.append(pl.BlockSpec((tm, V7X_LANES), lambda i, j: (i, 0)))
    in_specs += w_specs
    scratch = [] if prenormed else [pltpu.VMEM((tm, d), BF16), pltpu.VMEM((tm, V7X_LANES), F32)]
    scratch.append(pltpu.VMEM((V7X_SUBLANES, tm), F32))
    kernel_projs = tuple(p._replace(weight=None) for p in projs)
    kernel_gate = None
    if gate is not None:
        tps = gate.seq // tm
        kernel_gate = gate._replace(wf_t=None, bias=None, tiles_per_seq=tps)
        operands += [gate.wf_t, gate.bias]
        in_specs += [pl.BlockSpec((V7X_LANES, d), lambda i, j: (0, 0)), pl.BlockSpec((1, V7X_LANES), lambda i, j: (0, 0))]
        out_specs.append(pl.BlockSpec((None, gate.heads, tm), lambda i, j: (i // tps, 0, i % tps)))
        out_shapes.append(jax.ShapeDtypeStruct((t // gate.seq, gate.heads, gate.seq), F32))
        scratch.append(pltpu.VMEM((1, V7X_LANES), F32))
    side_in, side_out, side_shapes, side_ops = _side_cast_specs(side, (t // tm) * nj, lambda i, j: i * nj + j)
    operands += side_ops
    in_specs += side_in
    out_specs += side_out
    out_shapes += side_shapes
    return pl.pallas_call(
        functools.partial(_fused_proj_kernel, projs=kernel_projs, gate=kernel_gate, nside=len(side),
                          prenormed=prenormed),
        grid=(t // tm, nj),
        in_specs=in_specs,
        out_specs=out_specs,
        out_shape=out_shapes,
        scratch_shapes=scratch,
        compiler_params=pltpu.CompilerParams(dimension_semantics=("arbitrary", "arbitrary"),
                                             vmem_limit_bytes=vmem_limit),
        name=name,
    )(*operands)


def _out_proj_kernel(a_ref, w_ref, res_ref, gain_ref, o_ref, *, halves):
    rows = a_ref.shape[0] // halves
    for h in range(halves):
        r = slice(h * rows, (h + 1) * rows)
        y = jnp.dot(a_ref[r, :], w_ref[...], preferred_element_type=F32)
        o_ref[r, :] = res_ref[r, :] + _rms_scale(y, gain_ref[...])


def out_proj(a, w, res, gain, *, tm=512, halves=1, name="out_proj"):
    t, k = a.shape
    d = w.shape[1]
    return pl.pallas_call(
        functools.partial(_out_proj_kernel, halves=halves),
        grid=(t // tm,),
        in_specs=[
            pl.BlockSpec((tm, k), lambda i: (i, 0)),
            pl.BlockSpec((k, d), lambda i: (0, 0), pipeline_mode=pl.Buffered(1)),
            pl.BlockSpec((tm, d), lambda i: (i, 0)),
            pl.BlockSpec((1, d), lambda i: (0, 0)),
        ],
        out_specs=pl.BlockSpec((tm, d), lambda i: (i, 0)),
        out_shape=jax.ShapeDtypeStruct((t, d), F32),
        compiler_params=_params("parallel"),
        name=name,
    )(a, w, res, gain)


def _s5_tail_kernel(yg_ref, wg_ref, b_ref, sz_ref, wo_ref, res_ref, gain_ref, o_ref, ob_ref, inv_ref, *, halves):
    rows = yg_ref.shape[0] // halves
    for h in range(halves):
        r = slice(h * rows, (h + 1) * rows)
        yg = yg_ref[r, :]
        gate = jax.nn.sigmoid(jnp.dot(yg.astype(BF16), wg_ref[...], preferred_element_type=F32) + b_ref[...])
        y3 = (yg * gate * sz_ref[r, :].astype(F32)).astype(BF16)
        y = jnp.dot(y3, wo_ref[...], preferred_element_type=F32)
        out = res_ref[r, :] + _rms_scale(y, gain_ref[...])
        o_ref[r, :] = out
        ob_ref[r, :] = out.astype(BF16)
        inv_ref[r, :] = _inv_rms(out)


def s5_tail(yg, w_glu, b_glu, sz, w_out, res, gain, *, tm=512, halves=2, name="s5_tail"):
    t, k = yg.shape
    d = w_out.shape[1]
    rows = pl.BlockSpec((tm, k), lambda i: (i, 0))
    resident = functools.partial(pl.BlockSpec, index_map=lambda i: (0, 0), pipeline_mode=pl.Buffered(1))
    return pl.pallas_call(
        functools.partial(_s5_tail_kernel, halves=halves),
        grid=(t // tm,),
        in_specs=[rows, resident((k, k)), resident((1, k)), rows, resident((k, d)),
                  pl.BlockSpec((tm, d), lambda i: (i, 0)), resident((1, d))],
        out_specs=[pl.BlockSpec((tm, d), lambda i: (i, 0)), pl.BlockSpec((tm, d), lambda i: (i, 0)),
                   pl.BlockSpec((tm, V7X_LANES), lambda i: (i, 0))],
        out_shape=[jax.ShapeDtypeStruct((t, d), F32), jax.ShapeDtypeStruct((t, d), BF16),
                   jax.ShapeDtypeStruct((t, V7X_LANES), F32)],
        compiler_params=pltpu.CompilerParams(dimension_semantics=("parallel",),
                                             vmem_limit_bytes=S5_TAIL_VMEM_LIMIT),
        name=name,
    )(yg, w_glu, b_glu, sz, w_out, res, gain)


def _ssm_prep_kernel(are_ref, aim_ref, ldt_ref, b2_ref, c2_ref, rep_ref, *rest, groups, scan_steps):
    nside = (len(rest) - 5) // 3
    r_ref, g_ref, h_ref, sr_ref, si_ref = rest[2 * nside:2 * nside + 5]
    _do_side_casts(rest[:2 * nside], rest[2 * nside + 5:])
    lane = lax.broadcasted_iota(jnp.int32, (1, 2 * S5_STATE), 1)
    minus_plus = jnp.where(lane < S5_STATE, -1.0, 1.0).astype(F32)
    gw = S5_GROUP
    ns = 2 * S5_STATE
    krows = []

    ar = are_ref[...]
    ai = aim_ref[...]
    dt = jnp.exp(ldt_ref[...])
    mag = jnp.exp(ar * dt)
    lam_r = mag * jnp.cos(ai * dt)
    lam_i = mag * jnp.sin(ai * dt)
    den = ar * ar + ai * ai
    nr = lam_r - 1.0
    coef_r_all = (nr * ar + lam_i * ai) / den
    coef_i_all = (lam_i * ar - nr * ai) / den
    pow_r_all = [jnp.ones_like(lam_r)]
    pow_i_all = [jnp.zeros_like(lam_r)]
    for _ in range(SSM_CHUNK):
        pr, pi = pow_r_all[-1], pow_i_all[-1]
        pow_r_all.append(pr * lam_r - pi * lam_i)
        pow_i_all.append(pr * lam_i + pi * lam_r)
    mu_r, mu_i = pow_r_all[SSM_CHUNK], pow_i_all[SSM_CHUNK]
    scan_r_all, scan_i_all = [], []
    for _ in range(scan_steps):
        scan_r_all.append(mu_r)
        scan_i_all.append(mu_i * minus_plus)
        mu_r, mu_i = mu_r * mu_r - mu_i * mu_i, 2.0 * mu_r * mu_i

    for gi in range(groups):
        grp = slice(gi, gi + 1)
        coef_r, coef_i = coef_r_all[grp], coef_i_all[grp]
        pow_r = [p[grp] for p in pow_r_all]
        pow_i = [p[grp] for p in pow_i_all]

        b2 = b2_ref[gi]
        b2s = pltpu.roll(b2, S5_STATE, axis=1) * minus_plus
        c2 = c2_ref[gi]
        c2a = c2 * (-minus_plus)
        c2b = -pltpu.roll(c2, S5_STATE, axis=1)

        for step in range(SSM_CHUNK):
            pr, pi = pow_r[SSM_CHUNK - 1 - step], pow_i[SSM_CHUNK - 1 - step]
            wr = pr * coef_r - pi * coef_i
            wi = pr * coef_i + pi * coef_r
            g_ref[step, gi * gw:(gi + 1) * gw, :] = (wr * b2 + wi * b2s).astype(g_ref.dtype)

        cl = [pow_r[tau] * c2a + pow_i[tau] * c2b for tau in range(SSM_CHUNK + 1)]
        h_t = jnp.concatenate(cl[1:], axis=0)
        h_ref[gi * ns:(gi + 1) * ns, :] = h_t.T.astype(h_ref.dtype)

        bbar2 = coef_r * b2 + coef_i * b2s
        cl_all = jnp.concatenate(cl[:-1], axis=0)
        krows.append(lax.dot_general(bbar2, cl_all, (((1,), (1,)), ((), ())),
                                     precision=lax.Precision.HIGHEST,
                                     preferred_element_type=F32))

        pad = [jnp.zeros_like(coef_r)] * (sr_ref.shape[1] - scan_steps)
        sr_ref[gi] = jnp.concatenate([s[grp] for s in scan_r_all] + pad, axis=0)
        si_ref[gi] = jnp.concatenate([s[grp] for s in scan_i_all] + pad, axis=0)

    kst = jnp.concatenate(krows, axis=0).astype(BF16)
    spread = jnp.dot(kst, rep_ref[...], preferred_element_type=F32)
    rows_g = lax.broadcasted_iota(jnp.int32, spread.shape, 0) // gw
    cols_g = _group_of_lane(lax.broadcasted_iota(jnp.int32, spread.shape, 1))
    spread = jnp.where(rows_g == cols_g, spread, 0.0).astype(r_ref.dtype)
    blocks = [spread[:, tau * V7X_LANES:(tau + 1) * V7X_LANES] for tau in range(SSM_CHUNK)]
    zero = jnp.zeros_like(blocks[0])
    npairs = SSM_CHUNK // 2
    for d in range(npairs):
        base = (npairs - 1 - d) * 2 * V7X_LANES
        top = jnp.concatenate([blocks[2 * d], blocks[2 * d + 1]], axis=1)
        bottom = jnp.concatenate([blocks[2 * d - 1] if d else zero, blocks[2 * d]], axis=1)
        r_ref[base:base + V7X_LANES, :] = top
        r_ref[base + V7X_LANES:base + 2 * V7X_LANES, :] = bottom


def ssm_prep(a_re2, a_im2, log_dt2, b2, c2, rep, *, scan_steps, side=(), name="ssm_prep"):
    ng = a_re2.shape[0]
    gb = GROUPS_PER_SLAB
    nslab = ng // gb
    cat = SSM_CHUNK * V7X_LANES
    ns = 2 * S5_STATE
    rows = -(-scan_steps // V7X_SUBLANES) * V7X_SUBLANES
    vec = pl.BlockSpec((gb, ns), lambda i: (i, 0))
    mat = pl.BlockSpec((gb, S5_GROUP, ns), lambda i: (i, 0, 0))
    side_in, side_out, side_shapes, side_ops = _side_cast_specs(side, nslab, lambda i: i)
    return pl.pallas_call(
        functools.partial(_ssm_prep_kernel, groups=gb, scan_steps=scan_steps),
        grid=(nslab,),
        in_specs=[vec, vec, vec, mat, mat, pl.BlockSpec(rep.shape, lambda i: (0, 0))] + side_in,
        out_specs=[
            pl.BlockSpec((None, cat, 2 * V7X_LANES), lambda i: (i, 0, 0)),
            pl.BlockSpec((None, SSM_CHUNK, V7X_LANES, ns), lambda i: (i, 0, 0, 0)),
            pl.BlockSpec((None, gb * ns, SSM_CHUNK * S5_GROUP), lambda i: (i, 0, 0)),
            pl.BlockSpec((gb, rows, ns), lambda i: (i, 0, 0)),
            pl.BlockSpec((gb, rows, ns), lambda i: (i, 0, 0)),
        ] + side_out,
        out_shape=[
            jax.ShapeDtypeStruct((nslab, cat, 2 * V7X_LANES), BF16),
            jax.ShapeDtypeStruct((nslab, SSM_CHUNK, V7X_LANES, ns), BF16),
            jax.ShapeDtypeStruct((nslab, gb * ns, SSM_CHUNK * S5_GROUP), BF16),
            jax.ShapeDtypeStruct((ng, rows, ns), F32),
            jax.ShapeDtypeStruct((ng, rows, ns), F32),
        ] + side_shapes,
        compiler_params=_params("parallel"),
        name=name,
    )(a_re2, a_im2, log_dt2, b2, c2, rep, *side_ops)


def replication_matrix():
    src = jnp.arange(SSM_CHUNK * S5_GROUP)
    dst = jnp.arange(SSM_CHUNK * V7X_LANES)
    same_step = (src[:, None] // S5_GROUP) == (dst[None, :] // V7X_LANES)
    same_chan = (src[:, None] % S5_GROUP) == (dst[None, :] % S5_GROUP)
    return (same_step & same_chan).astype(BF16)


def _group_of_lane(idx):
    return (idx % V7X_LANES) // S5_GROUP


def _ssm_kernel(u_ref, r_ref, gst_ref, hst_ref, sr_ref, si_ref, d_ref, rep_ref, *rest,
                chunks_per_seq, scan_steps, nside):
    y_ref = rest[2 * nside]
    _do_side_casts(rest[:2 * nside], rest[2 * nside + 1:3 * nside + 1])
    ucat_ref, gexp_ref, hexp_ref, hprev_ref, intra_ref = rest[3 * nside + 1:]
    t = u_ref.shape[0]
    nchunk = t // SSM_CHUNK
    ns = 2 * S5_STATE
    pair = 2 * V7X_LANES
    npairs = SSM_CHUNK // 2

    for step in range(SSM_CHUNK):
        x = u_ref[pl.ds(step, nchunk, stride=SSM_CHUNK), :]
        ucat_ref[:, step * V7X_LANES:(step + 1) * V7X_LANES] = x.astype(BF16)

    g_rows = lax.broadcasted_iota(jnp.int32, (V7X_LANES, GROUPS_PER_SLAB * ns), 0) // S5_GROUP
    g_cols = lax.broadcasted_iota(jnp.int32, (V7X_LANES, GROUPS_PER_SLAB * ns), 1) // ns
    for step in range(SSM_CHUNK):
        tiled = jnp.concatenate([gst_ref[step]] * GROUPS_PER_SLAB, axis=1)
        gexp_ref[step * V7X_LANES:(step + 1) * V7X_LANES, :] = jnp.where(g_rows == g_cols, tiled, 0.0).astype(BF16)

    h_rows = lax.broadcasted_iota(jnp.int32, (GROUPS_PER_SLAB * ns, pair), 0) // ns
    h_cols = _group_of_lane(lax.broadcasted_iota(jnp.int32, (GROUPS_PER_SLAB * ns, pair), 1))
    for b in range(npairs):
        spread = jnp.dot(hst_ref[...], rep_ref[:, b * pair:(b + 1) * pair], preferred_element_type=F32)
        hexp_ref[:, b * pair:(b + 1) * pair] = jnp.where(h_rows == h_cols, spread, 0.0).astype(BF16)

    hs_all = jnp.dot(ucat_ref[...], gexp_ref[...], preferred_element_type=F32)
    for b in range(npairs):
        intra_ref[:, b * pair:(b + 1) * pair] = jnp.dot(
            ucat_ref[:, :(b + 1) * pair], r_ref[(npairs - 1 - b) * pair:, :], preferred_element_type=F32)

    row = lax.broadcasted_iota(jnp.int32, (nchunk, ns), 0) % chunks_per_seq
    for g in range(GROUPS_PER_SLAB):
        hs = hs_all[:, g * ns:(g + 1) * ns]
        for k in range(scan_steps):
            d = 1 << k
            sh = jnp.where(row >= d, pltpu.roll(hs, d, axis=0), 0.0)
            hs = hs + sr_ref[g, k:k + 1, :] * sh + si_ref[g, k:k + 1, :] * pltpu.roll(sh, S5_STATE, axis=1)
        hprev_ref[:, g * ns:(g + 1) * ns] = jnp.where(row >= 1, pltpu.roll(hs, 1, axis=0), 0.0).astype(BF16)

    for b in range(npairs):
        y = intra_ref[:, b * pair:(b + 1) * pair] + jnp.dot(
            hprev_ref[...], hexp_ref[:, b * pair:(b + 1) * pair], preferred_element_type=F32)
        for bit in range(2):
            step = 2 * b + bit
            lanes = slice(step * V7X_LANES, (step + 1) * V7X_LANES)
            yl = y[:, bit * V7X_LANES:(bit + 1) * V7X_LANES] + d_ref[...] * ucat_ref[:, lanes].astype(F32)
            y_ref[pl.ds(step, nchunk, stride=SSM_CHUNK), :] = jax.nn.gelu(yl, approximate=True)


def ssm_apply(u, r_all, gst, hst, sr, si, d_skip, rep, *, seq, side=(), name="s5_ssm"):
    t, w = u.shape
    nslab = w // V7X_LANES
    side_in, side_out, side_shapes, side_ops = _side_cast_specs(side, nslab, lambda s: s)
    nchunk = t // SSM_CHUNK
    cat = SSM_CHUNK * V7X_LANES
    ns = 2 * S5_STATE
    chunks_per_seq = seq // SSM_CHUNK
    scan_steps = (chunks_per_seq - 1).bit_length()
    gb = GROUPS_PER_SLAB
    rows = sr.shape[1]
    return pl.pallas_call(
        functools.partial(_ssm_kernel, chunks_per_seq=chunks_per_seq, scan_steps=scan_steps, nside=len(side)),
        grid=(nslab,),
        in_specs=[
            pl.BlockSpec((t, V7X_LANES), lambda s: (0, s)),
            pl.BlockSpec((None, cat, 2 * V7X_LANES), lambda s: (s, 0, 0)),
            pl.BlockSpec((None, SSM_CHUNK, V7X_LANES, ns), lambda s: (s, 0, 0, 0)),
            pl.BlockSpec((None, gb * ns, SSM_CHUNK * S5_GROUP), lambda s: (s, 0, 0)),
            pl.BlockSpec((gb, rows, ns), lambda s: (s, 0, 0)),
            pl.BlockSpec((gb, rows, ns), lambda s: (s, 0, 0)),
            pl.BlockSpec((1, V7X_LANES), lambda s: (0, s)),
            pl.BlockSpec((SSM_CHUNK * S5_GROUP, cat), lambda s: (0, 0)),
        ] + side_in,
        out_specs=[pl.BlockSpec((t, V7X_LANES), lambda s: (0, s))] + side_out,
        out_shape=[jax.ShapeDtypeStruct((t, w), F32)] + side_shapes,
        scratch_shapes=[pltpu.VMEM((nchunk, cat), BF16), pltpu.VMEM((cat, gb * ns), BF16),
                        pltpu.VMEM((gb * ns, cat), BF16), pltpu.VMEM((nchunk, gb * ns), BF16),
                        pltpu.VMEM((nchunk, cat), F32)],
        compiler_params=pltpu.CompilerParams(dimension_semantics=("parallel",),
                                             vmem_limit_bytes=SSM_VMEM_LIMIT),
        name=name,
    )(u, r_all, gst, hst, sr, si, d_skip, rep, *side_ops)


ATTN_SPLIT = 3
ATTN_EXTRA_ROWS = 16


def _split3(c):
    hi = c.astype(BF16).astype(F32)
    mid = (c - hi).astype(BF16).astype(F32)
    return hi, mid, c - hi - mid


def _fox_attn_kernel(qt_ref, k_ref, vt_ref, ct_ref, sz_ref, o_ref,
                     qa_ref, ka_ref, va_ref, s0_ref, s1_ref, smax_ref, m_ref, acc_ref, *, tile):
    dh, seq = qt_ref.shape
    ntiles = seq // tile
    ext = ATTN_EXTRA_ROWS

    @pl.when((pl.program_id(0) == 0) & (pl.program_id(1) == 0))
    def _():
        qa_ref[dh + ext:, :] = jnp.zeros((dh - ext, seq), BF16)
        rowv = lax.broadcasted_iota(jnp.int32, (ext, seq), 0)
        va_ref[dh:, :] = jnp.where(rowv == 0, 1.0, 0.0).astype(BF16)

    parts = _split3(ct_ref[...])
    qa_ref[:dh, :] = qt_ref[...]
    row16 = lax.broadcasted_iota(jnp.int32, (ext, seq), 0)
    ext_q = jnp.where(row16 < 2 * ATTN_SPLIT, 1.0, 0.0)
    for n, part in enumerate(parts):
        ext_q = jnp.where(row16 == n, part, ext_q)
    qa_ref[dh:dh + ext, :] = ext_q.astype(BF16)
    va_ref[:dh, :] = vt_ref[...]

    rowk = lax.broadcasted_iota(jnp.int32, (dh, tile), 0)
    for r in range(ntiles):
        cols = slice(r * tile, (r + 1) * tile)
        ext_kt = jnp.where(rowk < ATTN_SPLIT, 1.0, 0.0)
        for n, part in enumerate(parts):
            ext_kt = jnp.where(rowk == ATTN_SPLIT + n, -part[:, cols], ext_kt)
        ka_ref[cols, :dh] = k_ref[cols, :]
        ka_ref[cols, dh:] = ext_kt.T.astype(BF16)

    sub = tile // 2
    causal = (lax.broadcasted_iota(jnp.int32, (sub, sub), 0) <= lax.broadcasted_iota(jnp.int32, (sub, sub), 1))
    s_slots = (s0_ref, s1_ref)

    def scores(k0, nk, q0, nq):
        return jnp.dot(ka_ref[k0:k0 + nk, :], qa_ref[:, q0:q0 + nq], preferred_element_type=F32)

    def produce(slot, qi, kb):
        k0, q0 = kb * tile, qi * tile
        if kb == qi:
            s_slots[slot][:sub, :] = scores(k0, sub, q0, tile)
            s_slots[slot][sub:, sub:] = scores(k0 + sub, sub, q0 + sub, sub)
        else:
            st = scores(k0, tile, q0, tile)
            s_slots[slot][...] = st
            smax_ref[slot:slot + 1, :] = jnp.max(st, axis=0, keepdims=True)

    def update(par, st, st_max, k0, c0, first):
        nk, nq = st.shape
        cols = slice(c0, c0 + nq)
        pv = functools.partial(jnp.dot, va_ref[:, k0:k0 + nk], preferred_element_type=F32)
        if first:
            m_ref[par:par + 1, cols] = st_max
            acc_ref[par, :, cols] = pv(jnp.exp2(st - st_max).astype(BF16))
        else:
            m = m_ref[par:par + 1, cols]
            m_new = jnp.maximum(m, st_max)
            acc_ref[par, :, cols] = (jnp.exp2(m - m_new) * acc_ref[par, :, cols]
                                     + pv(jnp.exp2(st - m_new).astype(BF16)))
            m_ref[par:par + 1, cols] = m_new

    def consume(slot, qi, kb):
        par, k0, s_ref = qi % 2, kb * tile, s_slots[slot]
        if kb != qi:
            update(par, s_ref[...], smax_ref[slot:slot + 1, :], k0, 0, kb == 0)
            return
        early = jnp.where(causal, s_ref[:sub, :sub], NEG_INF)
        update(par, early, jnp.max(early, axis=0, keepdims=True), k0, 0, kb == 0)
        late = jnp.concatenate([s_ref[:sub, sub:], jnp.where(causal, s_ref[sub:, sub:], NEG_INF)], axis=0)
        update(par, late, jnp.max(late, axis=0, keepdims=True), k0, sub, kb == 0)

    def finalize(qi):
        acc = acc_ref[qi % 2]
        rows = slice(qi * tile, (qi + 1) * tile)
        o = (acc[:dh, :] / acc[dh:dh + 1, :]).T
        o_ref[rows, :] = (o * sz_ref[rows, :].astype(F32)).astype(o_ref.dtype)

    blocks = [(qi, kb) for qi in range(ntiles) for kb in range(qi + 1)]
    produce(0, *blocks[0])
    for n, (qi, kb) in enumerate(blocks):
        if n + 1 < len(blocks):
            produce((n + 1) % 2, *blocks[n + 1])
        consume(n % 2, qi, kb)
        if kb == qi:
            finalize(qi)


def fox_attention(qt, k, vt, cumt2, sz, *, batch, seq, heads, tile=1024, name="fox_attention"):
    width, t = qt.shape
    dh = width // heads
    natural = pl.BlockSpec((seq, dh), lambda b, h: (b, h))
    transposed = pl.BlockSpec((dh, seq), lambda b, h: (h, b))
    return pl.pallas_call(
        functools.partial(_fox_attn_kernel, tile=tile),
        grid=(batch, heads),
        in_specs=[transposed, natural, transposed,
                  pl.BlockSpec((None, None, 1, seq), lambda b, h: (b, h, 0, 0)),
                  natural],
        out_specs=natural,
        out_shape=jax.ShapeDtypeStruct((t, width), BF16),
        scratch_shapes=[pltpu.VMEM((2 * dh, seq), BF16), pltpu.VMEM((seq, 2 * dh), BF16),
                        pltpu.VMEM((dh + ATTN_EXTRA_ROWS, seq), BF16),
                        pltpu.VMEM((tile, tile), F32), pltpu.VMEM((tile, tile), F32),
                        pltpu.VMEM((V7X_SUBLANES, tile), F32), pltpu.VMEM((V7X_SUBLANES, tile), F32),
                        pltpu.VMEM((2, dh + ATTN_EXTRA_ROWS, tile), F32)],
        compiler_params=_params("arbitrary", "arbitrary"),
        name=name,
    )(qt, k, vt, cumt2, sz)


def _identity(v):
    return v


def _doubled(v):
    return jnp.concatenate([v, v], axis=-1).astype(F32)


def kernel(x, norm_pre, norm_post, s5_w_in, s5_a_re, s5_a_im, s5_log_dt, s5_b_re, s5_b_im, s5_c_re, s5_c_im,
           s5_d, s5_w_glu, s5_b_glu, s5_w_out, kv_norm, kv_w, kv_b_f, fox_w_in, fox_w_out):
    batch, seq, d_model = x.shape
    t = batch * seq
    heads = kv_b_f.shape[0]
    fox_width = fox_w_in.shape[-1] // 2
    assert norm_pre.shape[0] == 2 and s5_w_in.shape[0] == 1 and fox_w_in.shape[0] == 1
    assert fox_width // heads == FOX_HEAD_DIM and heads <= V7X_LANES
    ngroups = s5_a_re.shape[1]
    chunks_per_seq = seq // SSM_CHUNK
    scan_steps = (chunks_per_seq - 1).bit_length()

    h0 = x.reshape(t, d_model)

    s5_width = s5_w_in.shape[-1] // 2
    ldt2 = jnp.broadcast_to(s5_log_dt[0][:, None], (ngroups, 2 * S5_STATE)).astype(F32)
    b2 = jnp.concatenate([jnp.swapaxes(s5_b_re[0], 1, 2), jnp.swapaxes(s5_b_im[0], 1, 2)], axis=-1)
    c2 = jnp.concatenate([s5_c_re[0], s5_c_im[0]], axis=-1)
    rep = replication_matrix()
    r_all, gst, hst, sr, si, w_in = ssm_prep(_doubled(s5_a_re[0]), _doubled(s5_a_im[0]), ldt2, b2.astype(F32),
                                             c2.astype(F32), rep, scan_steps=scan_steps,
                                             side=[side_cast(s5_w_in[0], row_scale=norm_pre[0])])
    kv_cols = 2 * fox_width
    kv_w_t = kv_w.T
    u, sz, w_glu, w_out, w_kv_t = fused_proj(
        h0, [Proj(w_in, False, 0, _identity, F32, False), Proj(w_in, False, s5_width, jax.nn.silu, BF16, False)],
        n=s5_width, tm=1024, name="s5_in_proj",
        side=[side_cast(s5_w_glu[0]), side_cast(s5_w_out[0]), side_cast(kv_w_t, kv_cols, col_scale=kv_norm)])
    yg, w_in_fox, w_out_fox = ssm_apply(u, r_all, gst, hst, sr, si, s5_d[0][None, :].astype(F32), rep, seq=seq,
                                        side=[side_cast(fox_w_in[0], row_scale=norm_pre[1]),
                                              side_cast(fox_w_out[0])])
    h1, h1_bf16, h1_inv_rms = s5_tail(yg, w_glu, s5_b_glu[0][None, :], sz, w_out, h0, norm_post[0][None, :])

    wf_t = jnp.pad((kv_w_t[kv_cols:] * kv_norm[None, :]).astype(BF16), ((0, V7X_LANES - heads), (0, 0)))
    bf = jnp.pad(kv_b_f, (0, V7X_LANES - heads))[None, :].astype(F32)
    scale = FOX_HEAD_DIM ** -0.5 * LOG2_E
    k, vt, qt, sz2, cumt2 = fused_proj(
        h1_bf16,
        [Proj(w_kv_t, True, 0, _identity, BF16, False),
         Proj(w_kv_t, True, fox_width, _identity, BF16, True),
         Proj(w_in_fox, False, 0, lambda v: v * scale, BF16, True),
         Proj(w_in_fox, False, fox_width, jax.nn.silu, BF16, False)],
        n=fox_width, inv=h1_inv_rms, gate=Gate(wf_t, bf, heads, seq), tm=1024, vmem_limit=FOX_PROJ_VMEM_LIMIT,
        name="fox_proj")
    o = fox_attention(qt, k, vt, cumt2.reshape(batch, heads, 1, seq), sz2,
                      batch=batch, seq=seq, heads=heads)
    h2 = out_proj(o, w_out_fox, h1, norm_post[1][None, :], name="fox_out_proj")
    return h2.reshape(batch, seq, d_model)
```

```python
import functools
import math
from typing import Callable, NamedTuple

import jax
import jax.numpy as jnp
from jax import lax
from jax.experimental import pallas as pl
from jax.experimental.pallas import tpu as pltpu

F32 = jnp.float32
BF16 = jnp.bfloat16

RMS_EPS = 1e-6
NEG_INF = -1e30
LOG2_E = math.log2(math.e)

S5_GROUP = 16
S5_STATE = 64
FOX_HEAD_DIM = 128

V7X_LANES = 128
V7X_SUBLANES = 8
SSM_CHUNK = 16
GROUPS_PER_SLAB = V7X_LANES // S5_GROUP
V7X_VMEM_BYTES = 64 * 1024 * 1024
VMEM_LIMIT = V7X_VMEM_BYTES * 3 // 4
LARGE_VMEM_LIMIT = V7X_VMEM_BYTES * 7 // 8


def _params(*semantics):
    return pltpu.CompilerParams(dimension_semantics=semantics, vmem_limit_bytes=VMEM_LIMIT)


def _rms_scale(x, gain):
    ms = jnp.mean(x * x, axis=-1, keepdims=True)
    return x * lax.rsqrt(ms + RMS_EPS) * gain


class Proj(NamedTuple):
    weight: jax.Array
    weight_is_t: bool
    first: int
    act: Callable
    dtype: jnp.dtype
    transposed: bool


def _log_sigmoid(x):
    return -(jnp.maximum(-x, 0.0) + jnp.log1p(jnp.exp(-jnp.abs(x))))


def _cumsum_rows(c):
    rows = c.shape[0]
    row = lax.broadcasted_iota(jnp.int32, c.shape, 0)
    d = 1
    while d < rows:
        c = c + jnp.where(row >= d, pltpu.roll(c, d, axis=0), 0.0)
        d *= 2
    return c


class SideCast(NamedTuple):
    src: jax.Array
    rows_used: int
    scale: jax.Array


def side_cast(src, rows_used=None, *, row_scale=None, col_scale=None):
    rows_used = src.shape[0] if rows_used is None else rows_used
    if row_scale is not None:
        scale = row_scale.reshape(rows_used, 1)
    elif col_scale is not None:
        scale = col_scale.reshape(1, src.shape[1])
    else:
        scale = jnp.ones((1, src.shape[1]), F32)
    return SideCast(src, rows_used, scale.astype(F32))


def _side_cast_specs(casts, steps, flat_step):
    in_specs, out_specs, out_shapes, operands = [], [], [], []
    for c in casts:
        block = (c.rows_used // steps, c.src.shape[1])
        by_step = lambda *idx: (flat_step(*idx), 0)
        in_specs.append(pl.BlockSpec(block, by_step))
        if c.scale.shape[0] == 1:
            in_specs.append(pl.BlockSpec(c.scale.shape, lambda *idx: (0, 0)))
        else:
            in_specs.append(pl.BlockSpec((block[0], 1), by_step))
        operands += [c.src, c.scale]
        out_specs.append(pl.BlockSpec(block, by_step))
        out_shapes.append(jax.ShapeDtypeStruct((c.rows_used, c.src.shape[1]), BF16))
    return in_specs, out_specs, out_shapes, operands


def _do_side_casts(in_refs, dst_refs):
    for n, dst_ref in enumerate(dst_refs):
        dst_ref[...] = (in_refs[2 * n][...] * in_refs[2 * n + 1][...]).astype(BF16)


def _inv_rms(x):
    inv = lax.rsqrt(jnp.mean(x * x, axis=-1, keepdims=True) + RMS_EPS)
    return jnp.broadcast_to(inv, (x.shape[0], V7X_LANES))


def _fused_proj_kernel(*refs, projs, gate, nside, prenormed):
    nproj = len(projs)
    x_ref = refs[0]
    pos = 1
    if prenormed:
        inv_in_ref = refs[1]
        pos = 2
    w_refs = refs[pos:pos + nproj]
    pos += nproj
    if gate is not None:
        wf_ref, bf_ref = refs[pos:pos + 2]
        pos += 2
    side_in = refs[pos:pos + 2 * nside]
    pos += 2 * nside
    o_refs = refs[pos:pos + nproj]
    pos += nproj
    if gate is not None:
        cumt_ref = refs[pos]
        pos += 1
    _do_side_casts(side_in, refs[pos:pos + nside])
    pos += nside
    if prenormed:
        xb_ref, inv_ref = x_ref, inv_in_ref
    else:
        xb_ref, inv_ref = refs[pos:pos + 2]
        pos += 2
    invt_ref = refs[pos]
    pos += 1
    nt = (((1,), (1,)), ((), ()))
    any_transposed = any(p.transposed for p in projs)

    @pl.when(pl.program_id(1) == 0)
    def _():
        if not prenormed:
            x = x_ref[...]
            xb_ref[...] = x.astype(BF16)
            inv_ref[...] = _inv_rms(x)
        if any_transposed:
            invt_ref[...] = inv_ref[...].T[:V7X_SUBLANES, :]
        if gate is not None:
            carry_ref = refs[pos]

            @pl.when(pl.program_id(0) % gate.tiles_per_seq == 0)
            def _():
                carry_ref[...] = jnp.zeros_like(carry_ref)

            logit = lax.dot_general(xb_ref[...], wf_ref[...], nt, preferred_element_type=F32)
            c = _cumsum_rows(_log_sigmoid(logit * inv_ref[...] + bf_ref[...])) + carry_ref[...]
            carry_ref[...] = c[c.shape[0] - 1:, :]
            cumt_ref[...] = (c * LOG2_E).T[:gate.heads, :]

    xb = xb_ref[...]
    for p, w_ref, o_ref in zip(projs, w_refs, o_refs):
        if p.weight_is_t and p.transposed:
            y = lax.dot_general(w_ref[...], xb, nt, preferred_element_type=F32)
        elif p.weight_is_t:
            y = lax.dot_general(xb, w_ref[...], nt, preferred_element_type=F32)
        else:
            y = jnp.dot(xb, w_ref[...], preferred_element_type=F32)
            y = y.T if p.transposed else y
        if p.transposed:
            y = y * invt_ref[0:1, :]
        else:
            y = y * jnp.concatenate([inv_ref[...]] * (y.shape[1] // V7X_LANES), axis=1)
        o_ref[...] = p.act(y).astype(o_ref.dtype)


class Gate(NamedTuple):
    wf_t: jax.Array
    bias: jax.Array
    heads: int
    seq: int
    tiles_per_seq: int = 0


def fused_proj(x, projs, *, n, inv=None, gate=None, side=(), tm=512, tn=512, vmem_limit=VMEM_LIMIT,
               name="fused_proj"):
    t, d = x.shape
    prenormed = inv is not None
    nj = n // tn
    w_specs, out_specs, out_shapes = [], [], []
    for p in projs:
        if p.weight_is_t:
            w_specs.append(pl.BlockSpec((tn, d), lambda i, j, b0=p.first // tn: (j + b0, 0)))
        else:
            w_specs.append(pl.BlockSpec((d, tn), lambda i, j, b0=p.first // tn: (0, j + b0)))
        if p.transposed:
            out_specs.append(pl.BlockSpec((tn, tm), lambda i, j: (j, i)))
            out_shapes.append(jax.ShapeDtypeStruct((n, t), p.dtype))
        else:
            out_specs.append(pl.BlockSpec((tm, tn), lambda i, j: (i, j)))
            out_shapes.append(jax.ShapeDtypeStruct((t, n), p.dtype))
    operands = [x] + ([inv] if prenormed else []) + [p.weight for p in projs]
    in_specs = [pl.BlockSpec((tm, d), lambda i, j: (i, 0))]
    if prenormed:
        in_specs.append(pl.BlockSpec((tm, V7X_LANES), lambda i, j: (i, 0)))
    in_specs += w_specs
    scratch = [] if prenormed else [pltpu.VMEM((tm, d), BF16), pltpu.VMEM((tm, V7X_LANES), F32)]
    scratch.append(pltpu.VMEM((V7X_SUBLANES, tm), F32))
    kernel_projs = tuple(p._replace(weight=None) for p in projs)
    kernel_gate = None
    if gate is not None:
        tps = gate.seq // tm
        kernel_gate = gate._replace(wf_t=None, bias=None, tiles_per_seq=tps)
        operands += [gate.wf_t, gate.bias]
        in_specs += [pl.BlockSpec((V7X_LANES, d), lambda i, j: (0, 0)), pl.BlockSpec((1, V7X_LANES), lambda i, j: (0, 0))]
        out_specs.append(pl.BlockSpec((None, gate.heads, tm), lambda i, j: (i // tps, 0, i % tps)))
        out_shapes.append(jax.ShapeDtypeStruct((t // gate.seq, gate.heads, gate.seq), F32))
        scratch.append(pltpu.VMEM((1, V7X_LANES), F32))
    side_in, side_out, side_shapes, side_ops = _side_cast_specs(side, (t // tm) * nj, lambda i, j: i * nj + j)
    operands += side_ops
    in_specs += side_in
    out_specs += side_out
    out_shapes += side_shapes
    return pl.pallas_call(
        functools.partial(_fused_proj_kernel, projs=kernel_projs, gate=kernel_gate, nside=len(side),
                          prenormed=prenormed),
        grid=(t // tm, nj),
        in_specs=in_specs,
        out_specs=out_specs,
        out_shape=out_shapes,
        scratch_shapes=scratch,
        compiler_params=pltpu.CompilerParams(dimension_semantics=("arbitrary", "arbitrary"),
                                             vmem_limit_bytes=vmem_limit),
        name=name,
    )(*operands)


def _out_proj_kernel(a_ref, w_ref, res_ref, gain_ref, o_ref):
    y = jnp.dot(a_ref[...], w_ref[...], preferred_element_type=F32)
    o_ref[...] = res_ref[...] + _rms_scale(y, gain_ref[...])


def out_proj(a, w, res, gain, *, tm=512, name="out_proj"):
    t, k = a.shape
    d = w.shape[1]
    return pl.pallas_call(
        _out_proj_kernel,
        grid=(t // tm,),
        in_specs=[
            pl.BlockSpec((tm, k), lambda i: (i, 0)),
            pl.BlockSpec((k, d), lambda i: (0, 0), pipeline_mode=pl.Buffered(1)),
            pl.BlockSpec((tm, d), lambda i: (i, 0)),
            pl.BlockSpec((1, d), lambda i: (0, 0)),
        ],
        out_specs=pl.BlockSpec((tm, d), lambda i: (i, 0)),
        out_shape=jax.ShapeDtypeStruct((t, d), F32),
        compiler_params=_params("parallel"),
        name=name,
    )(a, w, res, gain)


def _s5_tail_kernel(yg_ref, wg_ref, b_ref, sz_ref, wo_ref, res_ref, gain_ref, o_ref, ob_ref, inv_ref, *, halves):
    rows = yg_ref.shape[0] // halves
    for h in range(halves):
        r = slice(h * rows, (h + 1) * rows)
        yg = yg_ref[r, :]
        gate = jax.nn.sigmoid(jnp.dot(yg.astype(BF16), wg_ref[...], preferred_element_type=F32) + b_ref[...])
        y3 = (yg * gate * sz_ref[r, :].astype(F32)).astype(BF16)
        y = jnp.dot(y3, wo_ref[...], preferred_element_type=F32)
        out = res_ref[r, :] + _rms_scale(y, gain_ref[...])
        o_ref[r, :] = out
        ob_ref[r, :] = out.astype(BF16)
        inv_ref[r, :] = _inv_rms(out)


def s5_tail(yg, w_glu, b_glu, sz, w_out, res, gain, *, tm=512, halves=2, name="s5_tail"):
    t, k = yg.shape
    d = w_out.shape[1]
    rows = pl.BlockSpec((tm, k), lambda i: (i, 0))
    resident = functools.partial(pl.BlockSpec, index_map=lambda i: (0, 0), pipeline_mode=pl.Buffered(1))
    return pl.pallas_call(
        functools.partial(_s5_tail_kernel, halves=halves),
        grid=(t // tm,),
        in_specs=[rows, resident((k, k)), resident((1, k)), rows, resident((k, d)),
                  pl.BlockSpec((tm, d), lambda i: (i, 0)), resident((1, d))],
        out_specs=[pl.BlockSpec((tm, d), lambda i: (i, 0)), pl.BlockSpec((tm, d), lambda i: (i, 0)),
                   pl.BlockSpec((tm, V7X_LANES), lambda i: (i, 0))],
        out_shape=[jax.ShapeDtypeStruct((t, d), F32), jax.ShapeDtypeStruct((t, d), BF16),
                   jax.ShapeDtypeStruct((t, V7X_LANES), F32)],
        compiler_params=pltpu.CompilerParams(dimension_semantics=("parallel",),
                                             vmem_limit_bytes=LARGE_VMEM_LIMIT),
        name=name,
    )(yg, w_glu, b_glu, sz, w_out, res, gain)


def _ssm_prep_kernel(are_ref, aim_ref, ldt_ref, b2_ref, c2_ref, rep_ref, *rest, groups, scan_steps):
    nside = (len(rest) - 5) // 3
    r_ref, g_ref, h_ref, sr_ref, si_ref = rest[2 * nside:2 * nside + 5]
    _do_side_casts(rest[:2 * nside], rest[2 * nside + 5:])
    lane = lax.broadcasted_iota(jnp.int32, (1, 2 * S5_STATE), 1)
    minus_plus = jnp.where(lane < S5_STATE, -1.0, 1.0).astype(F32)
    gw = S5_GROUP
    ns = 2 * S5_STATE
    krows = []

    ar = are_ref[...]
    ai = aim_ref[...]
    dt = jnp.exp(ldt_ref[...])
    mag = jnp.exp(ar * dt)
    lam_r = mag * jnp.cos(ai * dt)
    lam_i = mag * jnp.sin(ai * dt)
    den = ar * ar + ai * ai
    nr = lam_r - 1.0
    coef_r_all = (nr * ar + lam_i * ai) / den
    coef_i_all = (lam_i * ar - nr * ai) / den
    pow_r_all = [jnp.ones_like(lam_r)]
    pow_i_all = [jnp.zeros_like(lam_r)]
    for _ in range(SSM_CHUNK):
        pr, pi = pow_r_all[-1], pow_i_all[-1]
        pow_r_all.append(pr * lam_r - pi * lam_i)
        pow_i_all.append(pr * lam_i + pi * lam_r)
    mu_r, mu_i = pow_r_all[SSM_CHUNK], pow_i_all[SSM_CHUNK]
    scan_r_all, scan_i_all = [], []
    for _ in range(scan_steps):
        scan_r_all.append(mu_r)
        scan_i_all.append(mu_i * minus_plus)
        mu_r, mu_i = mu_r * mu_r - mu_i * mu_i, 2.0 * mu_r * mu_i

    for gi in range(groups):
        grp = slice(gi, gi + 1)
        coef_r, coef_i = coef_r_all[grp], coef_i_all[grp]
        pow_r = [p[grp] for p in pow_r_all]
        pow_i = [p[grp] for p in pow_i_all]

        b2 = b2_ref[gi]
        b2s = pltpu.roll(b2, S5_STATE, axis=1) * minus_plus
        c2 = c2_ref[gi]
        c2a = c2 * (-minus_plus)
        c2b = -pltpu.roll(c2, S5_STATE, axis=1)

        for step in range(SSM_CHUNK):
            pr, pi = pow_r[SSM_CHUNK - 1 - step], pow_i[SSM_CHUNK - 1 - step]
            wr = pr * coef_r - pi * coef_i
            wi = pr * coef_i + pi * coef_r
            g_ref[step, gi * gw:(gi + 1) * gw, :] = (wr * b2 + wi * b2s).astype(g_ref.dtype)

        cl = [pow_r[tau] * c2a + pow_i[tau] * c2b for tau in range(SSM_CHUNK + 1)]
        h_t = jnp.concatenate(cl[1:], axis=0)
        h_ref[gi * ns:(gi + 1) * ns, :] = h_t.T.astype(h_ref.dtype)

        bbar2 = coef_r * b2 + coef_i * b2s
        cl_all = jnp.concatenate(cl[:-1], axis=0)
        krows.append(lax.dot_general(bbar2, cl_all, (((1,), (1,)), ((), ())),
                                     precision=lax.Precision.HIGHEST,
                                     preferred_element_type=F32))

        pad = [jnp.zeros_like(coef_r)] * (sr_ref.shape[1] - scan_steps)
        sr_ref[gi] = jnp.concatenate([s[grp] for s in scan_r_all] + pad, axis=0)
        si_ref[gi] = jnp.concatenate([s[grp] for s in scan_i_all] + pad, axis=0)

    kst = jnp.concatenate(krows, axis=0).astype(BF16)
    spread = jnp.dot(kst, rep_ref[...], preferred_element_type=F32)
    rows_g = lax.broadcasted_iota(jnp.int32, spread.shape, 0) // gw
    cols_g = _group_of_lane(lax.broadcasted_iota(jnp.int32, spread.shape, 1))
    spread = jnp.where(rows_g == cols_g, spread, 0.0).astype(r_ref.dtype)
    blocks = [spread[:, tau * V7X_LANES:(tau + 1) * V7X_LANES] for tau in range(SSM_CHUNK)]
    zero = jnp.zeros_like(blocks[0])
    npairs = SSM_CHUNK // 2
    for d in range(npairs):
        base = (npairs - 1 - d) * 2 * V7X_LANES
        top = jnp.concatenate([blocks[2 * d], blocks[2 * d + 1]], axis=1)
        bottom = jnp.concatenate([blocks[2 * d - 1] if d else zero, blocks[2 * d]], axis=1)
        r_ref[base:base + V7X_LANES, :] = top
        r_ref[base + V7X_LANES:base + 2 * V7X_LANES, :] = bottom


def ssm_prep(a_re2, a_im2, log_dt2, b2, c2, rep, *, scan_steps, side=(), name="ssm_prep"):
    ng = a_re2.shape[0]
    gb = GROUPS_PER_SLAB
    nslab = ng // gb
    cat = SSM_CHUNK * V7X_LANES
    ns = 2 * S5_STATE
    rows = -(-scan_steps // V7X_SUBLANES) * V7X_SUBLANES
    vec = pl.BlockSpec((gb, ns), lambda i: (i, 0))
    mat = pl.BlockSpec((gb, S5_GROUP, ns), lambda i: (i, 0, 0))
    side_in, side_out, side_shapes, side_ops = _side_cast_specs(side, nslab, lambda i: i)
    return pl.pallas_call(
        functools.partial(_ssm_prep_kernel, groups=gb, scan_steps=scan_steps),
        grid=(nslab,),
        in_specs=[vec, vec, vec, mat, mat, pl.BlockSpec(rep.shape, lambda i: (0, 0))] + side_in,
        out_specs=[
            pl.BlockSpec((None, cat, 2 * V7X_LANES), lambda i: (i, 0, 0)),
            pl.BlockSpec((None, SSM_CHUNK, V7X_LANES, ns), lambda i: (i, 0, 0, 0)),
            pl.BlockSpec((None, gb * ns, SSM_CHUNK * S5_GROUP), lambda i: (i, 0, 0)),
            pl.BlockSpec((gb, rows, ns), lambda i: (i, 0, 0)),
            pl.BlockSpec((gb, rows, ns), lambda i: (i, 0, 0)),
        ] + side_out,
        out_shape=[
            jax.ShapeDtypeStruct((nslab, cat, 2 * V7X_LANES), BF16),
            jax.ShapeDtypeStruct((nslab, SSM_CHUNK, V7X_LANES, ns), BF16),
            jax.ShapeDtypeStruct((nslab, gb * ns, SSM_CHUNK * S5_GROUP), BF16),
            jax.ShapeDtypeStruct((ng, rows, ns), F32),
            jax.ShapeDtypeStruct((ng, rows, ns), F32),
        ] + side_shapes,
        compiler_params=_params("parallel"),
        name=name,
    )(a_re2, a_im2, log_dt2, b2, c2, rep, *side_ops)


def replication_matrix():
    src = jnp.arange(SSM_CHUNK * S5_GROUP)
    dst = jnp.arange(SSM_CHUNK * V7X_LANES)
    same_step = (src[:, None] // S5_GROUP) == (dst[None, :] // V7X_LANES)
    same_chan = (src[:, None] % S5_GROUP) == (dst[None, :] % S5_GROUP)
    return (same_step & same_chan).astype(BF16)


def _group_of_lane(idx):
    return (idx % V7X_LANES) // S5_GROUP


def _ssm_kernel(u_ref, r_ref, gst_ref, hst_ref, sr_ref, si_ref, d_ref, rep_ref, *rest,
                chunks_per_seq, scan_steps, nside):
    y_ref = rest[2 * nside]
    _do_side_casts(rest[:2 * nside], rest[2 * nside + 1:3 * nside + 1])
    ucat_ref, gexp_ref, hexp_ref, hprev_ref, intra_ref = rest[3 * nside + 1:]
    t = u_ref.shape[0]
    nchunk = t // SSM_CHUNK
    ns = 2 * S5_STATE
    pair = 2 * V7X_LANES
    npairs = SSM_CHUNK // 2

    for step in range(SSM_CHUNK):
        x = u_ref[pl.ds(step, nchunk, stride=SSM_CHUNK), :]
        ucat_ref[:, step * V7X_LANES:(step + 1) * V7X_LANES] = x.astype(BF16)

    g_rows = lax.broadcasted_iota(jnp.int32, (V7X_LANES, GROUPS_PER_SLAB * ns), 0) // S5_GROUP
    g_cols = lax.broadcasted_iota(jnp.int32, (V7X_LANES, GROUPS_PER_SLAB * ns), 1) // ns
    for step in range(SSM_CHUNK):
        tiled = jnp.concatenate([gst_ref[step]] * GROUPS_PER_SLAB, axis=1)
        gexp_ref[step * V7X_LANES:(step + 1) * V7X_LANES, :] = jnp.where(g_rows == g_cols, tiled, 0.0).astype(BF16)

    h_rows = lax.broadcasted_iota(jnp.int32, (GROUPS_PER_SLAB * ns, pair), 0) // ns
    h_cols = _group_of_lane(lax.broadcasted_iota(jnp.int32, (GROUPS_PER_SLAB * ns, pair), 1))
    for b in range(npairs):
        spread = jnp.dot(hst_ref[...], rep_ref[:, b * pair:(b + 1) * pair], preferred_element_type=F32)
        hexp_ref[:, b * pair:(b + 1) * pair] = jnp.where(h_rows == h_cols, spread, 0.0).astype(BF16)

    hs_all = jnp.dot(ucat_ref[...], gexp_ref[...], preferred_element_type=F32)
    for b in range(npairs):
        intra_ref[:, b * pair:(b + 1) * pair] = jnp.dot(
            ucat_ref[:, :(b + 1) * pair], r_ref[(npairs - 1 - b) * pair:, :], preferred_element_type=F32)

    row = lax.broadcasted_iota(jnp.int32, (nchunk, ns), 0) % chunks_per_seq
    for g in range(GROUPS_PER_SLAB):
        hs = hs_all[:, g * ns:(g + 1) * ns]
        for k in range(scan_steps):
            d = 1 << k
            sh = jnp.where(row >= d, pltpu.roll(hs, d, axis=0), 0.0)
            hs = hs + sr_ref[g, k:k + 1, :] * sh + si_ref[g, k:k + 1, :] * pltpu.roll(sh, S5_STATE, axis=1)
        hprev_ref[:, g * ns:(g + 1) * ns] = jnp.where(row >= 1, pltpu.roll(hs, 1, axis=0), 0.0).astype(BF16)

    for b in range(npairs):
        y = intra_ref[:, b * pair:(b + 1) * pair] + jnp.dot(
            hprev_ref[...], hexp_ref[:, b * pair:(b + 1) * pair], preferred_element_type=F32)
        for bit in range(2):
            step = 2 * b + bit
            lanes = slice(step * V7X_LANES, (step + 1) * V7X_LANES)
            yl = y[:, bit * V7X_LANES:(bit + 1) * V7X_LANES] + d_ref[...] * ucat_ref[:, lanes].astype(F32)
            y_ref[pl.ds(step, nchunk, stride=SSM_CHUNK), :] = jax.nn.gelu(yl, approximate=True)


def ssm_apply(u, r_all, gst, hst, sr, si, d_skip, rep, *, seq, side=(), name="s5_ssm"):
    t, w = u.shape
    nslab = w // V7X_LANES
    side_in, side_out, side_shapes, side_ops = _side_cast_specs(side, nslab, lambda s: s)
    nchunk = t // SSM_CHUNK
    cat = SSM_CHUNK * V7X_LANES
    ns = 2 * S5_STATE
    chunks_per_seq = seq // SSM_CHUNK
    scan_steps = (chunks_per_seq - 1).bit_length()
    gb = GROUPS_PER_SLAB
    rows = sr.shape[1]
    return pl.pallas_call(
        functools.partial(_ssm_kernel, chunks_per_seq=chunks_per_seq, scan_steps=scan_steps, nside=len(side)),
        grid=(nslab,),
        in_specs=[
            pl.BlockSpec((t, V7X_LANES), lambda s: (0, s)),
            pl.BlockSpec((None, cat, 2 * V7X_LANES), lambda s: (s, 0, 0)),
            pl.BlockSpec((None, SSM_CHUNK, V7X_LANES, ns), lambda s: (s, 0, 0, 0)),
            pl.BlockSpec((None, gb * ns, SSM_CHUNK * S5_GROUP), lambda s: (s, 0, 0)),
            pl.BlockSpec((gb, rows, ns), lambda s: (s, 0, 0)),
            pl.BlockSpec((gb, rows, ns), lambda s: (s, 0, 0)),
            pl.BlockSpec((1, V7X_LANES), lambda s: (0, s)),
            pl.BlockSpec((SSM_CHUNK * S5_GROUP, cat), lambda s: (0, 0)),
        ] + side_in,
        out_specs=[pl.BlockSpec((t, V7X_LANES), lambda s: (0, s))] + side_out,
        out_shape=[jax.ShapeDtypeStruct((t, w), F32)] + side_shapes,
        scratch_shapes=[pltpu.VMEM((nchunk, cat), BF16), pltpu.VMEM((cat, gb * ns), BF16),
                        pltpu.VMEM((gb * ns, cat), BF16), pltpu.VMEM((nchunk, gb * ns), BF16),
                        pltpu.VMEM((nchunk, cat), F32)],
        compiler_params=pltpu.CompilerParams(dimension_semantics=("parallel",),
                                             vmem_limit_bytes=LARGE_VMEM_LIMIT),
        name=name,
    )(u, r_all, gst, hst, sr, si, d_skip, rep, *side_ops)


ATTN_SPLIT = 3
ATTN_EXTRA_ROWS = 16


def _split3(c):
    hi = c.astype(BF16).astype(F32)
    mid = (c - hi).astype(BF16).astype(F32)
    return hi, mid, c - hi - mid


def _fox_attn_kernel(qt_ref, k_ref, vt_ref, ct_ref, sz_ref, o_ref,
                     qa_ref, ka_ref, va_ref, s0_ref, s1_ref, smax_ref, m_ref, acc_ref, *, tile):
    dh, seq = qt_ref.shape
    ntiles = seq // tile
    ext = ATTN_EXTRA_ROWS

    @pl.when((pl.program_id(0) == 0) & (pl.program_id(1) == 0))
    def _():
        qa_ref[dh + ext:, :] = jnp.zeros((dh - ext, seq), BF16)
        rowv = lax.broadcasted_iota(jnp.int32, (ext, seq), 0)
        va_ref[dh:, :] = jnp.where(rowv == 0, 1.0, 0.0).astype(BF16)

    parts = _split3(ct_ref[...])
    qa_ref[:dh, :] = qt_ref[...]
    row16 = lax.broadcasted_iota(jnp.int32, (ext, seq), 0)
    ext_q = jnp.where(row16 < 2 * ATTN_SPLIT, 1.0, 0.0)
    for n, part in enumerate(parts):
        ext_q = jnp.where(row16 == n, part, ext_q)
    qa_ref[dh:dh + ext, :] = ext_q.astype(BF16)
    va_ref[:dh, :] = vt_ref[...]

    rowk = lax.broadcasted_iota(jnp.int32, (dh, tile), 0)
    for r in range(ntiles):
        cols = slice(r * tile, (r + 1) * tile)
        ext_kt = jnp.where(rowk < ATTN_SPLIT, 1.0, 0.0)
        for n, part in enumerate(parts):
            ext_kt = jnp.where(rowk == ATTN_SPLIT + n, -part[:, cols], ext_kt)
        ka_ref[cols, :dh] = k_ref[cols, :]
        ka_ref[cols, dh:] = ext_kt.T.astype(BF16)

    sub = tile // 2
    causal = (lax.broadcasted_iota(jnp.int32, (sub, sub), 0) <= lax.broadcasted_iota(jnp.int32, (sub, sub), 1))
    s_slots = (s0_ref, s1_ref)

    def scores(k0, nk, q0, nq):
        return jnp.dot(ka_ref[k0:k0 + nk, :], qa_ref[:, q0:q0 + nq], preferred_element_type=F32)

    def produce(slot, qi, kb):
        k0, q0 = kb * tile, qi * tile
        if kb == qi:
            s_slots[slot][:sub, :] = scores(k0, sub, q0, tile)
            s_slots[slot][sub:, sub:] = scores(k0 + sub, sub, q0 + sub, sub)
        else:
            st = scores(k0, tile, q0, tile)
            s_slots[slot][...] = st
            smax_ref[slot:slot + 1, :] = jnp.max(st, axis=0, keepdims=True)

    def update(par, st, st_max, k0, c0, first):
        nk, nq = st.shape
        cols = slice(c0, c0 + nq)
        pv = functools.partial(jnp.dot, va_ref[:, k0:k0 + nk], preferred_element_type=F32)
        if first:
            m_ref[par:par + 1, cols] = st_max
            acc_ref[par, :, cols] = pv(jnp.exp2(st - st_max).astype(BF16))
        else:
            m = m_ref[par:par + 1, cols]
            m_new = jnp.maximum(m, st_max)
            acc_ref[par, :, cols] = (jnp.exp2(m - m_new) * acc_ref[par, :, cols]
                                     + pv(jnp.exp2(st - m_new).astype(BF16)))
            m_ref[par:par + 1, cols] = m_new

    def consume(slot, qi, kb):
        par, k0, s_ref = qi % 2, kb * tile, s_slots[slot]
        if kb != qi:
            update(par, s_ref[...], smax_ref[slot:slot + 1, :], k0, 0, kb == 0)
            return
        early = jnp.where(causal, s_ref[:sub, :sub], NEG_INF)
        update(par, early, jnp.max(early, axis=0, keepdims=True), k0, 0, kb == 0)
        late = jnp.concatenate([s_ref[:sub, sub:], jnp.where(causal, s_ref[sub:, sub:], NEG_INF)], axis=0)
        update(par, late, jnp.max(late, axis=0, keepdims=True), k0, sub, kb == 0)

    def finalize(qi):
        acc = acc_ref[qi % 2]
        rows = slice(qi * tile, (qi + 1) * tile)
        o = (acc[:dh, :] / acc[dh:dh + 1, :]).T
        o_ref[rows, :] = (o * sz_ref[rows, :].astype(F32)).astype(o_ref.dtype)

    blocks = [(qi, kb) for qi in range(ntiles) for kb in range(qi + 1)]
    produce(0, *blocks[0])
    for n, (qi, kb) in enumerate(blocks):
        if n + 1 < len(blocks):
            produce((n + 1) % 2, *blocks[n + 1])
        consume(n % 2, qi, kb)
        if kb == qi:
            finalize(qi)


def fox_attention(qt, k, vt, cumt2, sz, *, batch, seq, heads, tile=1024, name="fox_attention"):
    width, t = qt.shape
    dh = width // heads
    natural = pl.BlockSpec((seq, dh), lambda b, h: (b, h))
    transposed = pl.BlockSpec((dh, seq), lambda b, h: (h, b))
    return pl.pallas_call(
        functools.partial(_fox_attn_kernel, tile=tile),
        grid=(batch, heads),
        in_specs=[transposed, natural, transposed,
                  pl.BlockSpec((None, None, 1, seq), lambda b, h: (b, h, 0, 0)),
                  natural],
        out_specs=natural,
        out_shape=jax.ShapeDtypeStruct((t, width), BF16),
        scratch_shapes=[pltpu.VMEM((2 * dh, seq), BF16), pltpu.VMEM((seq, 2 * dh), BF16),
                        pltpu.VMEM((dh + ATTN_EXTRA_ROWS, seq), BF16),
                        pltpu.VMEM((tile, tile), F32), pltpu.VMEM((tile, tile), F32),
                        pltpu.VMEM((V7X_SUBLANES, tile), F32), pltpu.VMEM((V7X_SUBLANES, tile), F32),
                        pltpu.VMEM((2, dh + ATTN_EXTRA_ROWS, tile), F32)],
        compiler_params=_params("arbitrary", "arbitrary"),
        name=name,
    )(qt, k, vt, cumt2, sz)


def _identity(v):
    return v


def _doubled(v):
    return jnp.concatenate([v, v], axis=-1).astype(F32)


def kernel(x, norm_pre, norm_post, s5_w_in, s5_a_re, s5_a_im, s5_log_dt, s5_b_re, s5_b_im, s5_c_re, s5_c_im,
           s5_d, s5_w_glu, s5_b_glu, s5_w_out, kv_norm, kv_w, kv_b_f, fox_w_in, fox_w_out):
    batch, seq, d_model = x.shape
    t = batch * seq
    heads = kv_b_f.shape[0]
    fox_width = fox_w_in.shape[-1] // 2
    assert norm_pre.shape[0] == 2 and s5_w_in.shape[0] == 1 and fox_w_in.shape[0] == 1
    assert fox_width // heads == FOX_HEAD_DIM and heads <= V7X_LANES
    ngroups = s5_a_re.shape[1]
    chunks_per_seq = seq // SSM_CHUNK
    scan_steps = (chunks_per_seq - 1).bit_length()

    h0 = x.reshape(t, d_model)

    s5_width = s5_w_in.shape[-1] // 2
    ldt2 = jnp.broadcast_to(s5_log_dt[0][:, None], (ngroups, 2 * S5_STATE)).astype(F32)
    b2 = jnp.concatenate([jnp.swapaxes(s5_b_re[0], 1, 2), jnp.swapaxes(s5_b_im[0], 1, 2)], axis=-1)
    c2 = jnp.concatenate([s5_c_re[0], s5_c_im[0]], axis=-1)
    rep = replication_matrix()
    r_all, gst, hst, sr, si, w_in = ssm_prep(_doubled(s5_a_re[0]), _doubled(s5_a_im[0]), ldt2, b2.astype(F32),
                                             c2.astype(F32), rep, scan_steps=scan_steps,
                                             side=[side_cast(s5_w_in[0], row_scale=norm_pre[0])])
    kv_cols = 2 * fox_width
    kv_w_t = kv_w.T
    u, sz, w_glu, w_out, w_kv_t = fused_proj(
        h0, [Proj(w_in, False, 0, _identity, F32, False), Proj(w_in, False, s5_width, jax.nn.silu, BF16, False)],
        n=s5_width, tm=1024, name="s5_in_proj",
        side=[side_cast(s5_w_glu[0]), side_cast(s5_w_out[0]), side_cast(kv_w_t, kv_cols, col_scale=kv_norm)])
    yg, w_in_fox, w_out_fox = ssm_apply(u, r_all, gst, hst, sr, si, s5_d[0][None, :].astype(F32), rep, seq=seq,
                                        side=[side_cast(fox_w_in[0], row_scale=norm_pre[1]),
                                              side_cast(fox_w_out[0])])
    h1, h1_bf16, h1_inv_rms = s5_tail(yg, w_glu, s5_b_glu[0][None, :], sz, w_out, h0, norm_post[0][None, :])

    wf_t = jnp.pad((kv_w_t[kv_cols:] * kv_norm[None, :]).astype(BF16), ((0, V7X_LANES - heads), (0, 0)))
    bf = jnp.pad(kv_b_f, (0, V7X_LANES - heads))[None, :].astype(F32)
    scale = FOX_HEAD_DIM ** -0.5 * LOG2_E
    k, vt, qt, sz2, cumt2 = fused_proj(
        h1_bf16,
        [Proj(w_kv_t, True, 0, _identity, BF16, False),
         Proj(w_kv_t, True, fox_width, _identity, BF16, True),
         Proj(w_in_fox, False, 0, lambda v: v * scale, BF16, True),
         Proj(w_in_fox, False, fox_width, jax.nn.silu, BF16, False)],
        n=fox_width, inv=h1_inv_rms, gate=Gate(wf_t, bf, heads, seq), tm=1024, vmem_limit=LARGE_VMEM_LIMIT,
        name="fox_proj")
    o = fox_attention(qt, k, vt, cumt2.reshape(batch, heads, 1, seq), sz2,
                      batch=batch, seq=seq, heads=heads)
    h2 = out_proj(o, w_out_fox, h1, norm_post[1][None, :], name="fox_out_proj")
    return h2.reshape(batch, seq, d_model)
```

```python
import functools
import math
from typing import Callable, NamedTuple

import jax
import jax.numpy as jnp
from jax import lax
from jax.experimental import pallas as pl
from jax.experimental.pallas import tpu as pltpu

F32 = jnp.float32
BF16 = jnp.bfloat16

RMS_EPS = 1e-6
NEG_INF = -1e30
LOG2_E = math.log2(math.e)

S5_GROUP = 16
S5_STATE = 64
FOX_HEAD_DIM = 128

V7X_LANES = 128
V7X_SUBLANES = 8
SSM_CHUNK = 16
GROUPS_PER_SLAB = V7X_LANES // S5_GROUP
QUADS_PER_SLAB = 2
QUAD_LANES = V7X_LANES // QUADS_PER_SLAB
GROUPS_PER_QUAD = GROUPS_PER_SLAB // QUADS_PER_SLAB
STEPS_PER_TILE = 2 * V7X_LANES // QUAD_LANES
V7X_VMEM_BYTES = 64 * 1024 * 1024
VMEM_LIMIT = V7X_VMEM_BYTES * 3 // 4
LARGE_VMEM_LIMIT = V7X_VMEM_BYTES * 7 // 8


def _params(*semantics):
    return pltpu.CompilerParams(dimension_semantics=semantics, vmem_limit_bytes=VMEM_LIMIT)


def _rms_scale(x, gain):
    ms = jnp.mean(x * x, axis=-1, keepdims=True)
    return x * lax.rsqrt(ms + RMS_EPS) * gain


class Proj(NamedTuple):
    weight: jax.Array
    weight_is_t: bool
    first: int
    act: Callable
    dtype: jnp.dtype
    transposed: bool


def _log_sigmoid(x):
    return -(jnp.maximum(-x, 0.0) + jnp.log1p(jnp.exp(-jnp.abs(x))))


def _cumsum_rows(c):
    rows = c.shape[0]
    row = lax.broadcasted_iota(jnp.int32, c.shape, 0)
    d = 1
    while d < rows:
        c = c + jnp.where(row >= d, pltpu.roll(c, d, axis=0), 0.0)
        d *= 2
    return c


class SideCast(NamedTuple):
    src: jax.Array
    rows_used: int
    scale: jax.Array


def side_cast(src, rows_used=None, *, row_scale=None, col_scale=None):
    rows_used = src.shape[0] if rows_used is None else rows_used
    if row_scale is not None:
        scale = row_scale.reshape(rows_used, 1)
    elif col_scale is not None:
        scale = col_scale.reshape(1, src.shape[1])
    else:
        scale = jnp.ones((1, src.shape[1]), F32)
    return SideCast(src, rows_used, scale.astype(F32))


def _side_cast_specs(casts, steps, flat_step):
    in_specs, out_specs, out_shapes, operands = [], [], [], []
    for c in casts:
        block = (c.rows_used // steps, c.src.shape[1])
        by_step = lambda *idx: (flat_step(*idx), 0)
        in_specs.append(pl.BlockSpec(block, by_step))
        if c.scale.shape[0] == 1:
            in_specs.append(pl.BlockSpec(c.scale.shape, lambda *idx: (0, 0)))
        else:
            in_specs.append(pl.BlockSpec((block[0], 1), by_step))
        operands += [c.src, c.scale]
        out_specs.append(pl.BlockSpec(block, by_step))
        out_shapes.append(jax.ShapeDtypeStruct((c.rows_used, c.src.shape[1]), BF16))
    return in_specs, out_specs, out_shapes, operands


def _do_side_casts(in_refs, dst_refs):
    for n, dst_ref in enumerate(dst_refs):
        dst_ref[...] = (in_refs[2 * n][...] * in_refs[2 * n + 1][...]).astype(BF16)


def _inv_rms(x):
    inv = lax.rsqrt(jnp.mean(x * x, axis=-1, keepdims=True) + RMS_EPS)
    return jnp.broadcast_to(inv, (x.shape[0], V7X_LANES))


def _fused_proj_kernel(*refs, projs, gate, nside, prenormed):
    nproj = len(projs)
    x_ref = refs[0]
    pos = 1
    if prenormed:
        inv_in_ref = refs[1]
        pos = 2
    w_refs = refs[pos:pos + nproj]
    pos += nproj
    if gate is not None:
        wf_ref, bf_ref = refs[pos:pos + 2]
        pos += 2
    side_in = refs[pos:pos + 2 * nside]
    pos += 2 * nside
    o_refs = refs[pos:pos + nproj]
    pos += nproj
    if gate is not None:
        cumt_ref = refs[pos]
        pos += 1
    _do_side_casts(side_in, refs[pos:pos + nside])
    pos += nside
    if prenormed:
        xb_ref, inv_ref = x_ref, inv_in_ref
    else:
        xb_ref, inv_ref = refs[pos:pos + 2]
        pos += 2
    invt_ref = refs[pos]
    pos += 1
    nt = (((1,), (1,)), ((), ()))
    any_transposed = any(p.transposed for p in projs)

    @pl.when(pl.program_id(1) == 0)
    def _():
        if not prenormed:
            x = x_ref[...]
            xb_ref[...] = x.astype(BF16)
            inv_ref[...] = _inv_rms(x)
        if any_transposed:
            invt_ref[...] = inv_ref[...].T[:V7X_SUBLANES, :]
        if gate is not None:
            carry_ref = refs[pos]

            @pl.when(pl.program_id(0) % gate.tiles_per_seq == 0)
            def _():
                carry_ref[...] = jnp.zeros_like(carry_ref)

            logit = lax.dot_general(xb_ref[...], wf_ref[...], nt, preferred_element_type=F32)
            c = _cumsum_rows(_log_sigmoid(logit * inv_ref[...] + bf_ref[...])) + carry_ref[...]
            carry_ref[...] = c[c.shape[0] - 1:, :]
            cumt_ref[...] = (c * LOG2_E).T[:gate.heads, :]

    xb = xb_ref[...]
    for p, w_ref, o_ref in zip(projs, w_refs, o_refs):
        if p.weight_is_t and p.transposed:
            y = lax.dot_general(w_ref[...], xb, nt, preferred_element_type=F32)
        elif p.weight_is_t:
            y = lax.dot_general(xb, w_ref[...], nt, preferred_element_type=F32)
        else:
            y = jnp.dot(xb, w_ref[...], preferred_element_type=F32)
            y = y.T if p.transposed else y
        if p.transposed:
            y = y * invt_ref[0:1, :]
        else:
            y = y * jnp.concatenate([inv_ref[...]] * (y.shape[1] // V7X_LANES), axis=1)
        o_ref[...] = p.act(y).astype(o_ref.dtype)


class Gate(NamedTuple):
    wf_t: jax.Array
    bias: jax.Array
    heads: int
    seq: int
    tiles_per_seq: int = 0


def fused_proj(x, projs, *, n, inv=None, gate=None, side=(), tm=512, tn=512, vmem_limit=VMEM_LIMIT,
               name="fused_proj"):
    t, d = x.shape
    prenormed = inv is not None
    nj = n // tn
    w_specs, out_specs, out_shapes = [], [], []
    for p in projs:
        if p.weight_is_t:
            w_specs.append(pl.BlockSpec((tn, d), lambda i, j, b0=p.first // tn: (j + b0, 0)))
        else:
            w_specs.append(pl.BlockSpec((d, tn), lambda i, j, b0=p.first // tn: (0, j + b0)))
        if p.transposed:
            out_specs.append(pl.BlockSpec((tn, tm), lambda i, j: (j, i)))
            out_shapes.append(jax.ShapeDtypeStruct((n, t), p.dtype))
        else:
            out_specs.append(pl.BlockSpec((tm, tn), lambda i, j: (i, j)))
            out_shapes.append(jax.ShapeDtypeStruct((t, n), p.dtype))
    operands = [x] + ([inv] if prenormed else []) + [p.weight for p in projs]
    in_specs = [pl.BlockSpec((tm, d), lambda i, j: (i, 0))]
    if prenormed:
        in_specs.append(pl.BlockSpec((tm, V7X_LANES), lambda i, j: (i, 0)))
    in_specs += w_specs
    scratch = [] if prenormed else [pltpu.VMEM((tm, d), BF16), pltpu.VMEM((tm, V7X_LANES), F32)]
    scratch.append(pltpu.VMEM((V7X_SUBLANES, tm), F32))
    kernel_projs = tuple(p._replace(weight=None) for p in projs)
    kernel_gate = None
    if gate is not None:
        tps = gate.seq // tm
        kernel_gate = gate._replace(wf_t=None, bias=None, tiles_per_seq=tps)
        operands += [gate.wf_t, gate.bias]
        in_specs += [pl.BlockSpec((V7X_LANES, d), lambda i, j: (0, 0)), pl.BlockSpec((1, V7X_LANES), lambda i, j: (0, 0))]
        out_specs.append(pl.BlockSpec((None, gate.heads, tm), lambda i, j: (i // tps, 0, i % tps)))
        out_shapes.append(jax.ShapeDtypeStruct((t // gate.seq, gate.heads, gate.seq), F32))
        scratch.append(pltpu.VMEM((1, V7X_LANES), F32))
    side_in, side_out, side_shapes, side_ops = _side_cast_specs(side, (t // tm) * nj, lambda i, j: i * nj + j)
    operands += side_ops
    in_specs += side_in
    out_specs += side_out
    out_shapes += side_shapes
    return pl.pallas_call(
        functools.partial(_fused_proj_kernel, projs=kernel_projs, gate=kernel_gate, nside=len(side),
                          prenormed=prenormed),
        grid=(t // tm, nj),
        in_specs=in_specs,
        out_specs=out_specs,
        out_shape=out_shapes,
        scratch_shapes=scratch,
        compiler_params=pltpu.CompilerParams(dimension_semantics=("arbitrary", "arbitrary"),
                                             vmem_limit_bytes=vmem_limit),
        name=name,
    )(*operands)


def _out_proj_kernel(a_ref, w_ref, res_ref, gain_ref, o_ref):
    y = jnp.dot(a_ref[...], w_ref[...], preferred_element_type=F32)
    o_ref[...] = res_ref[...] + _rms_scale(y, gain_ref[...])


def out_proj(a, w, res, gain, *, tm=512, name="out_proj"):
    t, k = a.shape
    d = w.shape[1]
    return pl.pallas_call(
        _out_proj_kernel,
        grid=(t // tm,),
        in_specs=[
            pl.BlockSpec((tm, k), lambda i: (i, 0)),
            pl.BlockSpec((k, d), lambda i: (0, 0), pipeline_mode=pl.Buffered(1)),
            pl.BlockSpec((tm, d), lambda i: (i, 0)),
            pl.BlockSpec((1, d), lambda i: (0, 0)),
        ],
        out_specs=pl.BlockSpec((tm, d), lambda i: (i, 0)),
        out_shape=jax.ShapeDtypeStruct((t, d), F32),
        compiler_params=_params("parallel"),
        name=name,
    )(a, w, res, gain)


def _s5_tail_kernel(yg_ref, wg_ref, b_ref, sz_ref, wo_ref, res_ref, gain_ref, o_ref, ob_ref, inv_ref, *, halves):
    rows = yg_ref.shape[0] // halves
    for h in range(halves):
        r = slice(h * rows, (h + 1) * rows)
        yg = yg_ref[r, :]
        gate = jax.nn.sigmoid(jnp.dot(yg.astype(BF16), wg_ref[...], preferred_element_type=F32) + b_ref[...])
        y3 = (yg * gate * sz_ref[r, :].astype(F32)).astype(BF16)
        y = jnp.dot(y3, wo_ref[...], preferred_element_type=F32)
        out = res_ref[r, :] + _rms_scale(y, gain_ref[...])
        o_ref[r, :] = out
        ob_ref[r, :] = out.astype(BF16)
        inv_ref[r, :] = _inv_rms(out)


def s5_tail(yg, w_glu, b_glu, sz, w_out, res, gain, *, tm=512, halves=2, name="s5_tail"):
    t, k = yg.shape
    d = w_out.shape[1]
    rows = pl.BlockSpec((tm, k), lambda i: (i, 0))
    resident = functools.partial(pl.BlockSpec, index_map=lambda i: (0, 0), pipeline_mode=pl.Buffered(1))
    return pl.pallas_call(
        functools.partial(_s5_tail_kernel, halves=halves),
        grid=(t // tm,),
        in_specs=[rows, resident((k, k)), resident((1, k)), rows, resident((k, d)),
                  pl.BlockSpec((tm, d), lambda i: (i, 0)), resident((1, d))],
        out_specs=[pl.BlockSpec((tm, d), lambda i: (i, 0)), pl.BlockSpec((tm, d), lambda i: (i, 0)),
                   pl.BlockSpec((tm, V7X_LANES), lambda i: (i, 0))],
        out_shape=[jax.ShapeDtypeStruct((t, d), F32), jax.ShapeDtypeStruct((t, d), BF16),
                   jax.ShapeDtypeStruct((t, V7X_LANES), F32)],
        compiler_params=pltpu.CompilerParams(dimension_semantics=("parallel",),
                                             vmem_limit_bytes=LARGE_VMEM_LIMIT),
        name=name,
    )(yg, w_glu, b_glu, sz, w_out, res, gain)


def _ssm_prep_kernel(are_ref, aim_ref, ldt_ref, b2_ref, c2_ref, rep_ref, repb_ref, *rest, groups, scan_steps):
    nside = (len(rest) - 5) // 3
    r_ref, g_ref, h_ref, sr_ref, si_ref = rest[2 * nside:2 * nside + 5]
    _do_side_casts(rest[:2 * nside], rest[2 * nside + 5:])
    lane = lax.broadcasted_iota(jnp.int32, (1, 2 * S5_STATE), 1)
    minus_plus = jnp.where(lane < S5_STATE, -1.0, 1.0).astype(F32)
    gw = S5_GROUP
    ns = 2 * S5_STATE
    krows = []

    ar = are_ref[...]
    ai = aim_ref[...]
    dt = jnp.exp(ldt_ref[...])
    mag = jnp.exp(ar * dt)
    lam_r = mag * jnp.cos(ai * dt)
    lam_i = mag * jnp.sin(ai * dt)
    den = ar * ar + ai * ai
    nr = lam_r - 1.0
    coef_r_all = (nr * ar + lam_i * ai) / den
    coef_i_all = (lam_i * ar - nr * ai) / den
    pow_r_all = [jnp.ones_like(lam_r)]
    pow_i_all = [jnp.zeros_like(lam_r)]
    for _ in range(SSM_CHUNK):
        pr, pi = pow_r_all[-1], pow_i_all[-1]
        pow_r_all.append(pr * lam_r - pi * lam_i)
        pow_i_all.append(pr * lam_i + pi * lam_r)
    mu_r, mu_i = pow_r_all[SSM_CHUNK], pow_i_all[SSM_CHUNK]
    scan_r_all, scan_i_all = [], []
    for _ in range(scan_steps):
        scan_r_all.append(mu_r)
        scan_i_all.append(mu_i)
        mu_r, mu_i = mu_r * mu_r - mu_i * mu_i, 2.0 * mu_r * mu_i

    for gi in range(groups):
        grp = slice(gi, gi + 1)
        coef_r, coef_i = coef_r_all[grp], coef_i_all[grp]
        pow_r = [p[grp] for p in pow_r_all]
        pow_i = [p[grp] for p in pow_i_all]

        b2 = b2_ref[gi]
        b2s = pltpu.roll(b2, S5_STATE, axis=1) * minus_plus
        c2 = c2_ref[gi]
        c2a = c2 * (-minus_plus)
        c2b = -pltpu.roll(c2, S5_STATE, axis=1)

        pair, second = divmod(gi, 2)
        low = lane < S5_STATE

        for step in range(SSM_CHUNK):
            pr, pi = pow_r[SSM_CHUNK - 1 - step], pow_i[SSM_CHUNK - 1 - step]
            wr = pr * coef_r - pi * coef_i
            wi = pr * coef_i + pi * coef_r
            g_ri = wr * b2 + wi * b2s
            g_ir = pltpu.roll(g_ri, S5_STATE, axis=1)
            if second:
                re_cols, im_cols = jnp.where(low, 0.0, g_ir), jnp.where(low, 0.0, g_ri)
            else:
                re_cols, im_cols = jnp.where(low, g_ri, 0.0), jnp.where(low, g_ir, 0.0)
            g_ref[step, gi * gw:(gi + 1) * gw, :] = jnp.concatenate([re_cols, im_cols], axis=1).astype(g_ref.dtype)

        cl = [pow_r[tau] * c2a + pow_i[tau] * c2b for tau in range(SSM_CHUNK + 1)]
        h_t = jnp.concatenate(cl[1:], axis=0)
        h_rows = h_t.T.astype(h_ref.dtype)
        base = pair * 2 * ns + second * S5_STATE
        h_ref[base:base + S5_STATE, :] = h_rows[:S5_STATE]
        h_ref[base + ns:base + ns + S5_STATE, :] = h_rows[S5_STATE:]

        bbar2 = coef_r * b2 + coef_i * b2s
        cl_all = jnp.concatenate(cl[:-1], axis=0)
        krows.append(lax.dot_general(bbar2, cl_all, (((1,), (1,)), ((), ())),
                                     precision=lax.Precision.HIGHEST,
                                     preferred_element_type=F32))

        if second:
            first_grp = slice(gi - 1, gi)
            pad = [jnp.zeros_like(coef_r)] * (sr_ref.shape[1] - scan_steps)
            sr_ref[pair] = jnp.concatenate([jnp.where(low, s[first_grp], s[grp]) for s in scan_r_all] + pad, axis=0)
            si_ref[pair] = jnp.concatenate([jnp.where(low, s[first_grp], s[grp]) for s in scan_i_all] + pad, axis=0)

    for q in range(QUADS_PER_SLAB):
        kst = jnp.concatenate(krows[q * GROUPS_PER_QUAD:(q + 1) * GROUPS_PER_QUAD], axis=0).astype(BF16)
        units = []
        for rep in (rep_ref, repb_ref):
            spread = jnp.dot(kst, rep[...], preferred_element_type=F32)
            rows_g = lax.broadcasted_iota(jnp.int32, spread.shape, 0) // gw
            cols_g = (lax.broadcasted_iota(jnp.int32, spread.shape, 1) % QUAD_LANES) // gw
            units.append(jnp.where(rows_g == cols_g, spread, 0.0).astype(r_ref.dtype))
        blocks = [jnp.concatenate([s[:, e * V7X_LANES:(e + 1) * V7X_LANES] for s in units], axis=0)
                  for e in range(SSM_CHUNK // 2)]
        zero = jnp.zeros_like(blocks[0])
        ntile = SSM_CHUNK // STEPS_PER_TILE
        for d in range(ntile):
            base = (ntile - 1 - d) * 2 * V7X_LANES
            top = jnp.concatenate([blocks[2 * d], blocks[2 * d + 1]], axis=1)
            bottom = jnp.concatenate([blocks[2 * d - 1] if d else zero, blocks[2 * d]], axis=1)
            r_ref[q, base:base + V7X_LANES, :] = top
            r_ref[q, base + V7X_LANES:base + 2 * V7X_LANES, :] = bottom


def ssm_prep(a_re2, a_im2, log_dt2, b2, c2, rep, rep_shifted, *, scan_steps, side=(), name="ssm_prep"):
    ng = a_re2.shape[0]
    gb = GROUPS_PER_SLAB
    nslab = ng // gb
    cat = SSM_CHUNK * QUAD_LANES
    ns = 2 * S5_STATE
    rows = -(-scan_steps // V7X_SUBLANES) * V7X_SUBLANES
    vec = pl.BlockSpec((gb, ns), lambda i: (i, 0))
    mat = pl.BlockSpec((gb, S5_GROUP, ns), lambda i: (i, 0, 0))
    side_in, side_out, side_shapes, side_ops = _side_cast_specs(side, nslab, lambda i: i)
    return pl.pallas_call(
        functools.partial(_ssm_prep_kernel, groups=gb, scan_steps=scan_steps),
        grid=(nslab,),
        in_specs=[vec, vec, vec, mat, mat, pl.BlockSpec(rep.shape, lambda i: (0, 0)),
                  pl.BlockSpec(rep.shape, lambda i: (0, 0))] + side_in,
        out_specs=[
            pl.BlockSpec((None, QUADS_PER_SLAB, cat, 2 * V7X_LANES), lambda i: (i, 0, 0, 0)),
            pl.BlockSpec((None, SSM_CHUNK, V7X_LANES, 2 * ns), lambda i: (i, 0, 0, 0)),
            pl.BlockSpec((None, gb * ns, SSM_CHUNK * S5_GROUP), lambda i: (i, 0, 0)),
            pl.BlockSpec((gb // 2, rows, ns), lambda i: (i, 0, 0)),
            pl.BlockSpec((gb // 2, rows, ns), lambda i: (i, 0, 0)),
        ] + side_out,
        out_shape=[
            jax.ShapeDtypeStruct((nslab, QUADS_PER_SLAB, cat, 2 * V7X_LANES), BF16),
            jax.ShapeDtypeStruct((nslab, SSM_CHUNK, V7X_LANES, 2 * ns), BF16),
            jax.ShapeDtypeStruct((nslab, gb * ns, SSM_CHUNK * S5_GROUP), BF16),
            jax.ShapeDtypeStruct((ng // 2, rows, ns), F32),
            jax.ShapeDtypeStruct((ng // 2, rows, ns), F32),
        ] + side_shapes,
        compiler_params=_params("parallel"),
        name=name,
    )(a_re2, a_im2, log_dt2, b2, c2, rep, rep_shifted, *side_ops)


def replication_matrix(shift=0):
    src = jnp.arange(SSM_CHUNK * S5_GROUP)
    dst = jnp.arange(SSM_CHUNK * QUAD_LANES)
    same_step = (src[:, None] // S5_GROUP + shift) == (dst[None, :] // QUAD_LANES)
    same_chan = (src[:, None] % S5_GROUP) == (dst[None, :] % S5_GROUP)
    return (same_step & same_chan).astype(BF16)


def _ssm_kernel(u_ref, r_ref, gst_ref, hst_ref, sr_ref, si_ref, d_ref, rep_ref, *rest,
                chunks_per_seq, scan_steps, nside):
    y_ref = rest[2 * nside]
    _do_side_casts(rest[:2 * nside], rest[2 * nside + 1:3 * nside + 1])
    ucat_ref, gexp_ref, hexp_ref, hprev_ref, yq_ref = rest[3 * nside + 1:]
    t = u_ref.shape[0]
    nchunk = t // SSM_CHUNK
    ns = 2 * S5_STATE
    tile = 2 * V7X_LANES
    ntile = SSM_CHUNK // STEPS_PER_TILE
    qs = GROUPS_PER_QUAD * ns
    first_half = lax.broadcasted_iota(jnp.int32, (nchunk, V7X_LANES), 1) < QUAD_LANES

    def swap_halves(x):
        return pltpu.roll(x, QUAD_LANES, axis=1)

    for pair in range(SSM_CHUNK // 2):
        even = u_ref[pl.ds(2 * pair, nchunk, stride=SSM_CHUNK), :]
        odd = u_ref[pl.ds(2 * pair + 1, nchunk, stride=SSM_CHUNK), :]
        lanes = slice(pair * V7X_LANES, (pair + 1) * V7X_LANES)
        ucat_ref[0, :, lanes] = jnp.where(first_half, even, swap_halves(odd)).astype(BF16)
        ucat_ref[1, :, lanes] = jnp.where(first_half, swap_halves(even), odd).astype(BF16)

    pair_cols = 2 * ns
    pairs = GROUPS_PER_QUAD // 2
    g_rows = lax.broadcasted_iota(jnp.int32, (QUAD_LANES, qs), 0) // (2 * S5_GROUP)
    g_cols = lax.broadcasted_iota(jnp.int32, (QUAD_LANES, qs), 1) // pair_cols
    h_row = lax.broadcasted_iota(jnp.int32, (qs, tile), 0)
    h_rows = (h_row // pair_cols) * 2 + (h_row % ns) // S5_STATE
    h_cols = (lax.broadcasted_iota(jnp.int32, (qs, tile), 1) % QUAD_LANES) // S5_GROUP
    for q in range(QUADS_PER_SLAB):
        for step in range(SSM_CHUNK):
            compact = gst_ref[step, q * QUAD_LANES:(q + 1) * QUAD_LANES, :]
            tiled = jnp.concatenate([compact] * pairs, axis=1)
            gexp_ref[q, step * QUAD_LANES:(step + 1) * QUAD_LANES, :] = jnp.where(
                g_rows == g_cols, tiled, 0.0).astype(BF16)
        for a in range(ntile):
            spread = jnp.dot(hst_ref[q * qs:(q + 1) * qs, :], rep_ref[:, a * tile:(a + 1) * tile],
                             preferred_element_type=F32)
            hexp_ref[q, :, a * tile:(a + 1) * tile] = jnp.where(h_rows == h_cols, spread, 0.0).astype(BF16)

    hs_quads = [jnp.dot(ucat_ref[q], gexp_ref[q], preferred_element_type=F32) for q in range(QUADS_PER_SLAB)]
    for q in range(QUADS_PER_SLAB):
        for a in range(ntile):
            yq_ref[q, :, a * tile:(a + 1) * tile] = jnp.dot(
                ucat_ref[q, :, :(a + 1) * tile], r_ref[q, (ntile - 1 - a) * tile:, :], preferred_element_type=F32)

    row = lax.broadcasted_iota(jnp.int32, (nchunk, ns), 0) % chunks_per_seq

    def shifted(x, d):
        return jnp.where(row >= d, pltpu.roll(x, d, axis=0), 0.0)

    for p in range(GROUPS_PER_SLAB // 2):
        q, p2 = divmod(p, pairs)
        re = hs_quads[q][:, p2 * pair_cols:p2 * pair_cols + ns]
        im = hs_quads[q][:, p2 * pair_cols + ns:(p2 + 1) * pair_cols]
        for k in range(scan_steps):
            mr, mi = sr_ref[p, k:k + 1, :], si_ref[p, k:k + 1, :]
            d = 1 << k
            if d % V7X_SUBLANES:
                sre, sim = shifted(re, d), shifted(im, d)
                re, im = re + mr * sre - mi * sim, im + mr * sim + mi * sre
                continue
            re_parts, im_parts = [], []
            for s0 in range(0, nchunk, chunks_per_seq):
                r_seq, i_seq = re[s0:s0 + chunks_per_seq], im[s0:s0 + chunks_per_seq]
                r_src, i_src = r_seq[:chunks_per_seq - d], i_seq[:chunks_per_seq - d]
                re_parts += [r_seq[:d], r_seq[d:] + mr * r_src - mi * i_src]
                im_parts += [i_seq[:d], i_seq[d:] + mr * i_src + mi * r_src]
            re, im = jnp.concatenate(re_parts, axis=0), jnp.concatenate(im_parts, axis=0)
        hprev_ref[q, :, p2 * pair_cols:p2 * pair_cols + ns] = shifted(re, 1).astype(BF16)
        hprev_ref[q, :, p2 * pair_cols + ns:(p2 + 1) * pair_cols] = shifted(im, 1).astype(BF16)

    for q in range(QUADS_PER_SLAB):
        for a in range(ntile):
            lanes = slice(a * tile, (a + 1) * tile)
            y = (yq_ref[q, :, lanes]
                 + jnp.dot(hprev_ref[q], hexp_ref[q, :, lanes], preferred_element_type=F32)
                 + d_ref[q, :, lanes] * ucat_ref[q, :, lanes].astype(F32))
            yq_ref[q, :, lanes] = jax.nn.gelu(y, approximate=True)

    for pair in range(SSM_CHUNK // 2):
        lanes = slice(pair * V7X_LANES, (pair + 1) * V7X_LANES)
        y0, y1 = yq_ref[0, :, lanes], yq_ref[1, :, lanes]
        y_ref[pl.ds(2 * pair, nchunk, stride=SSM_CHUNK), :] = jnp.where(first_half, y0, swap_halves(y1))
        y_ref[pl.ds(2 * pair + 1, nchunk, stride=SSM_CHUNK), :] = jnp.where(first_half, swap_halves(y0), y1)


def ssm_apply(u, r_all, gst, hst, sr, si, d_quads, rep, *, seq, side=(), name="s5_ssm"):
    t, w = u.shape
    nslab = w // V7X_LANES
    side_in, side_out, side_shapes, side_ops = _side_cast_specs(side, nslab, lambda s: s)
    nchunk = t // SSM_CHUNK
    cat = SSM_CHUNK * QUAD_LANES
    ns = 2 * S5_STATE
    qs = GROUPS_PER_QUAD * ns
    nq = QUADS_PER_SLAB
    chunks_per_seq = seq // SSM_CHUNK
    scan_steps = (chunks_per_seq - 1).bit_length()
    gb = GROUPS_PER_SLAB
    rows = sr.shape[1]
    return pl.pallas_call(
        functools.partial(_ssm_kernel, chunks_per_seq=chunks_per_seq, scan_steps=scan_steps, nside=len(side)),
        grid=(nslab,),
        in_specs=[
            pl.BlockSpec((t, V7X_LANES), lambda s: (0, s)),
            pl.BlockSpec((None, nq, cat, 2 * V7X_LANES), lambda s: (s, 0, 0, 0)),
            pl.BlockSpec((None, SSM_CHUNK, V7X_LANES, 2 * ns), lambda s: (s, 0, 0, 0)),
            pl.BlockSpec((None, gb * ns, SSM_CHUNK * S5_GROUP), lambda s: (s, 0, 0)),
            pl.BlockSpec((gb // 2, rows, ns), lambda s: (s, 0, 0)),
            pl.BlockSpec((gb // 2, rows, ns), lambda s: (s, 0, 0)),
            pl.BlockSpec((None, nq, 1, cat), lambda s: (s, 0, 0, 0)),
            pl.BlockSpec((SSM_CHUNK * S5_GROUP, cat), lambda s: (0, 0)),
        ] + side_in,
        out_specs=[pl.BlockSpec((t, V7X_LANES), lambda s: (0, s))] + side_out,
        out_shape=[jax.ShapeDtypeStruct((t, w), F32)] + side_shapes,
        scratch_shapes=[pltpu.VMEM((nq, nchunk, cat), BF16), pltpu.VMEM((nq, cat, qs), BF16),
                        pltpu.VMEM((nq, qs, cat), BF16), pltpu.VMEM((nq, nchunk, qs), BF16),
                        pltpu.VMEM((nq, nchunk, cat), F32)],
        compiler_params=pltpu.CompilerParams(dimension_semantics=("parallel",),
                                             vmem_limit_bytes=LARGE_VMEM_LIMIT),
        name=name,
    )(u, r_all, gst, hst, sr, si, d_quads, rep, *side_ops)


ATTN_SPLIT = 3
ATTN_EXTRA_ROWS = 16


def _split3(c):
    hi = c.astype(BF16).astype(F32)
    mid = (c - hi).astype(BF16).astype(F32)
    return hi, mid, c - hi - mid


def _fox_attn_kernel(qt_ref, k_ref, vt_ref, ct_ref, sz_ref, o_ref,
                     qa_ref, ka_ref, va_ref, s0_ref, s1_ref, smax_ref, m_ref, acc_ref, *, tile, heads_per_step):
    seq = qt_ref.shape[1]
    dh = qt_ref.shape[0] // heads_per_step
    ntiles = seq // tile
    ext = ATTN_EXTRA_ROWS

    @pl.when((pl.program_id(0) == 0) & (pl.program_id(1) == 0))
    def _():
        rowv = lax.broadcasted_iota(jnp.int32, (ext, seq), 0)
        for hh in range(heads_per_step):
            qa_ref[hh, dh + ext:, :] = jnp.zeros((dh - ext, seq), BF16)
            va_ref[hh, dh:, :] = jnp.where(rowv == 0, 1.0, 0.0).astype(BF16)

    row16 = lax.broadcasted_iota(jnp.int32, (ext, seq), 0)
    rowk = lax.broadcasted_iota(jnp.int32, (dh, tile), 0)
    for hh in range(heads_per_step):
        feat = slice(hh * dh, (hh + 1) * dh)
        parts = _split3(ct_ref[hh])
        qa_ref[hh, :dh, :] = qt_ref[feat, :]
        ext_q = jnp.where(row16 < 2 * ATTN_SPLIT, 1.0, 0.0)
        for n, part in enumerate(parts):
            ext_q = jnp.where(row16 == n, part, ext_q)
        qa_ref[hh, dh:dh + ext, :] = ext_q.astype(BF16)
        va_ref[hh, :dh, :] = vt_ref[feat, :]
        for r in range(ntiles):
            cols = slice(r * tile, (r + 1) * tile)
            ext_kt = jnp.where(rowk < ATTN_SPLIT, 1.0, 0.0)
            for n, part in enumerate(parts):
                ext_kt = jnp.where(rowk == ATTN_SPLIT + n, -part[:, cols], ext_kt)
            ka_ref[hh, cols, :dh] = k_ref[cols, feat]
            ka_ref[hh, cols, dh:] = ext_kt.T.astype(BF16)

    sub = tile // 2
    causal = (lax.broadcasted_iota(jnp.int32, (sub, sub), 0) <= lax.broadcasted_iota(jnp.int32, (sub, sub), 1))
    s_slots = (s0_ref, s1_ref)

    def scores(hh, k0, nk, q0, nq):
        return jnp.dot(ka_ref[hh, k0:k0 + nk, :], qa_ref[hh, :, q0:q0 + nq], preferred_element_type=F32)

    def produce(slot, hh, qi, kb):
        k0, q0 = kb * tile, qi * tile
        if kb == qi:
            s_slots[slot][:sub, :] = scores(hh, k0, sub, q0, tile)
            s_slots[slot][sub:, sub:] = scores(hh, k0 + sub, sub, q0 + sub, sub)
        else:
            st = scores(hh, k0, tile, q0, tile)
            s_slots[slot][...] = st
            smax_ref[slot:slot + 1, :] = jnp.max(st, axis=0, keepdims=True)

    def update(hh, par, st, st_max, k0, c0, first):
        nk, nq = st.shape
        cols = slice(c0, c0 + nq)
        pv = functools.partial(jnp.dot, va_ref[hh, :, k0:k0 + nk], preferred_element_type=F32)
        if first:
            m_ref[par:par + 1, cols] = st_max
            acc_ref[par, :, cols] = pv(jnp.exp2(st - st_max).astype(BF16))
        else:
            m = m_ref[par:par + 1, cols]
            m_new = jnp.maximum(m, st_max)
            acc_ref[par, :, cols] = (jnp.exp2(m - m_new) * acc_ref[par, :, cols]
                                     + pv(jnp.exp2(st - m_new).astype(BF16)))
            m_ref[par:par + 1, cols] = m_new

    def consume(slot, hh, qi, kb):
        par, k0, s_ref = (hh * ntiles + qi) % 2, kb * tile, s_slots[slot]
        if kb != qi:
            update(hh, par, s_ref[...], smax_ref[slot:slot + 1, :], k0, 0, kb == 0)
            return
        early = jnp.where(causal, s_ref[:sub, :sub], NEG_INF)
        update(hh, par, early, jnp.max(early, axis=0, keepdims=True), k0, 0, kb == 0)
        late = jnp.concatenate([s_ref[:sub, sub:], jnp.where(causal, s_ref[sub:, sub:], NEG_INF)], axis=0)
        update(hh, par, late, jnp.max(late, axis=0, keepdims=True), k0, sub, kb == 0)

    def finalize(hh, qi):
        acc = acc_ref[(hh * ntiles + qi) % 2]
        rows, feat = slice(qi * tile, (qi + 1) * tile), slice(hh * dh, (hh + 1) * dh)
        o = (acc[:dh, :] / acc[dh:dh + 1, :]).T
        o_ref[rows, feat] = (o * sz_ref[rows, feat].astype(F32)).astype(o_ref.dtype)

    blocks = [(hh, qi, kb) for hh in range(heads_per_step) for qi in range(ntiles) for kb in range(qi + 1)]
    produce(0, *blocks[0])
    for n, (hh, qi, kb) in enumerate(blocks):
        if n + 1 < len(blocks):
            produce((n + 1) % 2, *blocks[n + 1])
        consume(n % 2, hh, qi, kb)
        if kb == qi:
            finalize(hh, qi)


def fox_attention(qt, k, vt, cumt2, sz, *, batch, seq, heads, tile=1024, heads_per_step=1, name="fox_attention"):
    width, t = qt.shape
    dh = width // heads
    hps = heads_per_step
    natural = pl.BlockSpec((seq, hps * dh), lambda b, h: (b, h))
    transposed = pl.BlockSpec((hps * dh, seq), lambda b, h: (h, b))
    return pl.pallas_call(
        functools.partial(_fox_attn_kernel, tile=tile, heads_per_step=hps),
        grid=(batch, heads // hps),
        in_specs=[transposed, natural, transposed,
                  pl.BlockSpec((None, hps, 1, seq), lambda b, h: (b, h, 0, 0)),
                  natural],
        out_specs=natural,
        out_shape=jax.ShapeDtypeStruct((t, width), BF16),
        scratch_shapes=[pltpu.VMEM((hps, 2 * dh, seq), BF16), pltpu.VMEM((hps, seq, 2 * dh), BF16),
                        pltpu.VMEM((hps, dh + ATTN_EXTRA_ROWS, seq), BF16),
                        pltpu.VMEM((tile, tile), F32), pltpu.VMEM((tile, tile), F32),
                        pltpu.VMEM((V7X_SUBLANES, tile), F32), pltpu.VMEM((V7X_SUBLANES, tile), F32),
                        pltpu.VMEM((2, dh + ATTN_EXTRA_ROWS, tile), F32)],
        compiler_params=_params("arbitrary", "arbitrary"),
        name=name,
    )(qt, k, vt, cumt2, sz)


def _identity(v):
    return v


def _doubled(v):
    return jnp.concatenate([v, v], axis=-1).astype(F32)


def kernel(x, norm_pre, norm_post, s5_w_in, s5_a_re, s5_a_im, s5_log_dt, s5_b_re, s5_b_im, s5_c_re, s5_c_im,
           s5_d, s5_w_glu, s5_b_glu, s5_w_out, kv_norm, kv_w, kv_b_f, fox_w_in, fox_w_out):
    batch, seq, d_model = x.shape
    t = batch * seq
    heads = kv_b_f.shape[0]
    fox_width = fox_w_in.shape[-1] // 2
    assert norm_pre.shape[0] == 2 and s5_w_in.shape[0] == 1 and fox_w_in.shape[0] == 1
    assert fox_width // heads == FOX_HEAD_DIM and heads <= V7X_LANES
    ngroups = s5_a_re.shape[1]
    chunks_per_seq = seq // SSM_CHUNK
    scan_steps = (chunks_per_seq - 1).bit_length()

    h0 = x.reshape(t, d_model)

    s5_width = s5_w_in.shape[-1] // 2
    ldt2 = jnp.broadcast_to(s5_log_dt[0][:, None], (ngroups, 2 * S5_STATE)).astype(F32)
    b2 = jnp.concatenate([jnp.swapaxes(s5_b_re[0], 1, 2), jnp.swapaxes(s5_b_im[0], 1, 2)], axis=-1)
    c2 = jnp.concatenate([s5_c_re[0], s5_c_im[0]], axis=-1)
    rep = replication_matrix()
    r_all, gst, hst, sr, si, w_in = ssm_prep(_doubled(s5_a_re[0]), _doubled(s5_a_im[0]), ldt2, b2.astype(F32),
                                             c2.astype(F32), rep, replication_matrix(shift=1),
                                             scan_steps=scan_steps,
                                             side=[side_cast(s5_w_in[0], row_scale=norm_pre[0])])
    d_quads = jnp.tile(s5_d[0].reshape(-1, QUADS_PER_SLAB, 1, QUAD_LANES), (1, 1, 1, SSM_CHUNK)).astype(F32)
    kv_cols = 2 * fox_width
    kv_w_t = kv_w.T
    u, sz, w_glu, w_out, w_kv_t = fused_proj(
        h0, [Proj(w_in, False, 0, _identity, F32, False), Proj(w_in, False, s5_width, jax.nn.silu, BF16, False)],
        n=s5_width, tm=1024, name="s5_in_proj",
        side=[side_cast(s5_w_glu[0]), side_cast(s5_w_out[0]), side_cast(kv_w_t, kv_cols, col_scale=kv_norm)])
    yg, w_in_fox, w_out_fox = ssm_apply(u, r_all, gst, hst, sr, si, d_quads, rep, seq=seq,
                                        side=[side_cast(fox_w_in[0], row_scale=norm_pre[1]),
                                              side_cast(fox_w_out[0])])
    h1, h1_bf16, h1_inv_rms = s5_tail(yg, w_glu, s5_b_glu[0][None, :], sz, w_out, h0, norm_post[0][None, :])

    wf_t = jnp.pad((kv_w_t[kv_cols:] * kv_norm[None, :]).astype(BF16), ((0, V7X_LANES - heads), (0, 0)))
    bf = jnp.pad(kv_b_f, (0, V7X_LANES - heads))[None, :].astype(F32)
    scale = FOX_HEAD_DIM ** -0.5 * LOG2_E
    k, vt, qt, sz2, cumt2 = fused_proj(
        h1_bf16,
        [Proj(w_kv_t, True, 0, _identity, BF16, False),
         Proj(w_kv_t, True, fox_width, _identity, BF16, True),
         Proj(w_in_fox, False, 0, lambda v: v * scale, BF16, True),
         Proj(w_in_fox, False, fox_width, jax.nn.silu, BF16, False)],
        n=fox_width, inv=h1_inv_rms, gate=Gate(wf_t, bf, heads, seq), tm=1024, vmem_limit=LARGE_VMEM_LIMIT,
        name="fox_proj")
    o = fox_attention(qt, k, vt, cumt2.reshape(batch, heads, 1, seq), sz2,
                      batch=batch, seq=seq, heads=heads)
    h2 = out_proj(o, w_out_fox, h1, norm_post[1][None, :], name="fox_out_proj")
    return h2.reshape(batch, seq, d_model)
```

```python
import functools
import math
from typing import Callable, NamedTuple

import jax
import jax.numpy as jnp
from jax import lax
from jax.experimental import pallas as pl
from jax.experimental.pallas import tpu as pltpu

F32 = jnp.float32
BF16 = jnp.bfloat16

RMS_EPS = 1e-6
NEG_INF = -1e30
LOG2_E = math.log2(math.e)

S5_GROUP = 16
S5_STATE = 64
FOX_HEAD_DIM = 128

V7X_LANES = 128
V7X_SUBLANES = 8
SSM_CHUNK = 16
GROUPS_PER_SLAB = V7X_LANES // S5_GROUP
QUADS_PER_SLAB = 2
QUAD_LANES = V7X_LANES // QUADS_PER_SLAB
GROUPS_PER_QUAD = GROUPS_PER_SLAB // QUADS_PER_SLAB
STEPS_PER_TILE = 2 * V7X_LANES // QUAD_LANES
V7X_VMEM_BYTES = 64 * 1024 * 1024
VMEM_LIMIT = V7X_VMEM_BYTES * 3 // 4
LARGE_VMEM_LIMIT = V7X_VMEM_BYTES * 7 // 8


def _params(*semantics):
    return pltpu.CompilerParams(dimension_semantics=semantics, vmem_limit_bytes=VMEM_LIMIT)


def _rms_scale(x, gain):
    ms = jnp.mean(x * x, axis=-1, keepdims=True)
    return x * lax.rsqrt(ms + RMS_EPS) * gain


class Proj(NamedTuple):
    weight: jax.Array
    weight_is_t: bool
    first: int
    act: Callable
    dtype: jnp.dtype
    transposed: bool


def _log_sigmoid(x):
    return -(jnp.maximum(-x, 0.0) + jnp.log1p(jnp.exp(-jnp.abs(x))))


def _cumsum_rows(c):
    rows = c.shape[0]
    row = lax.broadcasted_iota(jnp.int32, c.shape, 0)
    d = 1
    while d < rows:
        c = c + jnp.where(row >= d, pltpu.roll(c, d, axis=0), 0.0)
        d *= 2
    return c


class SideCast(NamedTuple):
    src: jax.Array
    rows_used: int
    scale: jax.Array


def side_cast(src, rows_used=None, *, row_scale=None, col_scale=None):
    rows_used = src.shape[0] if rows_used is None else rows_used
    if row_scale is not None:
        scale = row_scale.reshape(rows_used, 1)
    elif col_scale is not None:
        scale = col_scale.reshape(1, src.shape[1])
    else:
        scale = jnp.ones((1, src.shape[1]), F32)
    return SideCast(src, rows_used, scale.astype(F32))


def _side_cast_specs(casts, steps, flat_step):
    in_specs, out_specs, out_shapes, operands = [], [], [], []
    for c in casts:
        block = (c.rows_used // steps, c.src.shape[1])
        by_step = lambda *idx: (flat_step(*idx), 0)
        in_specs.append(pl.BlockSpec(block, by_step))
        if c.scale.shape[0] == 1:
            in_specs.append(pl.BlockSpec(c.scale.shape, lambda *idx: (0, 0)))
        else:
            in_specs.append(pl.BlockSpec((block[0], 1), by_step))
        operands += [c.src, c.scale]
        out_specs.append(pl.BlockSpec(block, by_step))
        out_shapes.append(jax.ShapeDtypeStruct((c.rows_used, c.src.shape[1]), BF16))
    return in_specs, out_specs, out_shapes, operands


def _do_side_casts(in_refs, dst_refs):
    for n, dst_ref in enumerate(dst_refs):
        dst_ref[...] = (in_refs[2 * n][...] * in_refs[2 * n + 1][...]).astype(BF16)


def _inv_rms(x):
    inv = lax.rsqrt(jnp.mean(x * x, axis=-1, keepdims=True) + RMS_EPS)
    return jnp.broadcast_to(inv, (x.shape[0], V7X_LANES))


def _fused_proj_kernel(*refs, projs, gate, nside, prenormed):
    nproj = len(projs)
    x_ref = refs[0]
    pos = 1
    if prenormed:
        inv_in_ref = refs[1]
        pos = 2
    w_refs = refs[pos:pos + nproj]
    pos += nproj
    if gate is not None:
        wf_ref, bf_ref = refs[pos:pos + 2]
        pos += 2
    side_in = refs[pos:pos + 2 * nside]
    pos += 2 * nside
    o_refs = refs[pos:pos + nproj]
    pos += nproj
    if gate is not None:
        cumt_ref = refs[pos]
        pos += 1
    _do_side_casts(side_in, refs[pos:pos + nside])
    pos += nside
    if prenormed:
        xb_ref, inv_ref = x_ref, inv_in_ref
    else:
        xb_ref, inv_ref = refs[pos:pos + 2]
        pos += 2
    invt_ref = refs[pos]
    pos += 1
    nt = (((1,), (1,)), ((), ()))
    any_transposed = any(p.transposed for p in projs)

    @pl.when(pl.program_id(1) == 0)
    def _():
        if not prenormed:
            x = x_ref[...]
            xb_ref[...] = x.astype(BF16)
            inv_ref[...] = _inv_rms(x)
        if any_transposed:
            invt_ref[...] = inv_ref[...].T[:V7X_SUBLANES, :]
        if gate is not None:
            carry_ref = refs[pos]

            @pl.when(pl.program_id(0) % gate.tiles_per_seq == 0)
            def _():
                carry_ref[...] = jnp.zeros_like(carry_ref)

            logit = lax.dot_general(xb_ref[...], wf_ref[...], nt, preferred_element_type=F32)
            c = _cumsum_rows(_log_sigmoid(logit * inv_ref[...] + bf_ref[...])) + carry_ref[...]
            carry_ref[...] = c[c.shape[0] - 1:, :]
            cumt_ref[...] = (c * LOG2_E).T[:gate.heads, :]

    xb = xb_ref[...]
    for p, w_ref, o_ref in zip(projs, w_refs, o_refs):
        if p.weight_is_t and p.transposed:
            y = lax.dot_general(w_ref[...], xb, nt, preferred_element_type=F32)
        elif p.weight_is_t:
            y = lax.dot_general(xb, w_ref[...], nt, preferred_element_type=F32)
        else:
            y = jnp.dot(xb, w_ref[...], preferred_element_type=F32)
            y = y.T if p.transposed else y
        if p.transposed:
            y = y * invt_ref[0:1, :]
        else:
            y = y * jnp.concatenate([inv_ref[...]] * (y.shape[1] // V7X_LANES), axis=1)
        o_ref[...] = p.act(y).astype(o_ref.dtype)


class Gate(NamedTuple):
    wf_t: jax.Array
    bias: jax.Array
    heads: int
    seq: int
    tiles_per_seq: int = 0


def fused_proj(x, projs, *, n, inv=None, gate=None, side=(), tm=512, tn=512, vmem_limit=VMEM_LIMIT,
               name="fused_proj"):
    t, d = x.shape
    prenormed = inv is not None
    nj = n // tn
    w_specs, out_specs, out_shapes = [], [], []
    for p in projs:
        if p.weight_is_t:
            w_specs.append(pl.BlockSpec((tn, d), lambda i, j, b0=p.first // tn: (j + b0, 0)))
        else:
            w_specs.append(pl.BlockSpec((d, tn), lambda i, j, b0=p.first // tn: (0, j + b0)))
        if p.transposed:
            out_specs.append(pl.BlockSpec((tn, tm), lambda i, j: (j, i)))
            out_shapes.append(jax.ShapeDtypeStruct((n, t), p.dtype))
        else:
            out_specs.append(pl.BlockSpec((tm, tn), lambda i, j: (i, j)))
            out_shapes.append(jax.ShapeDtypeStruct((t, n), p.dtype))
    operands = [x] + ([inv] if prenormed else []) + [p.weight for p in projs]
    in_specs = [pl.BlockSpec((tm, d), lambda i, j: (i, 0))]
    if prenormed:
        in_specs.append(pl.BlockSpec((tm, V7X_LANES), lambda i, j: (i, 0)))
    in_specs += w_specs
    scratch = [] if prenormed else [pltpu.VMEM((tm, d), BF16), pltpu.VMEM((tm, V7X_LANES), F32)]
    scratch.append(pltpu.VMEM((V7X_SUBLANES, tm), F32))
    kernel_projs = tuple(p._replace(weight=None) for p in projs)
    kernel_gate = None
    if gate is not None:
        tps = gate.seq // tm
        kernel_gate = gate._replace(wf_t=None, bias=None, tiles_per_seq=tps)
        operands += [gate.wf_t, gate.bias]
        in_specs += [pl.BlockSpec((V7X_LANES, d), lambda i, j: (0, 0)), pl.BlockSpec((1, V7X_LANES), lambda i, j: (0, 0))]
        out_specs.append(pl.BlockSpec((None, gate.heads, tm), lambda i, j: (i // tps, 0, i % tps)))
        out_shapes.append(jax.ShapeDtypeStruct((t // gate.seq, gate.heads, gate.seq), F32))
        scratch.append(pltpu.VMEM((1, V7X_LANES), F32))
    side_in, side_out, side_shapes, side_ops = _side_cast_specs(side, (t // tm) * nj, lambda i, j: i * nj + j)
    operands += side_ops
    in_specs += side_in
    out_specs += side_out
    out_shapes += side_shapes
    return pl.pallas_call(
        functools.partial(_fused_proj_kernel, projs=kernel_projs, gate=kernel_gate, nside=len(side),
                          prenormed=prenormed),
        grid=(t // tm, nj),
        in_specs=in_specs,
        out_specs=out_specs,
        out_shape=out_shapes,
        scratch_shapes=scratch,
        compiler_params=pltpu.CompilerParams(dimension_semantics=("arbitrary", "arbitrary"),
                                             vmem_limit_bytes=vmem_limit),
        name=name,
    )(*operands)


def _out_proj_kernel(a_ref, w_ref, res_ref, gain_ref, o_ref):
    y = jnp.dot(a_ref[...], w_ref[...], preferred_element_type=F32)
    o_ref[...] = res_ref[...] + _rms_scale(y, gain_ref[...])


def out_proj(a, w, res, gain, *, tm=512, name="out_proj"):
    t, k = a.shape
    d = w.shape[1]
    return pl.pallas_call(
        _out_proj_kernel,
        grid=(t // tm,),
        in_specs=[
            pl.BlockSpec((tm, k), lambda i: (i, 0)),
            pl.BlockSpec((k, d), lambda i: (0, 0), pipeline_mode=pl.Buffered(1)),
            pl.BlockSpec((tm, d), lambda i: (i, 0)),
            pl.BlockSpec((1, d), lambda i: (0, 0)),
        ],
        out_specs=pl.BlockSpec((tm, d), lambda i: (i, 0)),
        out_shape=jax.ShapeDtypeStruct((t, d), F32),
        compiler_params=_params("parallel"),
        name=name,
    )(a, w, res, gain)


def _s5_tail_kernel(yg_ref, wg_ref, b_ref, sz_ref, wo_ref, res_ref, gain_ref, o_ref, ob_ref, inv_ref, *, halves):
    rows = yg_ref.shape[0] // halves
    for h in range(halves):
        r = slice(h * rows, (h + 1) * rows)
        yg = yg_ref[r, :]
        gate = jax.nn.sigmoid(jnp.dot(yg.astype(BF16), wg_ref[...], preferred_element_type=F32) + b_ref[...])
        y3 = (yg * gate * sz_ref[r, :].astype(F32)).astype(BF16)
        y = jnp.dot(y3, wo_ref[...], preferred_element_type=F32)
        out = res_ref[r, :] + _rms_scale(y, gain_ref[...])
        o_ref[r, :] = out
        ob_ref[r, :] = out.astype(BF16)
        inv_ref[r, :] = _inv_rms(out)


def s5_tail(yg, w_glu, b_glu, sz, w_out, res, gain, *, tm=512, halves=2, name="s5_tail"):
    t, k = yg.shape
    d = w_out.shape[1]
    rows = pl.BlockSpec((tm, k), lambda i: (i, 0))
    resident = functools.partial(pl.BlockSpec, index_map=lambda i: (0, 0), pipeline_mode=pl.Buffered(1))
    return pl.pallas_call(
        functools.partial(_s5_tail_kernel, halves=halves),
        grid=(t // tm,),
        in_specs=[rows, resident((k, k)), resident((1, k)), rows, resident((k, d)),
                  pl.BlockSpec((tm, d), lambda i: (i, 0)), resident((1, d))],
        out_specs=[pl.BlockSpec((tm, d), lambda i: (i, 0)), pl.BlockSpec((tm, d), lambda i: (i, 0)),
                   pl.BlockSpec((tm, V7X_LANES), lambda i: (i, 0))],
        out_shape=[jax.ShapeDtypeStruct((t, d), F32), jax.ShapeDtypeStruct((t, d), BF16),
                   jax.ShapeDtypeStruct((t, V7X_LANES), F32)],
        compiler_params=pltpu.CompilerParams(dimension_semantics=("parallel",),
                                             vmem_limit_bytes=LARGE_VMEM_LIMIT),
        name=name,
    )(yg, w_glu, b_glu, sz, w_out, res, gain)


def _ssm_prep_kernel(are_ref, aim_ref, ldt_ref, b2_ref, c2_ref, rep_ref, repb_ref, *rest, groups, scan_steps):
    nside = (len(rest) - 5) // 3
    r_ref, g_ref, h_ref, sr_ref, si_ref = rest[2 * nside:2 * nside + 5]
    _do_side_casts(rest[:2 * nside], rest[2 * nside + 5:])
    lane = lax.broadcasted_iota(jnp.int32, (1, 2 * S5_STATE), 1)
    minus_plus = jnp.where(lane < S5_STATE, -1.0, 1.0).astype(F32)
    gw = S5_GROUP
    ns = 2 * S5_STATE
    krows = []

    ar = are_ref[...]
    ai = aim_ref[...]
    dt = jnp.exp(ldt_ref[...])
    mag = jnp.exp(ar * dt)
    lam_r = mag * jnp.cos(ai * dt)
    lam_i = mag * jnp.sin(ai * dt)
    den = ar * ar + ai * ai
    nr = lam_r - 1.0
    coef_r_all = (nr * ar + lam_i * ai) / den
    coef_i_all = (lam_i * ar - nr * ai) / den
    pow_r_all = [jnp.ones_like(lam_r)]
    pow_i_all = [jnp.zeros_like(lam_r)]
    for _ in range(SSM_CHUNK):
        pr, pi = pow_r_all[-1], pow_i_all[-1]
        pow_r_all.append(pr * lam_r - pi * lam_i)
        pow_i_all.append(pr * lam_i + pi * lam_r)
    mu_r, mu_i = pow_r_all[SSM_CHUNK], pow_i_all[SSM_CHUNK]
    scan_r_all, scan_i_all = [], []
    for _ in range(scan_steps):
        scan_r_all.append(mu_r)
        scan_i_all.append(mu_i)
        mu_r, mu_i = mu_r * mu_r - mu_i * mu_i, 2.0 * mu_r * mu_i

    for gi in range(groups):
        grp = slice(gi, gi + 1)
        coef_r, coef_i = coef_r_all[grp], coef_i_all[grp]
        pow_r = [p[grp] for p in pow_r_all]
        pow_i = [p[grp] for p in pow_i_all]

        b2 = b2_ref[gi]
        b2s = pltpu.roll(b2, S5_STATE, axis=1) * minus_plus
        c2 = c2_ref[gi]
        c2a = c2 * (-minus_plus)
        c2b = -pltpu.roll(c2, S5_STATE, axis=1)

        pair, second = divmod(gi, 2)
        low = lane < S5_STATE

        for step in range(SSM_CHUNK):
            pr, pi = pow_r[SSM_CHUNK - 1 - step], pow_i[SSM_CHUNK - 1 - step]
            wr = pr * coef_r - pi * coef_i
            wi = pr * coef_i + pi * coef_r
            g_ri = wr * b2 + wi * b2s
            g_ir = pltpu.roll(g_ri, S5_STATE, axis=1)
            if second:
                re_cols, im_cols = jnp.where(low, 0.0, g_ir), jnp.where(low, 0.0, g_ri)
            else:
                re_cols, im_cols = jnp.where(low, g_ri, 0.0), jnp.where(low, g_ir, 0.0)
            g_ref[step, gi * gw:(gi + 1) * gw, :] = jnp.concatenate([re_cols, im_cols], axis=1).astype(g_ref.dtype)

        cl = [pow_r[tau] * c2a + pow_i[tau] * c2b for tau in range(SSM_CHUNK + 1)]
        h_t = jnp.concatenate(cl[1:], axis=0)
        h_rows = h_t.T.astype(h_ref.dtype)
        base = pair * 2 * ns + second * S5_STATE
        h_ref[base:base + S5_STATE, :] = h_rows[:S5_STATE]
        h_ref[base + ns:base + ns + S5_STATE, :] = h_rows[S5_STATE:]

        bbar2 = coef_r * b2 + coef_i * b2s
        cl_all = jnp.concatenate(cl[:-1], axis=0)
        krows.append(lax.dot_general(bbar2, cl_all, (((1,), (1,)), ((), ())),
                                     precision=lax.Precision.HIGHEST,
                                     preferred_element_type=F32))

        if second:
            first_grp = slice(gi - 1, gi)
            pad = [jnp.zeros_like(coef_r)] * (sr_ref.shape[1] - scan_steps)
            sr_ref[pair] = jnp.concatenate([jnp.where(low, s[first_grp], s[grp]) for s in scan_r_all] + pad, axis=0)
            si_ref[pair] = jnp.concatenate([jnp.where(low, s[first_grp], s[grp]) for s in scan_i_all] + pad, axis=0)

    for q in range(QUADS_PER_SLAB):
        kst = jnp.concatenate(krows[q * GROUPS_PER_QUAD:(q + 1) * GROUPS_PER_QUAD], axis=0).astype(BF16)
        units = []
        for rep in (rep_ref, repb_ref):
            spread = jnp.dot(kst, rep[...], preferred_element_type=F32)
            rows_g = lax.broadcasted_iota(jnp.int32, spread.shape, 0) // gw
            cols_g = (lax.broadcasted_iota(jnp.int32, spread.shape, 1) % QUAD_LANES) // gw
            units.append(jnp.where(rows_g == cols_g, spread, 0.0).astype(r_ref.dtype))
        blocks = [jnp.concatenate([s[:, e * V7X_LANES:(e + 1) * V7X_LANES] for s in units], axis=0)
                  for e in range(SSM_CHUNK // 2)]
        zero = jnp.zeros_like(blocks[0])
        ntile = SSM_CHUNK // STEPS_PER_TILE
        for d in range(ntile):
            base = (ntile - 1 - d) * 2 * V7X_LANES
            top = jnp.concatenate([blocks[2 * d], blocks[2 * d + 1]], axis=1)
            bottom = jnp.concatenate([blocks[2 * d - 1] if d else zero, blocks[2 * d]], axis=1)
            r_ref[q, base:base + V7X_LANES, :] = top
            r_ref[q, base + V7X_LANES:base + 2 * V7X_LANES, :] = bottom


def ssm_prep(a_re2, a_im2, log_dt2, b2, c2, rep, rep_shifted, *, scan_steps, side=(), name="ssm_prep"):
    ng = a_re2.shape[0]
    gb = GROUPS_PER_SLAB
    nslab = ng // gb
    cat = SSM_CHUNK * QUAD_LANES
    ns = 2 * S5_STATE
    rows = -(-scan_steps // V7X_SUBLANES) * V7X_SUBLANES
    vec = pl.BlockSpec((gb, ns), lambda i: (i, 0))
    mat = pl.BlockSpec((gb, S5_GROUP, ns), lambda i: (i, 0, 0))
    side_in, side_out, side_shapes, side_ops = _side_cast_specs(side, nslab, lambda i: i)
    return pl.pallas_call(
        functools.partial(_ssm_prep_kernel, groups=gb, scan_steps=scan_steps),
        grid=(nslab,),
        in_specs=[vec, vec, vec, mat, mat, pl.BlockSpec(rep.shape, lambda i: (0, 0)),
                  pl.BlockSpec(rep.shape, lambda i: (0, 0))] + side_in,
        out_specs=[
            pl.BlockSpec((None, QUADS_PER_SLAB, cat, 2 * V7X_LANES), lambda i: (i, 0, 0, 0)),
            pl.BlockSpec((None, SSM_CHUNK, V7X_LANES, 2 * ns), lambda i: (i, 0, 0, 0)),
            pl.BlockSpec((None, gb * ns, SSM_CHUNK * S5_GROUP), lambda i: (i, 0, 0)),
            pl.BlockSpec((gb // 2, rows, ns), lambda i: (i, 0, 0)),
            pl.BlockSpec((gb // 2, rows, ns), lambda i: (i, 0, 0)),
        ] + side_out,
        out_shape=[
            jax.ShapeDtypeStruct((nslab, QUADS_PER_SLAB, cat, 2 * V7X_LANES), BF16),
            jax.ShapeDtypeStruct((nslab, SSM_CHUNK, V7X_LANES, 2 * ns), BF16),
            jax.ShapeDtypeStruct((nslab, gb * ns, SSM_CHUNK * S5_GROUP), BF16),
            jax.ShapeDtypeStruct((ng // 2, rows, ns), F32),
            jax.ShapeDtypeStruct((ng // 2, rows, ns), F32),
        ] + side_shapes,
        compiler_params=_params("parallel"),
        name=name,
    )(a_re2, a_im2, log_dt2, b2, c2, rep, rep_shifted, *side_ops)


def replication_matrix(shift=0):
    src = jnp.arange(SSM_CHUNK * S5_GROUP)
    dst = jnp.arange(SSM_CHUNK * QUAD_LANES)
    same_step = (src[:, None] // S5_GROUP + shift) == (dst[None, :] // QUAD_LANES)
    same_chan = (src[:, None] % S5_GROUP) == (dst[None, :] % S5_GROUP)
    return (same_step & same_chan).astype(BF16)


def _ssm_kernel(u_ref, r_ref, gst_ref, hst_ref, sr_ref, si_ref, d_ref, rep_ref, *rest,
                chunks_per_seq, scan_steps, nside):
    y_ref = rest[2 * nside]
    _do_side_casts(rest[:2 * nside], rest[2 * nside + 1:3 * nside + 1])
    ucat_ref, gexp_ref, hexp_ref, hprev_ref, yq_ref = rest[3 * nside + 1:]
    t = u_ref.shape[0]
    nchunk = t // SSM_CHUNK
    ns = 2 * S5_STATE
    tile = 2 * V7X_LANES
    ntile = SSM_CHUNK // STEPS_PER_TILE
    qs = GROUPS_PER_QUAD * ns
    first_half = lax.broadcasted_iota(jnp.int32, (nchunk, V7X_LANES), 1) < QUAD_LANES

    def swap_halves(x):
        return pltpu.roll(x, QUAD_LANES, axis=1)

    for pair in range(SSM_CHUNK // 2):
        even = u_ref[pl.ds(2 * pair, nchunk, stride=SSM_CHUNK), :]
        odd = u_ref[pl.ds(2 * pair + 1, nchunk, stride=SSM_CHUNK), :]
        lanes = slice(pair * V7X_LANES, (pair + 1) * V7X_LANES)
        ucat_ref[0, :, lanes] = jnp.where(first_half, even, swap_halves(odd)).astype(BF16)
        ucat_ref[1, :, lanes] = jnp.where(first_half, swap_halves(even), odd).astype(BF16)

    pair_cols = 2 * ns
    pairs = GROUPS_PER_QUAD // 2
    g_rows = lax.broadcasted_iota(jnp.int32, (QUAD_LANES, qs), 0) // (2 * S5_GROUP)
    g_cols = lax.broadcasted_iota(jnp.int32, (QUAD_LANES, qs), 1) // pair_cols
    h_row = lax.broadcasted_iota(jnp.int32, (qs, tile), 0)
    h_rows = (h_row // pair_cols) * 2 + (h_row % ns) // S5_STATE
    h_cols = (lax.broadcasted_iota(jnp.int32, (qs, tile), 1) % QUAD_LANES) // S5_GROUP
    for q in range(QUADS_PER_SLAB):
        for step in range(SSM_CHUNK):
            compact = gst_ref[step, q * QUAD_LANES:(q + 1) * QUAD_LANES, :]
            tiled = jnp.concatenate([compact] * pairs, axis=1)
            gexp_ref[q, step * QUAD_LANES:(step + 1) * QUAD_LANES, :] = jnp.where(
                g_rows == g_cols, tiled, 0.0).astype(BF16)
        for a in range(ntile):
            spread = jnp.dot(hst_ref[q * qs:(q + 1) * qs, :], rep_ref[:, a * tile:(a + 1) * tile],
                             preferred_element_type=F32)
            hexp_ref[q, :, a * tile:(a + 1) * tile] = jnp.where(h_rows == h_cols, spread, 0.0).astype(BF16)

    hs_quads = [jnp.dot(ucat_ref[q], gexp_ref[q], preferred_element_type=F32) for q in range(QUADS_PER_SLAB)]
    for q in range(QUADS_PER_SLAB):
        for a in range(ntile):
            yq_ref[q, :, a * tile:(a + 1) * tile] = jnp.dot(
                ucat_ref[q, :, :(a + 1) * tile], r_ref[q, (ntile - 1 - a) * tile:, :], preferred_element_type=F32)

    row = lax.broadcasted_iota(jnp.int32, (nchunk, ns), 0) % chunks_per_seq

    def shifted(x, d):
        return jnp.where(row >= d, pltpu.roll(x, d, axis=0), 0.0)

    for p in range(GROUPS_PER_SLAB // 2):
        q, p2 = divmod(p, pairs)
        re = hs_quads[q][:, p2 * pair_cols:p2 * pair_cols + ns]
        im = hs_quads[q][:, p2 * pair_cols + ns:(p2 + 1) * pair_cols]
        for k in range(scan_steps):
            mr, mi = sr_ref[p, k:k + 1, :], si_ref[p, k:k + 1, :]
            d = 1 << k
            if d % V7X_SUBLANES:
                sre, sim = shifted(re, d), shifted(im, d)
                re, im = re + mr * sre - mi * sim, im + mr * sim + mi * sre
                continue
            re_parts, im_parts = [], []
            for s0 in range(0, nchunk, chunks_per_seq):
                r_seq, i_seq = re[s0:s0 + chunks_per_seq], im[s0:s0 + chunks_per_seq]
                r_src, i_src = r_seq[:chunks_per_seq - d], i_seq[:chunks_per_seq - d]
                re_parts += [r_seq[:d], r_seq[d:] + mr * r_src - mi * i_src]
                im_parts += [i_seq[:d], i_seq[d:] + mr * i_src + mi * r_src]
            re, im = jnp.concatenate(re_parts, axis=0), jnp.concatenate(im_parts, axis=0)
        hprev_ref[q, :, p2 * pair_cols:p2 * pair_cols + ns] = shifted(re, 1).astype(BF16)
        hprev_ref[q, :, p2 * pair_cols + ns:(p2 + 1) * pair_cols] = shifted(im, 1).astype(BF16)

    for q in range(QUADS_PER_SLAB):
        for a in range(ntile):
            lanes = slice(a * tile, (a + 1) * tile)
            y = (yq_ref[q, :, lanes]
                 + jnp.dot(hprev_ref[q], hexp_ref[q, :, lanes], preferred_element_type=F32)
                 + d_ref[q, :, lanes] * ucat_ref[q, :, lanes].astype(F32))
            yq_ref[q, :, lanes] = jax.nn.gelu(y, approximate=True)

    for pair in range(SSM_CHUNK // 2):
        lanes = slice(pair * V7X_LANES, (pair + 1) * V7X_LANES)
        y0, y1 = yq_ref[0, :, lanes], yq_ref[1, :, lanes]
        y_ref[pl.ds(2 * pair, nchunk, stride=SSM_CHUNK), :] = jnp.where(first_half, y0, swap_halves(y1))
        y_ref[pl.ds(2 * pair + 1, nchunk, stride=SSM_CHUNK), :] = jnp.where(first_half, swap_halves(y0), y1)


def ssm_apply(u, r_all, gst, hst, sr, si, d_quads, rep, *, seq, side=(), name="s5_ssm"):
    t, w = u.shape
    nslab = w // V7X_LANES
    side_in, side_out, side_shapes, side_ops = _side_cast_specs(side, nslab, lambda s: s)
    nchunk = t // SSM_CHUNK
    cat = SSM_CHUNK * QUAD_LANES
    ns = 2 * S5_STATE
    qs = GROUPS_PER_QUAD * ns
    nq = QUADS_PER_SLAB
    chunks_per_seq = seq // SSM_CHUNK
    scan_steps = (chunks_per_seq - 1).bit_length()
    gb = GROUPS_PER_SLAB
    rows = sr.shape[1]
    return pl.pallas_call(
        functools.partial(_ssm_kernel, chunks_per_seq=chunks_per_seq, scan_steps=scan_steps, nside=len(side)),
        grid=(nslab,),
        in_specs=[
            pl.BlockSpec((t, V7X_LANES), lambda s: (0, s)),
            pl.BlockSpec((None, nq, cat, 2 * V7X_LANES), lambda s: (s, 0, 0, 0)),
            pl.BlockSpec((None, SSM_CHUNK, V7X_LANES, 2 * ns), lambda s: (s, 0, 0, 0)),
            pl.BlockSpec((None, gb * ns, SSM_CHUNK * S5_GROUP), lambda s: (s, 0, 0)),
            pl.BlockSpec((gb // 2, rows, ns), lambda s: (s, 0, 0)),
            pl.BlockSpec((gb // 2, rows, ns), lambda s: (s, 0, 0)),
            pl.BlockSpec((None, nq, 1, cat), lambda s: (s, 0, 0, 0)),
            pl.BlockSpec((SSM_CHUNK * S5_GROUP, cat), lambda s: (0, 0)),
        ] + side_in,
        out_specs=[pl.BlockSpec((t, V7X_LANES), lambda s: (0, s))] + side_out,
        out_shape=[jax.ShapeDtypeStruct((t, w), F32)] + side_shapes,
        scratch_shapes=[pltpu.VMEM((nq, nchunk, cat), BF16), pltpu.VMEM((nq, cat, qs), BF16),
                        pltpu.VMEM((nq, qs, cat), BF16), pltpu.VMEM((nq, nchunk, qs), BF16),
                        pltpu.VMEM((nq, nchunk, cat), F32)],
        compiler_params=pltpu.CompilerParams(dimension_semantics=("parallel",),
                                             vmem_limit_bytes=LARGE_VMEM_LIMIT),
        name=name,
    )(u, r_all, gst, hst, sr, si, d_quads, rep, *side_ops)


ATTN_SPLIT = 3
ATTN_EXTRA_ROWS = 16


def _split3(c):
    hi = c.astype(BF16).astype(F32)
    mid = (c - hi).astype(BF16).astype(F32)
    return hi, mid, c - hi - mid


def _fox_attn_kernel(qt_ref, k_ref, vt_ref, ct_ref, sz_ref, o_ref,
                     qa_ref, ka_ref, va_ref, s0_ref, s1_ref, smax_ref, m_ref, acc_ref, *, tile, heads_per_step):
    seq = qt_ref.shape[1]
    dh = qt_ref.shape[0] // heads_per_step
    ntiles = seq // tile
    ext = ATTN_EXTRA_ROWS

    @pl.when((pl.program_id(0) == 0) & (pl.program_id(1) == 0))
    def _():
        rowv = lax.broadcasted_iota(jnp.int32, (ext, seq), 0)
        for hh in range(heads_per_step):
            qa_ref[hh, dh + ext:, :] = jnp.zeros((dh - ext, seq), BF16)
            va_ref[hh, dh:, :] = jnp.where(rowv == 0, 1.0, 0.0).astype(BF16)

    row16 = lax.broadcasted_iota(jnp.int32, (ext, seq), 0)
    rowk = lax.broadcasted_iota(jnp.int32, (dh, tile), 0)
    for hh in range(heads_per_step):
        feat = slice(hh * dh, (hh + 1) * dh)
        parts = _split3(ct_ref[hh])
        qa_ref[hh, :dh, :] = qt_ref[feat, :]
        ext_q = jnp.where(row16 < 2 * ATTN_SPLIT, 1.0, 0.0)
        for n, part in enumerate(parts):
            ext_q = jnp.where(row16 == n, part, ext_q)
        qa_ref[hh, dh:dh + ext, :] = ext_q.astype(BF16)
        va_ref[hh, :dh, :] = vt_ref[feat, :]
        for r in range(ntiles):
            cols = slice(r * tile, (r + 1) * tile)
            ext_kt = jnp.where(rowk < ATTN_SPLIT, 1.0, 0.0)
            for n, part in enumerate(parts):
                ext_kt = jnp.where(rowk == ATTN_SPLIT + n, -part[:, cols], ext_kt)
            ka_ref[hh, cols, :dh] = k_ref[cols, feat]
            ka_ref[hh, cols, dh:] = ext_kt.T.astype(BF16)

    sub = tile // 2
    causal = (lax.broadcasted_iota(jnp.int32, (sub, sub), 0) <= lax.broadcasted_iota(jnp.int32, (sub, sub), 1))
    s_slots = (s0_ref, s1_ref)
    width = sub

    def scores(hh, k0, nk, q0, nq):
        return jnp.dot(ka_ref[hh, k0:k0 + nk, :], qa_ref[hh, :, q0:q0 + nq], preferred_element_type=F32)

    def produce(slot, hh, qi, kb):
        k0, q0 = kb * tile, qi * tile
        s_ref = s_slots[slot]
        if kb == qi:
            def top():
                s_ref[:sub, :] = scores(hh, k0, sub, q0, tile)

            def bottom():
                s_ref[sub:, sub:] = scores(hh, k0 + sub, sub, q0 + sub, sub)
            return [top, bottom]

        def half(c0):
            def run():
                st = scores(hh, k0, tile, q0 + c0, width)
                s_ref[:, c0:c0 + width] = st
                smax_ref[slot:slot + 1, c0:c0 + width] = jnp.max(st, axis=0, keepdims=True)
            return run
        return [half(c0) for c0 in range(0, tile, width)]

    def update(hh, par, st, st_max, k0, c0, first):
        nk, nq = st.shape
        cols = slice(c0, c0 + nq)
        pv = functools.partial(jnp.dot, va_ref[hh, :, k0:k0 + nk], preferred_element_type=F32)
        if first:
            m_ref[par:par + 1, cols] = st_max
            acc_ref[par, :, cols] = pv(jnp.exp2(st - st_max).astype(BF16))
        else:
            m = m_ref[par:par + 1, cols]
            m_new = jnp.maximum(m, st_max)
            acc_ref[par, :, cols] = (jnp.exp2(m - m_new) * acc_ref[par, :, cols]
                                     + pv(jnp.exp2(st - m_new).astype(BF16)))
            m_ref[par:par + 1, cols] = m_new

    def consume(slot, hh, qi, kb):
        par, k0, s_ref = (hh * ntiles + qi) % 2, kb * tile, s_slots[slot]
        if kb != qi:
            def half(c0):
                return lambda: update(hh, par, s_ref[:, c0:c0 + width], smax_ref[slot:slot + 1, c0:c0 + width],
                                      k0, c0, kb == 0)
            return [half(c0) for c0 in range(0, tile, width)]

        def early():
            st = jnp.where(causal, s_ref[:sub, :sub], NEG_INF)
            update(hh, par, st, jnp.max(st, axis=0, keepdims=True), k0, 0, kb == 0)

        def late():
            st = jnp.concatenate([s_ref[:sub, sub:], jnp.where(causal, s_ref[sub:, sub:], NEG_INF)], axis=0)
            update(hh, par, st, jnp.max(st, axis=0, keepdims=True), k0, sub, kb == 0)
        return [early, late]

    def finalize(hh, qi):
        acc = acc_ref[(hh * ntiles + qi) % 2]
        rows, feat = slice(qi * tile, (qi + 1) * tile), slice(hh * dh, (hh + 1) * dh)
        o = (acc[:dh, :] / acc[dh:dh + 1, :]).T
        o_ref[rows, feat] = (o * sz_ref[rows, feat].astype(F32)).astype(o_ref.dtype)

    blocks = [(hh, qi, kb) for hh in range(heads_per_step) for qi in range(ntiles) for kb in range(qi + 1)]
    for part in produce(0, *blocks[0]):
        part()
    for n, (hh, qi, kb) in enumerate(blocks):
        ahead = produce((n + 1) % 2, *blocks[n + 1]) if n + 1 < len(blocks) else []
        now = consume(n % 2, hh, qi, kb)
        for k in range(max(len(ahead), len(now))):
            for part in ahead[k:k + 1] + now[k:k + 1]:
                part()
        if kb == qi:
            finalize(hh, qi)


def fox_attention(qt, k, vt, cumt2, sz, *, batch, seq, heads, tile=1024, heads_per_step=1, name="fox_attention"):
    width, t = qt.shape
    dh = width // heads
    hps = heads_per_step
    natural = pl.BlockSpec((seq, hps * dh), lambda b, h: (b, h))
    transposed = pl.BlockSpec((hps * dh, seq), lambda b, h: (h, b))
    return pl.pallas_call(
        functools.partial(_fox_attn_kernel, tile=tile, heads_per_step=hps),
        grid=(batch, heads // hps),
        in_specs=[transposed, natural, transposed,
                  pl.BlockSpec((None, hps, 1, seq), lambda b, h: (b, h, 0, 0)),
                  natural],
        out_specs=natural,
        out_shape=jax.ShapeDtypeStruct((t, width), BF16),
        scratch_shapes=[pltpu.VMEM((hps, 2 * dh, seq), BF16), pltpu.VMEM((hps, seq, 2 * dh), BF16),
                        pltpu.VMEM((hps, dh + ATTN_EXTRA_ROWS, seq), BF16),
                        pltpu.VMEM((tile, tile), F32), pltpu.VMEM((tile, tile), F32),
                        pltpu.VMEM((V7X_SUBLANES, tile), F32), pltpu.VMEM((V7X_SUBLANES, tile), F32),
                        pltpu.VMEM((2, dh + ATTN_EXTRA_ROWS, tile), F32)],
        compiler_params=_params("arbitrary", "arbitrary"),
        name=name,
    )(qt, k, vt, cumt2, sz)


def _identity(v):
    return v


def _doubled(v):
    return jnp.concatenate([v, v], axis=-1).astype(F32)


def kernel(x, norm_pre, norm_post, s5_w_in, s5_a_re, s5_a_im, s5_log_dt, s5_b_re, s5_b_im, s5_c_re, s5_c_im,
           s5_d, s5_w_glu, s5_b_glu, s5_w_out, kv_norm, kv_w, kv_b_f, fox_w_in, fox_w_out):
    batch, seq, d_model = x.shape
    t = batch * seq
    heads = kv_b_f.shape[0]
    fox_width = fox_w_in.shape[-1] // 2
    assert norm_pre.shape[0] == 2 and s5_w_in.shape[0] == 1 and fox_w_in.shape[0] == 1
    assert fox_width // heads == FOX_HEAD_DIM and heads <= V7X_LANES
    ngroups = s5_a_re.shape[1]
    chunks_per_seq = seq // SSM_CHUNK
    scan_steps = (chunks_per_seq - 1).bit_length()

    h0 = x.reshape(t, d_model)

    s5_width = s5_w_in.shape[-1] // 2
    ldt2 = jnp.broadcast_to(s5_log_dt[0][:, None], (ngroups, 2 * S5_STATE)).astype(F32)
    b2 = jnp.concatenate([jnp.swapaxes(s5_b_re[0], 1, 2), jnp.swapaxes(s5_b_im[0], 1, 2)], axis=-1)
    c2 = jnp.concatenate([s5_c_re[0], s5_c_im[0]], axis=-1)
    rep = replication_matrix()
    r_all, gst, hst, sr, si, w_in = ssm_prep(_doubled(s5_a_re[0]), _doubled(s5_a_im[0]), ldt2, b2.astype(F32),
                                             c2.astype(F32), rep, replication_matrix(shift=1),
                                             scan_steps=scan_steps,
                                             side=[side_cast(s5_w_in[0], row_scale=norm_pre[0])])
    d_quads = jnp.tile(s5_d[0].reshape(-1, QUADS_PER_SLAB, 1, QUAD_LANES), (1, 1, 1, SSM_CHUNK)).astype(F32)
    kv_cols = 2 * fox_width
    kv_w_t = kv_w.T
    u, sz, w_glu, w_out, w_kv_t = fused_proj(
        h0, [Proj(w_in, False, 0, _identity, F32, False), Proj(w_in, False, s5_width, jax.nn.silu, BF16, False)],
        n=s5_width, tm=1024, name="s5_in_proj",
        side=[side_cast(s5_w_glu[0]), side_cast(s5_w_out[0]), side_cast(kv_w_t, kv_cols, col_scale=kv_norm)])
    yg, w_in_fox, w_out_fox = ssm_apply(u, r_all, gst, hst, sr, si, d_quads, rep, seq=seq,
                                        side=[side_cast(fox_w_in[0], row_scale=norm_pre[1]),
                                              side_cast(fox_w_out[0])])
    h1, h1_bf16, h1_inv_rms = s5_tail(yg, w_glu, s5_b_glu[0][None, :], sz, w_out, h0, norm_post[0][None, :])

    wf_t = jnp.pad((kv_w_t[kv_cols:] * kv_norm[None, :]).astype(BF16), ((0, V7X_LANES - heads), (0, 0)))
    bf = jnp.pad(kv_b_f, (0, V7X_LANES - heads))[None, :].astype(F32)
    scale = FOX_HEAD_DIM ** -0.5 * LOG2_E
    k, vt, qt, sz2, cumt2 = fused_proj(
        h1_bf16,
        [Proj(w_kv_t, True, 0, _identity, BF16, False),
         Proj(w_kv_t, True, fox_width, _identity, BF16, True),
         Proj(w_in_fox, False, 0, lambda v: v * scale, BF16, True),
         Proj(w_in_fox, False, fox_width, jax.nn.silu, BF16, False)],
        n=fox_width, inv=h1_inv_rms, gate=Gate(wf_t, bf, heads, seq), tm=1024, vmem_limit=LARGE_VMEM_LIMIT,
        name="fox_proj")
    o = fox_attention(qt, k, vt, cumt2.reshape(batch, heads, 1, seq), sz2,
                      batch=batch, seq=seq, heads=heads)
    h2 = out_proj(o, w_out_fox, h1, norm_post[1][None, :], name="fox_out_proj")
    return h2.reshape(batch, seq, d_model)
```

```python
import functools
import math
from typing import Callable, NamedTuple

import jax
import jax.numpy as jnp
from jax import lax
from jax.experimental import pallas as pl
from jax.experimental.pallas import tpu as pltpu

F32 = jnp.float32
BF16 = jnp.bfloat16

RMS_EPS = 1e-6
NEG_INF = -1e30
LOG2_E = math.log2(math.e)

S5_GROUP = 16
S5_STATE = 64
FOX_HEAD_DIM = 128

V7X_LANES = 128
V7X_SUBLANES = 8
BF16_ROWS = 2 * V7X_SUBLANES
SSM_CHUNK = 16
GROUPS_PER_SLAB = V7X_LANES // S5_GROUP
QUADS_PER_SLAB = 2
QUAD_LANES = V7X_LANES // QUADS_PER_SLAB
GROUPS_PER_QUAD = GROUPS_PER_SLAB // QUADS_PER_SLAB
STEPS_PER_TILE = 2 * V7X_LANES // QUAD_LANES
V7X_VMEM_BYTES = 64 * 1024 * 1024
VMEM_LIMIT = V7X_VMEM_BYTES * 3 // 4
LARGE_VMEM_LIMIT = V7X_VMEM_BYTES * 7 // 8


def _params(*semantics):
    return pltpu.CompilerParams(dimension_semantics=semantics, vmem_limit_bytes=VMEM_LIMIT)


def _rms_scale(x, gain):
    ms = jnp.mean(x * x, axis=-1, keepdims=True)
    return x * lax.rsqrt(ms + RMS_EPS) * gain


class Proj(NamedTuple):
    weight: jax.Array
    weight_is_t: bool
    first: int
    act: Callable
    dtype: jnp.dtype
    transposed: bool


def _log_sigmoid(x):
    return -(jnp.maximum(-x, 0.0) + jnp.log1p(jnp.exp(-jnp.abs(x))))


def _cumsum_lanes(c):
    lanes = c.shape[1]
    lane = lax.broadcasted_iota(jnp.int32, c.shape, 1)
    d = 1
    while d < lanes:
        c = c + jnp.where(lane >= d, pltpu.roll(c, d, axis=1), 0.0)
        d *= 2
    return c


class SideCast(NamedTuple):
    src: jax.Array
    rows_used: int
    scale: jax.Array


def side_cast(src, rows_used=None, *, row_scale=None, col_scale=None):
    rows_used = src.shape[0] if rows_used is None else rows_used
    if row_scale is not None:
        scale = row_scale.reshape(rows_used, 1)
    elif col_scale is not None:
        scale = col_scale.reshape(1, src.shape[1])
    else:
        scale = jnp.ones((1, src.shape[1]), F32)
    return SideCast(src, rows_used, scale.astype(F32))


def _side_cast_specs(casts, steps, flat_step):
    in_specs, out_specs, out_shapes, operands = [], [], [], []
    for c in casts:
        block = (c.rows_used // steps, c.src.shape[1])
        by_step = lambda *idx: (flat_step(*idx), 0)
        in_specs.append(pl.BlockSpec(block, by_step))
        if c.scale.shape[0] == 1:
            in_specs.append(pl.BlockSpec(c.scale.shape, lambda *idx: (0, 0)))
        else:
            in_specs.append(pl.BlockSpec((block[0], 1), by_step))
        operands += [c.src, c.scale]
        out_specs.append(pl.BlockSpec(block, by_step))
        out_shapes.append(jax.ShapeDtypeStruct((c.rows_used, c.src.shape[1]), BF16))
    return in_specs, out_specs, out_shapes, operands


def _do_side_casts(in_refs, dst_refs):
    for n, dst_ref in enumerate(dst_refs):
        dst_ref[...] = (in_refs[2 * n][...] * in_refs[2 * n + 1][...]).astype(BF16)


def _inv_rms(x):
    inv = lax.rsqrt(jnp.mean(x * x, axis=-1, keepdims=True) + RMS_EPS)
    return jnp.broadcast_to(inv, (x.shape[0], V7X_LANES))


def _fused_proj_kernel(*refs, projs, gate, nside, prenormed):
    nproj = len(projs)
    x_ref = refs[0]
    pos = 1
    if prenormed:
        inv_in_ref = refs[1]
        pos = 2
    w_refs = refs[pos:pos + nproj]
    pos += nproj
    if gate is not None:
        wf_ref, bf_ref = refs[pos:pos + 2]
        pos += 2
    side_in = refs[pos:pos + 2 * nside]
    pos += 2 * nside
    o_refs = refs[pos:pos + nproj]
    pos += nproj
    if gate is not None:
        cumt_ref = refs[pos]
        pos += 1
    _do_side_casts(side_in, refs[pos:pos + nside])
    pos += nside
    if prenormed:
        xb_ref, inv_ref = x_ref, inv_in_ref
    else:
        xb_ref, inv_ref = refs[pos:pos + 2]
        pos += 2
    invt_ref = refs[pos]
    pos += 1
    nt = (((1,), (1,)), ((), ()))
    any_transposed = any(p.transposed for p in projs)

    @pl.when(pl.program_id(1) == 0)
    def _():
        if not prenormed:
            x = x_ref[...]
            xb_ref[...] = x.astype(BF16)
            inv_ref[...] = _inv_rms(x)
        if any_transposed:
            invt_ref[...] = inv_ref[...].T[:V7X_SUBLANES, :]

    gate_host = None
    if gate is not None:
        carry_ref = refs[pos]
        gate_host = [p.weight_is_t and p.transposed for p in projs].index(True)
    logit_t = None
    xb = xb_ref[...]
    for n, (p, w_ref, o_ref) in enumerate(zip(projs, w_refs, o_refs)):
        if n == gate_host:
            lhs = jnp.concatenate([w_ref[...], wf_ref[:carry_ref.shape[0], :]], axis=0)
            y = lax.dot_general(lhs, xb, nt, preferred_element_type=F32) * invt_ref[0:1, :]
            logit_t = y[w_ref.shape[0]:, :]
            o_ref[...] = p.act(y[:w_ref.shape[0], :]).astype(o_ref.dtype)
            continue
        if p.weight_is_t and p.transposed:
            y = lax.dot_general(w_ref[...], xb, nt, preferred_element_type=F32)
        elif p.weight_is_t:
            y = lax.dot_general(xb, w_ref[...], nt, preferred_element_type=F32)
        else:
            y = jnp.dot(xb, w_ref[...], preferred_element_type=F32)
            y = y.T if p.transposed else y
        if p.transposed:
            y = y * invt_ref[0:1, :]
        else:
            y = y * jnp.concatenate([inv_ref[...]] * (y.shape[1] // V7X_LANES), axis=1)
        o_ref[...] = p.act(y).astype(o_ref.dtype)

    if gate is not None:
        @pl.when(pl.program_id(1) == 0)
        def _():
            @pl.when(pl.program_id(0) % gate.tiles_per_seq == 0)
            def _():
                carry_ref[...] = jnp.zeros_like(carry_ref)

            log_f = _log_sigmoid(logit_t + bf_ref[:carry_ref.shape[0], :])
            c = _cumsum_lanes(log_f) + carry_ref[:, 0:1]
            carry_ref[...] += jnp.broadcast_to(jnp.sum(log_f, axis=1, keepdims=True), carry_ref.shape)
            cumt_ref[...] = (c * LOG2_E)[:gate.heads, :]


class Gate(NamedTuple):
    wf_t: jax.Array
    bias: jax.Array
    heads: int
    seq: int
    tiles_per_seq: int = 0


def fused_proj(x, projs, *, n, inv=None, gate=None, side=(), tm=512, tn=512, vmem_limit=VMEM_LIMIT,
               name="fused_proj"):
    t, d = x.shape
    prenormed = inv is not None
    nj = n // tn
    w_specs, out_specs, out_shapes = [], [], []
    for p in projs:
        if p.weight_is_t:
            w_specs.append(pl.BlockSpec((tn, d), lambda i, j, b0=p.first // tn: (j + b0, 0)))
        else:
            w_specs.append(pl.BlockSpec((d, tn), lambda i, j, b0=p.first // tn: (0, j + b0)))
        if p.transposed:
            out_specs.append(pl.BlockSpec((tn, tm), lambda i, j: (j, i)))
            out_shapes.append(jax.ShapeDtypeStruct((n, t), p.dtype))
        else:
            out_specs.append(pl.BlockSpec((tm, tn), lambda i, j: (i, j)))
            out_shapes.append(jax.ShapeDtypeStruct((t, n), p.dtype))
    operands = [x] + ([inv] if prenormed else []) + [p.weight for p in projs]
    in_specs = [pl.BlockSpec((tm, d), lambda i, j: (i, 0))]
    if prenormed:
        in_specs.append(pl.BlockSpec((tm, V7X_LANES), lambda i, j: (i, 0)))
    in_specs += w_specs
    scratch = [] if prenormed else [pltpu.VMEM((tm, d), BF16), pltpu.VMEM((tm, V7X_LANES), F32)]
    scratch.append(pltpu.VMEM((V7X_SUBLANES, tm), F32))
    kernel_projs = tuple(p._replace(weight=None) for p in projs)
    kernel_gate = None
    if gate is not None:
        tps = gate.seq // tm
        kernel_gate = gate._replace(wf_t=None, bias=None, tiles_per_seq=tps)
        operands += [gate.wf_t, gate.bias]
        in_specs += [pl.BlockSpec((V7X_LANES, d), lambda i, j: (0, 0)), pl.BlockSpec((V7X_LANES, 1), lambda i, j: (0, 0))]
        out_specs.append(pl.BlockSpec((None, gate.heads, tm), lambda i, j: (i // tps, 0, i % tps)))
        out_shapes.append(jax.ShapeDtypeStruct((t // gate.seq, gate.heads, gate.seq), F32))
        head_rows = -(-gate.heads // BF16_ROWS) * BF16_ROWS
        scratch.append(pltpu.VMEM((head_rows, V7X_LANES), F32))
    side_in, side_out, side_shapes, side_ops = _side_cast_specs(side, (t // tm) * nj, lambda i, j: i * nj + j)
    operands += side_ops
    in_specs += side_in
    out_specs += side_out
    out_shapes += side_shapes
    return pl.pallas_call(
        functools.partial(_fused_proj_kernel, projs=kernel_projs, gate=kernel_gate, nside=len(side),
                          prenormed=prenormed),
        grid=(t // tm, nj),
        in_specs=in_specs,
        out_specs=out_specs,
        out_shape=out_shapes,
        scratch_shapes=scratch,
        compiler_params=pltpu.CompilerParams(dimension_semantics=("arbitrary", "arbitrary"),
                                             vmem_limit_bytes=vmem_limit),
        name=name,
    )(*operands)


def _out_proj_kernel(a_ref, w_ref, res_ref, gain_ref, o_ref):
    y = jnp.dot(a_ref[...], w_ref[...], preferred_element_type=F32)
    o_ref[...] = res_ref[...] + _rms_scale(y, gain_ref[...])


def out_proj(a, w, res, gain, *, tm=512, name="out_proj"):
    t, k = a.shape
    d = w.shape[1]
    return pl.pallas_call(
        _out_proj_kernel,
        grid=(t // tm,),
        in_specs=[
            pl.BlockSpec((tm, k), lambda i: (i, 0)),
            pl.BlockSpec((k, d), lambda i: (0, 0), pipeline_mode=pl.Buffered(1)),
            pl.BlockSpec((tm, d), lambda i: (i, 0)),
            pl.BlockSpec((1, d), lambda i: (0, 0)),
        ],
        out_specs=pl.BlockSpec((tm, d), lambda i: (i, 0)),
        out_shape=jax.ShapeDtypeStruct((t, d), F32),
        compiler_params=_params("parallel"),
        name=name,
    )(a, w, res, gain)


def _s5_tail_kernel(yg_ref, wg_ref, b_ref, sz_ref, wo_ref, res_ref, gain_ref, o_ref, ob_ref, inv_ref, *, halves):
    rows = yg_ref.shape[0] // halves
    for h in range(halves):
        r = slice(h * rows, (h + 1) * rows)
        yg = yg_ref[r, :]
        gate = jax.nn.sigmoid(jnp.dot(yg.astype(BF16), wg_ref[...], preferred_element_type=F32) + b_ref[...])
        y3 = (yg * gate * sz_ref[r, :].astype(F32)).astype(BF16)
        y = jnp.dot(y3, wo_ref[...], preferred_element_type=F32)
        out = res_ref[r, :] + _rms_scale(y, gain_ref[...])
        o_ref[r, :] = out
        ob_ref[r, :] = out.astype(BF16)
        inv_ref[r, :] = _inv_rms(out)


def s5_tail(yg, w_glu, b_glu, sz, w_out, res, gain, *, tm=512, halves=2, name="s5_tail"):
    t, k = yg.shape
    d = w_out.shape[1]
    rows = pl.BlockSpec((tm, k), lambda i: (i, 0))
    resident = functools.partial(pl.BlockSpec, index_map=lambda i: (0, 0), pipeline_mode=pl.Buffered(1))
    return pl.pallas_call(
        functools.partial(_s5_tail_kernel, halves=halves),
        grid=(t // tm,),
        in_specs=[rows, resident((k, k)), resident((1, k)), rows, resident((k, d)),
                  pl.BlockSpec((tm, d), lambda i: (i, 0)), resident((1, d))],
        out_specs=[pl.BlockSpec((tm, d), lambda i: (i, 0)), pl.BlockSpec((tm, d), lambda i: (i, 0)),
                   pl.BlockSpec((tm, V7X_LANES), lambda i: (i, 0))],
        out_shape=[jax.ShapeDtypeStruct((t, d), F32), jax.ShapeDtypeStruct((t, d), BF16),
                   jax.ShapeDtypeStruct((t, V7X_LANES), F32)],
        compiler_params=pltpu.CompilerParams(dimension_semantics=("parallel",),
                                             vmem_limit_bytes=LARGE_VMEM_LIMIT),
        name=name,
    )(yg, w_glu, b_glu, sz, w_out, res, gain)


def _ssm_prep_kernel(are_ref, aim_ref, ldt_ref, b2_ref, c2_ref, rep_ref, repb_ref, *rest, groups, scan_steps):
    nside = (len(rest) - 5) // 3
    r_ref, g_ref, h_ref, sr_ref, si_ref = rest[2 * nside:2 * nside + 5]
    _do_side_casts(rest[:2 * nside], rest[2 * nside + 5:])
    lane = lax.broadcasted_iota(jnp.int32, (1, 2 * S5_STATE), 1)
    minus_plus = jnp.where(lane < S5_STATE, -1.0, 1.0).astype(F32)
    gw = S5_GROUP
    ns = 2 * S5_STATE
    krows = []

    ar = are_ref[...]
    ai = aim_ref[...]
    dt = jnp.exp(ldt_ref[...])
    mag = jnp.exp(ar * dt)
    lam_r = mag * jnp.cos(ai * dt)
    lam_i = mag * jnp.sin(ai * dt)
    den = ar * ar + ai * ai
    nr = lam_r - 1.0
    coef_r_all = (nr * ar + lam_i * ai) / den
    coef_i_all = (lam_i * ar - nr * ai) / den
    pow_r_all = [jnp.ones_like(lam_r)]
    pow_i_all = [jnp.zeros_like(lam_r)]
    for _ in range(SSM_CHUNK):
        pr, pi = pow_r_all[-1], pow_i_all[-1]
        pow_r_all.append(pr * lam_r - pi * lam_i)
        pow_i_all.append(pr * lam_i + pi * lam_r)
    mu_r, mu_i = pow_r_all[SSM_CHUNK], pow_i_all[SSM_CHUNK]
    scan_r_all, scan_i_all = [], []
    for _ in range(scan_steps):
        scan_r_all.append(mu_r)
        scan_i_all.append(mu_i)
        mu_r, mu_i = mu_r * mu_r - mu_i * mu_i, 2.0 * mu_r * mu_i

    for gi in range(groups):
        grp = slice(gi, gi + 1)
        coef_r, coef_i = coef_r_all[grp], coef_i_all[grp]
        pow_r = [p[grp] for p in pow_r_all]
        pow_i = [p[grp] for p in pow_i_all]

        b2 = b2_ref[gi]
        b2s = pltpu.roll(b2, S5_STATE, axis=1) * minus_plus
        c2 = c2_ref[gi]
        c2a = c2 * (-minus_plus)
        c2b = -pltpu.roll(c2, S5_STATE, axis=1)

        pair, second = divmod(gi, 2)
        low = lane < S5_STATE

        for step in range(SSM_CHUNK):
            pr, pi = pow_r[SSM_CHUNK - 1 - step], pow_i[SSM_CHUNK - 1 - step]
            wr = pr * coef_r - pi * coef_i
            wi = pr * coef_i + pi * coef_r
            g_ri = wr * b2 + wi * b2s
            g_ir = pltpu.roll(g_ri, S5_STATE, axis=1)
            if second:
                re_cols, im_cols = jnp.where(low, 0.0, g_ir), jnp.where(low, 0.0, g_ri)
            else:
                re_cols, im_cols = jnp.where(low, g_ri, 0.0), jnp.where(low, g_ir, 0.0)
            g_ref[step, gi * gw:(gi + 1) * gw, :] = jnp.concatenate([re_cols, im_cols], axis=1).astype(g_ref.dtype)

        cl = [pow_r[tau] * c2a + pow_i[tau] * c2b for tau in range(SSM_CHUNK + 1)]
        h_t = jnp.concatenate(cl[1:], axis=0)
        h_rows = h_t.T.astype(h_ref.dtype)
        base = pair * 2 * ns + second * S5_STATE
        h_ref[base:base + S5_STATE, :] = h_rows[:S5_STATE]
        h_ref[base + ns:base + ns + S5_STATE, :] = h_rows[S5_STATE:]

        bbar2 = coef_r * b2 + coef_i * b2s
        cl_all = jnp.concatenate(cl[:-1], axis=0)
        krows.append(lax.dot_general(bbar2, cl_all, (((1,), (1,)), ((), ())),
                                     precision=lax.Precision.HIGHEST,
                                     preferred_element_type=F32))

        if second:
            first_grp = slice(gi - 1, gi)
            pad = [jnp.zeros_like(coef_r)] * (sr_ref.shape[1] - scan_steps)
            sr_ref[pair] = jnp.concatenate([jnp.where(low, s[first_grp], s[grp]) for s in scan_r_all] + pad, axis=0)
            si_ref[pair] = jnp.concatenate([jnp.where(low, s[first_grp], s[grp]) for s in scan_i_all] + pad, axis=0)

    for q in range(QUADS_PER_SLAB):
        kst = jnp.concatenate(krows[q * GROUPS_PER_QUAD:(q + 1) * GROUPS_PER_QUAD], axis=0).astype(BF16)
        units = []
        for rep in (rep_ref, repb_ref):
            spread = jnp.dot(kst, rep[...], preferred_element_type=F32)
            rows_g = lax.broadcasted_iota(jnp.int32, spread.shape, 0) // gw
            cols_g = (lax.broadcasted_iota(jnp.int32, spread.shape, 1) % QUAD_LANES) // gw
            units.append(jnp.where(rows_g == cols_g, spread, 0.0).astype(r_ref.dtype))
        blocks = [jnp.concatenate([s[:, e * V7X_LANES:(e + 1) * V7X_LANES] for s in units], axis=0)
                  for e in range(SSM_CHUNK // 2)]
        zero = jnp.zeros_like(blocks[0])
        ntile = SSM_CHUNK // STEPS_PER_TILE
        for d in range(ntile):
            base = (ntile - 1 - d) * 2 * V7X_LANES
            top = jnp.concatenate([blocks[2 * d], blocks[2 * d + 1]], axis=1)
            bottom = jnp.concatenate([blocks[2 * d - 1] if d else zero, blocks[2 * d]], axis=1)
            r_ref[q, base:base + V7X_LANES, :] = top
            r_ref[q, base + V7X_LANES:base + 2 * V7X_LANES, :] = bottom


def ssm_prep(a_re2, a_im2, log_dt2, b2, c2, rep, rep_shifted, *, scan_steps, side=(), name="ssm_prep"):
    ng = a_re2.shape[0]
    gb = GROUPS_PER_SLAB
    nslab = ng // gb
    cat = SSM_CHUNK * QUAD_LANES
    ns = 2 * S5_STATE
    rows = -(-scan_steps // V7X_SUBLANES) * V7X_SUBLANES
    vec = pl.BlockSpec((gb, ns), lambda i: (i, 0))
    mat = pl.BlockSpec((gb, S5_GROUP, ns), lambda i: (i, 0, 0))
    side_in, side_out, side_shapes, side_ops = _side_cast_specs(side, nslab, lambda i: i)
    return pl.pallas_call(
        functools.partial(_ssm_prep_kernel, groups=gb, scan_steps=scan_steps),
        grid=(nslab,),
        in_specs=[vec, vec, vec, mat, mat, pl.BlockSpec(rep.shape, lambda i: (0, 0)),
                  pl.BlockSpec(rep.shape, lambda i: (0, 0))] + side_in,
        out_specs=[
            pl.BlockSpec((None, QUADS_PER_SLAB, cat, 2 * V7X_LANES), lambda i: (i, 0, 0, 0)),
            pl.BlockSpec((None, SSM_CHUNK, V7X_LANES, 2 * ns), lambda i: (i, 0, 0, 0)),
            pl.BlockSpec((None, gb * ns, SSM_CHUNK * S5_GROUP), lambda i: (i, 0, 0)),
            pl.BlockSpec((gb // 2, rows, ns), lambda i: (i, 0, 0)),
            pl.BlockSpec((gb // 2, rows, ns), lambda i: (i, 0, 0)),
        ] + side_out,
        out_shape=[
            jax.ShapeDtypeStruct((nslab, QUADS_PER_SLAB, cat, 2 * V7X_LANES), BF16),
            jax.ShapeDtypeStruct((nslab, SSM_CHUNK, V7X_LANES, 2 * ns), BF16),
            jax.ShapeDtypeStruct((nslab, gb * ns, SSM_CHUNK * S5_GROUP), BF16),
            jax.ShapeDtypeStruct((ng // 2, rows, ns), F32),
            jax.ShapeDtypeStruct((ng // 2, rows, ns), F32),
        ] + side_shapes,
        compiler_params=_params("parallel"),
        name=name,
    )(a_re2, a_im2, log_dt2, b2, c2, rep, rep_shifted, *side_ops)


def replication_matrix(shift=0):
    src = jnp.arange(SSM_CHUNK * S5_GROUP)
    dst = jnp.arange(SSM_CHUNK * QUAD_LANES)
    same_step = (src[:, None] // S5_GROUP + shift) == (dst[None, :] // QUAD_LANES)
    same_chan = (src[:, None] % S5_GROUP) == (dst[None, :] % S5_GROUP)
    return (same_step & same_chan).astype(BF16)


def _ssm_kernel(u_ref, r_ref, gst_ref, hst_ref, sr_ref, si_ref, d_ref, rep_ref, *rest,
                chunks_per_seq, scan_steps, nside):
    y_ref = rest[2 * nside]
    _do_side_casts(rest[:2 * nside], rest[2 * nside + 1:3 * nside + 1])
    ucat_ref, gexp_ref, hexp_ref, hprev_ref, yq_ref = rest[3 * nside + 1:]
    t = u_ref.shape[0]
    nchunk = t // SSM_CHUNK
    ns = 2 * S5_STATE
    tile = 2 * V7X_LANES
    ntile = SSM_CHUNK // STEPS_PER_TILE
    qs = GROUPS_PER_QUAD * ns
    first_half = lax.broadcasted_iota(jnp.int32, (nchunk, V7X_LANES), 1) < QUAD_LANES

    def swap_halves(x):
        return pltpu.roll(x, QUAD_LANES, axis=1)

    for pair in range(SSM_CHUNK // 2):
        even = u_ref[pl.ds(2 * pair, nchunk, stride=SSM_CHUNK), :]
        odd = u_ref[pl.ds(2 * pair + 1, nchunk, stride=SSM_CHUNK), :]
        lanes = slice(pair * V7X_LANES, (pair + 1) * V7X_LANES)
        ucat_ref[0, :, lanes] = jnp.where(first_half, even, swap_halves(odd)).astype(BF16)
        ucat_ref[1, :, lanes] = jnp.where(first_half, swap_halves(even), odd).astype(BF16)

    pair_cols = 2 * ns
    pairs = GROUPS_PER_QUAD // 2
    g_rows = lax.broadcasted_iota(jnp.int32, (QUAD_LANES, qs), 0) // (2 * S5_GROUP)
    g_cols = lax.broadcasted_iota(jnp.int32, (QUAD_LANES, qs), 1) // pair_cols
    h_row = lax.broadcasted_iota(jnp.int32, (qs, tile), 0)
    h_rows = (h_row // pair_cols) * 2 + (h_row % ns) // S5_STATE
    h_cols = (lax.broadcasted_iota(jnp.int32, (qs, tile), 1) % QUAD_LANES) // S5_GROUP
    for q in range(QUADS_PER_SLAB):
        for step in range(SSM_CHUNK):
            compact = gst_ref[step, q * QUAD_LANES:(q + 1) * QUAD_LANES, :]
            tiled = jnp.concatenate([compact] * pairs, axis=1)
            gexp_ref[q, step * QUAD_LANES:(step + 1) * QUAD_LANES, :] = jnp.where(
                g_rows == g_cols, tiled, 0.0).astype(BF16)
        for a in range(ntile):
            spread = jnp.dot(hst_ref[q * qs:(q + 1) * qs, :], rep_ref[:, a * tile:(a + 1) * tile],
                             preferred_element_type=F32)
            hexp_ref[q, :, a * tile:(a + 1) * tile] = jnp.where(h_rows == h_cols, spread, 0.0).astype(BF16)

    hs_quads = [jnp.dot(ucat_ref[q], gexp_ref[q], preferred_element_type=F32) for q in range(QUADS_PER_SLAB)]
    for q in range(QUADS_PER_SLAB):
        for a in range(ntile):
            yq_ref[q, :, a * tile:(a + 1) * tile] = jnp.dot(
                ucat_ref[q, :, :(a + 1) * tile], r_ref[q, (ntile - 1 - a) * tile:, :], preferred_element_type=F32)

    row = lax.broadcasted_iota(jnp.int32, (nchunk, ns), 0) % chunks_per_seq

    def shifted(x, d):
        return jnp.where(row >= d, pltpu.roll(x, d, axis=0), 0.0)

    for p in range(GROUPS_PER_SLAB // 2):
        q, p2 = divmod(p, pairs)
        re = hs_quads[q][:, p2 * pair_cols:p2 * pair_cols + ns]
        im = hs_quads[q][:, p2 * pair_cols + ns:(p2 + 1) * pair_cols]
        for k in range(scan_steps):
            mr, mi = sr_ref[p, k:k + 1, :], si_ref[p, k:k + 1, :]
            d = 1 << k
            if d % V7X_SUBLANES:
                sre, sim = shifted(re, d), shifted(im, d)
                re, im = re + mr * sre - mi * sim, im + mr * sim + mi * sre
                continue
            re_parts, im_parts = [], []
            for s0 in range(0, nchunk, chunks_per_seq):
                r_seq, i_seq = re[s0:s0 + chunks_per_seq], im[s0:s0 + chunks_per_seq]
                r_src, i_src = r_seq[:chunks_per_seq - d], i_seq[:chunks_per_seq - d]
                re_parts += [r_seq[:d], r_seq[d:] + mr * r_src - mi * i_src]
                im_parts += [i_seq[:d], i_seq[d:] + mr * i_src + mi * r_src]
            re, im = jnp.concatenate(re_parts, axis=0), jnp.concatenate(im_parts, axis=0)
        hprev_ref[q, :, p2 * pair_cols:p2 * pair_cols + ns] = shifted(re, 1).astype(BF16)
        hprev_ref[q, :, p2 * pair_cols + ns:(p2 + 1) * pair_cols] = shifted(im, 1).astype(BF16)

    for q in range(QUADS_PER_SLAB):
        for a in range(ntile):
            lanes = slice(a * tile, (a + 1) * tile)
            y = (yq_ref[q, :, lanes]
                 + jnp.dot(hprev_ref[q], hexp_ref[q, :, lanes], preferred_element_type=F32)
                 + d_ref[q, :, lanes] * ucat_ref[q, :, lanes].astype(F32))
            yq_ref[q, :, lanes] = jax.nn.gelu(y, approximate=True)

    for pair in range(SSM_CHUNK // 2):
        lanes = slice(pair * V7X_LANES, (pair + 1) * V7X_LANES)
        y0, y1 = yq_ref[0, :, lanes], yq_ref[1, :, lanes]
        y_ref[pl.ds(2 * pair, nchunk, stride=SSM_CHUNK), :] = jnp.where(first_half, y0, swap_halves(y1))
        y_ref[pl.ds(2 * pair + 1, nchunk, stride=SSM_CHUNK), :] = jnp.where(first_half, swap_halves(y0), y1)


def ssm_apply(u, r_all, gst, hst, sr, si, d_quads, rep, *, seq, side=(), name="s5_ssm"):
    t, w = u.shape
    nslab = w // V7X_LANES
    side_in, side_out, side_shapes, side_ops = _side_cast_specs(side, nslab, lambda s: s)
    nchunk = t // SSM_CHUNK
    cat = SSM_CHUNK * QUAD_LANES
    ns = 2 * S5_STATE
    qs = GROUPS_PER_QUAD * ns
    nq = QUADS_PER_SLAB
    chunks_per_seq = seq // SSM_CHUNK
    scan_steps = (chunks_per_seq - 1).bit_length()
    gb = GROUPS_PER_SLAB
    rows = sr.shape[1]
    return pl.pallas_call(
        functools.partial(_ssm_kernel, chunks_per_seq=chunks_per_seq, scan_steps=scan_steps, nside=len(side)),
        grid=(nslab,),
        in_specs=[
            pl.BlockSpec((t, V7X_LANES), lambda s: (0, s)),
            pl.BlockSpec((None, nq, cat, 2 * V7X_LANES), lambda s: (s, 0, 0, 0)),
            pl.BlockSpec((None, SSM_CHUNK, V7X_LANES, 2 * ns), lambda s: (s, 0, 0, 0)),
            pl.BlockSpec((None, gb * ns, SSM_CHUNK * S5_GROUP), lambda s: (s, 0, 0)),
            pl.BlockSpec((gb // 2, rows, ns), lambda s: (s, 0, 0)),
            pl.BlockSpec((gb // 2, rows, ns), lambda s: (s, 0, 0)),
            pl.BlockSpec((None, nq, 1, cat), lambda s: (s, 0, 0, 0)),
            pl.BlockSpec((SSM_CHUNK * S5_GROUP, cat), lambda s: (0, 0)),
        ] + side_in,
        out_specs=[pl.BlockSpec((t, V7X_LANES), lambda s: (0, s))] + side_out,
        out_shape=[jax.ShapeDtypeStruct((t, w), F32)] + side_shapes,
        scratch_shapes=[pltpu.VMEM((nq, nchunk, cat), BF16), pltpu.VMEM((nq, cat, qs), BF16),
                        pltpu.VMEM((nq, qs, cat), BF16), pltpu.VMEM((nq, nchunk, qs), BF16),
                        pltpu.VMEM((nq, nchunk, cat), F32)],
        compiler_params=pltpu.CompilerParams(dimension_semantics=("parallel",),
                                             vmem_limit_bytes=LARGE_VMEM_LIMIT),
        name=name,
    )(u, r_all, gst, hst, sr, si, d_quads, rep, *side_ops)


ATTN_SPLIT = 3
ATTN_EXTRA_ROWS = 16


def _split3(c):
    hi = c.astype(BF16).astype(F32)
    mid = (c - hi).astype(BF16).astype(F32)
    return hi, mid, c - hi - mid


def _fox_attn_kernel(qt_ref, k_ref, vt_ref, ct_ref, sz_ref, o_ref,
                     qa_ref, ka_ref, va_ref, s0_ref, s1_ref, smax_ref, m_ref, acc_ref, *, tile, heads_per_step):
    seq = qt_ref.shape[1]
    dh = qt_ref.shape[0] // heads_per_step
    ntiles = seq // tile
    ext = ATTN_EXTRA_ROWS

    @pl.when((pl.program_id(0) == 0) & (pl.program_id(1) == 0))
    def _():
        rowv = lax.broadcasted_iota(jnp.int32, (ext, seq), 0)
        for hh in range(heads_per_step):
            qa_ref[hh, dh + ext:, :] = jnp.zeros((dh - ext, seq), BF16)
            va_ref[hh, dh:, :] = jnp.where(rowv == 0, 1.0, 0.0).astype(BF16)

    row16 = lax.broadcasted_iota(jnp.int32, (ext, seq), 0)
    rowk = lax.broadcasted_iota(jnp.int32, (dh, tile), 0)
    for hh in range(heads_per_step):
        feat = slice(hh * dh, (hh + 1) * dh)
        parts = _split3(ct_ref[hh])
        qa_ref[hh, :dh, :] = qt_ref[feat, :]
        ext_q = jnp.where(row16 < 2 * ATTN_SPLIT, 1.0, 0.0)
        for n, part in enumerate(parts):
            ext_q = jnp.where(row16 == n, part, ext_q)
        qa_ref[hh, dh:dh + ext, :] = ext_q.astype(BF16)
        va_ref[hh, :dh, :] = vt_ref[feat, :]
        for r in range(ntiles):
            cols = slice(r * tile, (r + 1) * tile)
            ext_kt = jnp.where(rowk < ATTN_SPLIT, 1.0, 0.0)
            for n, part in enumerate(parts):
                ext_kt = jnp.where(rowk == ATTN_SPLIT + n, -part[:, cols], ext_kt)
            ka_ref[hh, cols, :dh] = k_ref[cols, feat]
            ka_ref[hh, cols, dh:] = ext_kt.T.astype(BF16)

    sub = tile // 2
    causal = (lax.broadcasted_iota(jnp.int32, (sub, sub), 0) <= lax.broadcasted_iota(jnp.int32, (sub, sub), 1))
    s_slots = (s0_ref, s1_ref)
    width = sub

    def scores(hh, k0, nk, q0, nq):
        return jnp.dot(ka_ref[hh, k0:k0 + nk, :], qa_ref[hh, :, q0:q0 + nq], preferred_element_type=F32)

    def produce(slot, hh, qi, kb):
        k0, q0 = kb * tile, qi * tile
        s_ref = s_slots[slot]
        if kb == qi:
            def top():
                s_ref[:sub, :] = scores(hh, k0, sub, q0, tile)

            def bottom():
                s_ref[sub:, sub:] = scores(hh, k0 + sub, sub, q0 + sub, sub)
            return [top, bottom]

        def half(c0):
            def run():
                st = scores(hh, k0, tile, q0 + c0, width)
                s_ref[:, c0:c0 + width] = st
                smax_ref[slot:slot + 1, c0:c0 + width] = jnp.max(st, axis=0, keepdims=True)
            return run
        return [half(c0) for c0 in range(0, tile, width)]

    def update(hh, par, st, st_max, k0, c0, first):
        nk, nq = st.shape
        cols = slice(c0, c0 + nq)
        pv = functools.partial(jnp.dot, va_ref[hh, :, k0:k0 + nk], preferred_element_type=F32)
        if first:
            m_ref[par:par + 1, cols] = st_max
            acc_ref[par, :, cols] = pv(jnp.exp2(st - st_max).astype(BF16))
        else:
            m = m_ref[par:par + 1, cols]
            m_new = jnp.maximum(m, st_max)
            acc_ref[par, :, cols] = (jnp.exp2(m - m_new) * acc_ref[par, :, cols]
                                     + pv(jnp.exp2(st - m_new).astype(BF16)))
            m_ref[par:par + 1, cols] = m_new

    def consume(slot, hh, qi, kb):
        par, k0, s_ref = (hh * ntiles + qi) % 2, kb * tile, s_slots[slot]
        if kb != qi:
            def half(c0):
                return lambda: update(hh, par, s_ref[:, c0:c0 + width], smax_ref[slot:slot + 1, c0:c0 + width],
                                      k0, c0, kb == 0)
            return [half(c0) for c0 in range(0, tile, width)]

        def early():
            st = jnp.where(causal, s_ref[:sub, :sub], NEG_INF)
            update(hh, par, st, jnp.max(st, axis=0, keepdims=True), k0, 0, kb == 0)

        def late():
            st = jnp.concatenate([s_ref[:sub, sub:], jnp.where(causal, s_ref[sub:, sub:], NEG_INF)], axis=0)
            update(hh, par, st, jnp.max(st, axis=0, keepdims=True), k0, sub, kb == 0)
        return [early, late]

    def finalize(hh, qi):
        acc = acc_ref[(hh * ntiles + qi) % 2]
        rows, feat = slice(qi * tile, (qi + 1) * tile), slice(hh * dh, (hh + 1) * dh)
        o = (acc[:dh, :] / acc[dh:dh + 1, :]).T
        o_ref[rows, feat] = (o * sz_ref[rows, feat].astype(F32)).astype(o_ref.dtype)

    blocks = [(hh, qi, kb) for hh in range(heads_per_step) for qi in range(ntiles) for kb in range(qi + 1)]
    for part in produce(0, *blocks[0]):
        part()
    for n, (hh, qi, kb) in enumerate(blocks):
        ahead = produce((n + 1) % 2, *blocks[n + 1]) if n + 1 < len(blocks) else []
        now = consume(n % 2, hh, qi, kb)
        for k in range(max(len(ahead), len(now))):
            for part in ahead[k:k + 1] + now[k:k + 1]:
                part()
        if kb == qi:
            finalize(hh, qi)


def fox_attention(qt, k, vt, cumt2, sz, *, batch, seq, heads, tile=1024, heads_per_step=1, name="fox_attention"):
    width, t = qt.shape
    dh = width // heads
    hps = heads_per_step
    natural = pl.BlockSpec((seq, hps * dh), lambda b, h: (b, h))
    transposed = pl.BlockSpec((hps * dh, seq), lambda b, h: (h, b))
    return pl.pallas_call(
        functools.partial(_fox_attn_kernel, tile=tile, heads_per_step=hps),
        grid=(batch, heads // hps),
        in_specs=[transposed, natural, transposed,
                  pl.BlockSpec((None, hps, 1, seq), lambda b, h: (b, h, 0, 0)),
                  natural],
        out_specs=natural,
        out_shape=jax.ShapeDtypeStruct((t, width), BF16),
        scratch_shapes=[pltpu.VMEM((hps, 2 * dh, seq), BF16), pltpu.VMEM((hps, seq, 2 * dh), BF16),
                        pltpu.VMEM((hps, dh + ATTN_EXTRA_ROWS, seq), BF16),
                        pltpu.VMEM((tile, tile), F32), pltpu.VMEM((tile, tile), F32),
                        pltpu.VMEM((V7X_SUBLANES, tile), F32), pltpu.VMEM((V7X_SUBLANES, tile), F32),
                        pltpu.VMEM((2, dh + ATTN_EXTRA_ROWS, tile), F32)],
        compiler_params=_params("arbitrary", "arbitrary"),
        name=name,
    )(qt, k, vt, cumt2, sz)


def _identity(v):
    return v


def _doubled(v):
    return jnp.concatenate([v, v], axis=-1).astype(F32)


def kernel(x, norm_pre, norm_post, s5_w_in, s5_a_re, s5_a_im, s5_log_dt, s5_b_re, s5_b_im, s5_c_re, s5_c_im,
           s5_d, s5_w_glu, s5_b_glu, s5_w_out, kv_norm, kv_w, kv_b_f, fox_w_in, fox_w_out):
    batch, seq, d_model = x.shape
    t = batch * seq
    heads = kv_b_f.shape[0]
    fox_width = fox_w_in.shape[-1] // 2
    assert norm_pre.shape[0] == 2 and s5_w_in.shape[0] == 1 and fox_w_in.shape[0] == 1
    assert fox_width // heads == FOX_HEAD_DIM and heads <= V7X_LANES
    ngroups = s5_a_re.shape[1]
    chunks_per_seq = seq // SSM_CHUNK
    scan_steps = (chunks_per_seq - 1).bit_length()

    h0 = x.reshape(t, d_model)

    s5_width = s5_w_in.shape[-1] // 2
    ldt2 = jnp.broadcast_to(s5_log_dt[0][:, None], (ngroups, 2 * S5_STATE)).astype(F32)
    b2 = jnp.concatenate([jnp.swapaxes(s5_b_re[0], 1, 2), jnp.swapaxes(s5_b_im[0], 1, 2)], axis=-1)
    c2 = jnp.concatenate([s5_c_re[0], s5_c_im[0]], axis=-1)
    rep = replication_matrix()
    r_all, gst, hst, sr, si, w_in = ssm_prep(_doubled(s5_a_re[0]), _doubled(s5_a_im[0]), ldt2, b2.astype(F32),
                                             c2.astype(F32), rep, replication_matrix(shift=1),
                                             scan_steps=scan_steps,
                                             side=[side_cast(s5_w_in[0], row_scale=norm_pre[0])])
    d_quads = jnp.tile(s5_d[0].reshape(-1, QUADS_PER_SLAB, 1, QUAD_LANES), (1, 1, 1, SSM_CHUNK)).astype(F32)
    kv_cols = 2 * fox_width
    kv_w_t = kv_w.T
    u, sz, w_glu, w_out, w_kv_t = fused_proj(
        h0, [Proj(w_in, False, 0, _identity, F32, False), Proj(w_in, False, s5_width, jax.nn.silu, BF16, False)],
        n=s5_width, tm=1024, name="s5_in_proj",
        side=[side_cast(s5_w_glu[0]), side_cast(s5_w_out[0]), side_cast(kv_w_t, kv_cols, col_scale=kv_norm)])
    yg, w_in_fox, w_out_fox = ssm_apply(u, r_all, gst, hst, sr, si, d_quads, rep, seq=seq,
                                        side=[side_cast(fox_w_in[0], row_scale=norm_pre[1]),
                                              side_cast(fox_w_out[0])])
    h1, h1_bf16, h1_inv_rms = s5_tail(yg, w_glu, s5_b_glu[0][None, :], sz, w_out, h0, norm_post[0][None, :])

    wf_t = jnp.pad((kv_w_t[kv_cols:] * kv_norm[None, :]).astype(BF16), ((0, V7X_LANES - heads), (0, 0)))
    bf = jnp.pad(kv_b_f, (0, V7X_LANES - heads))[:, None].astype(F32)
    scale = FOX_HEAD_DIM ** -0.5 * LOG2_E
    k, vt, qt, sz2, cumt2 = fused_proj(
        h1_bf16,
        [Proj(w_kv_t, True, 0, _identity, BF16, False),
         Proj(w_kv_t, True, fox_width, _identity, BF16, True),
         Proj(w_in_fox, False, 0, lambda v: v * scale, BF16, True),
         Proj(w_in_fox, False, fox_width, jax.nn.silu, BF16, False)],
        n=fox_width, inv=h1_inv_rms, gate=Gate(wf_t, bf, heads, seq), tm=1024, vmem_limit=LARGE_VMEM_LIMIT,
        name="fox_proj")
    o = fox_attention(qt, k, vt, cumt2.reshape(batch, heads, 1, seq), sz2,
                      batch=batch, seq=seq, heads=heads)
    h2 = out_proj(o, w_out_fox, h1, norm_post[1][None, :], name="fox_out_proj")
    return h2.reshape(batch, seq, d_model)
```

```python
import functools
import math
from typing import Callable, NamedTuple

import jax
import jax.numpy as jnp
from jax import lax
from jax.experimental import pallas as pl
from jax.experimental.pallas import tpu as pltpu

F32 = jnp.float32
BF16 = jnp.bfloat16

RMS_EPS = 1e-6
NEG_INF = -1e30
LOG2_E = math.log2(math.e)

S5_GROUP = 16
S5_STATE = 64
FOX_HEAD_DIM = 128

V7X_LANES = 128
V7X_SUBLANES = 8
BF16_ROWS = 2 * V7X_SUBLANES
SSM_CHUNK = 16
GROUPS_PER_SLAB = V7X_LANES // S5_GROUP
QUADS_PER_SLAB = 2
QUAD_LANES = V7X_LANES // QUADS_PER_SLAB
GROUPS_PER_QUAD = GROUPS_PER_SLAB // QUADS_PER_SLAB
STEPS_PER_TILE = 2 * V7X_LANES // QUAD_LANES
V7X_VMEM_BYTES = 64 * 1024 * 1024
VMEM_LIMIT = V7X_VMEM_BYTES * 3 // 4
LARGE_VMEM_LIMIT = V7X_VMEM_BYTES * 7 // 8


def _params(*semantics):
    return pltpu.CompilerParams(dimension_semantics=semantics, vmem_limit_bytes=VMEM_LIMIT)


def _rms_scale(x, gain):
    ms = jnp.mean(x * x, axis=-1, keepdims=True)
    return x * lax.rsqrt(ms + RMS_EPS) * gain


class Proj(NamedTuple):
    weight: jax.Array
    weight_is_t: bool
    first: int
    act: Callable
    dtype: jnp.dtype
    transposed: bool


def _log_sigmoid(x):
    return -(jnp.maximum(-x, 0.0) + jnp.log1p(jnp.exp(-jnp.abs(x))))


def _cumsum_lanes(c):
    lanes = c.shape[1]
    lane = lax.broadcasted_iota(jnp.int32, c.shape, 1)
    d = 1
    while d < lanes:
        c = c + jnp.where(lane >= d, pltpu.roll(c, d, axis=1), 0.0)
        d *= 2
    return c


class SideCast(NamedTuple):
    src: jax.Array
    rows_used: int
    scale: jax.Array


def side_cast(src, rows_used=None, *, row_scale=None, col_scale=None):
    rows_used = src.shape[0] if rows_used is None else rows_used
    if row_scale is not None:
        scale = row_scale.reshape(rows_used, 1)
    elif col_scale is not None:
        scale = col_scale.reshape(1, src.shape[1])
    else:
        scale = jnp.ones((1, src.shape[1]), F32)
    return SideCast(src, rows_used, scale.astype(F32))


def _side_cast_specs(casts, steps, flat_step):
    in_specs, out_specs, out_shapes, operands = [], [], [], []
    for c in casts:
        block = (c.rows_used // steps, c.src.shape[1])
        by_step = lambda *idx: (flat_step(*idx), 0)
        in_specs.append(pl.BlockSpec(block, by_step))
        if c.scale.shape[0] == 1:
            in_specs.append(pl.BlockSpec(c.scale.shape, lambda *idx: (0, 0)))
        else:
            in_specs.append(pl.BlockSpec((block[0], 1), by_step))
        operands += [c.src, c.scale]
        out_specs.append(pl.BlockSpec(block, by_step))
        out_shapes.append(jax.ShapeDtypeStruct((c.rows_used, c.src.shape[1]), BF16))
    return in_specs, out_specs, out_shapes, operands


def _do_side_casts(in_refs, dst_refs):
    for n, dst_ref in enumerate(dst_refs):
        dst_ref[...] = (in_refs[2 * n][...] * in_refs[2 * n + 1][...]).astype(BF16)


def _inv_rms(x):
    inv = lax.rsqrt(jnp.mean(x * x, axis=-1, keepdims=True) + RMS_EPS)
    return jnp.broadcast_to(inv, (x.shape[0], V7X_LANES))


def _fused_proj_kernel(*refs, projs, gate, nside, prenormed):
    nproj = len(projs)
    x_ref = refs[0]
    pos = 1
    if prenormed:
        inv_in_ref = refs[1]
        pos = 2
    w_refs = refs[pos:pos + nproj]
    pos += nproj
    if gate is not None:
        wf_ref, bf_ref = refs[pos:pos + 2]
        pos += 2
    side_in = refs[pos:pos + 2 * nside]
    pos += 2 * nside
    o_refs = refs[pos:pos + nproj]
    pos += nproj
    if gate is not None:
        cumt_ref = refs[pos]
        pos += 1
    _do_side_casts(side_in, refs[pos:pos + nside])
    pos += nside
    if prenormed:
        xb_ref, inv_ref = x_ref, inv_in_ref
    else:
        xb_ref, inv_ref = refs[pos:pos + 2]
        pos += 2
    invt_ref = refs[pos]
    pos += 1
    nt = (((1,), (1,)), ((), ()))
    any_transposed = any(p.transposed for p in projs)

    @pl.when(pl.program_id(1) == 0)
    def _():
        if not prenormed:
            x = x_ref[...]
            xb_ref[...] = x.astype(BF16)
            inv_ref[...] = _inv_rms(x)
        if any_transposed:
            invt_ref[...] = inv_ref[...].T[:V7X_SUBLANES, :]

    gate_host = None
    if gate is not None:
        carry_ref = refs[pos]
        gate_host = [p.weight_is_t and p.transposed for p in projs].index(True)
    logit_t = None
    xb = xb_ref[...]
    for n, (p, w_ref, o_ref) in enumerate(zip(projs, w_refs, o_refs)):
        if n == gate_host:
            lhs = jnp.concatenate([w_ref[...], wf_ref[:carry_ref.shape[0], :]], axis=0)
            y = lax.dot_general(lhs, xb, nt, preferred_element_type=F32) * invt_ref[0:1, :]
            logit_t = y[w_ref.shape[0]:, :]
            o_ref[...] = p.act(y[:w_ref.shape[0], :]).astype(o_ref.dtype)
            continue
        if p.weight_is_t and p.transposed:
            y = lax.dot_general(w_ref[...], xb, nt, preferred_element_type=F32)
        elif p.weight_is_t:
            y = lax.dot_general(xb, w_ref[...], nt, preferred_element_type=F32)
        else:
            y = jnp.dot(xb, w_ref[...], preferred_element_type=F32)
            y = y.T if p.transposed else y
        if p.transposed:
            y = y * invt_ref[0:1, :]
        else:
            y = y * jnp.concatenate([inv_ref[...]] * (y.shape[1] // V7X_LANES), axis=1)
        o_ref[...] = p.act(y).astype(o_ref.dtype)

    if gate is not None:
        log_f = _log_sigmoid(logit_t + bf_ref[:carry_ref.shape[0], :])
        c = _cumsum_lanes(log_f)
        total = jnp.sum(log_f, axis=1, keepdims=True)

        @pl.when(pl.program_id(1) == 0)
        def _():
            @pl.when(pl.program_id(0) % gate.tiles_per_seq == 0)
            def _():
                carry_ref[...] = jnp.zeros_like(carry_ref)

            cumt_ref[...] = ((c + carry_ref[:, 0:1]) * LOG2_E)[:gate.heads, :]
            carry_ref[...] += jnp.broadcast_to(total, carry_ref.shape)


class Gate(NamedTuple):
    wf_t: jax.Array
    bias: jax.Array
    heads: int
    seq: int
    tiles_per_seq: int = 0


def fused_proj(x, projs, *, n, inv=None, gate=None, side=(), tm=512, tn=512, vmem_limit=VMEM_LIMIT,
               name="fused_proj"):
    t, d = x.shape
    prenormed = inv is not None
    nj = n // tn
    w_specs, out_specs, out_shapes = [], [], []
    for p in projs:
        if p.weight_is_t:
            w_specs.append(pl.BlockSpec((tn, d), lambda i, j, b0=p.first // tn: (j + b0, 0)))
        else:
            w_specs.append(pl.BlockSpec((d, tn), lambda i, j, b0=p.first // tn: (0, j + b0)))
        if p.transposed:
            out_specs.append(pl.BlockSpec((tn, tm), lambda i, j: (j, i)))
            out_shapes.append(jax.ShapeDtypeStruct((n, t), p.dtype))
        else:
            out_specs.append(pl.BlockSpec((tm, tn), lambda i, j: (i, j)))
            out_shapes.append(jax.ShapeDtypeStruct((t, n), p.dtype))
    operands = [x] + ([inv] if prenormed else []) + [p.weight for p in projs]
    in_specs = [pl.BlockSpec((tm, d), lambda i, j: (i, 0))]
    if prenormed:
        in_specs.append(pl.BlockSpec((tm, V7X_LANES), lambda i, j: (i, 0)))
    in_specs += w_specs
    scratch = [] if prenormed else [pltpu.VMEM((tm, d), BF16), pltpu.VMEM((tm, V7X_LANES), F32)]
    scratch.append(pltpu.VMEM((V7X_SUBLANES, tm), F32))
    kernel_projs = tuple(p._replace(weight=None) for p in projs)
    kernel_gate = None
    if gate is not None:
        tps = gate.seq // tm
        kernel_gate = gate._replace(wf_t=None, bias=None, tiles_per_seq=tps)
        operands += [gate.wf_t, gate.bias]
        in_specs += [pl.BlockSpec((V7X_LANES, d), lambda i, j: (0, 0)), pl.BlockSpec((V7X_LANES, 1), lambda i, j: (0, 0))]
        out_specs.append(pl.BlockSpec((None, gate.heads, tm), lambda i, j: (i // tps, 0, i % tps)))
        out_shapes.append(jax.ShapeDtypeStruct((t // gate.seq, gate.heads, gate.seq), F32))
        head_rows = -(-gate.heads // BF16_ROWS) * BF16_ROWS
        scratch.append(pltpu.VMEM((head_rows, V7X_LANES), F32))
    side_in, side_out, side_shapes, side_ops = _side_cast_specs(side, (t // tm) * nj, lambda i, j: i * nj + j)
    operands += side_ops
    in_specs += side_in
    out_specs += side_out
    out_shapes += side_shapes
    return pl.pallas_call(
        functools.partial(_fused_proj_kernel, projs=kernel_projs, gate=kernel_gate, nside=len(side),
                          prenormed=prenormed),
        grid=(t // tm, nj),
        in_specs=in_specs,
        out_specs=out_specs,
        out_shape=out_shapes,
        scratch_shapes=scratch,
        compiler_params=pltpu.CompilerParams(dimension_semantics=("arbitrary", "arbitrary"),
                                             vmem_limit_bytes=vmem_limit),
        name=name,
    )(*operands)


def _out_proj_kernel(a_ref, w_ref, res_ref, gain_ref, o_ref):
    y = jnp.dot(a_ref[...], w_ref[...], preferred_element_type=F32)
    o_ref[...] = res_ref[...] + _rms_scale(y, gain_ref[...])


def out_proj(a, w, res, gain, *, tm=512, name="out_proj"):
    t, k = a.shape
    d = w.shape[1]
    return pl.pallas_call(
        _out_proj_kernel,
        grid=(t // tm,),
        in_specs=[
            pl.BlockSpec((tm, k), lambda i: (i, 0)),
            pl.BlockSpec((k, d), lambda i: (0, 0), pipeline_mode=pl.Buffered(1)),
            pl.BlockSpec((tm, d), lambda i: (i, 0)),
            pl.BlockSpec((1, d), lambda i: (0, 0)),
        ],
        out_specs=pl.BlockSpec((tm, d), lambda i: (i, 0)),
        out_shape=jax.ShapeDtypeStruct((t, d), F32),
        compiler_params=_params("parallel"),
        name=name,
    )(a, w, res, gain)


def _s5_tail_kernel(yg_ref, wg_ref, b_ref, sz_ref, wo_ref, res_ref, gain_ref, o_ref, ob_ref, inv_ref, *, halves):
    rows = yg_ref.shape[0] // halves
    for h in range(halves):
        r = slice(h * rows, (h + 1) * rows)
        yg = yg_ref[r, :]
        gate = jax.nn.sigmoid(jnp.dot(yg.astype(BF16), wg_ref[...], preferred_element_type=F32) + b_ref[...])
        y3 = (yg * gate * sz_ref[r, :].astype(F32)).astype(BF16)
        y = jnp.dot(y3, wo_ref[...], preferred_element_type=F32)
        out = res_ref[r, :] + _rms_scale(y, gain_ref[...])
        o_ref[r, :] = out
        ob_ref[r, :] = out.astype(BF16)
        inv_ref[r, :] = _inv_rms(out)


def s5_tail(yg, w_glu, b_glu, sz, w_out, res, gain, *, tm=512, halves=2, name="s5_tail"):
    t, k = yg.shape
    d = w_out.shape[1]
    rows = pl.BlockSpec((tm, k), lambda i: (i, 0))
    resident = functools.partial(pl.BlockSpec, index_map=lambda i: (0, 0), pipeline_mode=pl.Buffered(1))
    return pl.pallas_call(
        functools.partial(_s5_tail_kernel, halves=halves),
        grid=(t // tm,),
        in_specs=[rows, resident((k, k)), resident((1, k)), rows, resident((k, d)),
                  pl.BlockSpec((tm, d), lambda i: (i, 0)), resident((1, d))],
        out_specs=[pl.BlockSpec((tm, d), lambda i: (i, 0)), pl.BlockSpec((tm, d), lambda i: (i, 0)),
                   pl.BlockSpec((tm, V7X_LANES), lambda i: (i, 0))],
        out_shape=[jax.ShapeDtypeStruct((t, d), F32), jax.ShapeDtypeStruct((t, d), BF16),
                   jax.ShapeDtypeStruct((t, V7X_LANES), F32)],
        compiler_params=pltpu.CompilerParams(dimension_semantics=("parallel",),
                                             vmem_limit_bytes=LARGE_VMEM_LIMIT),
        name=name,
    )(yg, w_glu, b_glu, sz, w_out, res, gain)


def _ssm_prep_kernel(are_ref, aim_ref, ldt_ref, b2_ref, c2_ref, rep_ref, repb_ref, *rest, groups, scan_steps):
    nside = (len(rest) - 5) // 3
    r_ref, g_ref, h_ref, sr_ref, si_ref = rest[2 * nside:2 * nside + 5]
    _do_side_casts(rest[:2 * nside], rest[2 * nside + 5:])
    lane = lax.broadcasted_iota(jnp.int32, (1, 2 * S5_STATE), 1)
    minus_plus = jnp.where(lane < S5_STATE, -1.0, 1.0).astype(F32)
    gw = S5_GROUP
    ns = 2 * S5_STATE
    krows = []

    ar = are_ref[...]
    ai = aim_ref[...]
    dt = jnp.exp(ldt_ref[...])
    mag = jnp.exp(ar * dt)
    lam_r = mag * jnp.cos(ai * dt)
    lam_i = mag * jnp.sin(ai * dt)
    den = ar * ar + ai * ai
    nr = lam_r - 1.0
    coef_r_all = (nr * ar + lam_i * ai) / den
    coef_i_all = (lam_i * ar - nr * ai) / den
    pow_r_all = [jnp.ones_like(lam_r)]
    pow_i_all = [jnp.zeros_like(lam_r)]
    for _ in range(SSM_CHUNK):
        pr, pi = pow_r_all[-1], pow_i_all[-1]
        pow_r_all.append(pr * lam_r - pi * lam_i)
        pow_i_all.append(pr * lam_i + pi * lam_r)
    mu_r, mu_i = pow_r_all[SSM_CHUNK], pow_i_all[SSM_CHUNK]
    scan_r_all, scan_i_all = [], []
    for _ in range(scan_steps):
        scan_r_all.append(mu_r)
        scan_i_all.append(mu_i)
        mu_r, mu_i = mu_r * mu_r - mu_i * mu_i, 2.0 * mu_r * mu_i

    for gi in range(groups):
        grp = slice(gi, gi + 1)
        coef_r, coef_i = coef_r_all[grp], coef_i_all[grp]
        pow_r = [p[grp] for p in pow_r_all]
        pow_i = [p[grp] for p in pow_i_all]

        b2 = b2_ref[gi]
        b2s = pltpu.roll(b2, S5_STATE, axis=1) * minus_plus
        c2 = c2_ref[gi]
        c2a = c2 * (-minus_plus)
        c2b = -pltpu.roll(c2, S5_STATE, axis=1)

        pair, second = divmod(gi, 2)
        low = lane < S5_STATE

        for step in range(SSM_CHUNK):
            pr, pi = pow_r[SSM_CHUNK - 1 - step], pow_i[SSM_CHUNK - 1 - step]
            wr = pr * coef_r - pi * coef_i
            wi = pr * coef_i + pi * coef_r
            g_ri = wr * b2 + wi * b2s
            g_ir = pltpu.roll(g_ri, S5_STATE, axis=1)
            if second:
                re_cols, im_cols = jnp.where(low, 0.0, g_ir), jnp.where(low, 0.0, g_ri)
            else:
                re_cols, im_cols = jnp.where(low, g_ri, 0.0), jnp.where(low, g_ir, 0.0)
            g_ref[step, gi * gw:(gi + 1) * gw, :] = jnp.concatenate([re_cols, im_cols], axis=1).astype(g_ref.dtype)

        cl = [pow_r[tau] * c2a + pow_i[tau] * c2b for tau in range(SSM_CHUNK + 1)]
        h_t = jnp.concatenate(cl[1:], axis=0)
        h_rows = h_t.T.astype(h_ref.dtype)
        base = pair * 2 * ns + second * S5_STATE
        h_ref[base:base + S5_STATE, :] = h_rows[:S5_STATE]
        h_ref[base + ns:base + ns + S5_STATE, :] = h_rows[S5_STATE:]

        bbar2 = coef_r * b2 + coef_i * b2s
        cl_all = jnp.concatenate(cl[:-1], axis=0)
        krows.append(lax.dot_general(bbar2, cl_all, (((1,), (1,)), ((), ())),
                                     precision=lax.Precision.HIGHEST,
                                     preferred_element_type=F32))

        if second:
            first_grp = slice(gi - 1, gi)
            pad = [jnp.zeros_like(coef_r)] * (sr_ref.shape[1] - scan_steps)
            sr_ref[pair] = jnp.concatenate([jnp.where(low, s[first_grp], s[grp]) for s in scan_r_all] + pad, axis=0)
            si_ref[pair] = jnp.concatenate([jnp.where(low, s[first_grp], s[grp]) for s in scan_i_all] + pad, axis=0)

    for q in range(QUADS_PER_SLAB):
        kst = jnp.concatenate(krows[q * GROUPS_PER_QUAD:(q + 1) * GROUPS_PER_QUAD], axis=0).astype(BF16)
        units = []
        for rep in (rep_ref, repb_ref):
            spread = jnp.dot(kst, rep[...], preferred_element_type=F32)
            rows_g = lax.broadcasted_iota(jnp.int32, spread.shape, 0) // gw
            cols_g = (lax.broadcasted_iota(jnp.int32, spread.shape, 1) % QUAD_LANES) // gw
            units.append(jnp.where(rows_g == cols_g, spread, 0.0).astype(r_ref.dtype))
        blocks = [jnp.concatenate([s[:, e * V7X_LANES:(e + 1) * V7X_LANES] for s in units], axis=0)
                  for e in range(SSM_CHUNK // 2)]
        zero = jnp.zeros_like(blocks[0])
        ntile = SSM_CHUNK // STEPS_PER_TILE
        for d in range(ntile):
            base = (ntile - 1 - d) * 2 * V7X_LANES
            top = jnp.concatenate([blocks[2 * d], blocks[2 * d + 1]], axis=1)
            bottom = jnp.concatenate([blocks[2 * d - 1] if d else zero, blocks[2 * d]], axis=1)
            r_ref[q, base:base + V7X_LANES, :] = top
            r_ref[q, base + V7X_LANES:base + 2 * V7X_LANES, :] = bottom


def ssm_prep(a_re2, a_im2, log_dt2, b2, c2, rep, rep_shifted, *, scan_steps, side=(), name="ssm_prep"):
    ng = a_re2.shape[0]
    gb = GROUPS_PER_SLAB
    nslab = ng // gb
    cat = SSM_CHUNK * QUAD_LANES
    ns = 2 * S5_STATE
    rows = -(-scan_steps // V7X_SUBLANES) * V7X_SUBLANES
    vec = pl.BlockSpec((gb, ns), lambda i: (i, 0))
    mat = pl.BlockSpec((gb, S5_GROUP, ns), lambda i: (i, 0, 0))
    side_in, side_out, side_shapes, side_ops = _side_cast_specs(side, nslab, lambda i: i)
    return pl.pallas_call(
        functools.partial(_ssm_prep_kernel, groups=gb, scan_steps=scan_steps),
        grid=(nslab,),
        in_specs=[vec, vec, vec, mat, mat, pl.BlockSpec(rep.shape, lambda i: (0, 0)),
                  pl.BlockSpec(rep.shape, lambda i: (0, 0))] + side_in,
        out_specs=[
            pl.BlockSpec((None, QUADS_PER_SLAB, cat, 2 * V7X_LANES), lambda i: (i, 0, 0, 0)),
            pl.BlockSpec((None, SSM_CHUNK, V7X_LANES, 2 * ns), lambda i: (i, 0, 0, 0)),
            pl.BlockSpec((None, gb * ns, SSM_CHUNK * S5_GROUP), lambda i: (i, 0, 0)),
            pl.BlockSpec((gb // 2, rows, ns), lambda i: (i, 0, 0)),
            pl.BlockSpec((gb // 2, rows, ns), lambda i: (i, 0, 0)),
        ] + side_out,
        out_shape=[
            jax.ShapeDtypeStruct((nslab, QUADS_PER_SLAB, cat, 2 * V7X_LANES), BF16),
            jax.ShapeDtypeStruct((nslab, SSM_CHUNK, V7X_LANES, 2 * ns), BF16),
            jax.ShapeDtypeStruct((nslab, gb * ns, SSM_CHUNK * S5_GROUP), BF16),
            jax.ShapeDtypeStruct((ng // 2, rows, ns), F32),
            jax.ShapeDtypeStruct((ng // 2, rows, ns), F32),
        ] + side_shapes,
        compiler_params=_params("parallel"),
        name=name,
    )(a_re2, a_im2, log_dt2, b2, c2, rep, rep_shifted, *side_ops)


def replication_matrix(shift=0):
    src = jnp.arange(SSM_CHUNK * S5_GROUP)
    dst = jnp.arange(SSM_CHUNK * QUAD_LANES)
    same_step = (src[:, None] // S5_GROUP + shift) == (dst[None, :] // QUAD_LANES)
    same_chan = (src[:, None] % S5_GROUP) == (dst[None, :] % S5_GROUP)
    return (same_step & same_chan).astype(BF16)


def _ssm_kernel(u_ref, r_ref, gst_ref, hst_ref, sr_ref, si_ref, d_ref, rep_ref, *rest,
                chunks_per_seq, scan_steps, nside):
    y_ref = rest[2 * nside]
    _do_side_casts(rest[:2 * nside], rest[2 * nside + 1:3 * nside + 1])
    ucat_ref, gexp_ref, hexp_ref, hprev_ref, yq_ref = rest[3 * nside + 1:]
    t = u_ref.shape[0]
    nchunk = t // SSM_CHUNK
    ns = 2 * S5_STATE
    tile = 2 * V7X_LANES
    ntile = SSM_CHUNK // STEPS_PER_TILE
    qs = GROUPS_PER_QUAD * ns
    first_half = lax.broadcasted_iota(jnp.int32, (nchunk, V7X_LANES), 1) < QUAD_LANES

    def swap_halves(x):
        return pltpu.roll(x, QUAD_LANES, axis=1)

    for pair in range(SSM_CHUNK // 2):
        even = u_ref[pl.ds(2 * pair, nchunk, stride=SSM_CHUNK), :]
        odd = u_ref[pl.ds(2 * pair + 1, nchunk, stride=SSM_CHUNK), :]
        lanes = slice(pair * V7X_LANES, (pair + 1) * V7X_LANES)
        ucat_ref[0, :, lanes] = jnp.where(first_half, even, swap_halves(odd)).astype(BF16)
        ucat_ref[1, :, lanes] = jnp.where(first_half, swap_halves(even), odd).astype(BF16)

    pair_cols = 2 * ns
    pairs = GROUPS_PER_QUAD // 2
    g_rows = lax.broadcasted_iota(jnp.int32, (QUAD_LANES, qs), 0) // (2 * S5_GROUP)
    g_cols = lax.broadcasted_iota(jnp.int32, (QUAD_LANES, qs), 1) // pair_cols
    h_row = lax.broadcasted_iota(jnp.int32, (qs, tile), 0)
    h_rows = (h_row // pair_cols) * 2 + (h_row % ns) // S5_STATE
    h_cols = (lax.broadcasted_iota(jnp.int32, (qs, tile), 1) % QUAD_LANES) // S5_GROUP
    for q in range(QUADS_PER_SLAB):
        for step in range(SSM_CHUNK):
            compact = gst_ref[step, q * QUAD_LANES:(q + 1) * QUAD_LANES, :]
            tiled = jnp.concatenate([compact] * pairs, axis=1)
            gexp_ref[q, step * QUAD_LANES:(step + 1) * QUAD_LANES, :] = jnp.where(
                g_rows == g_cols, tiled, 0.0).astype(BF16)
        for a in range(ntile):
            spread = jnp.dot(hst_ref[q * qs:(q + 1) * qs, :], rep_ref[:, a * tile:(a + 1) * tile],
                             preferred_element_type=F32)
            hexp_ref[q, :, a * tile:(a + 1) * tile] = jnp.where(h_rows == h_cols, spread, 0.0).astype(BF16)

    hs_quads = [jnp.dot(ucat_ref[q], gexp_ref[q], preferred_element_type=F32) for q in range(QUADS_PER_SLAB)]
    for q in range(QUADS_PER_SLAB):
        for a in range(ntile):
            yq_ref[q, :, a * tile:(a + 1) * tile] = jnp.dot(
                ucat_ref[q, :, :(a + 1) * tile], r_ref[q, (ntile - 1 - a) * tile:, :], preferred_element_type=F32)

    row = lax.broadcasted_iota(jnp.int32, (nchunk, ns), 0) % chunks_per_seq

    def shifted(x, d):
        return jnp.where(row >= d, pltpu.roll(x, d, axis=0), 0.0)

    for p in range(GROUPS_PER_SLAB // 2):
        q, p2 = divmod(p, pairs)
        re = hs_quads[q][:, p2 * pair_cols:p2 * pair_cols + ns]
        im = hs_quads[q][:, p2 * pair_cols + ns:(p2 + 1) * pair_cols]
        for k in range(scan_steps):
            mr, mi = sr_ref[p, k:k + 1, :], si_ref[p, k:k + 1, :]
            d = 1 << k
            if d % V7X_SUBLANES:
                sre, sim = shifted(re, d), shifted(im, d)
                re, im = re + mr * sre - mi * sim, im + mr * sim + mi * sre
                continue
            re_parts, im_parts = [], []
            for s0 in range(0, nchunk, chunks_per_seq):
                r_seq, i_seq = re[s0:s0 + chunks_per_seq], im[s0:s0 + chunks_per_seq]
                r_src, i_src = r_seq[:chunks_per_seq - d], i_seq[:chunks_per_seq - d]
                re_parts += [r_seq[:d], r_seq[d:] + mr * r_src - mi * i_src]
                im_parts += [i_seq[:d], i_seq[d:] + mr * i_src + mi * r_src]
            re, im = jnp.concatenate(re_parts, axis=0), jnp.concatenate(im_parts, axis=0)
        hprev_ref[q, :, p2 * pair_cols:p2 * pair_cols + ns] = shifted(re, 1).astype(BF16)
        hprev_ref[q, :, p2 * pair_cols + ns:(p2 + 1) * pair_cols] = shifted(im, 1).astype(BF16)

    for q in range(QUADS_PER_SLAB):
        for a in range(ntile):
            lanes = slice(a * tile, (a + 1) * tile)
            y = (yq_ref[q, :, lanes]
                 + jnp.dot(hprev_ref[q], hexp_ref[q, :, lanes], preferred_element_type=F32)
                 + d_ref[q, :, lanes] * ucat_ref[q, :, lanes].astype(F32))
            yq_ref[q, :, lanes] = jax.nn.gelu(y, approximate=True)

    for pair in range(SSM_CHUNK // 2):
        lanes = slice(pair * V7X_LANES, (pair + 1) * V7X_LANES)
        y0, y1 = yq_ref[0, :, lanes], yq_ref[1, :, lanes]
        y_ref[pl.ds(2 * pair, nchunk, stride=SSM_CHUNK), :] = jnp.where(first_half, y0, swap_halves(y1))
        y_ref[pl.ds(2 * pair + 1, nchunk, stride=SSM_CHUNK), :] = jnp.where(first_half, swap_halves(y0), y1)


def ssm_apply(u, r_all, gst, hst, sr, si, d_quads, rep, *, seq, side=(), name="s5_ssm"):
    t, w = u.shape
    nslab = w // V7X_LANES
    side_in, side_out, side_shapes, side_ops = _side_cast_specs(side, nslab, lambda s: s)
    nchunk = t // SSM_CHUNK
    cat = SSM_CHUNK * QUAD_LANES
    ns = 2 * S5_STATE
    qs = GROUPS_PER_QUAD * ns
    nq = QUADS_PER_SLAB
    chunks_per_seq = seq // SSM_CHUNK
    scan_steps = (chunks_per_seq - 1).bit_length()
    gb = GROUPS_PER_SLAB
    rows = sr.shape[1]
    return pl.pallas_call(
        functools.partial(_ssm_kernel, chunks_per_seq=chunks_per_seq, scan_steps=scan_steps, nside=len(side)),
        grid=(nslab,),
        in_specs=[
            pl.BlockSpec((t, V7X_LANES), lambda s: (0, s)),
            pl.BlockSpec((None, nq, cat, 2 * V7X_LANES), lambda s: (s, 0, 0, 0)),
            pl.BlockSpec((None, SSM_CHUNK, V7X_LANES, 2 * ns), lambda s: (s, 0, 0, 0)),
            pl.BlockSpec((None, gb * ns, SSM_CHUNK * S5_GROUP), lambda s: (s, 0, 0)),
            pl.BlockSpec((gb // 2, rows, ns), lambda s: (s, 0, 0)),
            pl.BlockSpec((gb // 2, rows, ns), lambda s: (s, 0, 0)),
            pl.BlockSpec((None, nq, 1, cat), lambda s: (s, 0, 0, 0)),
            pl.BlockSpec((SSM_CHUNK * S5_GROUP, cat), lambda s: (0, 0)),
        ] + side_in,
        out_specs=[pl.BlockSpec((t, V7X_LANES), lambda s: (0, s))] + side_out,
        out_shape=[jax.ShapeDtypeStruct((t, w), F32)] + side_shapes,
        scratch_shapes=[pltpu.VMEM((nq, nchunk, cat), BF16), pltpu.VMEM((nq, cat, qs), BF16),
                        pltpu.VMEM((nq, qs, cat), BF16), pltpu.VMEM((nq, nchunk, qs), BF16),
                        pltpu.VMEM((nq, nchunk, cat), F32)],
        compiler_params=pltpu.CompilerParams(dimension_semantics=("parallel",),
                                             vmem_limit_bytes=LARGE_VMEM_LIMIT),
        name=name,
    )(u, r_all, gst, hst, sr, si, d_quads, rep, *side_ops)


ATTN_SPLIT = 3
ATTN_EXTRA_ROWS = 16


def _split3(c):
    hi = c.astype(BF16).astype(F32)
    mid = (c - hi).astype(BF16).astype(F32)
    return hi, mid, c - hi - mid


def _fox_attn_kernel(qt_ref, k_ref, vt_ref, ct_ref, sz_ref, o_ref,
                     qa_ref, ka_ref, va_ref, s0_ref, s1_ref, smax_ref, m_ref, acc_ref, *, tile, heads_per_step):
    seq = qt_ref.shape[1]
    dh = qt_ref.shape[0] // heads_per_step
    ntiles = seq // tile
    ext = ATTN_EXTRA_ROWS

    @pl.when((pl.program_id(0) == 0) & (pl.program_id(1) == 0))
    def _():
        rowv = lax.broadcasted_iota(jnp.int32, (ext, seq), 0)
        for hh in range(heads_per_step):
            qa_ref[hh, dh + ext:, :] = jnp.zeros((dh - ext, seq), BF16)
            va_ref[hh, dh:, :] = jnp.where(rowv == 0, 1.0, 0.0).astype(BF16)

    row16 = lax.broadcasted_iota(jnp.int32, (ext, seq), 0)
    rowk = lax.broadcasted_iota(jnp.int32, (dh, tile), 0)
    for hh in range(heads_per_step):
        feat = slice(hh * dh, (hh + 1) * dh)
        parts = _split3(ct_ref[hh])
        qa_ref[hh, :dh, :] = qt_ref[feat, :]
        ext_q = jnp.where(row16 < 2 * ATTN_SPLIT, 1.0, 0.0)
        for n, part in enumerate(parts):
            ext_q = jnp.where(row16 == n, part, ext_q)
        qa_ref[hh, dh:dh + ext, :] = ext_q.astype(BF16)
        va_ref[hh, :dh, :] = vt_ref[feat, :]
        for r in range(ntiles):
            cols = slice(r * tile, (r + 1) * tile)
            ext_kt = jnp.where(rowk < ATTN_SPLIT, 1.0, 0.0)
            for n, part in enumerate(parts):
                ext_kt = jnp.where(rowk == ATTN_SPLIT + n, -part[:, cols], ext_kt)
            ka_ref[hh, cols, :dh] = k_ref[cols, feat]
            ka_ref[hh, cols, dh:] = ext_kt.T.astype(BF16)

    sub = tile // 2
    causal = (lax.broadcasted_iota(jnp.int32, (sub, sub), 0) <= lax.broadcasted_iota(jnp.int32, (sub, sub), 1))
    s_slots = (s0_ref, s1_ref)
    width = sub

    def scores(hh, k0, nk, q0, nq):
        return jnp.dot(ka_ref[hh, k0:k0 + nk, :], qa_ref[hh, :, q0:q0 + nq], preferred_element_type=F32)

    def produce(slot, hh, qi, kb):
        k0, q0 = kb * tile, qi * tile
        s_ref = s_slots[slot]
        if kb == qi:
            def top():
                s_ref[:sub, :] = scores(hh, k0, sub, q0, tile)

            def bottom():
                s_ref[sub:, sub:] = scores(hh, k0 + sub, sub, q0 + sub, sub)
            return [top, bottom]

        def half(c0):
            def run():
                st = scores(hh, k0, tile, q0 + c0, width)
                s_ref[:, c0:c0 + width] = st
                smax_ref[slot:slot + 1, c0:c0 + width] = jnp.max(st, axis=0, keepdims=True)
            return run
        return [half(c0) for c0 in range(0, tile, width)]

    def update(hh, par, st, st_max, k0, c0, first):
        nk, nq = st.shape
        cols = slice(c0, c0 + nq)
        pv = functools.partial(jnp.dot, va_ref[hh, :, k0:k0 + nk], preferred_element_type=F32)
        if first:
            m_ref[par:par + 1, cols] = st_max
            acc_ref[par, :, cols] = pv(jnp.exp2(st - st_max).astype(BF16))
        else:
            m = m_ref[par:par + 1, cols]
            m_new = jnp.maximum(m, st_max)
            acc_ref[par, :, cols] = (jnp.exp2(m - m_new) * acc_ref[par, :, cols]
                                     + pv(jnp.exp2(st - m_new).astype(BF16)))
            m_ref[par:par + 1, cols] = m_new

    def consume(slot, hh, qi, kb):
        par, k0, s_ref = (hh * ntiles + qi) % 2, kb * tile, s_slots[slot]
        if kb != qi:
            def half(c0):
                return lambda: update(hh, par, s_ref[:, c0:c0 + width], smax_ref[slot:slot + 1, c0:c0 + width],
                                      k0, c0, kb == 0)
            return [half(c0) for c0 in range(0, tile, width)]

        def early():
            st = jnp.where(causal, s_ref[:sub, :sub], NEG_INF)
            update(hh, par, st, jnp.max(st, axis=0, keepdims=True), k0, 0, kb == 0)

        def late():
            st = jnp.concatenate([s_ref[:sub, sub:], jnp.where(causal, s_ref[sub:, sub:], NEG_INF)], axis=0)
            update(hh, par, st, jnp.max(st, axis=0, keepdims=True), k0, sub, kb == 0)
        return [early, late]

    def finalize(hh, qi):
        acc = acc_ref[(hh * ntiles + qi) % 2]
        rows, feat = slice(qi * tile, (qi + 1) * tile), slice(hh * dh, (hh + 1) * dh)
        o = (acc[:dh, :] / acc[dh:dh + 1, :]).T
        o_ref[rows, feat] = (o * sz_ref[rows, feat].astype(F32)).astype(o_ref.dtype)

    blocks = [(hh, qi, kb) for hh in range(heads_per_step) for qi in range(ntiles) for kb in range(qi + 1)]
    for part in produce(0, *blocks[0]):
        part()
    for n, (hh, qi, kb) in enumerate(blocks):
        ahead = produce((n + 1) % 2, *blocks[n + 1]) if n + 1 < len(blocks) else []
        now = consume(n % 2, hh, qi, kb)
        for k in range(max(len(ahead), len(now))):
            for part in ahead[k:k + 1] + now[k:k + 1]:
                part()
        if kb == qi:
            finalize(hh, qi)


def fox_attention(qt, k, vt, cumt2, sz, *, batch, seq, heads, tile=1024, heads_per_step=1, name="fox_attention"):
    width, t = qt.shape
    dh = width // heads
    hps = heads_per_step
    natural = pl.BlockSpec((seq, hps * dh), lambda b, h: (b, h))
    transposed = pl.BlockSpec((hps * dh, seq), lambda b, h: (h, b))
    return pl.pallas_call(
        functools.partial(_fox_attn_kernel, tile=tile, heads_per_step=hps),
        grid=(batch, heads // hps),
        in_specs=[transposed, natural, transposed,
                  pl.BlockSpec((None, hps, 1, seq), lambda b, h: (b, h, 0, 0)),
                  natural],
        out_specs=natural,
        out_shape=jax.ShapeDtypeStruct((t, width), BF16),
        scratch_shapes=[pltpu.VMEM((hps, 2 * dh, seq), BF16), pltpu.VMEM((hps, seq, 2 * dh), BF16),
                        pltpu.VMEM((hps, dh + ATTN_EXTRA_ROWS, seq), BF16),
                        pltpu.VMEM((tile, tile), F32), pltpu.VMEM((tile, tile), F32),
                        pltpu.VMEM((V7X_SUBLANES, tile), F32), pltpu.VMEM((V7X_SUBLANES, tile), F32),
                        pltpu.VMEM((2, dh + ATTN_EXTRA_ROWS, tile), F32)],
        compiler_params=_params("arbitrary", "arbitrary"),
        name=name,
    )(qt, k, vt, cumt2, sz)


def _identity(v):
    return v


def _doubled(v):
    return jnp.concatenate([v, v], axis=-1).astype(F32)


def kernel(x, norm_pre, norm_post, s5_w_in, s5_a_re, s5_a_im, s5_log_dt, s5_b_re, s5_b_im, s5_c_re, s5_c_im,
           s5_d, s5_w_glu, s5_b_glu, s5_w_out, kv_norm, kv_w, kv_b_f, fox_w_in, fox_w_out):
    batch, seq, d_model = x.shape
    t = batch * seq
    heads = kv_b_f.shape[0]
    fox_width = fox_w_in.shape[-1] // 2
    assert norm_pre.shape[0] == 2 and s5_w_in.shape[0] == 1 and fox_w_in.shape[0] == 1
    assert fox_width // heads == FOX_HEAD_DIM and heads <= V7X_LANES
    ngroups = s5_a_re.shape[1]
    chunks_per_seq = seq // SSM_CHUNK
    scan_steps = (chunks_per_seq - 1).bit_length()

    h0 = x.reshape(t, d_model)

    s5_width = s5_w_in.shape[-1] // 2
    ldt2 = jnp.broadcast_to(s5_log_dt[0][:, None], (ngroups, 2 * S5_STATE)).astype(F32)
    b2 = jnp.concatenate([jnp.swapaxes(s5_b_re[0], 1, 2), jnp.swapaxes(s5_b_im[0], 1, 2)], axis=-1)
    c2 = jnp.concatenate([s5_c_re[0], s5_c_im[0]], axis=-1)
    rep = replication_matrix()
    r_all, gst, hst, sr, si, w_in = ssm_prep(_doubled(s5_a_re[0]), _doubled(s5_a_im[0]), ldt2, b2.astype(F32),
                                             c2.astype(F32), rep, replication_matrix(shift=1),
                                             scan_steps=scan_steps,
                                             side=[side_cast(s5_w_in[0], row_scale=norm_pre[0])])
    d_quads = jnp.tile(s5_d[0].reshape(-1, QUADS_PER_SLAB, 1, QUAD_LANES), (1, 1, 1, SSM_CHUNK)).astype(F32)
    kv_cols = 2 * fox_width
    kv_w_t = kv_w.T
    u, sz, w_glu, w_out, w_kv_t = fused_proj(
        h0, [Proj(w_in, False, 0, _identity, F32, False), Proj(w_in, False, s5_width, jax.nn.silu, BF16, False)],
        n=s5_width, tm=1024, name="s5_in_proj",
        side=[side_cast(s5_w_glu[0]), side_cast(s5_w_out[0]), side_cast(kv_w_t, kv_cols, col_scale=kv_norm)])
    yg, w_in_fox, w_out_fox = ssm_apply(u, r_all, gst, hst, sr, si, d_quads, rep, seq=seq,
                                        side=[side_cast(fox_w_in[0], row_scale=norm_pre[1]),
                                              side_cast(fox_w_out[0])])
    h1, h1_bf16, h1_inv_rms = s5_tail(yg, w_glu, s5_b_glu[0][None, :], sz, w_out, h0, norm_post[0][None, :])

    wf_t = jnp.pad((kv_w_t[kv_cols:] * kv_norm[None, :]).astype(BF16), ((0, V7X_LANES - heads), (0, 0)))
    bf = jnp.pad(kv_b_f, (0, V7X_LANES - heads))[:, None].astype(F32)
    scale = FOX_HEAD_DIM ** -0.5 * LOG2_E
    k, vt, qt, sz2, cumt2 = fused_proj(
        h1_bf16,
        [Proj(w_kv_t, True, 0, _identity, BF16, False),
         Proj(w_kv_t, True, fox_width, _identity, BF16, True),
         Proj(w_in_fox, False, 0, lambda v: v * scale, BF16, True),
         Proj(w_in_fox, False, fox_width, jax.nn.silu, BF16, False)],
        n=fox_width, inv=h1_inv_rms, gate=Gate(wf_t, bf, heads, seq), tm=1024, vmem_limit=LARGE_VMEM_LIMIT,
        name="fox_proj")
    o = fox_attention(qt, k, vt, cumt2.reshape(batch, heads, 1, seq), sz2,
                      batch=batch, seq=seq, heads=heads)
    h2 = out_proj(o, w_out_fox, h1, norm_post[1][None, :], name="fox_out_proj")
    return h2.reshape(batch, seq, d_model)
```
